```python
import math
import jax
import jax.numpy as jnp
from jax import lax
import numpy as np

D_MODEL = 1024
BATCH = 16
SEQ = 4096
DEPTH = 1
DEC_BATCH = 4
DEC_SEQ = 8192
PAST_LEN = 128

A_HEADS = 8
A_HEAD_DIM = 64
A_WIDTH = A_HEADS * A_HEAD_DIM
A_RANK_W = 64
A_RANK_A = 64
A_RANK_G = 128
A_GN_EPS = 64e-5
A_COLS = 3 * A_WIDTH + A_RANK_W + A_RANK_A + A_RANK_G
A_SPLIT = (A_WIDTH, 2 * A_WIDTH, 3 * A_WIDTH, 3 * A_WIDTH + A_RANK_W, 3 * A_WIDTH + A_RANK_W + A_RANK_A)
B_HEADS = 8
B_HEAD_DIM = 128
B_WIDTH = B_HEADS * B_HEAD_DIM
B_CONV = 5
B_CHUNK = 64
IN_COLS = A_COLS + 4 * B_WIDTH + 4 * B_HEADS + 2 * D_MODEL
IN_SPLIT = (A_COLS, A_COLS + 3 * B_WIDTH, A_COLS + 4 * B_WIDTH, A_COLS + 4 * B_WIDTH + 2 * B_HEADS, A_COLS + 4 * B_WIDTH + 4 * B_HEADS, A_COLS + 4 * B_WIDTH + 4 * B_HEADS + D_MODEL)
N_EXPERTS = 64
TOP_K = 6
N_GROUPS = 8
TOPK_GROUPS = 4
D_EXPERT = 256
D_SHARED = 256
ROUTED_SCALE = 2.5
MOE_BLOCK = 128
EPS = 1e-6

kernel_name = 'hybrid_rwkv7_gdn_moe_adaln_encoder'


def rms_norm(x, g):
    xf = x.astype(jnp.float32)
    y = xf * lax.rsqrt(jnp.mean(xf * xf, axis=-1, keepdims=True) + EPS)
    return (y * g).astype(x.dtype)


def l2_normalize(x):
    return x * lax.rsqrt(jnp.sum(x * x, axis=-1, keepdims=True) + 1e-6)


def centred_shift(z):
    zp = jnp.pad(z, ((0, 0), (1, 1), (0, 0)))
    return 0.5 * (zp[:, :-2] + zp[:, 2:])


def centred_depthwise_conv(z, w):
    C = z.shape[-1]
    return lax.conv_general_dilated(z, w[:, None, :].astype(z.dtype), window_strides=(1,), padding=((B_CONV // 2, B_CONV // 2),), dimension_numbers=('NWC', 'WIO', 'NWC'), feature_group_count=C)


def rwkv7_scan(r, w, kk, b, v, k, reverse):
    B, T, H, N = r.shape

    def step(S, inp):
        r_t, w_t, kk_t, b_t, v_t, k_t = inp
        S = (S * w_t[:, :, None, :]
             - jnp.einsum('bhvk,bhk->bhv', S, kk_t)[..., None] * b_t[:, :, None, :]
             + v_t[..., None] * k_t[:, :, None, :])
        return S, jnp.einsum('bhvk,bhk->bhv', S, r_t)

    xs = tuple(jnp.moveaxis(t, 1, 0) for t in (r, w, kk, b, v, k))
    _, y = lax.scan(step, jnp.zeros((B, H, N, N), jnp.float32), xs, reverse=reverse)
    return jnp.moveaxis(y, 0, 1)


def rwkv7_branch(z_a, mu, w0, w_up, a0, a_up, g_up, k_k, k_a, r_k, gn_w, gn_b, proj):
    B, T, _ = z_a.shape
    out_dtype = z_a.dtype
    z_a = (z_a + (centred_shift(z_a) - z_a) * mu).astype(jnp.float32)
    zr, zk, zv, zw, za, zg = jnp.split(z_a, A_SPLIT, axis=-1)

    def heads(t):
        return t.reshape(B, T, A_HEADS, A_HEAD_DIM)

    r, v = heads(zr), heads(zv)
    kk = l2_normalize(heads(zk * k_k))
    tw = jnp.tanh(zw)
    y = jnp.zeros_like(v)
    k_sum = jnp.zeros_like(zk)
    for d in range(2):
        w_log = -jax.nn.softplus(-(w0[d] + tw @ w_up[d])) - 0.5
        decay = jnp.exp(-jnp.exp(w_log))
        a = jax.nn.sigmoid(a0[d] + za @ a_up[d])
        k_d = zk * (1.0 + (a - 1.0) * k_a)
        y = y + rwkv7_scan(r, heads(decay), kk, kk * heads(a), v, heads(k_d), reverse=(d == 1))
        k_sum = k_sum + k_d
    mean = jnp.mean(y, axis=-1, keepdims=True)
    var = jnp.mean(jnp.square(y - mean), axis=-1, keepdims=True)
    yn = ((y - mean) * lax.rsqrt(var + A_GN_EPS)).reshape(B, T, A_WIDTH) * gn_w + gn_b
    bonus = (jnp.sum(r * heads(k_sum) * r_k, axis=-1, keepdims=True) * v).reshape(B, T, A_WIDTH)
    g = jax.nn.sigmoid(zg) @ g_up
    return ((yn + bonus) * g).astype(out_dtype) @ proj


def chunk_gated_delta(q, k, v, g, beta):
    B, T, H, DK = q.shape
    DV = v.shape[-1]
    C = B_CHUNK
    N = T // C

    def to_chunks(t):
        t = t.reshape((B, N, C) + t.shape[2:])
        return jnp.moveaxis(t, 3, 1)

    q, k, v, g, beta = to_chunks(q), to_chunks(k), to_chunks(v), to_chunks(g), to_chunks(beta)
    gc = jnp.cumsum(g, axis=-1)
    tri = jnp.tril(jnp.ones((C, C), dtype=bool))
    gam = jnp.exp(jnp.where(tri, gc[..., :, None] - gc[..., None, :], -jnp.inf))
    kb = k * beta[..., None]
    lower = jnp.einsum('bhnid,bhnjd->bhnij', kb, k) * gam
    u = lax.linalg.triangular_solve(lower, v * beta[..., None], left_side=True, lower=True, unit_diagonal=True)
    w = lax.linalg.triangular_solve(lower, kb * jnp.exp(gc)[..., None], left_side=True, lower=True, unit_diagonal=True)
    aqk = jnp.einsum('bhnid,bhnjd->bhnij', q, k) * gam
    g_last = gc[..., -1]
    k_tail = k * jnp.exp(g_last[..., None] - gc)[..., None]
    q_head = q * jnp.exp(gc)[..., None]

    def step(S, inp):
        u_n, w_n, aqk_n, q_n, k_n, gl_n = inp
        v_new = u_n - jnp.einsum('bhcd,bhde->bhce', w_n, S)
        o = jnp.einsum('bhcd,bhde->bhce', q_n, S) + jnp.einsum('bhij,bhje->bhie', aqk_n, v_new)
        S = S * jnp.exp(gl_n)[..., None, None] + jnp.einsum('bhcd,bhce->bhde', k_n, v_new)
        return S, o

    xs = tuple(jnp.moveaxis(t, 2, 0) for t in (u, w, aqk, q_head, k_tail, g_last))
    _, o = lax.scan(step, jnp.zeros((B, H, DK, DV), jnp.float32), xs)
    return jnp.transpose(o, (1, 0, 3, 2, 4)).reshape(B, T, H, DV)


def gated_deltanet_branch(z_qkv, z_z, z_beta, z_alpha, conv_w, a_log, dt_bias, norm_g, proj):
    B, T, _ = z_qkv.shape
    out_dtype = z_qkv.dtype
    qkv = jax.nn.silu(centred_depthwise_conv(z_qkv, conv_w)).astype(jnp.float32)
    q, k, v = jnp.split(qkv, 3, axis=-1)
    q = l2_normalize(q.reshape(B, T, B_HEADS, B_HEAD_DIM)) * (B_HEAD_DIM ** -0.5)
    k = l2_normalize(k.reshape(B, T, B_HEADS, B_HEAD_DIM))
    v = v.reshape(B, T, B_HEADS, B_HEAD_DIM)
    beta = jax.nn.sigmoid(z_beta.astype(jnp.float32)).reshape(B, T, 2, B_HEADS)
    g = -jnp.exp(a_log) * jax.nn.softplus(z_alpha.astype(jnp.float32).reshape(B, T, 2, B_HEADS) + dt_bias)
    o_fwd = chunk_gated_delta(q, k, v, g[:, :, 0], beta[:, :, 0])

    def rev(t):
        return jnp.flip(t, axis=1)

    o_bwd = rev(chunk_gated_delta(rev(q), rev(k), rev(v), rev(g[:, :, 1]), rev(beta[:, :, 1])))
    o = o_fwd + o_bwd
    z = z_z.astype(jnp.float32).reshape(B, T, B_HEADS, B_HEAD_DIM)
    o = o * lax.rsqrt(jnp.mean(o * o, axis=-1, keepdims=True) + EPS) * norm_g * jax.nn.silu(z)
    return o.reshape(B, T, B_WIDTH).astype(out_dtype) @ proj


def token_mixer(h, p, l):
    z = jnp.einsum('btd,de->bte', h, p['w_in'][l])
    z_a, z_qkv, z_z, z_beta, z_alpha, z_ga, z_gb = jnp.split(z, IN_SPLIT, axis=-1)
    y_a = rwkv7_branch(z_a, p['shift_mu'][l], p['a_w0'][l], p['a_w_up'][l], p['a_a0'][l], p['a_a_up'][l], p['a_g_up'][l], p['a_k_k'][l], p['a_k_a'][l], p['a_r_k'][l], p['a_gn_w'][l], p['a_gn_b'][l], p['a_proj'][l])
    y_b = gated_deltanet_branch(z_qkv, z_z, z_beta, z_alpha, p['b_conv_w'][l], p['b_a_log'][l], p['b_dt_bias'][l], p['b_norm_g'][l], p['b_proj'][l])
    u = jax.nn.sigmoid(z_ga) * y_a + jax.nn.sigmoid(z_gb) * y_b
    return jnp.einsum('btd,de->bte', u, p['w_out'][l])


def swiglu(x, wg, wu, wd):
    return (jax.nn.silu(x @ wg) * (x @ wu)) @ wd


def route(h, router_w, router_bias):
    B, T, _ = h.shape
    scores = jax.nn.sigmoid(jnp.einsum('btd,de->bte', h.astype(jnp.float32), router_w.astype(jnp.float32)))
    biased = scores + router_bias.astype(jnp.float32)
    group_score = jnp.sum(lax.top_k(biased.reshape(B, T, N_GROUPS, N_EXPERTS // N_GROUPS), 2)[0], axis=-1)
    _, top_groups = lax.top_k(group_score, TOPK_GROUPS)
    group_keep = jnp.sum(jax.nn.one_hot(top_groups, N_GROUPS, dtype=jnp.float32), axis=-2) > 0
    expert_keep = jnp.repeat(group_keep, N_EXPERTS // N_GROUPS, axis=-1)
    _, idx = lax.top_k(jnp.where(expert_keep, biased, -jnp.inf), TOP_K)
    wts = jnp.take_along_axis(scores, idx, axis=-1)
    wts = wts / jnp.sum(wts, axis=-1, keepdims=True) * ROUTED_SCALE
    return idx, wts


def routed_experts(h, idx, wts, w_gate, w_up, w_down):
    T, D = h.shape
    n_slots = T * TOP_K
    n_blocks = -(-n_slots // MOE_BLOCK) + N_EXPERTS
    e_flat = idx.reshape(-1)
    tok_flat = jnp.arange(n_slots, dtype=jnp.int32) // TOP_K
    order = jnp.argsort(e_flat).astype(jnp.int32)
    e_sorted = e_flat[order]
    counts = jnp.zeros((N_EXPERTS,), jnp.int32).at[e_flat].add(1)
    starts = jnp.cumsum(counts) - counts
    padded = (counts + MOE_BLOCK - 1) // MOE_BLOCK * MOE_BLOCK
    pad_ends = jnp.cumsum(padded)
    pad_starts = pad_ends - padded
    dest = pad_starts[e_sorted] + jnp.arange(n_slots, dtype=jnp.int32) - starts[e_sorted]
    row_slot = jnp.full((n_blocks * MOE_BLOCK,), n_slots, jnp.int32).at[dest].set(order)
    row_tok = jnp.concatenate([tok_flat, jnp.zeros((1,), jnp.int32)])[row_slot]
    row_wt = jnp.concatenate([wts.reshape(-1), jnp.zeros((1,), wts.dtype)])[row_slot]
    block_start = jnp.arange(n_blocks, dtype=jnp.int32) * MOE_BLOCK
    block_expert = jnp.minimum(jnp.searchsorted(pad_ends, block_start, side='right'), N_EXPERTS - 1)
    x_blocks = h[row_tok].reshape(n_blocks, MOE_BLOCK, D)

    def expert_block(args):
        xb, e = args
        return swiglu(xb, w_gate[e], w_up[e], w_down[e])

    y = lax.map(expert_block, (x_blocks, block_expert)).reshape(n_blocks * MOE_BLOCK, D)
    return jnp.zeros_like(h).at[row_tok].add(y * row_wt[:, None].astype(h.dtype))


def moe_ffn(h, p, l):
    idx, wts = route(h, p['router_w'][l], p['router_bias'][l])
    eg, eu, ed = p['exp_gate'][l], p['exp_up'][l], p['exp_down'][l]
    routed = lax.map(lambda args: routed_experts(args[0], args[1], args[2], eg, eu, ed), (h, idx, wts))
    return routed + swiglu(h, p['sh_gate'][l], p['sh_up'][l], p['sh_down'][l])


def encoder_layer_stack(x, c, p):
    for l in range(DEPTH):
        mod = jax.nn.silu(c) @ p['ada_w'][l] + p['ada_b'][l]
        sh1, sc1, gt1, sh2, sc2, gt2 = jnp.split(mod[:, None, :], 6, axis=-1)
        h = rms_norm(x, p['norm1_g'][l]) * (1.0 + sc1) + sh1
        x = x + gt1 * token_mixer(h, p, l)
        h = rms_norm(x, p['norm2_g'][l]) * (1.0 + sc2) + sh2
        x = x + gt2 * moe_ffn(h, p, l)
    return rms_norm(x, p['final_g'])


def setup_inputs(seed: int = 0) -> dict:
    key = jax.random.key(seed)
    counter = [0]

    def nk():
        counter[0] += 1
        return jax.random.fold_in(key, counter[0])

    def nrm(shape, scale):
        return jax.random.normal(nk(), shape, jnp.float32) * scale

    def uni(shape, lo, hi):
        return jax.random.uniform(nk(), shape, jnp.float32, lo, hi)

    L, D = DEPTH, D_MODEL
    dt_init = jnp.exp(uni((L, 2, B_HEADS), math.log(1e-3), math.log(1e-1)))
    return {
        'x_prompt': nrm((BATCH, SEQ, D), 1.0),
        'x_sample': nrm((DEC_BATCH, DEC_SEQ, D), 1.0),
        'c_prompt': nrm((BATCH, D), 1.0),
        'c_sample': nrm((DEC_BATCH, D), 1.0),
        'ada_w': nrm((L, D, 6 * D), 0.5 * D ** -0.5),
        'ada_b': nrm((L, 6 * D), 0.02),
        'norm1_g': 1.0 + nrm((L, D), 0.02),
        'norm2_g': 1.0 + nrm((L, D), 0.02),
        'w_in': nrm((L, D, IN_COLS), D ** -0.5),
        'shift_mu': uni((L, A_COLS), 0.0, 1.0),
        'a_w0': uni((L, 2, A_WIDTH), -7.0, 0.0),
        'a_w_up': nrm((L, 2, A_RANK_W, A_WIDTH), 0.1 * A_RANK_W ** -0.5),
        'a_a0': nrm((L, 2, A_WIDTH), 0.5),
        'a_a_up': nrm((L, 2, A_RANK_A, A_WIDTH), 0.1 * A_RANK_A ** -0.5),
        'a_g_up': nrm((L, A_RANK_G, A_WIDTH), A_RANK_G ** -0.5),
        'a_k_k': 0.85 + nrm((L, A_WIDTH), 0.02),
        'a_k_a': 1.0 + nrm((L, A_WIDTH), 0.02),
        'a_r_k': nrm((L, A_HEADS, A_HEAD_DIM), 0.1),
        'a_gn_w': 1.0 + nrm((L, A_WIDTH), 0.02),
        'a_gn_b': nrm((L, A_WIDTH), 0.02),
        'a_proj': nrm((L, A_WIDTH, D), A_WIDTH ** -0.5),
        'b_conv_w': nrm((L, B_CONV, 3 * B_WIDTH), B_CONV ** -0.5),
        'b_a_log': jnp.log(uni((L, 2, B_HEADS), 1.0, 16.0)),
        'b_dt_bias': dt_init + jnp.log(-jnp.expm1(-dt_init)),
        'b_norm_g': 1.0 + nrm((L, B_HEAD_DIM), 0.02),
        'b_proj': nrm((L, B_WIDTH, D), B_WIDTH ** -0.5),
        'w_out': nrm((L, D, D), D ** -0.5),
        'router_w': nrm((L, D, N_EXPERTS), D ** -0.5),
        'router_bias': nrm((L, N_EXPERTS), 0.01),
        'exp_gate': nrm((L, N_EXPERTS, D, D_EXPERT), D ** -0.5),
        'exp_up': nrm((L, N_EXPERTS, D, D_EXPERT), D ** -0.5),
        'exp_down': nrm((L, N_EXPERTS, D_EXPERT, D), D_EXPERT ** -0.5),
        'sh_gate': nrm((L, D, D_SHARED), D ** -0.5),
        'sh_up': nrm((L, D, D_SHARED), D ** -0.5),
        'sh_down': nrm((L, D_SHARED, D), D_SHARED ** -0.5),
        'final_g': 1.0 + nrm((D,), 0.02),
    }


def reference(x_prompt, x_sample, c_prompt, c_sample, ada_w, ada_b, norm1_g, norm2_g, w_in, shift_mu, a_w0, a_w_up, a_a0, a_a_up, a_g_up, a_k_k, a_k_a, a_r_k, a_gn_w, a_gn_b, a_proj, b_conv_w, b_a_log, b_dt_bias, b_norm_g, b_proj, w_out, router_w, router_bias, exp_gate, exp_up, exp_down, sh_gate, sh_up, sh_down, final_g):
    p = dict(ada_w=ada_w, ada_b=ada_b, norm1_g=norm1_g, norm2_g=norm2_g, w_in=w_in, shift_mu=shift_mu,
             a_w0=a_w0, a_w_up=a_w_up, a_a0=a_a0, a_a_up=a_a_up, a_g_up=a_g_up, a_k_k=a_k_k, a_k_a=a_k_a,
             a_r_k=a_r_k, a_gn_w=a_gn_w, a_gn_b=a_gn_b, a_proj=a_proj, b_conv_w=b_conv_w, b_a_log=b_a_log,
             b_dt_bias=b_dt_bias, b_norm_g=b_norm_g, b_proj=b_proj, w_out=w_out, router_w=router_w,
             router_bias=router_bias, exp_gate=exp_gate, exp_up=exp_up, exp_down=exp_down, sh_gate=sh_gate,
             sh_up=sh_up, sh_down=sh_down, final_g=final_g)
    y_prompt = encoder_layer_stack(x_prompt, c_prompt, p)
    y_sample = encoder_layer_stack(x_sample, c_sample, p)
    return (y_prompt, y_sample)
```

```python
import functools

import jax
import jax.numpy as jnp
from jax import lax
from jax.experimental import pallas as pl
from jax.experimental.pallas import tpu as pltpu

F32 = jnp.float32
BF16 = jnp.bfloat16
HIGHEST = lax.Precision.HIGHEST

D_MODEL = 1024
A_HEADS = 8
A_HEAD_DIM = 64
A_WIDTH = A_HEADS * A_HEAD_DIM
A_RANK_W = 64
A_RANK_A = 64
A_RANK_G = 128
A_GN_EPS = 64e-5
A_COLS = 3 * A_WIDTH + A_RANK_W + A_RANK_A + A_RANK_G
B_HEADS = 8
B_HEAD_DIM = 128
B_WIDTH = B_HEADS * B_HEAD_DIM
B_CONV = 5
CHUNK = 64
N_EXPERTS = 64
TOP_K = 6
N_GROUPS = 8
TOPK_GROUPS = 4
GROUP_SIZE = N_EXPERTS // N_GROUPS
D_EXPERT = 256
ROUTED_SCALE = 2.5
EPS = 1e-6
LANES = 128
SUBLANES = 8
SMALL_COLS = LANES
VMEM_LIMIT = 56 * 1024 * 1024
NEG_INF = float("-inf")


def _dot(a, b):
    return jnp.dot(a.astype(BF16), b.astype(BF16), preferred_element_type=F32)


def _dot_nt(a, b):
    return lax.dot_general(a.astype(BF16), b.astype(BF16), (((1,), (1,)), ((), ())),
                           preferred_element_type=F32)


def _dot_tn(a, b):
    return lax.dot_general(a.astype(BF16), b.astype(BF16), (((0,), (0,)), ((), ())),
                           preferred_element_type=F32)


def _dot_hi(a, b):
    return jnp.dot(a, b, precision=HIGHEST, preferred_element_type=F32)


def _seg_sum(x, ones_bd):
    hi = x.astype(BF16)
    lo = (x - hi.astype(F32)).astype(BF16)
    return (jnp.dot(hi, ones_bd, preferred_element_type=F32)
            + jnp.dot(lo, ones_bd, preferred_element_type=F32))


def _softplus(x):
    return jnp.maximum(x, 0.0) + jnp.log1p(jnp.exp(-jnp.abs(x)))


def _sigmoid(x):
    return 1.0 / (1.0 + jnp.exp(-x))


def _silu(x):
    return x * _sigmoid(x)


def _tri_inverse(a_strict, eye):
    x = -a_strict
    t = eye + x
    power = 1
    while 2 * power < CHUNK:
        x = _dot(x, x)
        t = t + _dot(t, x)
        power *= 2
    return t


def _params(sem):
    return pltpu.CompilerParams(dimension_semantics=sem, vmem_limit_bytes=VMEM_LIMIT)


def _mod_kernel(c_ref, w_ref, b_ref, o_ref):
    c = c_ref[...]
    o_ref[...] = _dot_hi(_silu(c), w_ref[...]) + b_ref[...]


def _adaln_mod(c, ada_w, ada_b):
    rows = c.shape[0]
    n = ada_w.shape[1]
    return pl.pallas_call(
        _mod_kernel,
        grid=(n // D_MODEL,),
        in_specs=[pl.BlockSpec((rows, D_MODEL), lambda j: (0, 0)),
                  pl.BlockSpec((D_MODEL, D_MODEL), lambda j: (0, j)),
                  pl.BlockSpec((1, D_MODEL), lambda j: (0, j))],
        out_specs=pl.BlockSpec((rows, D_MODEL), lambda j: (0, j)),
        out_shape=jax.ShapeDtypeStruct((rows, n), F32),
        compiler_params=_params(("arbitrary",)),
        name="adaln_mod",
    )(c, ada_w, ada_b.reshape(1, n))


def _rms_mod(x, g, sc, sh):
    y = x * lax.rsqrt(jnp.mean(x * x, axis=-1, keepdims=True) + EPS)
    return (y * g) * (1.0 + sc) + sh


def _inproj_kernel(x_ref, g_ref, sc_ref, sh_ref, w_ref, o_ref, h_scr):
    @pl.when(pl.program_id(2) == 0)
    def _():
        h_scr[...] = _rms_mod(x_ref[0], g_ref[...], sc_ref[0], sh_ref[0]).astype(BF16)

    o_ref[0] = jnp.dot(h_scr[...], w_ref[...], preferred_element_type=F32)


def _inproj(x, g, sc, sh, w, tn):
    B, T, D = x.shape
    n = w.shape[1]
    tm = min(512, T)
    return pl.pallas_call(
        _inproj_kernel,
        grid=(B, T // tm, n // tn),
        in_specs=[pl.BlockSpec((1, tm, D), lambda b, i, j: (b, i, 0)),
                  pl.BlockSpec((1, D), lambda b, i, j: (0, 0)),
                  pl.BlockSpec((1, 1, D), lambda b, i, j: (b, 0, 0)),
                  pl.BlockSpec((1, 1, D), lambda b, i, j: (b, 0, 0)),
                  pl.BlockSpec((D, tn), lambda b, i, j: (0, j))],
        out_specs=pl.BlockSpec((1, tm, tn), lambda b, i, j: (b, i, j)),
        out_shape=jax.ShapeDtypeStruct((B, T, n), F32),
        scratch_shapes=[pltpu.VMEM((tm, D), BF16)],
        compiler_params=_params(("parallel", "parallel", "arbitrary")),
        name="in_proj",
    )(x, g, sc, sh, w)


def _fill_halo(ext_ref, cur, prev8, next8, tt):
    i = pl.program_id(1)
    last = pl.num_programs(1) - 1
    ext_ref[pl.ds(0, SUBLANES), :] = jnp.where(i == 0, 0.0, prev8)
    ext_ref[pl.ds(SUBLANES, tt), :] = cur
    ext_ref[pl.ds(SUBLANES + tt, SUBLANES), :] = jnp.where(i == last, 0.0, next8)


def _halo_specs(tt, width, col_block, seq_len):
    nb = tt // SUBLANES
    last = seq_len // SUBLANES - 1
    return [
        pl.BlockSpec((1, tt, width), lambda b, i: (b, i, col_block)),
        pl.BlockSpec((1, SUBLANES, width), lambda b, i: (b, jnp.maximum(i * nb - 1, 0), col_block)),
        pl.BlockSpec((1, SUBLANES, width), lambda b, i: (b, jnp.minimum((i + 1) * nb, last), col_block)),
    ]


def _rwkv_prep_kernel(z_ref, zp_ref, zn_ref, mu_ref, w0_ref, a0_ref, up_ref, gup_ref, kk_ref, ka_ref,
                      rk_ref, ones_ref,
                      r_out, kk_out, v_out, g_out, bonus_out, lw_out, kd_out, bd_out, ext_scr):
    tt = z_ref.shape[1]
    z = z_ref[0]
    _fill_halo(ext_scr, z, zp_ref[0], zn_ref[0], tt)
    z_prev = ext_scr[pl.ds(SUBLANES - 1, tt), :]
    z_next = ext_scr[pl.ds(SUBLANES + 1, tt), :]
    zs = z + (0.5 * (z_prev + z_next) - z) * mu_ref[...]
    W = A_WIDTH
    zr, zk, zv = zs[:, 0:W], zs[:, W:2 * W], zs[:, 2 * W:3 * W]
    zwa = zs[:, 3 * W:3 * W + LANES]
    zg = zs[:, 3 * W + LANES:3 * W + 2 * LANES]
    ones_bd = ones_ref[...]
    kk_raw = zk * kk_ref[...]
    kk = kk_raw * lax.rsqrt(_seg_sum(kk_raw * kk_raw, ones_bd) + 1e-6)
    lane = lax.broadcasted_iota(jnp.int32, zwa.shape, 1)
    lhs = jnp.where(lane < A_RANK_W, jnp.tanh(zwa), zwa)
    k_sum = jnp.zeros_like(zk)
    for d in range(2):
        up = _dot_hi(lhs, up_ref[d])
        wl = w0_ref[d:d + 1, :] + up[:, 0:W]
        w_log = -_softplus(-wl) - 0.5
        lw_out[d, 0] = -jnp.exp(w_log)
        a = _sigmoid(a0_ref[d:d + 1, :] + up[:, W:2 * W])
        k_d = zk * (1.0 + (a - 1.0) * ka_ref[...])
        kd_out[d, 0] = k_d
        bd_out[d, 0] = kk * a
        k_sum = k_sum + k_d
    r_out[0] = zr
    kk_out[0] = kk
    v_out[0] = zv
    g_out[0] = _dot_hi(_sigmoid(zg), gup_ref[...])
    bonus_out[0] = _seg_sum(zr * k_sum * rk_ref[...], ones_bd) * zv


def _rwkv_prep(z_a, mu, w0, a0, up_comb, g_up, k_k, k_a, r_k, ones_bd):
    B, T, _ = z_a.shape
    tt = min(256, T)
    W = A_WIDTH
    full = lambda shape: pl.BlockSpec(shape, lambda b, i: (0,) * len(shape))
    tok = pl.BlockSpec((1, tt, W), lambda b, i: (b, i, 0))
    tok2 = pl.BlockSpec((2, 1, tt, W), lambda b, i: (0, b, i, 0))
    s1 = jax.ShapeDtypeStruct((B, T, W), F32)
    s2 = jax.ShapeDtypeStruct((2, B, T, W), F32)
    return pl.pallas_call(
        _rwkv_prep_kernel,
        grid=(B, T // tt),
        in_specs=_halo_specs(tt, A_COLS, 0, T) + [
            full((1, A_COLS)), full((2, W)), full((2, W)), full((2, LANES, 2 * W)),
            full((A_RANK_G, W)), full((1, W)), full((1, W)), full((1, W)), full((W, W))],
        out_specs=[tok, tok, tok, tok, tok, tok2, tok2, tok2],
        out_shape=[s1, s1, s1, s1, s1, s2, s2, s2],
        scratch_shapes=[pltpu.VMEM((tt + 2 * SUBLANES, A_COLS), F32)],
        compiler_params=_params(("parallel", "parallel")),
        name="rwkv_prep",
    )(z_a, z_a, z_a, mu, w0, a0, up_comb, g_up, k_k, k_a, r_k, ones_bd)


def _chunk_masks(reverse):
    n = 2 * CHUNK
    row = lax.broadcasted_iota(jnp.int32, (n, n), 0)
    col = lax.broadcasted_iota(jnp.int32, (n, n), 1)
    same = (row // CHUNK) == (col // CHUNK)
    ti, tj = row % CHUNK, col % CHUNK
    if reverse:
        return same & (ti < tj), same & (ti <= tj), row == col
    return same & (ti > tj), same & (ti >= tj), row == col


def _cumsum_matrix(reverse):
    row = lax.broadcasted_iota(jnp.int32, (CHUNK, CHUNK), 0)
    col = lax.broadcasted_iota(jnp.int32, (CHUNK, CHUNK), 1)
    return ((row <= col) if reverse else (row >= col)).astype(F32)


def _rwkv_pair_chunk(r, kk, v, k, b, cs, lw, ht, reverse):
    C = CHUNK
    strict, incl, eye = _chunk_masks(reverse)
    cs_end = cs[0:1, :] if reverse else cs[C - 1:C, :]
    lane = lax.broadcasted_iota(jnp.int32, (C, LANES), 1)
    m0 = lane < A_HEAD_DIM

    def stack(x):
        return jnp.concatenate([jnp.where(m0, x, 0.0), jnp.where(m0, 0.0, x)], axis=0)

    g_inv = jnp.exp(-cs)
    rg = stack(r * jnp.exp(cs))
    kkg = stack(kk * jnp.exp(cs - lw))
    ki = stack(k * g_inv)
    bi = stack(b * g_inv)
    g_tail = jnp.exp(cs_end - cs)
    kt = stack(k * g_tail)
    bt = stack(b * g_tail)
    vs = stack(v)

    p = _dot_nt(jnp.concatenate([kkg, rg], axis=0), jnp.concatenate([bi, ki], axis=0))
    n = 2 * C
    a_ab = jnp.where(strict, p[0:n, 0:n], 0.0)
    a_ak = jnp.where(strict, p[0:n, n:2 * n], 0.0)
    a_rb = jnp.where(incl, p[n:2 * n, 0:n], 0.0)
    a_rk = jnp.where(incl, p[n:2 * n, n:2 * n], 0.0)
    tinv = _tri_inverse(a_ab, eye.astype(F32))
    wu = _dot(tinv, jnp.concatenate([kkg, _dot(a_ak, vs)], axis=1))
    r2 = _dot(a_rb, wu)
    q_s = rg - r2[:, 0:LANES]
    yloc_s = _dot(a_rk, vs) - r2[:, LANES:2 * LANES]
    q = q_s[0:C] + q_s[C:n]
    y = yloc_s[0:C] + yloc_s[C:n] + _dot_nt(q, ht)
    btw = _dot_tn(bt, wu[:, 0:LANES])
    hloc_t = _dot_tn(vs, kt) - _dot_tn(wu[:, LANES:2 * LANES], bt)
    ht_new = ht * jnp.exp(cs_end) - _dot_nt(ht, btw) + hloc_t
    return y, ht_new


def _rwkv_scan_kernel(rf, kkf, vf, lwf, kf, bf, rb, kkb, vb, lwb, kb, bb, yf_out, yb_out, h_scr):
    @pl.when(pl.program_id(1) == 0)
    def _():
        h_scr[...] = jnp.zeros_like(h_scr)

    for d, (r_ref, kk_ref, v_ref, lw_ref, k_ref, b_ref, y_out) in enumerate(
            ((rf, kkf, vf, lwf, kf, bf, yf_out), (rb, kkb, vb, lwb, kb, bb, yb_out))):
        reverse = d == 1
        lw_all = lw_ref[0, 0]
        cs_all = _dot_hi(_cumsum_matrix(reverse), lw_all)
        for pidx in range(A_WIDTH // LANES):
            sl = slice(pidx * LANES, (pidx + 1) * LANES)
            y, ht = _rwkv_pair_chunk(r_ref[0, :, sl], kk_ref[0, :, sl], v_ref[0, :, sl],
                                     k_ref[0, 0, :, sl], b_ref[0, 0, :, sl], cs_all[:, sl],
                                     lw_all[:, sl], h_scr[d, pidx], reverse)
            h_scr[d, pidx] = ht
            y_out[0, :, sl] = y


def _rwkv_scan(r, kk, v, lw, kd, bd):
    B, T, W = r.shape
    N = T // CHUNK
    fwd = pl.BlockSpec((1, CHUNK, W), lambda b, c: (b, c, 0))
    bwd = pl.BlockSpec((1, CHUNK, W), lambda b, c: (b, N - 1 - c, 0))
    fwd2 = pl.BlockSpec((1, 1, CHUNK, W), lambda b, c: (0, b, c, 0))
    bwd2 = pl.BlockSpec((1, 1, CHUNK, W), lambda b, c: (1, b, N - 1 - c, 0))
    out = jax.ShapeDtypeStruct((B, T, W), F32)
    return pl.pallas_call(
        _rwkv_scan_kernel,
        grid=(B, N),
        in_specs=[fwd, fwd, fwd, fwd2, fwd2, fwd2, bwd, bwd, bwd, bwd2, bwd2, bwd2],
        out_specs=[fwd, bwd],
        out_shape=[out, out],
        scratch_shapes=[pltpu.VMEM((2, W // LANES, LANES, LANES), F32)],
        compiler_params=_params(("parallel", "arbitrary")),
        name="rwkv_scan",
    )(r, kk, v, lw, kd, bd, r, kk, v, lw, kd, bd)


def _gdn_prep_kernel(z_ref, zp_ref, zn_ref, small_ref, cw_ref, alog_ref, dtb_ref,
                     q_out, k_out, v_out, bg_out, ext_scr):
    tt = z_ref.shape[1]
    _fill_halo(ext_scr, z_ref[0], zp_ref[0], zn_ref[0], tt)
    half = B_CONV // 2
    acc = None
    for j in range(B_CONV):
        term = ext_scr[pl.ds(SUBLANES - half + j, tt), :] * cw_ref[j:j + 1, :]
        acc = term if acc is None else acc + term
    qkv = _silu(acc)
    for h in range(B_HEADS):
        for idx, out in enumerate((q_out, k_out)):
            x = qkv[:, idx * B_WIDTH + h * B_HEAD_DIM: idx * B_WIDTH + (h + 1) * B_HEAD_DIM]
            xn = x * lax.rsqrt(jnp.sum(x * x, axis=-1, keepdims=True) + 1e-6)
            if idx == 0:
                xn = xn * (B_HEAD_DIM ** -0.5)
            out[0, :, h * B_HEAD_DIM:(h + 1) * B_HEAD_DIM] = xn
    v_out[0] = qkv[:, 2 * B_WIDTH:3 * B_WIDTH]
    s = small_ref[0]
    lane = lax.broadcasted_iota(jnp.int32, s.shape, 1)
    beta = _sigmoid(s)
    g = -jnp.exp(alog_ref[...]) * _softplus(s + dtb_ref[...])
    bg_out[0] = jnp.where(lane < 2 * B_HEADS, beta, g)


def _gdn_prep(z_b, conv_w, alog_vec, dtb_vec):
    B, T, _ = z_b.shape
    tt = min(256, T)
    Wq = 3 * B_WIDTH
    full = lambda shape: pl.BlockSpec(shape, lambda b, i: (0,) * len(shape))
    tok = pl.BlockSpec((1, tt, B_WIDTH), lambda b, i: (b, i, 0))
    small = pl.BlockSpec((1, tt, SMALL_COLS), lambda b, i: (b, i, Wq // SMALL_COLS))
    s1 = jax.ShapeDtypeStruct((B, T, B_WIDTH), F32)
    return pl.pallas_call(
        _gdn_prep_kernel,
        grid=(B, T // tt),
        in_specs=_halo_specs(tt, Wq, 0, T) + [small, full((B_CONV, Wq)), full((1, SMALL_COLS)),
                                            full((1, SMALL_COLS))],
        out_specs=[tok, tok, tok, pl.BlockSpec((1, tt, SMALL_COLS), lambda b, i: (b, i, 0))],
        out_shape=[s1, s1, s1, jax.ShapeDtypeStruct((B, T, SMALL_COLS), F32)],
        scratch_shapes=[pltpu.VMEM((tt + 2 * SUBLANES, Wq), F32)],
        compiler_params=_params(("parallel", "parallel")),
        name="gdn_prep",
    )(z_b, z_b, z_b, z_b, conv_w, alog_vec, dtb_vec)


def _gdn_pair_chunk(q0, q1, k0, k1, v0, v1, gc2, gct2, bg2, j0, j1, s0, s1, reverse):
    C = CHUNK
    n = 2 * C
    strict, incl, eye = _chunk_masks(reverse)
    top = lax.broadcasted_iota(jnp.int32, (n, 1), 0) < C
    left = lax.broadcasted_iota(jnp.int32, (1, n), 1) < C
    g0, g1 = j0 + 2 * B_HEADS, j1 + 2 * B_HEADS
    gcol = jnp.where(top, gc2[:, g0:g0 + 1], gc2[:, g1:g1 + 1])
    grow = jnp.where(left, gct2[g0:g0 + 1, :], gct2[g1:g1 + 1, :])
    beta = jnp.where(top, bg2[:, j0:j0 + 1], bg2[:, j1:j1 + 1])
    e = 0 if reverse else C - 1
    glast = jnp.where(top, gc2[e:e + 1, g0:g0 + 1], gc2[e:e + 1, g1:g1 + 1])
    gam = jnp.exp(jnp.where(incl, gcol - grow, NEG_INF))
    ks = jnp.concatenate([k0, k1], axis=0)
    qs = jnp.concatenate([q0, q1], axis=0)
    vs = jnp.concatenate([v0, v1], axis=0)
    kb = ks * beta
    p = _dot_nt(jnp.concatenate([kb, qs], axis=0), ks)
    lower = jnp.where(strict, p[0:n] * gam, 0.0)
    aqk = p[n:2 * n] * gam
    tinv = _tri_inverse(lower, eye.astype(F32))
    egc = jnp.exp(gcol)
    uw = _dot(tinv, jnp.concatenate([vs * beta, kb * egc], axis=1))
    u, w = uw[:, 0:LANES], uw[:, LANES:2 * LANES]
    qh = qs * egc
    ktail = ks * jnp.exp(glast - gcol)
    v_new = u - jnp.concatenate([_dot(w[0:C], s0), _dot(w[C:n], s1)], axis=0)
    o = jnp.concatenate([_dot(qh[0:C], s0), _dot(qh[C:n], s1)], axis=0) + _dot(aqk, v_new)
    decay = jnp.exp(glast)
    s0_new = s0 * decay[0:1, :] + _dot_tn(ktail[0:C], v_new[0:C])
    s1_new = s1 * decay[C:C + 1, :] + _dot_tn(ktail[C:n], v_new[C:n])
    return o[0:C], o[C:n], s0_new, s1_new


def _gdn_scan_kernel(qf, kf, vf, bgf, qb, kb, vb, bgb, of_out, ob_out, s_scr):
    @pl.when(pl.program_id(1) == 0)
    def _():
        s_scr[...] = jnp.zeros_like(s_scr)

    D = B_HEAD_DIM
    for d, (q_ref, k_ref, v_ref, bg_ref, o_out) in enumerate(
            ((qf, kf, vf, bgf, of_out), (qb, kb, vb, bgb, ob_out))):
        reverse = d == 1
        bg = bg_ref[0]
        gc = _dot_hi(_cumsum_matrix(reverse), bg)
        bg2 = jnp.concatenate([bg, bg], axis=0)
        gc2 = jnp.concatenate([gc, gc], axis=0)
        gct2 = gc2.T
        for pidx in range(B_HEADS // 2):
            h0, h1 = 2 * pidx, 2 * pidx + 1
            sl0, sl1 = slice(h0 * D, (h0 + 1) * D), slice(h1 * D, (h1 + 1) * D)
            o0, o1, s0, s1 = _gdn_pair_chunk(
                q_ref[0, :, sl0], q_ref[0, :, sl1], k_ref[0, :, sl0], k_ref[0, :, sl1],
                v_ref[0, :, sl0], v_ref[0, :, sl1], gc2, gct2, bg2,
                d * B_HEADS + h0, d * B_HEADS + h1, s_scr[d, h0], s_scr[d, h1], reverse)
            s_scr[d, h0] = s0
            s_scr[d, h1] = s1
            o_out[0, :, sl0] = o0
            o_out[0, :, sl1] = o1


def _gdn_scan(q, k, v, bg):
    B, T, W = q.shape
    N = T // CHUNK
    fwd = pl.BlockSpec((1, CHUNK, W), lambda b, c: (b, c, 0))
    bwd = pl.BlockSpec((1, CHUNK, W), lambda b, c: (b, N - 1 - c, 0))
    sfwd = pl.BlockSpec((1, CHUNK, SMALL_COLS), lambda b, c: (b, c, 0))
    sbwd = pl.BlockSpec((1, CHUNK, SMALL_COLS), lambda b, c: (b, N - 1 - c, 0))
    out = jax.ShapeDtypeStruct((B, T, W), F32)
    return pl.pallas_call(
        _gdn_scan_kernel,
        grid=(B, N),
        in_specs=[fwd, fwd, fwd, sfwd, bwd, bwd, bwd, sbwd],
        out_specs=[fwd, bwd],
        out_shape=[out, out],
        scratch_shapes=[pltpu.VMEM((2, B_HEADS, B_HEAD_DIM, B_HEAD_DIM), F32)],
        compiler_params=_params(("parallel", "arbitrary")),
        name="gdn_scan",
    )(q, k, v, bg, q, k, v, bg)


def _route(scores, biased):
    tt = scores.shape[-1]
    shape3 = (N_GROUPS, GROUP_SIZE, tt)
    s3 = scores.reshape(shape3)
    b3 = biased.reshape(shape3)
    jid = lax.broadcasted_iota(jnp.int32, shape3, 1).astype(F32)
    gid = lax.broadcasted_iota(jnp.int32, shape3, 0).astype(F32)
    m1 = jnp.max(b3, axis=1, keepdims=True)
    first = jnp.min(jnp.where(b3 == m1, jid, float(GROUP_SIZE)), axis=1, keepdims=True)
    m2 = jnp.max(jnp.where(jid == first, NEG_INF, b3), axis=1, keepdims=True)
    gs = m1 + m2
    grp = lax.broadcasted_iota(jnp.int32, (N_GROUPS, 1, tt), 0).astype(F32)
    keep = jnp.zeros((N_GROUPS, 1, tt), F32)
    for _ in range(TOPK_GROUPS):
        m = jnp.max(gs, axis=0, keepdims=True)
        pick = grp == jnp.min(jnp.where(gs == m, grp, float(N_GROUPS)), axis=0, keepdims=True)
        keep = jnp.where(pick, 1.0, keep)
        gs = jnp.where(pick, NEG_INF, gs)
    work = jnp.where(keep > 0.0, b3, NEG_INF)
    eid = gid * float(GROUP_SIZE) + jid
    chosen = jnp.zeros(shape3, F32)
    for _ in range(TOP_K):
        m = jnp.max(jnp.max(work, axis=1, keepdims=True), axis=0, keepdims=True)
        cand = jnp.where(work == m, eid, float(N_EXPERTS))
        first = jnp.min(jnp.min(cand, axis=1, keepdims=True), axis=0, keepdims=True)
        pick = eid == first
        chosen = jnp.where(pick, 1.0, chosen)
        work = jnp.where(pick, NEG_INF, work)
    w = jnp.where(chosen > 0.0, s3, 0.0)
    total = jnp.sum(jnp.sum(w, axis=1, keepdims=True), axis=0, keepdims=True)
    return (w / total * ROUTED_SCALE).reshape(N_EXPERTS, tt)


def _mixer_post_kernel(yf, yb, bonus, g, of, ob, zz, zga, zgb, x_ref, gt1, sc2, sh2,
                       gnw, gnb, ones_ref, aproj, bng, bproj, wout, n2g, rwt, rbias,
                       x1_out, h2_out, gates_out):
    ones_bd = ones_ref[...]
    y = yf[0] + yb[0]
    mean = _seg_sum(y, ones_bd) * (1.0 / A_HEAD_DIM)
    dlt = y - mean
    var = _seg_sum(dlt * dlt, ones_bd) * (1.0 / A_HEAD_DIM)
    yn = dlt * lax.rsqrt(var + A_GN_EPS) * gnw[...] + gnb[...]
    y_a = _dot((yn + bonus[0]) * g[0], aproj[...])

    o = of[0] + ob[0]
    z = zz[0]
    parts = []
    for h in range(B_HEADS):
        oh = o[:, h * B_HEAD_DIM:(h + 1) * B_HEAD_DIM]
        parts.append(oh * lax.rsqrt(jnp.mean(oh * oh, axis=-1, keepdims=True) + EPS))
    on = jnp.concatenate(parts, axis=1) * bng[...] * _silu(z)
    y_b = _dot(on, bproj[...])

    u = _sigmoid(zga[0]) * y_a + _sigmoid(zgb[0]) * y_b
    x1 = x_ref[0] + gt1[0] * _dot(u, wout[...])
    x1_out[0] = x1
    h2 = _rms_mod(x1, n2g[...], sc2[0], sh2[0])
    h2_out[0] = h2.astype(BF16)
    logits_t = lax.dot_general(rwt[...], h2, (((1,), (1,)), ((), ())), precision=HIGHEST,
                               preferred_element_type=F32)
    scores = _sigmoid(logits_t)
    gates_out[0] = _route(scores, scores + rbias[...])


def _mixer_post(yf, yb, bonus, g, of, ob, z_c, x, gt1, sc2, sh2, gn_w, gn_b, ones_bd, a_proj, bng,
                b_proj, w_out, n2g, rwt, rbias):
    B, T, D = x.shape
    tt = min(256, T)
    full = lambda shape: pl.BlockSpec(shape, lambda b, i: (0,) * len(shape))
    tokw = lambda w, col=0: pl.BlockSpec((1, tt, w), lambda b, i: (b, i, col))
    modrow = pl.BlockSpec((1, 1, D), lambda b, i: (b, 0, 0))
    W = A_WIDTH
    return pl.pallas_call(
        _mixer_post_kernel,
        grid=(B, T // tt),
        in_specs=[tokw(W), tokw(W), tokw(W), tokw(W), tokw(D), tokw(D),
                  tokw(D, 0), tokw(D, 1), tokw(D, 2), tokw(D), modrow, modrow, modrow,
                  full((1, W)), full((1, W)), full((W, W)), full((W, D)), full((1, D)), full((D, D)),
                  full((D, D)), full((1, D)), full((N_EXPERTS, D)), full((N_EXPERTS, 1))],
        out_specs=[tokw(D), tokw(D), pl.BlockSpec((1, N_EXPERTS, tt), lambda b, i: (b, 0, i))],
        out_shape=[jax.ShapeDtypeStruct((B, T, D), F32), jax.ShapeDtypeStruct((B, T, D), BF16),
                   jax.ShapeDtypeStruct((B, N_EXPERTS, T), F32)],
        compiler_params=_params(("parallel", "parallel")),
        name="mixer_post",
    )(yf, yb, bonus, g, of, ob, z_c, z_c, z_c, x, gt1, sc2, sh2, gn_w, gn_b, ones_bd, a_proj, bng,
      b_proj, w_out, n2g, rwt, rbias)


def _moe_kernel(h_ref, gates_ref, x1_ref, gt2, fg, wg, wu, wd, sg, su, sd, o_ref, acc):
    e = pl.program_id(2)
    h = h_ref[0]

    @pl.when(e == 0)
    def _():
        hid = _silu(jnp.dot(h, sg[...], preferred_element_type=F32)) * jnp.dot(
            h, su[...], preferred_element_type=F32)
        acc[...] = _dot(hid, sd[...])

    gates = gates_ref[0]
    lane = lax.broadcasted_iota(jnp.int32, gates.shape, 1)
    gate = jnp.sum(jnp.where(lane == e, gates, 0.0), axis=-1, keepdims=True)
    hid = _silu(jnp.dot(h, wg[0], preferred_element_type=F32)) * jnp.dot(
        h, wu[0], preferred_element_type=F32)
    acc[...] += _dot(hid * gate, wd[0])

    @pl.when(e == pl.num_programs(2) - 1)
    def _():
        x2 = x1_ref[0] + gt2[0] * acc[...]
        o_ref[0] = x2 * lax.rsqrt(jnp.mean(x2 * x2, axis=-1, keepdims=True) + EPS) * fg[...]


def _moe(h2, gates, x1, gt2, fg, wg, wu, wd, sg, su, sd):
    B, T, D = x1.shape
    tm = min(1024, T)
    tok = lambda w: pl.BlockSpec((1, tm, w), lambda b, i, e: (b, i, 0))
    full = lambda shape: pl.BlockSpec(shape, lambda b, i, e: (0,) * len(shape))
    return pl.pallas_call(
        _moe_kernel,
        grid=(B, T // tm, N_EXPERTS),
        in_specs=[tok(D), tok(N_EXPERTS), tok(D), pl.BlockSpec((1, 1, D), lambda b, i, e: (b, 0, 0)),
                  full((1, D)),
                  pl.BlockSpec((1, D, D_EXPERT), lambda b, i, e: (e, 0, 0)),
                  pl.BlockSpec((1, D, D_EXPERT), lambda b, i, e: (e, 0, 0)),
                  pl.BlockSpec((1, D_EXPERT, D), lambda b, i, e: (e, 0, 0)),
                  full((D, D_EXPERT)), full((D, D_EXPERT)), full((D_EXPERT, D))],
        out_specs=tok(D),
        out_shape=jax.ShapeDtypeStruct((B, T, D), F32),
        scratch_shapes=[pltpu.VMEM((tm, D), F32)],
        compiler_params=_params(("parallel", "parallel", "arbitrary")),
        name="moe",
    )(h2, gates, x1, gt2, fg, wg, wu, wd, sg, su, sd)


def _prepare_weights(w_in, shift_mu, a_w0, a_w_up, a_a0, a_a_up, a_g_up, a_k_k, a_k_a, a_r_k, a_gn_w,
                     a_gn_b, a_proj, b_conv_w, b_a_log, b_dt_bias, b_norm_g, b_proj, w_out, router_w,
                     router_bias, exp_gate, exp_up, exp_down, sh_gate, sh_up, sh_down, norm1_g, norm2_g,
                     final_g):
    W = A_WIDTH
    w = w_in[0]
    c0 = A_COLS
    c1 = c0 + 3 * B_WIDTH
    c2 = c1 + B_WIDTH
    c3 = c2 + 4 * B_HEADS
    pad = jnp.zeros((D_MODEL, SMALL_COLS - 4 * B_HEADS), F32)
    w_a = w[:, :c0].astype(BF16)
    w_b = jnp.concatenate([w[:, c0:c1], w[:, c2:c3], pad], axis=1).astype(BF16)
    w_c = jnp.concatenate([w[:, c1:c2], w[:, c3:]], axis=1).astype(BF16)
    zeros = jnp.zeros((2, A_RANK_W, W), F32)
    up_comb = jnp.concatenate([jnp.concatenate([a_w_up[0], zeros], axis=2),
                               jnp.concatenate([zeros, a_a_up[0]], axis=2)], axis=1)
    head = jnp.arange(W) // A_HEAD_DIM
    ones_bd = (head[:, None] == head[None, :]).astype(BF16)
    small = lambda v: jnp.zeros((1, SMALL_COLS), F32).at[0, 2 * B_HEADS:4 * B_HEADS].set(v.reshape(-1))
    return dict(
        w_a=w_a, w_b=w_b, w_c=w_c, mu=shift_mu[0].reshape(1, A_COLS), w0=a_w0[0], a0=a_a0[0],
        up_comb=up_comb, g_up=a_g_up[0], k_k=a_k_k[0].reshape(1, W), k_a=a_k_a[0].reshape(1, W),
        r_k=a_r_k[0].reshape(1, W), ones_bd=ones_bd, gn_w=a_gn_w[0].reshape(1, W),
        gn_b=a_gn_b[0].reshape(1, W), a_proj=a_proj[0].astype(BF16), conv_w=b_conv_w[0],
        alog=small(b_a_log[0]), dtb=small(b_dt_bias[0]),
        bng=jnp.tile(b_norm_g[0], B_HEADS).reshape(1, B_WIDTH), b_proj=b_proj[0].astype(BF16),
        w_out=w_out[0].astype(BF16), rwt=router_w[0].T, rbias=router_bias[0].reshape(N_EXPERTS, 1),
        wg=exp_gate[0].astype(BF16), wu=exp_up[0].astype(BF16), wd=exp_down[0].astype(BF16),
        sg=sh_gate[0].astype(BF16), su=sh_up[0].astype(BF16), sd=sh_down[0].astype(BF16),
        n1g=norm1_g[0].reshape(1, D_MODEL), n2g=norm2_g[0].reshape(1, D_MODEL),
        fg=final_g.reshape(1, D_MODEL))


def _layer(x, mod, p):
    B = x.shape[0]
    sh1, sc1, gt1, sh2, sc2, gt2 = (m.reshape(B, 1, D_MODEL) for m in jnp.split(mod, 6, axis=-1))
    z_a = _inproj(x, p["n1g"], sc1, sh1, p["w_a"], 896)
    z_b = _inproj(x, p["n1g"], sc1, sh1, p["w_b"], 640)
    z_c = _inproj(x, p["n1g"], sc1, sh1, p["w_c"], 1024)
    r, kk, v, g, bonus, lw, kd, bd = _rwkv_prep(z_a, p["mu"], p["w0"], p["a0"], p["up_comb"], p["g_up"],
                                                p["k_k"], p["k_a"], p["r_k"], p["ones_bd"])
    yf, yb = _rwkv_scan(r, kk, v, lw, kd, bd)
    q, k, vv, bg = _gdn_prep(z_b, p["conv_w"], p["alog"], p["dtb"])
    of, ob = _gdn_scan(q, k, vv, bg)
    x1, h2, gates_t = _mixer_post(yf, yb, bonus, g, of, ob, z_c, x, gt1, sc2, sh2, p["gn_w"], p["gn_b"],
                                  p["ones_bd"], p["a_proj"], p["bng"], p["b_proj"], p["w_out"], p["n2g"],
                                  p["rwt"], p["rbias"])
    gates = jnp.swapaxes(gates_t, 1, 2)
    return _moe(h2, gates, x1, gt2, p["fg"], p["wg"], p["wu"], p["wd"], p["sg"], p["su"], p["sd"])


def kernel(x_prompt, x_sample, c_prompt, c_sample, ada_w, ada_b, norm1_g, norm2_g, w_in, shift_mu, a_w0, a_w_up, a_a0, a_a_up, a_g_up, a_k_k, a_k_a, a_r_k, a_gn_w, a_gn_b, a_proj, b_conv_w, b_a_log, b_dt_bias, b_norm_g, b_proj, w_out, router_w, router_bias, exp_gate, exp_up, exp_down, sh_gate, sh_up, sh_down, final_g):
    p = _prepare_weights(w_in, shift_mu, a_w0, a_w_up, a_a0, a_a_up, a_g_up, a_k_k, a_k_a, a_r_k, a_gn_w,
                         a_gn_b, a_proj, b_conv_w, b_a_log, b_dt_bias, b_norm_g, b_proj, w_out, router_w,
                         router_bias, exp_gate, exp_up, exp_down, sh_gate, sh_up, sh_down, norm1_g,
                         norm2_g, final_g)
    nb_p, nb_s = c_prompt.shape[0], c_sample.shape[0]
    rows = -(-(nb_p + nb_s) // SUBLANES) * SUBLANES
    c_all = jnp.concatenate([c_prompt, c_sample, jnp.zeros((rows - nb_p - nb_s, D_MODEL), F32)], axis=0)
    mod = _adaln_mod(c_all, ada_w[0], ada_b[0])
    y_prompt = _layer(x_prompt, mod[:nb_p], p)
    y_sample = _layer(x_sample, mod[nb_p:nb_p + nb_s], p)
    return (y_prompt, y_sample)
```

```python
import functools

import jax
import jax.numpy as jnp
from jax import lax
from jax.experimental import pallas as pl
from jax.experimental.pallas import tpu as pltpu

F32 = jnp.float32
BF16 = jnp.bfloat16
HIGHEST = lax.Precision.HIGHEST

D_MODEL = 1024
A_HEADS = 8
A_HEAD_DIM = 64
A_WIDTH = A_HEADS * A_HEAD_DIM
A_RANK_W = 64
A_RANK_A = 64
A_RANK_G = 128
A_GN_EPS = 64e-5
A_COLS = 3 * A_WIDTH + A_RANK_W + A_RANK_A + A_RANK_G
B_HEADS = 8
B_HEAD_DIM = 128
B_WIDTH = B_HEADS * B_HEAD_DIM
B_CONV = 5
CHUNK = 64
N_EXPERTS = 64
TOP_K = 6
N_GROUPS = 8
TOPK_GROUPS = 4
GROUP_SIZE = N_EXPERTS // N_GROUPS
D_EXPERT = 256
ROUTED_SCALE = 2.5
EPS = 1e-6
LANES = 128
SUBLANES = 8
SMALL_COLS = LANES
VMEM_LIMIT = 56 * 1024 * 1024
NEG_INF = float("-inf")


def _dot(a, b):
    return jnp.dot(a.astype(BF16), b.astype(BF16), preferred_element_type=F32)


def _dot_nt(a, b):
    return lax.dot_general(a.astype(BF16), b.astype(BF16), (((1,), (1,)), ((), ())),
                           preferred_element_type=F32)


def _dot_tn(a, b):
    return lax.dot_general(a.astype(BF16), b.astype(BF16), (((0,), (0,)), ((), ())),
                           preferred_element_type=F32)


def _dot_hi(a, b):
    return jnp.dot(a, b, precision=HIGHEST, preferred_element_type=F32)


def _split(x):
    hi = x.astype(BF16)
    return hi, (x - hi.astype(F32)).astype(BF16)


def _dot3(a, b, dims=(((1,), (0,)), ((), ()))):
    ah, al = _split(a)
    bh, bl = _split(b)
    d = lambda u, v: lax.dot_general(u, v, dims, preferred_element_type=F32)
    return d(ah, bh) + (d(ah, bl) + d(al, bh))


def _seg_sum(x, ones_bd):
    hi = x.astype(BF16)
    lo = (x - hi.astype(F32)).astype(BF16)
    return (jnp.dot(hi, ones_bd, preferred_element_type=F32)
            + jnp.dot(lo, ones_bd, preferred_element_type=F32))


def _softplus(x):
    return jnp.maximum(x, 0.0) + jnp.log1p(jnp.exp(-jnp.abs(x)))


def _sigmoid(x):
    return 1.0 / (1.0 + jnp.exp(-x))


def _silu(x):
    return x * _sigmoid(x)


def _tri_inverse(a_list, eye):
    xs = [(-a).astype(BF16) for a in a_list]
    ts = [eye - a for a in a_list]
    power = 1
    while 2 * power < CHUNK:
        xs = [jnp.dot(x, x, preferred_element_type=F32).astype(BF16) for x in xs]
        ts = [t + jnp.dot(t.astype(BF16), x, preferred_element_type=F32) for t, x in zip(ts, xs)]
        power *= 2
    return ts


def _params(sem):
    return pltpu.CompilerParams(dimension_semantics=sem, vmem_limit_bytes=VMEM_LIMIT)


def _mod_kernel(c_ref, w_ref, b_ref, o_ref):
    c = c_ref[...]
    o_ref[...] = _dot_hi(_silu(c), w_ref[...]) + b_ref[...]


def _adaln_mod(c, ada_w, ada_b):
    rows = c.shape[0]
    n = ada_w.shape[1]
    return pl.pallas_call(
        _mod_kernel,
        grid=(n // D_MODEL,),
        in_specs=[pl.BlockSpec((rows, D_MODEL), lambda j: (0, 0)),
                  pl.BlockSpec((D_MODEL, D_MODEL), lambda j: (0, j)),
                  pl.BlockSpec((1, D_MODEL), lambda j: (0, j))],
        out_specs=pl.BlockSpec((rows, D_MODEL), lambda j: (0, j)),
        out_shape=jax.ShapeDtypeStruct((rows, n), F32),
        compiler_params=_params(("arbitrary",)),
        name="adaln_mod",
    )(c, ada_w, ada_b.reshape(1, n))


def _rms_mod(x, g, sc, sh):
    y = x * lax.rsqrt(jnp.mean(x * x, axis=-1, keepdims=True) + EPS)
    return (y * g) * (1.0 + sc) + sh


def _inproj_kernel(x_ref, g_ref, sc_ref, sh_ref, w_ref, o_ref, h_scr):
    @pl.when(pl.program_id(2) == 0)
    def _():
        h_scr[...] = _rms_mod(x_ref[0], g_ref[...], sc_ref[0], sh_ref[0]).astype(BF16)

    o_ref[0] = jnp.dot(h_scr[...], w_ref[...], preferred_element_type=F32)


def _inproj(x, g, sc, sh, w, tn):
    B, T, D = x.shape
    n = w.shape[1]
    tm = min(512, T)
    return pl.pallas_call(
        _inproj_kernel,
        grid=(B, T // tm, n // tn),
        in_specs=[pl.BlockSpec((1, tm, D), lambda b, i, j: (b, i, 0)),
                  pl.BlockSpec((1, D), lambda b, i, j: (0, 0)),
                  pl.BlockSpec((1, 1, D), lambda b, i, j: (b, 0, 0)),
                  pl.BlockSpec((1, 1, D), lambda b, i, j: (b, 0, 0)),
                  pl.BlockSpec((D, tn), lambda b, i, j: (0, j))],
        out_specs=pl.BlockSpec((1, tm, tn), lambda b, i, j: (b, i, j)),
        out_shape=jax.ShapeDtypeStruct((B, T, n), F32),
        scratch_shapes=[pltpu.VMEM((tm, D), BF16)],
        compiler_params=_params(("parallel", "parallel", "arbitrary")),
        name="in_proj",
    )(x, g, sc, sh, w)


def _fill_halo(ext_ref, cur, prev8, next8, tt):
    i = pl.program_id(1)
    last = pl.num_programs(1) - 1
    ext_ref[pl.ds(0, SUBLANES), :] = jnp.where(i == 0, 0.0, prev8)
    ext_ref[pl.ds(SUBLANES, tt), :] = cur
    ext_ref[pl.ds(SUBLANES + tt, SUBLANES), :] = jnp.where(i == last, 0.0, next8)


def _halo_specs(tt, width, col_block, seq_len):
    nb = tt // SUBLANES
    last = seq_len // SUBLANES - 1
    return [
        pl.BlockSpec((1, tt, width), lambda b, i: (b, i, col_block)),
        pl.BlockSpec((1, SUBLANES, width), lambda b, i: (b, jnp.maximum(i * nb - 1, 0), col_block)),
        pl.BlockSpec((1, SUBLANES, width), lambda b, i: (b, jnp.minimum((i + 1) * nb, last), col_block)),
    ]


def _rwkv_prep_kernel(z_ref, zp_ref, zn_ref, mu_ref, w0_ref, a0_ref, up_ref, gup_ref, kk_ref, ka_ref,
                      rk_ref, ones_ref,
                      r_out, kk_out, v_out, g_out, bonus_out, lw_out, kd_out, bd_out, ext_scr):
    tt = z_ref.shape[1]
    z = z_ref[0]
    _fill_halo(ext_scr, z, zp_ref[0], zn_ref[0], tt)
    z_prev = ext_scr[pl.ds(SUBLANES - 1, tt), :]
    z_next = ext_scr[pl.ds(SUBLANES + 1, tt), :]
    zs = z + (0.5 * (z_prev + z_next) - z) * mu_ref[...]
    W = A_WIDTH
    zr, zk, zv = zs[:, 0:W], zs[:, W:2 * W], zs[:, 2 * W:3 * W]
    zwa = zs[:, 3 * W:3 * W + LANES]
    zg = zs[:, 3 * W + LANES:3 * W + 2 * LANES]
    ones_bd = ones_ref[...]
    kk_raw = zk * kk_ref[...]
    kk = kk_raw * lax.rsqrt(_seg_sum(kk_raw * kk_raw, ones_bd) + 1e-6)
    lane = lax.broadcasted_iota(jnp.int32, zwa.shape, 1)
    lhs = jnp.where(lane < A_RANK_W, jnp.tanh(zwa), zwa)
    k_sum = jnp.zeros_like(zk)
    for d in range(2):
        up = _dot_hi(lhs, up_ref[d])
        wl = w0_ref[d:d + 1, :] + up[:, 0:W]
        w_log = -_softplus(-wl) - 0.5
        lw_out[d, 0] = -jnp.exp(w_log)
        a = _sigmoid(a0_ref[d:d + 1, :] + up[:, W:2 * W])
        k_d = zk * (1.0 + (a - 1.0) * ka_ref[...])
        kd_out[d, 0] = k_d
        bd_out[d, 0] = kk * a
        k_sum = k_sum + k_d
    r_out[0] = zr
    kk_out[0] = kk
    v_out[0] = zv
    g_out[0] = _dot_hi(_sigmoid(zg), gup_ref[...])
    bonus_out[0] = _seg_sum(zr * k_sum * rk_ref[...], ones_bd) * zv


def _rwkv_prep(z_a, mu, w0, a0, up_comb, g_up, k_k, k_a, r_k, ones_bd):
    B, T, _ = z_a.shape
    tt = min(256, T)
    W = A_WIDTH
    full = lambda shape: pl.BlockSpec(shape, lambda b, i: (0,) * len(shape))
    tok = pl.BlockSpec((1, tt, W), lambda b, i: (b, i, 0))
    tok2 = pl.BlockSpec((2, 1, tt, W), lambda b, i: (0, b, i, 0))
    s1 = jax.ShapeDtypeStruct((B, T, W), F32)
    s2 = jax.ShapeDtypeStruct((2, B, T, W), F32)
    return pl.pallas_call(
        _rwkv_prep_kernel,
        grid=(B, T // tt),
        in_specs=_halo_specs(tt, A_COLS, 0, T) + [
            full((1, A_COLS)), full((2, W)), full((2, W)), full((2, LANES, 2 * W)),
            full((A_RANK_G, W)), full((1, W)), full((1, W)), full((1, W)), full((W, W))],
        out_specs=[tok, tok, tok, tok, tok, tok2, tok2, tok2],
        out_shape=[s1, s1, s1, s1, s1, s2, s2, s2],
        scratch_shapes=[pltpu.VMEM((tt + 2 * SUBLANES, A_COLS), F32)],
        compiler_params=_params(("parallel", "parallel")),
        name="rwkv_prep",
    )(z_a, z_a, z_a, mu, w0, a0, up_comb, g_up, k_k, k_a, r_k, ones_bd)


def _chunk_masks(reverse):
    n = 2 * CHUNK
    row = lax.broadcasted_iota(jnp.int32, (n, n), 0)
    col = lax.broadcasted_iota(jnp.int32, (n, n), 1)
    same = (row // CHUNK) == (col // CHUNK)
    ti, tj = row % CHUNK, col % CHUNK
    if reverse:
        return same & (ti < tj), same & (ti <= tj), row == col
    return same & (ti > tj), same & (ti >= tj), row == col


def _cumsum_matrix(reverse):
    row = lax.broadcasted_iota(jnp.int32, (CHUNK, CHUNK), 0)
    col = lax.broadcasted_iota(jnp.int32, (CHUNK, CHUNK), 1)
    return ((row <= col) if reverse else (row >= col)).astype(F32)


def _rwkv_chunk_operands(r, kk, v, k, b, cs, lw, reverse):
    C = CHUNK
    cs_end = cs[0:1, :] if reverse else cs[C - 1:C, :]
    m0 = lax.broadcasted_iota(jnp.int32, (C, LANES), 1) < A_HEAD_DIM

    def stack(x):
        return jnp.concatenate([jnp.where(m0, x, 0.0), jnp.where(m0, 0.0, x)], axis=0)

    g_inv = jnp.exp(-cs)
    g_tail = jnp.exp(cs_end - cs)
    strict, incl, _ = _chunk_masks(reverse)
    return dict(rg=stack(r * jnp.exp(cs)), kkg=stack(kk * jnp.exp(cs - lw)), ki=stack(k * g_inv),
                bi=stack(b * g_inv), kt=stack(k * g_tail), bt=stack(b * g_tail), vs=stack(v),
                g_end=jnp.exp(cs_end), strict=strict, incl=incl)


def _rwkv_chunks(ops, hts):
    C = CHUNK
    n = 2 * C
    eye = _chunk_masks(False)[2].astype(F32)
    ps = [_dot_nt(jnp.concatenate([o["kkg"], o["rg"]], axis=0), jnp.concatenate([o["bi"], o["ki"]], axis=0))
          for o in ops]
    a_ab = [jnp.where(o["strict"], p[0:n, 0:n], 0.0) for o, p in zip(ops, ps)]
    a_ak = [jnp.where(o["strict"], p[0:n, n:2 * n], 0.0).astype(BF16) for o, p in zip(ops, ps)]
    a_rb = [jnp.where(o["incl"], p[n:2 * n, 0:n], 0.0).astype(BF16) for o, p in zip(ops, ps)]
    a_rk = [jnp.where(o["incl"], p[n:2 * n, n:2 * n], 0.0).astype(BF16) for o, p in zip(ops, ps)]
    vsb = [o["vs"].astype(BF16) for o in ops]
    akv = [jnp.dot(a, v, preferred_element_type=F32) for a, v in zip(a_ak, vsb)]
    arkv = [jnp.dot(a, v, preferred_element_type=F32) for a, v in zip(a_rk, vsb)]
    tinv = _tri_inverse(a_ab, eye)
    wu = [_dot(t, jnp.concatenate([o["kkg"], x], axis=1)).astype(BF16)
          for t, o, x in zip(tinv, ops, akv)]
    r2 = [jnp.dot(a, w, preferred_element_type=F32) for a, w in zip(a_rb, wu)]
    btb = [o["bt"].astype(BF16) for o in ops]
    btw = [_dot_tn(b, w[:, 0:LANES]) for b, w in zip(btb, wu)]
    hloc = [_dot_tn(v, o["kt"]) - _dot_tn(w[:, LANES:2 * LANES], b)
            for v, o, w, b in zip(vsb, ops, wu, btb)]
    ys, hts_new = [], []
    for o, r, yl, ht, bw, hl in zip(ops, r2, arkv, hts, btw, hloc):
        q_s = o["rg"] - r[:, 0:LANES]
        yloc_s = yl - r[:, LANES:2 * LANES]
        ys.append(yloc_s[0:C] + yloc_s[C:n] + _dot_nt(q_s[0:C] + q_s[C:n], ht))
        hts_new.append(ht * o["g_end"] - _dot_nt(ht, bw) + hl)
    return ys, hts_new


def _rwkv_scan_kernel(rf, kkf, vf, lwf, kf, bf, rb, kkb, vb, lwb, kb, bb, yf_out, yb_out, h_scr):
    @pl.when(pl.program_id(1) == 0)
    def _():
        h_scr[...] = jnp.zeros_like(h_scr)

    n_pairs = A_WIDTH // LANES
    sls = [slice(i * LANES, (i + 1) * LANES) for i in range(n_pairs)]
    ops, hts = [], []
    for d, (r_ref, kk_ref, v_ref, lw_ref, k_ref, b_ref) in enumerate(
            ((rf, kkf, vf, lwf, kf, bf), (rb, kkb, vb, lwb, kb, bb))):
        reverse = d == 1
        lw_all = lw_ref[0, 0]
        cs_all = _dot_hi(_cumsum_matrix(reverse), lw_all)
        ops += [_rwkv_chunk_operands(r_ref[0, :, sl], kk_ref[0, :, sl], v_ref[0, :, sl], k_ref[0, 0, :, sl],
                                     b_ref[0, 0, :, sl], cs_all[:, sl], lw_all[:, sl], reverse) for sl in sls]
        hts += [h_scr[d, i] for i in range(n_pairs)]
    ys, hts = _rwkv_chunks(ops, hts)
    for d, y_out in enumerate((yf_out, yb_out)):
        for i, sl in enumerate(sls):
            h_scr[d, i] = hts[d * n_pairs + i]
            y_out[0, :, sl] = ys[d * n_pairs + i]


def _rwkv_scan(r, kk, v, lw, kd, bd):
    B, T, W = r.shape
    N = T // CHUNK
    fwd = pl.BlockSpec((1, CHUNK, W), lambda b, c: (b, c, 0))
    bwd = pl.BlockSpec((1, CHUNK, W), lambda b, c: (b, N - 1 - c, 0))
    fwd2 = pl.BlockSpec((1, 1, CHUNK, W), lambda b, c: (0, b, c, 0))
    bwd2 = pl.BlockSpec((1, 1, CHUNK, W), lambda b, c: (1, b, N - 1 - c, 0))
    out = jax.ShapeDtypeStruct((B, T, W), F32)
    return pl.pallas_call(
        _rwkv_scan_kernel,
        grid=(B, N),
        in_specs=[fwd, fwd, fwd, fwd2, fwd2, fwd2, bwd, bwd, bwd, bwd2, bwd2, bwd2],
        out_specs=[fwd, bwd],
        out_shape=[out, out],
        scratch_shapes=[pltpu.VMEM((2, W // LANES, LANES, LANES), F32)],
        compiler_params=_params(("parallel", "arbitrary")),
        name="rwkv_scan",
    )(r, kk, v, lw, kd, bd, r, kk, v, lw, kd, bd)


def _gdn_prep_kernel(z_ref, zp_ref, zn_ref, small_ref, cw_ref, alog_ref, dtb_ref,
                     q_out, k_out, v_out, bg_out, ext_scr):
    tt = z_ref.shape[1]
    _fill_halo(ext_scr, z_ref[0], zp_ref[0], zn_ref[0], tt)
    half = B_CONV // 2
    acc = None
    for j in range(B_CONV):
        term = ext_scr[pl.ds(SUBLANES - half + j, tt), :] * cw_ref[j:j + 1, :]
        acc = term if acc is None else acc + term
    qkv = _silu(acc)
    for h in range(B_HEADS):
        for idx, out in enumerate((q_out, k_out)):
            x = qkv[:, idx * B_WIDTH + h * B_HEAD_DIM: idx * B_WIDTH + (h + 1) * B_HEAD_DIM]
            xn = x * lax.rsqrt(jnp.sum(x * x, axis=-1, keepdims=True) + 1e-6)
            if idx == 0:
                xn = xn * (B_HEAD_DIM ** -0.5)
            out[0, :, h * B_HEAD_DIM:(h + 1) * B_HEAD_DIM] = xn
    v_out[0] = qkv[:, 2 * B_WIDTH:3 * B_WIDTH]
    s = small_ref[0]
    lane = lax.broadcasted_iota(jnp.int32, s.shape, 1)
    beta = _sigmoid(s)
    g = -jnp.exp(alog_ref[...]) * _softplus(s + dtb_ref[...])
    bg_out[0] = jnp.where(lane < 2 * B_HEADS, beta, g)


def _gdn_prep(z_b, conv_w, alog_vec, dtb_vec):
    B, T, _ = z_b.shape
    tt = min(256, T)
    Wq = 3 * B_WIDTH
    full = lambda shape: pl.BlockSpec(shape, lambda b, i: (0,) * len(shape))
    tok = pl.BlockSpec((1, tt, B_WIDTH), lambda b, i: (b, i, 0))
    small = pl.BlockSpec((1, tt, SMALL_COLS), lambda b, i: (b, i, Wq // SMALL_COLS))
    s1 = jax.ShapeDtypeStruct((B, T, B_WIDTH), F32)
    return pl.pallas_call(
        _gdn_prep_kernel,
        grid=(B, T // tt),
        in_specs=_halo_specs(tt, Wq, 0, T) + [small, full((B_CONV, Wq)), full((1, SMALL_COLS)),
                                            full((1, SMALL_COLS))],
        out_specs=[tok, tok, tok, pl.BlockSpec((1, tt, SMALL_COLS), lambda b, i: (b, i, 0))],
        out_shape=[s1, s1, s1, jax.ShapeDtypeStruct((B, T, SMALL_COLS), F32)],
        scratch_shapes=[pltpu.VMEM((tt + 2 * SUBLANES, Wq), F32)],
        compiler_params=_params(("parallel", "parallel")),
        name="gdn_prep",
    )(z_b, z_b, z_b, z_b, conv_w, alog_vec, dtb_vec)


def _gdn_chunk_operands(q0, q1, k0, k1, v0, v1, gc2, gct2, bg2, j0, j1, reverse):
    C = CHUNK
    n = 2 * C
    strict, incl, _ = _chunk_masks(reverse)
    top = lax.broadcasted_iota(jnp.int32, (n, 1), 0) < C
    left = lax.broadcasted_iota(jnp.int32, (1, n), 1) < C
    g0, g1 = j0 + 2 * B_HEADS, j1 + 2 * B_HEADS
    gcol = jnp.where(top, gc2[:, g0:g0 + 1], gc2[:, g1:g1 + 1])
    grow = jnp.where(left, gct2[g0:g0 + 1, :], gct2[g1:g1 + 1, :])
    beta = jnp.where(top, bg2[:, j0:j0 + 1], bg2[:, j1:j1 + 1])
    e = 0 if reverse else C - 1
    glast = jnp.where(top, gc2[e:e + 1, g0:g0 + 1], gc2[e:e + 1, g1:g1 + 1])
    ks = jnp.concatenate([k0, k1], axis=0)
    qs = jnp.concatenate([q0, q1], axis=0)
    vs = jnp.concatenate([v0, v1], axis=0)
    egc = jnp.exp(gcol)
    kb = ks * beta
    return dict(gam=jnp.exp(jnp.where(incl, gcol - grow, NEG_INF)), strict=strict, ks=ks,
                kbq=jnp.concatenate([kb, qs], axis=0),
                rhs=jnp.concatenate([vs * beta, kb * egc], axis=1),
                qh=(qs * egc).astype(BF16), ktail=(ks * jnp.exp(glast - gcol)).astype(BF16),
                decay=jnp.exp(glast))


def _gdn_chunks(ops, states):
    C = CHUNK
    n = 2 * C
    eye = _chunk_masks(False)[2].astype(F32)
    ps = [_dot3(o["kbq"], o["ks"], (((1,), (1,)), ((), ()))) for o in ops]
    lower = [jnp.where(o["strict"], p[0:n] * o["gam"], 0.0) for o, p in zip(ops, ps)]
    aqk = [(p[n:2 * n] * o["gam"]).astype(BF16) for o, p in zip(ops, ps)]
    tinv = [t.astype(BF16) for t in _tri_inverse(lower, eye)]
    uw = [jnp.dot(t, o["rhs"].astype(BF16), preferred_element_type=F32) for t, o in zip(tinv, ops)]
    resid = [o["rhs"] - x - _dot3(l, x) for o, l, x in zip(ops, lower, uw)]
    uw = [x + jnp.dot(t, r.astype(BF16), preferred_element_type=F32) for x, t, r in zip(uw, tinv, resid)]
    sb = [(s0.astype(BF16), s1.astype(BF16)) for s0, s1 in states]
    ws = [jnp.concatenate([jnp.dot(u[0:C, LANES:].astype(BF16), s0, preferred_element_type=F32),
                           jnp.dot(u[C:n, LANES:].astype(BF16), s1, preferred_element_type=F32)], axis=0)
          for u, (s0, s1) in zip(uw, sb)]
    qss = [jnp.concatenate([jnp.dot(o["qh"][0:C], s0, preferred_element_type=F32),
                            jnp.dot(o["qh"][C:n], s1, preferred_element_type=F32)], axis=0)
           for o, (s0, s1) in zip(ops, sb)]
    v_new = [(u[:, 0:LANES] - w).astype(BF16) for u, w in zip(uw, ws)]
    outs = [q + jnp.dot(a, v, preferred_element_type=F32) for q, a, v in zip(qss, aqk, v_new)]
    new_states = []
    for o, v, (s0, s1) in zip(ops, v_new, states):
        new_states.append((s0 * o["decay"][0:1, :] + _dot_tn(o["ktail"][0:C], v[0:C]),
                           s1 * o["decay"][C:C + 1, :] + _dot_tn(o["ktail"][C:n], v[C:n])))
    return [(x[0:C], x[C:n]) for x in outs], new_states


def _gdn_scan_kernel(qf, kf, vf, bgf, qb, kb, vb, bgb, of_out, ob_out, s_scr):
    @pl.when(pl.program_id(1) == 0)
    def _():
        s_scr[...] = jnp.zeros_like(s_scr)

    D = B_HEAD_DIM
    n_pairs = B_HEADS // 2
    sl = lambda h: slice(h * D, (h + 1) * D)
    ops, states = [], []
    for d, (q_ref, k_ref, v_ref, bg_ref) in enumerate(((qf, kf, vf, bgf), (qb, kb, vb, bgb))):
        reverse = d == 1
        bg = bg_ref[0]
        gc = _dot_hi(_cumsum_matrix(reverse), bg)
        bg2 = jnp.concatenate([bg, bg], axis=0)
        gc2 = jnp.concatenate([gc, gc], axis=0)
        gct2 = gc2.T
        for i in range(n_pairs):
            h0, h1 = 2 * i, 2 * i + 1
            ops.append(_gdn_chunk_operands(
                q_ref[0, :, sl(h0)], q_ref[0, :, sl(h1)], k_ref[0, :, sl(h0)], k_ref[0, :, sl(h1)],
                v_ref[0, :, sl(h0)], v_ref[0, :, sl(h1)], gc2, gct2, bg2,
                d * B_HEADS + h0, d * B_HEADS + h1, reverse))
            states.append((s_scr[d, h0], s_scr[d, h1]))
    outs, states = _gdn_chunks(ops, states)
    for d, o_out in enumerate((of_out, ob_out)):
        for i in range(n_pairs):
            h0, h1 = 2 * i, 2 * i + 1
            (o0, o1), (s0, s1) = outs[d * n_pairs + i], states[d * n_pairs + i]
            s_scr[d, h0] = s0
            s_scr[d, h1] = s1
            o_out[0, :, sl(h0)] = o0
            o_out[0, :, sl(h1)] = o1


def _gdn_scan(q, k, v, bg):
    B, T, W = q.shape
    N = T // CHUNK
    fwd = pl.BlockSpec((1, CHUNK, W), lambda b, c: (b, c, 0))
    bwd = pl.BlockSpec((1, CHUNK, W), lambda b, c: (b, N - 1 - c, 0))
    sfwd = pl.BlockSpec((1, CHUNK, SMALL_COLS), lambda b, c: (b, c, 0))
    sbwd = pl.BlockSpec((1, CHUNK, SMALL_COLS), lambda b, c: (b, N - 1 - c, 0))
    out = jax.ShapeDtypeStruct((B, T, W), F32)
    return pl.pallas_call(
        _gdn_scan_kernel,
        grid=(B, N),
        in_specs=[fwd, fwd, fwd, sfwd, bwd, bwd, bwd, sbwd],
        out_specs=[fwd, bwd],
        out_shape=[out, out],
        scratch_shapes=[pltpu.VMEM((2, B_HEADS, B_HEAD_DIM, B_HEAD_DIM), F32)],
        compiler_params=_params(("parallel", "arbitrary")),
        name="gdn_scan",
    )(q, k, v, bg, q, k, v, bg)


def _route(scores, biased):
    tt = scores.shape[-1]
    shape3 = (N_GROUPS, GROUP_SIZE, tt)
    s3 = scores.reshape(shape3)
    b3 = biased.reshape(shape3)
    jid = lax.broadcasted_iota(jnp.int32, shape3, 1).astype(F32)
    gid = lax.broadcasted_iota(jnp.int32, shape3, 0).astype(F32)
    m1 = jnp.max(b3, axis=1, keepdims=True)
    first = jnp.min(jnp.where(b3 == m1, jid, float(GROUP_SIZE)), axis=1, keepdims=True)
    m2 = jnp.max(jnp.where(jid == first, NEG_INF, b3), axis=1, keepdims=True)
    gs = m1 + m2
    grp = lax.broadcasted_iota(jnp.int32, (N_GROUPS, 1, tt), 0).astype(F32)
    keep = jnp.zeros((N_GROUPS, 1, tt), F32)
    for _ in range(TOPK_GROUPS):
        m = jnp.max(gs, axis=0, keepdims=True)
        pick = grp == jnp.min(jnp.where(gs == m, grp, float(N_GROUPS)), axis=0, keepdims=True)
        keep = jnp.where(pick, 1.0, keep)
        gs = jnp.where(pick, NEG_INF, gs)
    work = jnp.where(keep > 0.0, b3, NEG_INF)
    eid = gid * float(GROUP_SIZE) + jid
    chosen = jnp.zeros(shape3, F32)
    for _ in range(TOP_K):
        m = jnp.max(jnp.max(work, axis=1, keepdims=True), axis=0, keepdims=True)
        cand = jnp.where(work == m, eid, float(N_EXPERTS))
        first = jnp.min(jnp.min(cand, axis=1, keepdims=True), axis=0, keepdims=True)
        pick = eid == first
        chosen = jnp.where(pick, 1.0, chosen)
        work = jnp.where(pick, NEG_INF, work)
    w = jnp.where(chosen > 0.0, s3, 0.0)
    total = jnp.sum(jnp.sum(w, axis=1, keepdims=True), axis=0, keepdims=True)
    return (w / total * ROUTED_SCALE).reshape(N_EXPERTS, tt)


def _mixer_post_kernel(yf, yb, bonus, g, of, ob, zz, zga, zgb, x_ref, gt1, sc2, sh2,
                       gnw, gnb, ones_ref, aproj, bng, bproj, wout, n2g, rwt, rbias,
                       x1_out, h2_out, gates_out):
    ones_bd = ones_ref[...]
    y = yf[0] + yb[0]
    mean = _seg_sum(y, ones_bd) * (1.0 / A_HEAD_DIM)
    dlt = y - mean
    var = _seg_sum(dlt * dlt, ones_bd) * (1.0 / A_HEAD_DIM)
    yn = dlt * lax.rsqrt(var + A_GN_EPS) * gnw[...] + gnb[...]
    y_a = _dot((yn + bonus[0]) * g[0], aproj[...])

    o = of[0] + ob[0]
    z = zz[0]
    parts = []
    for h in range(B_HEADS):
        oh = o[:, h * B_HEAD_DIM:(h + 1) * B_HEAD_DIM]
        parts.append(oh * lax.rsqrt(jnp.mean(oh * oh, axis=-1, keepdims=True) + EPS))
    on = jnp.concatenate(parts, axis=1) * bng[...] * _silu(z)
    y_b = _dot(on, bproj[...])

    u = _sigmoid(zga[0]) * y_a + _sigmoid(zgb[0]) * y_b
    x1 = x_ref[0] + gt1[0] * _dot(u, wout[...])
    x1_out[0] = x1
    h2 = _rms_mod(x1, n2g[...], sc2[0], sh2[0])
    h2_out[0] = h2.astype(BF16)
    logits_t = _dot_nt(rwt[...], h2)
    scores = _sigmoid(logits_t)
    gates_out[0] = _route(scores, scores + rbias[...])


def _mixer_post(yf, yb, bonus, g, of, ob, z_c, x, gt1, sc2, sh2, gn_w, gn_b, ones_bd, a_proj, bng,
                b_proj, w_out, n2g, rwt, rbias):
    B, T, D = x.shape
    tt = min(256, T)
    full = lambda shape: pl.BlockSpec(shape, lambda b, i: (0,) * len(shape))
    tokw = lambda w, col=0: pl.BlockSpec((1, tt, w), lambda b, i: (b, i, col))
    modrow = pl.BlockSpec((1, 1, D), lambda b, i: (b, 0, 0))
    W = A_WIDTH
    return pl.pallas_call(
        _mixer_post_kernel,
        grid=(B, T // tt),
        in_specs=[tokw(W), tokw(W), tokw(W), tokw(W), tokw(D), tokw(D),
                  tokw(D, 0), tokw(D, 1), tokw(D, 2), tokw(D), modrow, modrow, modrow,
                  full((1, W)), full((1, W)), full((W, W)), full((W, D)), full((1, D)), full((D, D)),
                  full((D, D)), full((1, D)), full((N_EXPERTS, D)), full((N_EXPERTS, 1))],
        out_specs=[tokw(D), tokw(D), pl.BlockSpec((1, N_EXPERTS, tt), lambda b, i: (b, 0, i))],
        out_shape=[jax.ShapeDtypeStruct((B, T, D), F32), jax.ShapeDtypeStruct((B, T, D), BF16),
                   jax.ShapeDtypeStruct((B, N_EXPERTS, T), F32)],
        compiler_params=_params(("parallel", "parallel")),
        name="mixer_post",
    )(yf, yb, bonus, g, of, ob, z_c, z_c, z_c, x, gt1, sc2, sh2, gn_w, gn_b, ones_bd, a_proj, bng,
      b_proj, w_out, n2g, rwt, rbias)


def _moe_kernel(h_ref, gates_ref, x1_ref, gt2, fg, wg, wu, wd, sg, su, sd, o_ref, acc):
    e = pl.program_id(2)
    h = h_ref[0]

    @pl.when(e == 0)
    def _():
        hid = _silu(jnp.dot(h, sg[...], preferred_element_type=F32)) * jnp.dot(
            h, su[...], preferred_element_type=F32)
        acc[...] = _dot(hid, sd[...])

    gates = gates_ref[0]
    lane = lax.broadcasted_iota(jnp.int32, gates.shape, 1)
    gate = jnp.sum(jnp.where(lane == e, gates, 0.0), axis=-1, keepdims=True)
    hid = _silu(jnp.dot(h, wg[0], preferred_element_type=F32)) * jnp.dot(
        h, wu[0], preferred_element_type=F32)
    acc[...] += _dot(hid * gate, wd[0])

    @pl.when(e == pl.num_programs(2) - 1)
    def _():
        x2 = x1_ref[0] + gt2[0] * acc[...]
        o_ref[0] = x2 * lax.rsqrt(jnp.mean(x2 * x2, axis=-1, keepdims=True) + EPS) * fg[...]


def _moe(h2, gates, x1, gt2, fg, wg, wu, wd, sg, su, sd):
    B, T, D = x1.shape
    tm = min(1024, T)
    tok = lambda w: pl.BlockSpec((1, tm, w), lambda b, i, e: (b, i, 0))
    full = lambda shape: pl.BlockSpec(shape, lambda b, i, e: (0,) * len(shape))
    return pl.pallas_call(
        _moe_kernel,
        grid=(B, T // tm, N_EXPERTS),
        in_specs=[tok(D), tok(N_EXPERTS), tok(D), pl.BlockSpec((1, 1, D), lambda b, i, e: (b, 0, 0)),
                  full((1, D)),
                  pl.BlockSpec((1, D, D_EXPERT), lambda b, i, e: (e, 0, 0)),
                  pl.BlockSpec((1, D, D_EXPERT), lambda b, i, e: (e, 0, 0)),
                  pl.BlockSpec((1, D_EXPERT, D), lambda b, i, e: (e, 0, 0)),
                  full((D, D_EXPERT)), full((D, D_EXPERT)), full((D_EXPERT, D))],
        out_specs=tok(D),
        out_shape=jax.ShapeDtypeStruct((B, T, D), F32),
        scratch_shapes=[pltpu.VMEM((tm, D), F32)],
        compiler_params=_params(("parallel", "parallel", "arbitrary")),
        name="moe",
    )(h2, gates, x1, gt2, fg, wg, wu, wd, sg, su, sd)


def _prepare_weights(w_in, shift_mu, a_w0, a_w_up, a_a0, a_a_up, a_g_up, a_k_k, a_k_a, a_r_k, a_gn_w,
                     a_gn_b, a_proj, b_conv_w, b_a_log, b_dt_bias, b_norm_g, b_proj, w_out, router_w,
                     router_bias, exp_gate, exp_up, exp_down, sh_gate, sh_up, sh_down, norm1_g, norm2_g,
                     final_g):
    W = A_WIDTH
    w = w_in[0]
    c0 = A_COLS
    c1 = c0 + 3 * B_WIDTH
    c2 = c1 + B_WIDTH
    c3 = c2 + 4 * B_HEADS
    pad = jnp.zeros((D_MODEL, SMALL_COLS - 4 * B_HEADS), F32)
    w_a = w[:, :c0].astype(BF16)
    w_b = jnp.concatenate([w[:, c0:c1], w[:, c2:c3], pad], axis=1).astype(BF16)
    w_c = jnp.concatenate([w[:, c1:c2], w[:, c3:]], axis=1).astype(BF16)
    zeros = jnp.zeros((2, A_RANK_W, W), F32)
    up_comb = jnp.concatenate([jnp.concatenate([a_w_up[0], zeros], axis=2),
                               jnp.concatenate([zeros, a_a_up[0]], axis=2)], axis=1)
    head = jnp.arange(W) // A_HEAD_DIM
    ones_bd = (head[:, None] == head[None, :]).astype(BF16)
    small = lambda v: jnp.zeros((1, SMALL_COLS), F32).at[0, 2 * B_HEADS:4 * B_HEADS].set(v.reshape(-1))
    return dict(
        w_a=w_a, w_b=w_b, w_c=w_c, mu=shift_mu[0].reshape(1, A_COLS), w0=a_w0[0], a0=a_a0[0],
        up_comb=up_comb, g_up=a_g_up[0], k_k=a_k_k[0].reshape(1, W), k_a=a_k_a[0].reshape(1, W),
        r_k=a_r_k[0].reshape(1, W), ones_bd=ones_bd, gn_w=a_gn_w[0].reshape(1, W),
        gn_b=a_gn_b[0].reshape(1, W), a_proj=a_proj[0].astype(BF16), conv_w=b_conv_w[0],
        alog=small(b_a_log[0]), dtb=small(b_dt_bias[0]),
        bng=jnp.tile(b_norm_g[0], B_HEADS).reshape(1, B_WIDTH), b_proj=b_proj[0].astype(BF16),
        w_out=w_out[0].astype(BF16), rwt=router_w[0].T, rbias=router_bias[0].reshape(N_EXPERTS, 1),
        wg=exp_gate[0].astype(BF16), wu=exp_up[0].astype(BF16), wd=exp_down[0].astype(BF16),
        sg=sh_gate[0].astype(BF16), su=sh_up[0].astype(BF16), sd=sh_down[0].astype(BF16),
        n1g=norm1_g[0].reshape(1, D_MODEL), n2g=norm2_g[0].reshape(1, D_MODEL),
        fg=final_g.reshape(1, D_MODEL))


def _layer(x, mod, p):
    B = x.shape[0]
    sh1, sc1, gt1, sh2, sc2, gt2 = (m.reshape(B, 1, D_MODEL) for m in jnp.split(mod, 6, axis=-1))
    z_a = _inproj(x, p["n1g"], sc1, sh1, p["w_a"], 896)
    z_b = _inproj(x, p["n1g"], sc1, sh1, p["w_b"], 640)
    z_c = _inproj(x, p["n1g"], sc1, sh1, p["w_c"], 1024)
    r, kk, v, g, bonus, lw, kd, bd = _rwkv_prep(z_a, p["mu"], p["w0"], p["a0"], p["up_comb"], p["g_up"],
                                                p["k_k"], p["k_a"], p["r_k"], p["ones_bd"])
    yf, yb = _rwkv_scan(r, kk, v, lw, kd, bd)
    q, k, vv, bg = _gdn_prep(z_b, p["conv_w"], p["alog"], p["dtb"])
    of, ob = _gdn_scan(q, k, vv, bg)
    x1, h2, gates_t = _mixer_post(yf, yb, bonus, g, of, ob, z_c, x, gt1, sc2, sh2, p["gn_w"], p["gn_b"],
                                  p["ones_bd"], p["a_proj"], p["bng"], p["b_proj"], p["w_out"], p["n2g"],
                                  p["rwt"], p["rbias"])
    gates = jnp.swapaxes(gates_t, 1, 2)
    return _moe(h2, gates, x1, gt2, p["fg"], p["wg"], p["wu"], p["wd"], p["sg"], p["su"], p["sd"])


def kernel(x_prompt, x_sample, c_prompt, c_sample, ada_w, ada_b, norm1_g, norm2_g, w_in, shift_mu, a_w0, a_w_up, a_a0, a_a_up, a_g_up, a_k_k, a_k_a, a_r_k, a_gn_w, a_gn_b, a_proj, b_conv_w, b_a_log, b_dt_bias, b_norm_g, b_proj, w_out, router_w, router_bias, exp_gate, exp_up, exp_down, sh_gate, sh_up, sh_down, final_g):
    p = _prepare_weights(w_in, shift_mu, a_w0, a_w_up, a_a0, a_a_up, a_g_up, a_k_k, a_k_a, a_r_k, a_gn_w,
                         a_gn_b, a_proj, b_conv_w, b_a_log, b_dt_bias, b_norm_g, b_proj, w_out, router_w,
                         router_bias, exp_gate, exp_up, exp_down, sh_gate, sh_up, sh_down, norm1_g,
                         norm2_g, final_g)
    nb_p, nb_s = c_prompt.shape[0], c_sample.shape[0]
    rows = -(-(nb_p + nb_s) // SUBLANES) * SUBLANES
    c_all = jnp.concatenate([c_prompt, c_sample, jnp.zeros((rows - nb_p - nb_s, D_MODEL), F32)], axis=0)
    mod = _adaln_mod(c_all, ada_w[0], ada_b[0])
    y_prompt = _layer(x_prompt, mod[:nb_p], p)
    y_sample = _layer(x_sample, mod[nb_p:nb_p + nb_s], p)
    return (y_prompt, y_sample)
```

```python
import functools

import jax
import jax.numpy as jnp
from jax import lax
from jax.experimental import pallas as pl
from jax.experimental.pallas import tpu as pltpu
from jax.experimental.pallas import tpu_sc as plsc

F32 = jnp.float32
BF16 = jnp.bfloat16
HIGHEST = lax.Precision.HIGHEST

D_MODEL = 1024
A_HEADS = 8
A_HEAD_DIM = 64
A_WIDTH = A_HEADS * A_HEAD_DIM
A_RANK_W = 64
A_RANK_A = 64
A_RANK_G = 128
A_GN_EPS = 64e-5
A_COLS = 3 * A_WIDTH + A_RANK_W + A_RANK_A + A_RANK_G
B_HEADS = 8
B_HEAD_DIM = 128
B_WIDTH = B_HEADS * B_HEAD_DIM
B_CONV = 5
CHUNK = 64
N_EXPERTS = 64
TOP_K = 6
N_GROUPS = 8
TOPK_GROUPS = 4
GROUP_SIZE = N_EXPERTS // N_GROUPS
D_EXPERT = 256
ROUTED_SCALE = 2.5
EPS = 1e-6
LANES = 128
SUBLANES = 8
SMALL_COLS = LANES
MOE_BLOCK = 512
SC_CORES = 2
SC_SUBCORES = 16
SC_WORKERS = SC_CORES * SC_SUBCORES
GATHER_WINDOW = 64
VMEM_LIMIT = 56 * 1024 * 1024
NEG_INF = float("-inf")


def _dot(a, b):
    return jnp.dot(a.astype(BF16), b.astype(BF16), preferred_element_type=F32)


def _dot_nt(a, b):
    return lax.dot_general(a.astype(BF16), b.astype(BF16), (((1,), (1,)), ((), ())),
                           preferred_element_type=F32)


def _dot_tn(a, b):
    return lax.dot_general(a.astype(BF16), b.astype(BF16), (((0,), (0,)), ((), ())),
                           preferred_element_type=F32)


def _dot_hi(a, b):
    return jnp.dot(a, b, precision=HIGHEST, preferred_element_type=F32)


def _split(x):
    hi = x.astype(BF16)
    return hi, (x - hi.astype(F32)).astype(BF16)


def _dot3(a, b, dims=(((1,), (0,)), ((), ()))):
    ah, al = _split(a)
    bh, bl = _split(b)
    d = lambda u, v: lax.dot_general(u, v, dims, preferred_element_type=F32)
    return d(ah, bh) + (d(ah, bl) + d(al, bh))


def _seg_sum(x, ones_bd):
    hi = x.astype(BF16)
    lo = (x - hi.astype(F32)).astype(BF16)
    return (jnp.dot(hi, ones_bd, preferred_element_type=F32)
            + jnp.dot(lo, ones_bd, preferred_element_type=F32))


def _softplus(x):
    return jnp.maximum(x, 0.0) + jnp.log1p(jnp.exp(-jnp.abs(x)))


def _sigmoid(x):
    return 1.0 / (1.0 + jnp.exp(-x))


def _silu(x):
    return x * _sigmoid(x)


def _tri_inverse(a_list, eye):
    xs = [(-a).astype(BF16) for a in a_list]
    ts = [eye - a for a in a_list]
    power = 1
    while 2 * power < CHUNK:
        xs = [jnp.dot(x, x, preferred_element_type=F32).astype(BF16) for x in xs]
        ts = [t + jnp.dot(t.astype(BF16), x, preferred_element_type=F32) for t, x in zip(ts, xs)]
        power *= 2
    return ts


def _params(sem):
    return pltpu.CompilerParams(dimension_semantics=sem, vmem_limit_bytes=VMEM_LIMIT)


def _mod_kernel(c_ref, w_ref, b_ref, o_ref):
    c = c_ref[...]
    o_ref[...] = _dot_hi(_silu(c), w_ref[...]) + b_ref[...]


def _adaln_mod(c, ada_w, ada_b):
    rows = c.shape[0]
    n = ada_w.shape[1]
    return pl.pallas_call(
        _mod_kernel,
        grid=(n // D_MODEL,),
        in_specs=[pl.BlockSpec((rows, D_MODEL), lambda j: (0, 0)),
                  pl.BlockSpec((D_MODEL, D_MODEL), lambda j: (0, j)),
                  pl.BlockSpec((1, D_MODEL), lambda j: (0, j))],
        out_specs=pl.BlockSpec((rows, D_MODEL), lambda j: (0, j)),
        out_shape=jax.ShapeDtypeStruct((rows, n), F32),
        compiler_params=_params(("arbitrary",)),
        name="adaln_mod",
    )(c, ada_w, ada_b.reshape(1, n))


def _rms_mod(x, g, sc, sh):
    y = x * lax.rsqrt(jnp.mean(x * x, axis=-1, keepdims=True) + EPS)
    return (y * g) * (1.0 + sc) + sh


def _inproj_kernel(x_ref, g_ref, sc_ref, sh_ref, w_ref, o_ref, h_scr):
    @pl.when(pl.program_id(2) == 0)
    def _():
        h_scr[...] = _rms_mod(x_ref[0], g_ref[...], sc_ref[0], sh_ref[0]).astype(BF16)

    o_ref[0] = jnp.dot(h_scr[...], w_ref[...], preferred_element_type=F32)


def _inproj(x, g, sc, sh, w, tn):
    B, T, D = x.shape
    n = w.shape[1]
    tm = min(512, T)
    return pl.pallas_call(
        _inproj_kernel,
        grid=(B, T // tm, n // tn),
        in_specs=[pl.BlockSpec((1, tm, D), lambda b, i, j: (b, i, 0)),
                  pl.BlockSpec((1, D), lambda b, i, j: (0, 0)),
                  pl.BlockSpec((1, 1, D), lambda b, i, j: (b, 0, 0)),
                  pl.BlockSpec((1, 1, D), lambda b, i, j: (b, 0, 0)),
                  pl.BlockSpec((D, tn), lambda b, i, j: (0, j))],
        out_specs=pl.BlockSpec((1, tm, tn), lambda b, i, j: (b, i, j)),
        out_shape=jax.ShapeDtypeStruct((B, T, n), F32),
        scratch_shapes=[pltpu.VMEM((tm, D), BF16)],
        compiler_params=_params(("parallel", "parallel", "arbitrary")),
        name="in_proj",
    )(x, g, sc, sh, w)


def _fill_halo(ext_ref, cur, prev8, next8, tt):
    i = pl.program_id(1)
    last = pl.num_programs(1) - 1
    ext_ref[pl.ds(0, SUBLANES), :] = jnp.where(i == 0, 0.0, prev8)
    ext_ref[pl.ds(SUBLANES, tt), :] = cur
    ext_ref[pl.ds(SUBLANES + tt, SUBLANES), :] = jnp.where(i == last, 0.0, next8)


def _halo_specs(tt, width, col_block, seq_len):
    nb = tt // SUBLANES
    last = seq_len // SUBLANES - 1
    return [
        pl.BlockSpec((1, tt, width), lambda b, i: (b, i, col_block)),
        pl.BlockSpec((1, SUBLANES, width), lambda b, i: (b, jnp.maximum(i * nb - 1, 0), col_block)),
        pl.BlockSpec((1, SUBLANES, width), lambda b, i: (b, jnp.minimum((i + 1) * nb, last), col_block)),
    ]


def _rwkv_prep_kernel(z_ref, zp_ref, zn_ref, mu_ref, w0_ref, a0_ref, up_ref, gup_ref, kk_ref, ka_ref,
                      rk_ref, ones_ref,
                      r_out, kk_out, v_out, g_out, bonus_out, lw_out, kd_out, bd_out, ext_scr):
    tt = z_ref.shape[1]
    z = z_ref[0]
    _fill_halo(ext_scr, z, zp_ref[0], zn_ref[0], tt)
    z_prev = ext_scr[pl.ds(SUBLANES - 1, tt), :]
    z_next = ext_scr[pl.ds(SUBLANES + 1, tt), :]
    zs = z + (0.5 * (z_prev + z_next) - z) * mu_ref[...]
    W = A_WIDTH
    zr, zk, zv = zs[:, 0:W], zs[:, W:2 * W], zs[:, 2 * W:3 * W]
    zwa = zs[:, 3 * W:3 * W + LANES]
    zg = zs[:, 3 * W + LANES:3 * W + 2 * LANES]
    ones_bd = ones_ref[...]
    kk_raw = zk * kk_ref[...]
    kk = kk_raw * lax.rsqrt(_seg_sum(kk_raw * kk_raw, ones_bd) + 1e-6)
    lane = lax.broadcasted_iota(jnp.int32, zwa.shape, 1)
    lhs = jnp.where(lane < A_RANK_W, jnp.tanh(zwa), zwa)
    k_sum = jnp.zeros_like(zk)
    for d in range(2):
        up = _dot_hi(lhs, up_ref[d])
        wl = w0_ref[d:d + 1, :] + up[:, 0:W]
        w_log = -_softplus(-wl) - 0.5
        lw_out[d, 0] = -jnp.exp(w_log)
        a = _sigmoid(a0_ref[d:d + 1, :] + up[:, W:2 * W])
        k_d = zk * (1.0 + (a - 1.0) * ka_ref[...])
        kd_out[d, 0] = k_d
        bd_out[d, 0] = kk * a
        k_sum = k_sum + k_d
    r_out[0] = zr
    kk_out[0] = kk
    v_out[0] = zv
    g_out[0] = _dot_hi(_sigmoid(zg), gup_ref[...])
    bonus_out[0] = _seg_sum(zr * k_sum * rk_ref[...], ones_bd) * zv


def _rwkv_prep(z_a, mu, w0, a0, up_comb, g_up, k_k, k_a, r_k, ones_bd):
    B, T, _ = z_a.shape
    tt = min(256, T)
    W = A_WIDTH
    full = lambda shape: pl.BlockSpec(shape, lambda b, i: (0,) * len(shape))
    tok = pl.BlockSpec((1, tt, W), lambda b, i: (b, i, 0))
    tok2 = pl.BlockSpec((2, 1, tt, W), lambda b, i: (0, b, i, 0))
    s1 = jax.ShapeDtypeStruct((B, T, W), F32)
    s2 = jax.ShapeDtypeStruct((2, B, T, W), F32)
    return pl.pallas_call(
        _rwkv_prep_kernel,
        grid=(B, T // tt),
        in_specs=_halo_specs(tt, A_COLS, 0, T) + [
            full((1, A_COLS)), full((2, W)), full((2, W)), full((2, LANES, 2 * W)),
            full((A_RANK_G, W)), full((1, W)), full((1, W)), full((1, W)), full((W, W))],
        out_specs=[tok, tok, tok, tok, tok, tok2, tok2, tok2],
        out_shape=[s1, s1, s1, s1, s1, s2, s2, s2],
        scratch_shapes=[pltpu.VMEM((tt + 2 * SUBLANES, A_COLS), F32)],
        compiler_params=_params(("parallel", "parallel")),
        name="rwkv_prep",
    )(z_a, z_a, z_a, mu, w0, a0, up_comb, g_up, k_k, k_a, r_k, ones_bd)


def _chunk_masks(reverse):
    n = 2 * CHUNK
    row = lax.broadcasted_iota(jnp.int32, (n, n), 0)
    col = lax.broadcasted_iota(jnp.int32, (n, n), 1)
    same = (row // CHUNK) == (col // CHUNK)
    ti, tj = row % CHUNK, col % CHUNK
    if reverse:
        return same & (ti < tj), same & (ti <= tj), row == col
    return same & (ti > tj), same & (ti >= tj), row == col


def _cumsum_matrix(reverse):
    row = lax.broadcasted_iota(jnp.int32, (CHUNK, CHUNK), 0)
    col = lax.broadcasted_iota(jnp.int32, (CHUNK, CHUNK), 1)
    return ((row <= col) if reverse else (row >= col)).astype(F32)


def _rwkv_chunk_operands(r, kk, v, k, b, cs, lw, reverse):
    C = CHUNK
    cs_end = cs[0:1, :] if reverse else cs[C - 1:C, :]
    m0 = lax.broadcasted_iota(jnp.int32, (C, LANES), 1) < A_HEAD_DIM

    def stack(x):
        return jnp.concatenate([jnp.where(m0, x, 0.0), jnp.where(m0, 0.0, x)], axis=0)

    g_inv = jnp.exp(-cs)
    g_tail = jnp.exp(cs_end - cs)
    strict, incl, _ = _chunk_masks(reverse)
    return dict(rg=stack(r * jnp.exp(cs)), kkg=stack(kk * jnp.exp(cs - lw)), ki=stack(k * g_inv),
                bi=stack(b * g_inv), kt=stack(k * g_tail), bt=stack(b * g_tail), vs=stack(v),
                g_end=jnp.exp(cs_end), strict=strict, incl=incl)


def _rwkv_chunks(ops, hts):
    C = CHUNK
    n = 2 * C
    eye = _chunk_masks(False)[2].astype(F32)
    ps = [_dot_nt(jnp.concatenate([o["kkg"], o["rg"]], axis=0), jnp.concatenate([o["bi"], o["ki"]], axis=0))
          for o in ops]
    a_ab = [jnp.where(o["strict"], p[0:n, 0:n], 0.0) for o, p in zip(ops, ps)]
    a_ak = [jnp.where(o["strict"], p[0:n, n:2 * n], 0.0).astype(BF16) for o, p in zip(ops, ps)]
    a_rb = [jnp.where(o["incl"], p[n:2 * n, 0:n], 0.0).astype(BF16) for o, p in zip(ops, ps)]
    a_rk = [jnp.where(o["incl"], p[n:2 * n, n:2 * n], 0.0).astype(BF16) for o, p in zip(ops, ps)]
    vsb = [o["vs"].astype(BF16) for o in ops]
    akv = [jnp.dot(a, v, preferred_element_type=F32) for a, v in zip(a_ak, vsb)]
    arkv = [jnp.dot(a, v, preferred_element_type=F32) for a, v in zip(a_rk, vsb)]
    tinv = _tri_inverse(a_ab, eye)
    wu = [_dot(t, jnp.concatenate([o["kkg"], x], axis=1)).astype(BF16)
          for t, o, x in zip(tinv, ops, akv)]
    r2 = [jnp.dot(a, w, preferred_element_type=F32) for a, w in zip(a_rb, wu)]
    btb = [o["bt"].astype(BF16) for o in ops]
    btw = [_dot_tn(b, w[:, 0:LANES]) for b, w in zip(btb, wu)]
    hloc = [_dot_tn(v, o["kt"]) - _dot_tn(w[:, LANES:2 * LANES], b)
            for v, o, w, b in zip(vsb, ops, wu, btb)]
    ys, hts_new = [], []
    for o, r, yl, ht, bw, hl in zip(ops, r2, arkv, hts, btw, hloc):
        q_s = o["rg"] - r[:, 0:LANES]
        yloc_s = yl - r[:, LANES:2 * LANES]
        ys.append(yloc_s[0:C] + yloc_s[C:n] + _dot_nt(q_s[0:C] + q_s[C:n], ht))
        hts_new.append(ht * o["g_end"] - _dot_nt(ht, bw) + hl)
    return ys, hts_new


def _rwkv_scan_kernel(rf, kkf, vf, lwf, kf, bf, rb, kkb, vb, lwb, kb, bb, yf_out, yb_out, h_scr):
    @pl.when(pl.program_id(1) == 0)
    def _():
        h_scr[...] = jnp.zeros_like(h_scr)

    n_pairs = A_WIDTH // LANES
    sls = [slice(i * LANES, (i + 1) * LANES) for i in range(n_pairs)]
    ops, hts = [], []
    for d, (r_ref, kk_ref, v_ref, lw_ref, k_ref, b_ref) in enumerate(
            ((rf, kkf, vf, lwf, kf, bf), (rb, kkb, vb, lwb, kb, bb))):
        reverse = d == 1
        lw_all = lw_ref[0, 0]
        cs_all = _dot_hi(_cumsum_matrix(reverse), lw_all)
        ops += [_rwkv_chunk_operands(r_ref[0, :, sl], kk_ref[0, :, sl], v_ref[0, :, sl], k_ref[0, 0, :, sl],
                                     b_ref[0, 0, :, sl], cs_all[:, sl], lw_all[:, sl], reverse) for sl in sls]
        hts += [h_scr[d, i] for i in range(n_pairs)]
    ys, hts = _rwkv_chunks(ops, hts)
    for d, y_out in enumerate((yf_out, yb_out)):
        for i, sl in enumerate(sls):
            h_scr[d, i] = hts[d * n_pairs + i]
            y_out[0, :, sl] = ys[d * n_pairs + i]


def _rwkv_scan(r, kk, v, lw, kd, bd):
    B, T, W = r.shape
    N = T // CHUNK
    fwd = pl.BlockSpec((1, CHUNK, W), lambda b, c: (b, c, 0))
    bwd = pl.BlockSpec((1, CHUNK, W), lambda b, c: (b, N - 1 - c, 0))
    fwd2 = pl.BlockSpec((1, 1, CHUNK, W), lambda b, c: (0, b, c, 0))
    bwd2 = pl.BlockSpec((1, 1, CHUNK, W), lambda b, c: (1, b, N - 1 - c, 0))
    out = jax.ShapeDtypeStruct((B, T, W), F32)
    return pl.pallas_call(
        _rwkv_scan_kernel,
        grid=(B, N),
        in_specs=[fwd, fwd, fwd, fwd2, fwd2, fwd2, bwd, bwd, bwd, bwd2, bwd2, bwd2],
        out_specs=[fwd, bwd],
        out_shape=[out, out],
        scratch_shapes=[pltpu.VMEM((2, W // LANES, LANES, LANES), F32)],
        compiler_params=_params(("parallel", "arbitrary")),
        name="rwkv_scan",
    )(r, kk, v, lw, kd, bd, r, kk, v, lw, kd, bd)


def _gdn_prep_kernel(z_ref, zp_ref, zn_ref, small_ref, cw_ref, alog_ref, dtb_ref,
                     q_out, k_out, v_out, bg_out, ext_scr):
    tt = z_ref.shape[1]
    _fill_halo(ext_scr, z_ref[0], zp_ref[0], zn_ref[0], tt)
    half = B_CONV // 2
    acc = None
    for j in range(B_CONV):
        term = ext_scr[pl.ds(SUBLANES - half + j, tt), :] * cw_ref[j:j + 1, :]
        acc = term if acc is None else acc + term
    qkv = _silu(acc)
    for h in range(B_HEADS):
        for idx, out in enumerate((q_out, k_out)):
            x = qkv[:, idx * B_WIDTH + h * B_HEAD_DIM: idx * B_WIDTH + (h + 1) * B_HEAD_DIM]
            xn = x * lax.rsqrt(jnp.sum(x * x, axis=-1, keepdims=True) + 1e-6)
            if idx == 0:
                xn = xn * (B_HEAD_DIM ** -0.5)
            out[0, :, h * B_HEAD_DIM:(h + 1) * B_HEAD_DIM] = xn
    v_out[0] = qkv[:, 2 * B_WIDTH:3 * B_WIDTH]
    s = small_ref[0]
    lane = lax.broadcasted_iota(jnp.int32, s.shape, 1)
    beta = _sigmoid(s)
    g = -jnp.exp(alog_ref[...]) * _softplus(s + dtb_ref[...])
    bg_out[0] = jnp.where(lane < 2 * B_HEADS, beta, g)


def _gdn_prep(z_b, conv_w, alog_vec, dtb_vec):
    B, T, _ = z_b.shape
    tt = min(256, T)
    Wq = 3 * B_WIDTH
    full = lambda shape: pl.BlockSpec(shape, lambda b, i: (0,) * len(shape))
    tok = pl.BlockSpec((1, tt, B_WIDTH), lambda b, i: (b, i, 0))
    small = pl.BlockSpec((1, tt, SMALL_COLS), lambda b, i: (b, i, Wq // SMALL_COLS))
    s1 = jax.ShapeDtypeStruct((B, T, B_WIDTH), F32)
    return pl.pallas_call(
        _gdn_prep_kernel,
        grid=(B, T // tt),
        in_specs=_halo_specs(tt, Wq, 0, T) + [small, full((B_CONV, Wq)), full((1, SMALL_COLS)),
                                            full((1, SMALL_COLS))],
        out_specs=[tok, tok, tok, pl.BlockSpec((1, tt, SMALL_COLS), lambda b, i: (b, i, 0))],
        out_shape=[s1, s1, s1, jax.ShapeDtypeStruct((B, T, SMALL_COLS), F32)],
        scratch_shapes=[pltpu.VMEM((tt + 2 * SUBLANES, Wq), F32)],
        compiler_params=_params(("parallel", "parallel")),
        name="gdn_prep",
    )(z_b, z_b, z_b, z_b, conv_w, alog_vec, dtb_vec)


def _gdn_chunk_operands(q0, q1, k0, k1, v0, v1, gc2, gct2, bg2, j0, j1, reverse):
    C = CHUNK
    n = 2 * C
    strict, incl, _ = _chunk_masks(reverse)
    top = lax.broadcasted_iota(jnp.int32, (n, 1), 0) < C
    left = lax.broadcasted_iota(jnp.int32, (1, n), 1) < C
    g0, g1 = j0 + 2 * B_HEADS, j1 + 2 * B_HEADS
    gcol = jnp.where(top, gc2[:, g0:g0 + 1], gc2[:, g1:g1 + 1])
    grow = jnp.where(left, gct2[g0:g0 + 1, :], gct2[g1:g1 + 1, :])
    beta = jnp.where(top, bg2[:, j0:j0 + 1], bg2[:, j1:j1 + 1])
    e = 0 if reverse else C - 1
    glast = jnp.where(top, gc2[e:e + 1, g0:g0 + 1], gc2[e:e + 1, g1:g1 + 1])
    ks = jnp.concatenate([k0, k1], axis=0)
    qs = jnp.concatenate([q0, q1], axis=0)
    vs = jnp.concatenate([v0, v1], axis=0)
    egc = jnp.exp(gcol)
    kb = ks * beta
    return dict(gam=jnp.exp(jnp.where(incl, gcol - grow, NEG_INF)), strict=strict, ks=ks,
                kbq=jnp.concatenate([kb, qs], axis=0),
                rhs=jnp.concatenate([vs * beta, kb * egc], axis=1),
                qh=(qs * egc).astype(BF16), ktail=(ks * jnp.exp(glast - gcol)).astype(BF16),
                decay=jnp.exp(glast))


def _gdn_chunks(ops, states):
    C = CHUNK
    n = 2 * C
    eye = _chunk_masks(False)[2].astype(F32)
    ps = [_dot3(o["kbq"], o["ks"], (((1,), (1,)), ((), ()))) for o in ops]
    lower = [jnp.where(o["strict"], p[0:n] * o["gam"], 0.0) for o, p in zip(ops, ps)]
    aqk = [(p[n:2 * n] * o["gam"]).astype(BF16) for o, p in zip(ops, ps)]
    tinv = [t.astype(BF16) for t in _tri_inverse(lower, eye)]
    uw = [jnp.dot(t, o["rhs"].astype(BF16), preferred_element_type=F32) for t, o in zip(tinv, ops)]
    resid = [o["rhs"] - x - _dot3(l, x) for o, l, x in zip(ops, lower, uw)]
    uw = [x + jnp.dot(t, r.astype(BF16), preferred_element_type=F32) for x, t, r in zip(uw, tinv, resid)]
    sb = [(s0.astype(BF16), s1.astype(BF16)) for s0, s1 in states]
    ws = [jnp.concatenate([jnp.dot(u[0:C, LANES:].astype(BF16), s0, preferred_element_type=F32),
                           jnp.dot(u[C:n, LANES:].astype(BF16), s1, preferred_element_type=F32)], axis=0)
          for u, (s0, s1) in zip(uw, sb)]
    qss = [jnp.concatenate([jnp.dot(o["qh"][0:C], s0, preferred_element_type=F32),
                            jnp.dot(o["qh"][C:n], s1, preferred_element_type=F32)], axis=0)
           for o, (s0, s1) in zip(ops, sb)]
    v_new = [(u[:, 0:LANES] - w).astype(BF16) for u, w in zip(uw, ws)]
    outs = [q + jnp.dot(a, v, preferred_element_type=F32) for q, a, v in zip(qss, aqk, v_new)]
    new_states = []
    for o, v, (s0, s1) in zip(ops, v_new, states):
        new_states.append((s0 * o["decay"][0:1, :] + _dot_tn(o["ktail"][0:C], v[0:C]),
                           s1 * o["decay"][C:C + 1, :] + _dot_tn(o["ktail"][C:n], v[C:n])))
    return [(x[0:C], x[C:n]) for x in outs], new_states


def _gdn_scan_kernel(qf, kf, vf, bgf, qb, kb, vb, bgb, of_out, ob_out, s_scr):
    @pl.when(pl.program_id(1) == 0)
    def _():
        s_scr[...] = jnp.zeros_like(s_scr)

    D = B_HEAD_DIM
    n_pairs = B_HEADS // 2
    sl = lambda h: slice(h * D, (h + 1) * D)
    ops, states = [], []
    for d, (q_ref, k_ref, v_ref, bg_ref) in enumerate(((qf, kf, vf, bgf), (qb, kb, vb, bgb))):
        reverse = d == 1
        bg = bg_ref[0]
        gc = _dot_hi(_cumsum_matrix(reverse), bg)
        bg2 = jnp.concatenate([bg, bg], axis=0)
        gc2 = jnp.concatenate([gc, gc], axis=0)
        gct2 = gc2.T
        for i in range(n_pairs):
            h0, h1 = 2 * i, 2 * i + 1
            ops.append(_gdn_chunk_operands(
                q_ref[0, :, sl(h0)], q_ref[0, :, sl(h1)], k_ref[0, :, sl(h0)], k_ref[0, :, sl(h1)],
                v_ref[0, :, sl(h0)], v_ref[0, :, sl(h1)], gc2, gct2, bg2,
                d * B_HEADS + h0, d * B_HEADS + h1, reverse))
            states.append((s_scr[d, h0], s_scr[d, h1]))
    outs, states = _gdn_chunks(ops, states)
    for d, o_out in enumerate((of_out, ob_out)):
        for i in range(n_pairs):
            h0, h1 = 2 * i, 2 * i + 1
            (o0, o1), (s0, s1) = outs[d * n_pairs + i], states[d * n_pairs + i]
            s_scr[d, h0] = s0
            s_scr[d, h1] = s1
            o_out[0, :, sl(h0)] = o0
            o_out[0, :, sl(h1)] = o1


def _gdn_scan(q, k, v, bg):
    B, T, W = q.shape
    N = T // CHUNK
    fwd = pl.BlockSpec((1, CHUNK, W), lambda b, c: (b, c, 0))
    bwd = pl.BlockSpec((1, CHUNK, W), lambda b, c: (b, N - 1 - c, 0))
    sfwd = pl.BlockSpec((1, CHUNK, SMALL_COLS), lambda b, c: (b, c, 0))
    sbwd = pl.BlockSpec((1, CHUNK, SMALL_COLS), lambda b, c: (b, N - 1 - c, 0))
    out = jax.ShapeDtypeStruct((B, T, W), F32)
    return pl.pallas_call(
        _gdn_scan_kernel,
        grid=(B, N),
        in_specs=[fwd, fwd, fwd, sfwd, bwd, bwd, bwd, sbwd],
        out_specs=[fwd, bwd],
        out_shape=[out, out],
        scratch_shapes=[pltpu.VMEM((2, B_HEADS, B_HEAD_DIM, B_HEAD_DIM), F32)],
        compiler_params=_params(("parallel", "arbitrary")),
        name="gdn_scan",
    )(q, k, v, bg, q, k, v, bg)


def _pack_bf16_pairs(x):
    n = x.shape[1] // 2
    hi = lax.bitcast_convert_type(x[:, :n].astype(BF16).astype(F32), jnp.int32)
    lo = lax.bitcast_convert_type(x[:, n:].astype(BF16).astype(F32), jnp.int32)
    return hi | lax.shift_right_logical(lo, 16)


def _unpack_bf16_pairs(p):
    a = lax.bitcast_convert_type(p & jnp.int32(-65536), F32)
    b = lax.bitcast_convert_type(lax.shift_left(p, 16), F32)
    return a.astype(BF16), b.astype(BF16)


def _route(scores, biased, base):
    tt = scores.shape[-1]
    shape3 = (N_GROUPS, GROUP_SIZE, tt)
    s3 = scores.reshape(shape3)
    b3 = biased.reshape(shape3)
    jid = lax.broadcasted_iota(jnp.int32, shape3, 1).astype(F32)
    gid = lax.broadcasted_iota(jnp.int32, shape3, 0).astype(F32)
    m1 = jnp.max(b3, axis=1, keepdims=True)
    first = jnp.min(jnp.where(b3 == m1, jid, float(GROUP_SIZE)), axis=1, keepdims=True)
    m2 = jnp.max(jnp.where(jid == first, NEG_INF, b3), axis=1, keepdims=True)
    gs = m1 + m2
    grp = lax.broadcasted_iota(jnp.int32, (N_GROUPS, 1, tt), 0).astype(F32)
    keep = jnp.zeros((N_GROUPS, 1, tt), F32)
    for _ in range(TOPK_GROUPS):
        m = jnp.max(gs, axis=0, keepdims=True)
        pick = grp == jnp.min(jnp.where(gs == m, grp, float(N_GROUPS)), axis=0, keepdims=True)
        keep = jnp.where(pick, 1.0, keep)
        gs = jnp.where(pick, NEG_INF, gs)
    work = jnp.where(keep > 0.0, b3, NEG_INF)
    eid = gid * float(GROUP_SIZE) + jid
    chosen = jnp.zeros(shape3, F32)
    sum01 = lambda a: jnp.sum(jnp.sum(a, axis=1, keepdims=True), axis=0, keepdims=True)
    ids, raw = [], []
    for _ in range(TOP_K):
        m = jnp.max(jnp.max(work, axis=1, keepdims=True), axis=0, keepdims=True)
        cand = jnp.where(work == m, eid, float(N_EXPERTS))
        first = jnp.min(jnp.min(cand, axis=1, keepdims=True), axis=0, keepdims=True)
        pick = eid == first
        ids.append(first)
        raw.append(sum01(jnp.where(pick, s3, 0.0)))
        chosen = jnp.where(pick, 1.0, chosen)
        work = jnp.where(pick, NEG_INF, work)
    total = raw[0]
    for r in raw[1:]:
        total = total + r
    flat = chosen.reshape(N_EXPERTS, tt)
    earlier = (lax.broadcasted_iota(jnp.int32, (tt, tt), 0)
               < lax.broadcasted_iota(jnp.int32, (tt, tt), 1)).astype(BF16)
    prefix = (jnp.dot(flat.astype(BF16), earlier, preferred_element_type=F32) + base).reshape(shape3)
    row = lax.broadcasted_iota(jnp.int32, (SUBLANES, tt), 0)
    idx8 = jnp.zeros((SUBLANES, tt), F32)
    w8 = jnp.zeros((SUBLANES, tt), F32)
    pos8 = jnp.zeros((SUBLANES, tt), F32)
    for k in range(TOP_K):
        rank = sum01(jnp.where(eid == ids[k], prefix, 0.0))
        idx8 = jnp.where(row == k, ids[k].reshape(1, tt), idx8)
        w8 = jnp.where(row == k, (raw[k] / total * ROUTED_SCALE).reshape(1, tt), w8)
        pos8 = jnp.where(row == k, rank.reshape(1, tt), pos8)
    counts = jnp.sum(flat, axis=1, keepdims=True)
    return idx8.astype(jnp.int32), w8, pos8.astype(jnp.int32), counts


def _mixer_post_kernel(yf, yb, bonus, g, of, ob, zz, zga, zgb, x_ref, gt1, sc2, sh2,
                       gnw, gnb, ones_ref, aproj, bng, bproj, wout, n2g, rwt, rbias,
                       x1_out, h2_out, idx_out, w_out_ref, pos_out, counts_out):
    @pl.when((pl.program_id(0) == 0) & (pl.program_id(1) == 0))
    def _():
        counts_out[...] = jnp.zeros_like(counts_out)

    ones_bd = ones_ref[...]
    y = yf[0] + yb[0]
    mean = _seg_sum(y, ones_bd) * (1.0 / A_HEAD_DIM)
    dlt = y - mean
    var = _seg_sum(dlt * dlt, ones_bd) * (1.0 / A_HEAD_DIM)
    yn = dlt * lax.rsqrt(var + A_GN_EPS) * gnw[...] + gnb[...]
    y_a = _dot((yn + bonus[0]) * g[0], aproj[...])

    o = of[0] + ob[0]
    z = zz[0]
    parts = []
    for h in range(B_HEADS):
        oh = o[:, h * B_HEAD_DIM:(h + 1) * B_HEAD_DIM]
        parts.append(oh * lax.rsqrt(jnp.mean(oh * oh, axis=-1, keepdims=True) + EPS))
    on = jnp.concatenate(parts, axis=1) * bng[...] * _silu(z)
    y_b = _dot(on, bproj[...])

    u = _sigmoid(zga[0]) * y_a + _sigmoid(zgb[0]) * y_b
    x1 = x_ref[0] + gt1[0] * _dot(u, wout[...])
    x1_out[0] = x1
    h2 = _rms_mod(x1, n2g[...], sc2[0], sh2[0])
    h2_out[0] = _pack_bf16_pairs(h2)
    logits_t = _dot_nt(rwt[...], h2)
    scores = _sigmoid(logits_t)
    base = counts_out[:, 0:1]
    idx8, w8, pos8, counts = _route(scores, scores + rbias[...], base)
    idx_out[0] = idx8
    w_out_ref[0] = w8
    pos_out[0] = pos8
    counts_out[...] = counts_out[...] + counts


def _mixer_post(yf, yb, bonus, g, of, ob, z_c, x, gt1, sc2, sh2, gn_w, gn_b, ones_bd, a_proj, bng,
                b_proj, w_out, n2g, rwt, rbias):
    B, T, D = x.shape
    tt = min(256, T)
    full = lambda shape: pl.BlockSpec(shape, lambda b, i: (0,) * len(shape))
    tokw = lambda w, col=0: pl.BlockSpec((1, tt, w), lambda b, i: (b, i, col))
    modrow = pl.BlockSpec((1, 1, D), lambda b, i: (b, 0, 0))
    pick = pl.BlockSpec((1, SUBLANES, tt), lambda b, i: (b, 0, i))
    W = A_WIDTH
    return pl.pallas_call(
        _mixer_post_kernel,
        grid=(B, T // tt),
        in_specs=[tokw(W), tokw(W), tokw(W), tokw(W), tokw(D), tokw(D),
                  tokw(D, 0), tokw(D, 1), tokw(D, 2), tokw(D), modrow, modrow, modrow,
                  full((1, W)), full((1, W)), full((W, W)), full((W, D)), full((1, D)), full((D, D)),
                  full((D, D)), full((1, D)), full((N_EXPERTS, D)), full((N_EXPERTS, 1))],
        out_specs=[tokw(D), tokw(D // 2), pick, pick, pick,
                   pl.BlockSpec((N_EXPERTS, LANES), lambda b, i: (0, 0))],
        out_shape=[jax.ShapeDtypeStruct((B, T, D), F32), jax.ShapeDtypeStruct((B, T, D // 2), jnp.int32),
                   jax.ShapeDtypeStruct((B, SUBLANES, T), jnp.int32),
                   jax.ShapeDtypeStruct((B, SUBLANES, T), F32),
                   jax.ShapeDtypeStruct((B, SUBLANES, T), jnp.int32),
                   jax.ShapeDtypeStruct((N_EXPERTS, LANES), F32)],
        compiler_params=_params(("arbitrary", "arbitrary")),
        name="mixer_post",
    )(yf, yb, bonus, g, of, ob, z_c, z_c, z_c, x, gt1, sc2, sh2, gn_w, gn_b, ones_bd, a_proj, bng,
      b_proj, w_out, n2g, rwt, rbias)


def _gather_rows(table, idx):
    n_rows, width = idx.shape[0], table.shape[1]
    per_worker = n_rows // SC_WORKERS
    n_windows = per_worker // GATHER_WINDOW
    mesh = plsc.VectorSubcoreMesh(core_axis_name="c", subcore_axis_name="s")

    @functools.partial(
        pl.kernel, mesh=mesh, out_type=jax.ShapeDtypeStruct((n_rows, width), table.dtype),
        scratch_types=[pltpu.VMEM((GATHER_WINDOW,), jnp.int32),
                       pltpu.VMEM((GATHER_WINDOW, width), table.dtype),
                       pltpu.SemaphoreType.DMA])
    def gather(table_hbm, idx_hbm, out_hbm, idx_v, rows_v, sem):
        worker = lax.axis_index("s") * SC_CORES + lax.axis_index("c")

        @pl.loop(0, n_windows)
        def _(j):
            off = pl.multiple_of(worker * per_worker + j * GATHER_WINDOW, GATHER_WINDOW)
            pltpu.sync_copy(idx_hbm.at[pl.ds(off, GATHER_WINDOW)], idx_v)
            pltpu.async_copy(table_hbm.at[idx_v], rows_v, sem).wait()
            pltpu.sync_copy(rows_v, out_hbm.at[pl.ds(off, GATHER_WINDOW)])

    return gather(table, idx)


def _expert_kernel(be_ref, used_ref, xs_ref, wg, wu, wd, ys_ref):
    @pl.when(pl.program_id(0) < used_ref[0])
    def _():
        half = D_MODEL // 2
        a, b = _unpack_bf16_pairs(xs_ref[...])
        mm = lambda w: (jnp.dot(a, w[0, 0:half, :], preferred_element_type=F32)
                        + jnp.dot(b, w[0, half:D_MODEL, :], preferred_element_type=F32))
        hid = _silu(mm(wg)) * mm(wu)
        ys_ref[...] = _pack_bf16_pairs(_dot(hid, wd[0]))


def _experts(block_expert, n_used, xs, wg, wu, wd):
    n_rows, half = xs.shape
    rows = pl.BlockSpec((MOE_BLOCK, half), lambda i, be, used: (i, 0))
    wspec = lambda shape: pl.BlockSpec(shape, lambda i, be, used: (be[i], 0, 0))
    return pl.pallas_call(
        _expert_kernel,
        grid_spec=pltpu.PrefetchScalarGridSpec(
            num_scalar_prefetch=2, grid=(n_rows // MOE_BLOCK,),
            in_specs=[rows, wspec((1, D_MODEL, D_EXPERT)), wspec((1, D_MODEL, D_EXPERT)),
                      wspec((1, D_EXPERT, D_MODEL))],
            out_specs=rows),
        out_shape=jax.ShapeDtypeStruct((n_rows, half), jnp.int32),
        compiler_params=_params(("arbitrary",)),
        name="experts",
    )(block_expert, n_used, xs, wg, wu, wd)


def _moe_final_kernel(yg_ref, w_ref, h_ref, x1_ref, gt2, fg, sg, su, sd, o_ref):
    half = D_MODEL // 2
    a, b = _unpack_bf16_pairs(h_ref[0])
    mm = lambda w: (jnp.dot(a, w[0:half, :], preferred_element_type=F32)
                    + jnp.dot(b, w[half:D_MODEL, :], preferred_element_type=F32))
    shared = _dot(_silu(mm(sg)) * mm(su), sd[...])
    w = w_ref[0]
    lo = hi = None
    for k in range(TOP_K):
        ya, yb = _unpack_bf16_pairs(yg_ref[k, 0])
        wk = w[:, k:k + 1]
        lo = ya.astype(F32) * wk if lo is None else lo + ya.astype(F32) * wk
        hi = yb.astype(F32) * wk if hi is None else hi + yb.astype(F32) * wk
    x2 = x1_ref[0] + gt2[0] * (jnp.concatenate([lo, hi], axis=1) + shared)
    o_ref[0] = x2 * lax.rsqrt(jnp.mean(x2 * x2, axis=-1, keepdims=True) + EPS) * fg[...]


def _moe_final(yg, w_t, h2p, x1, gt2, fg, sg, su, sd):
    B, T, D = x1.shape
    tm = min(512, T)
    tok = lambda w: pl.BlockSpec((1, tm, w), lambda b, i: (b, i, 0))
    full = lambda shape: pl.BlockSpec(shape, lambda b, i: (0,) * len(shape))
    return pl.pallas_call(
        _moe_final_kernel,
        grid=(B, T // tm),
        in_specs=[pl.BlockSpec((TOP_K, 1, tm, D // 2), lambda b, i: (0, b, i, 0)), tok(SUBLANES),
                  tok(D // 2), tok(D), pl.BlockSpec((1, 1, D), lambda b, i: (b, 0, 0)), full((1, D)),
                  full((D, D_EXPERT)), full((D, D_EXPERT)), full((D_EXPERT, D))],
        out_specs=tok(D),
        out_shape=jax.ShapeDtypeStruct((B, T, D), F32),
        compiler_params=_params(("parallel", "parallel")),
        name="moe_final",
    )(yg, w_t, h2p, x1, gt2, fg, sg, su, sd)


def _moe_routed(h2p, idx, w, pos, counts, x1, gt2, p):
    B, T, half = h2p.shape
    n_slots = B * T * TOP_K
    n_rows = n_slots + N_EXPERTS * MOE_BLOCK
    cnt = counts[:, 0].astype(jnp.int32)
    padded = (cnt + MOE_BLOCK - 1) // MOE_BLOCK * MOE_BLOCK
    pad_end = jnp.cumsum(padded)
    pad_start = pad_end - padded
    dest = pad_start[idx[:, :TOP_K]] + pos[:, :TOP_K]
    tok = (jnp.arange(B, dtype=jnp.int32)[:, None, None] * T
           + jnp.arange(T, dtype=jnp.int32)[None, None, :]) + jnp.zeros_like(dest)
    row_tok = jnp.zeros((n_rows,), jnp.int32).at[dest.reshape(-1)].set(tok.reshape(-1), unique_indices=True)
    starts = jnp.arange(n_rows // MOE_BLOCK, dtype=jnp.int32) * MOE_BLOCK
    block_expert = jnp.minimum(jnp.searchsorted(pad_end, starts, side="right"), N_EXPERTS - 1).astype(jnp.int32)
    n_used = (pad_end[-1:] // MOE_BLOCK).astype(jnp.int32)
    xs = _gather_rows(h2p.reshape(B * T, half), row_tok)
    ys = _experts(block_expert, n_used, xs, p["wg"], p["wu"], p["wd"])
    yg = _gather_rows(ys, jnp.swapaxes(dest, 0, 1).reshape(-1)).reshape(TOP_K, B, T, half)
    return _moe_final(yg, jnp.swapaxes(w, 1, 2), h2p, x1, gt2, p["fg"], p["sg"], p["su"], p["sd"])


def _prepare_weights(w_in, shift_mu, a_w0, a_w_up, a_a0, a_a_up, a_g_up, a_k_k, a_k_a, a_r_k, a_gn_w,
                     a_gn_b, a_proj, b_conv_w, b_a_log, b_dt_bias, b_norm_g, b_proj, w_out, router_w,
                     router_bias, exp_gate, exp_up, exp_down, sh_gate, sh_up, sh_down, norm1_g, norm2_g,
                     final_g):
    W = A_WIDTH
    w = w_in[0]
    c0 = A_COLS
    c1 = c0 + 3 * B_WIDTH
    c2 = c1 + B_WIDTH
    c3 = c2 + 4 * B_HEADS
    pad = jnp.zeros((D_MODEL, SMALL_COLS - 4 * B_HEADS), F32)
    w_a = w[:, :c0].astype(BF16)
    w_b = jnp.concatenate([w[:, c0:c1], w[:, c2:c3], pad], axis=1).astype(BF16)
    w_c = jnp.concatenate([w[:, c1:c2], w[:, c3:]], axis=1).astype(BF16)
    zeros = jnp.zeros((2, A_RANK_W, W), F32)
    up_comb = jnp.concatenate([jnp.concatenate([a_w_up[0], zeros], axis=2),
                               jnp.concatenate([zeros, a_a_up[0]], axis=2)], axis=1)
    head = jnp.arange(W) // A_HEAD_DIM
    ones_bd = (head[:, None] == head[None, :]).astype(BF16)
    small = lambda v: jnp.zeros((1, SMALL_COLS), F32).at[0, 2 * B_HEADS:4 * B_HEADS].set(v.reshape(-1))
    return dict(
        w_a=w_a, w_b=w_b, w_c=w_c, mu=shift_mu[0].reshape(1, A_COLS), w0=a_w0[0], a0=a_a0[0],
        up_comb=up_comb, g_up=a_g_up[0], k_k=a_k_k[0].reshape(1, W), k_a=a_k_a[0].reshape(1, W),
        r_k=a_r_k[0].reshape(1, W), ones_bd=ones_bd, gn_w=a_gn_w[0].reshape(1, W),
        gn_b=a_gn_b[0].reshape(1, W), a_proj=a_proj[0].astype(BF16), conv_w=b_conv_w[0],
        alog=small(b_a_log[0]), dtb=small(b_dt_bias[0]),
        bng=jnp.tile(b_norm_g[0], B_HEADS).reshape(1, B_WIDTH), b_proj=b_proj[0].astype(BF16),
        w_out=w_out[0].astype(BF16), rwt=router_w[0].T, rbias=router_bias[0].reshape(N_EXPERTS, 1),
        wg=exp_gate[0].astype(BF16), wu=exp_up[0].astype(BF16), wd=exp_down[0].astype(BF16),
        sg=sh_gate[0].astype(BF16), su=sh_up[0].astype(BF16), sd=sh_down[0].astype(BF16),
        n1g=norm1_g[0].reshape(1, D_MODEL), n2g=norm2_g[0].reshape(1, D_MODEL),
        fg=final_g.reshape(1, D_MODEL))


def _layer(x, mod, p):
    B = x.shape[0]
    sh1, sc1, gt1, sh2, sc2, gt2 = (m.reshape(B, 1, D_MODEL) for m in jnp.split(mod, 6, axis=-1))
    z_a = _inproj(x, p["n1g"], sc1, sh1, p["w_a"], 896)
    z_b = _inproj(x, p["n1g"], sc1, sh1, p["w_b"], 640)
    z_c = _inproj(x, p["n1g"], sc1, sh1, p["w_c"], 1024)
    r, kk, v, g, bonus, lw, kd, bd = _rwkv_prep(z_a, p["mu"], p["w0"], p["a0"], p["up_comb"], p["g_up"],
                                                p["k_k"], p["k_a"], p["r_k"], p["ones_bd"])
    yf, yb = _rwkv_scan(r, kk, v, lw, kd, bd)
    q, k, vv, bg = _gdn_prep(z_b, p["conv_w"], p["alog"], p["dtb"])
    of, ob = _gdn_scan(q, k, vv, bg)
    x1, h2p, idx, w, pos, counts = _mixer_post(yf, yb, bonus, g, of, ob, z_c, x, gt1, sc2, sh2, p["gn_w"],
                                               p["gn_b"], p["ones_bd"], p["a_proj"], p["bng"], p["b_proj"],
                                               p["w_out"], p["n2g"], p["rwt"], p["rbias"])
    return _moe_routed(h2p, idx, w, pos, counts, x1, gt2, p)


def kernel(x_prompt, x_sample, c_prompt, c_sample, ada_w, ada_b, norm1_g, norm2_g, w_in, shift_mu, a_w0, a_w_up, a_a0, a_a_up, a_g_up, a_k_k, a_k_a, a_r_k, a_gn_w, a_gn_b, a_proj, b_conv_w, b_a_log, b_dt_bias, b_norm_g, b_proj, w_out, router_w, router_bias, exp_gate, exp_up, exp_down, sh_gate, sh_up, sh_down, final_g):
    p = _prepare_weights(w_in, shift_mu, a_w0, a_w_up, a_a0, a_a_up, a_g_up, a_k_k, a_k_a, a_r_k, a_gn_w,
                         a_gn_b, a_proj, b_conv_w, b_a_log, b_dt_bias, b_norm_g, b_proj, w_out, router_w,
                         router_bias, exp_gate, exp_up, exp_down, sh_gate, sh_up, sh_down, norm1_g,
                         norm2_g, final_g)
    nb_p, nb_s = c_prompt.shape[0], c_sample.shape[0]
    rows = -(-(nb_p + nb_s) // SUBLANES) * SUBLANES
    c_all = jnp.concatenate([c_prompt, c_sample, jnp.zeros((rows - nb_p - nb_s, D_MODEL), F32)], axis=0)
    mod = _adaln_mod(c_all, ada_w[0], ada_b[0])
    y_prompt = _layer(x_prompt, mod[:nb_p], p)
    y_sample = _layer(x_sample, mod[nb_p:nb_p + nb_s], p)
    return (y_prompt, y_sample)
```

```python
import functools

import jax
import jax.numpy as jnp
from jax import lax
from jax.experimental import pallas as pl
from jax.experimental.pallas import tpu as pltpu
from jax.experimental.pallas import tpu_sc as plsc

F32 = jnp.float32
BF16 = jnp.bfloat16
HIGHEST = lax.Precision.HIGHEST

D_MODEL = 1024
A_HEADS = 8
A_HEAD_DIM = 64
A_WIDTH = A_HEADS * A_HEAD_DIM
A_RANK_W = 64
A_RANK_A = 64
A_RANK_G = 128
A_GN_EPS = 64e-5
A_COLS = 3 * A_WIDTH + A_RANK_W + A_RANK_A + A_RANK_G
B_HEADS = 8
B_HEAD_DIM = 128
B_WIDTH = B_HEADS * B_HEAD_DIM
B_CONV = 5
CHUNK = 64
N_EXPERTS = 64
TOP_K = 6
N_GROUPS = 8
TOPK_GROUPS = 4
GROUP_SIZE = N_EXPERTS // N_GROUPS
D_EXPERT = 256
ROUTED_SCALE = 2.5
EPS = 1e-6
LANES = 128
SUBLANES = 8
SMALL_COLS = LANES
MOE_BLOCK = 512
SC_CORES = 2
SC_SUBCORES = 16
SC_WORKERS = SC_CORES * SC_SUBCORES
GATHER_WINDOW = 64
VMEM_LIMIT = 56 * 1024 * 1024
NEG_INF = float("-inf")


def _dot(a, b):
    return jnp.dot(a.astype(BF16), b.astype(BF16), preferred_element_type=F32)


def _dot_nt(a, b):
    return lax.dot_general(a.astype(BF16), b.astype(BF16), (((1,), (1,)), ((), ())),
                           preferred_element_type=F32)


def _dot_tn(a, b):
    return lax.dot_general(a.astype(BF16), b.astype(BF16), (((0,), (0,)), ((), ())),
                           preferred_element_type=F32)


def _dot_hi(a, b):
    return jnp.dot(a, b, precision=HIGHEST, preferred_element_type=F32)


def _split(x):
    hi = x.astype(BF16)
    return hi, (x - hi.astype(F32)).astype(BF16)


def _dot3(a, b, dims=(((1,), (0,)), ((), ()))):
    ah, al = _split(a)
    bh, bl = _split(b)
    d = lambda u, v: lax.dot_general(u, v, dims, preferred_element_type=F32)
    return d(ah, bh) + (d(ah, bl) + d(al, bh))


def _seg_sum(x, ones_bd):
    hi = x.astype(BF16)
    lo = (x - hi.astype(F32)).astype(BF16)
    return (jnp.dot(hi, ones_bd, preferred_element_type=F32)
            + jnp.dot(lo, ones_bd, preferred_element_type=F32))


def _softplus(x):
    return jnp.maximum(x, 0.0) + jnp.log1p(jnp.exp(-jnp.abs(x)))


def _sigmoid(x):
    return 1.0 / (1.0 + jnp.exp(-x))


def _silu(x):
    return x * _sigmoid(x)


def _tri_inverse(a_list, eye):
    xs = [(-a).astype(BF16) for a in a_list]
    ts = [eye - a for a in a_list]
    power = 1
    while 2 * power < CHUNK:
        xs = [jnp.dot(x, x, preferred_element_type=F32).astype(BF16) for x in xs]
        ts = [t + jnp.dot(t.astype(BF16), x, preferred_element_type=F32) for t, x in zip(ts, xs)]
        power *= 2
    return ts


def _params(sem):
    return pltpu.CompilerParams(dimension_semantics=sem, vmem_limit_bytes=VMEM_LIMIT)


def _mod_kernel(c_ref, w_ref, b_ref, o_ref):
    c = c_ref[...]
    o_ref[...] = _dot_hi(_silu(c), w_ref[...]) + b_ref[...]


def _adaln_mod(c, ada_w, ada_b):
    rows = c.shape[0]
    n = ada_w.shape[1]
    return pl.pallas_call(
        _mod_kernel,
        grid=(n // D_MODEL,),
        in_specs=[pl.BlockSpec((rows, D_MODEL), lambda j: (0, 0)),
                  pl.BlockSpec((D_MODEL, D_MODEL), lambda j: (0, j)),
                  pl.BlockSpec((1, D_MODEL), lambda j: (0, j))],
        out_specs=pl.BlockSpec((rows, D_MODEL), lambda j: (0, j)),
        out_shape=jax.ShapeDtypeStruct((rows, n), F32),
        compiler_params=_params(("arbitrary",)),
        name="adaln_mod",
    )(c, ada_w, ada_b.reshape(1, n))


def _rms_mod(x, g, sc, sh):
    y = x * lax.rsqrt(jnp.mean(x * x, axis=-1, keepdims=True) + EPS)
    return (y * g) * (1.0 + sc) + sh


def _inproj_kernel(x_ref, g_ref, sc_ref, sh_ref, w_ref, o_ref, h_scr):
    @pl.when(pl.program_id(2) == 0)
    def _():
        h_scr[...] = _rms_mod(x_ref[0], g_ref[...], sc_ref[0], sh_ref[0]).astype(BF16)

    o_ref[0] = jnp.dot(h_scr[...], w_ref[...], preferred_element_type=F32)


def _inproj(x, g, sc, sh, w, tn):
    B, T, D = x.shape
    n = w.shape[1]
    tm = min(512, T)
    return pl.pallas_call(
        _inproj_kernel,
        grid=(B, T // tm, n // tn),
        in_specs=[pl.BlockSpec((1, tm, D), lambda b, i, j: (b, i, 0)),
                  pl.BlockSpec((1, D), lambda b, i, j: (0, 0)),
                  pl.BlockSpec((1, 1, D), lambda b, i, j: (b, 0, 0)),
                  pl.BlockSpec((1, 1, D), lambda b, i, j: (b, 0, 0)),
                  pl.BlockSpec((D, tn), lambda b, i, j: (0, j))],
        out_specs=pl.BlockSpec((1, tm, tn), lambda b, i, j: (b, i, j)),
        out_shape=jax.ShapeDtypeStruct((B, T, n), F32),
        scratch_shapes=[pltpu.VMEM((tm, D), BF16)],
        compiler_params=_params(("parallel", "parallel", "arbitrary")),
        name="in_proj",
    )(x, g, sc, sh, w)


def _fill_halo(ext_ref, cur, prev8, next8, tt):
    i = pl.program_id(1)
    last = pl.num_programs(1) - 1
    ext_ref[pl.ds(0, SUBLANES), :] = jnp.where(i == 0, 0.0, prev8)
    ext_ref[pl.ds(SUBLANES, tt), :] = cur
    ext_ref[pl.ds(SUBLANES + tt, SUBLANES), :] = jnp.where(i == last, 0.0, next8)


def _halo_specs(tt, width, col_block, seq_len):
    nb = tt // SUBLANES
    last = seq_len // SUBLANES - 1
    return [
        pl.BlockSpec((1, tt, width), lambda b, i: (b, i, col_block)),
        pl.BlockSpec((1, SUBLANES, width), lambda b, i: (b, jnp.maximum(i * nb - 1, 0), col_block)),
        pl.BlockSpec((1, SUBLANES, width), lambda b, i: (b, jnp.minimum((i + 1) * nb, last), col_block)),
    ]


def _rwkv_prep_kernel(z_ref, zp_ref, zn_ref, mu_ref, w0_ref, a0_ref, up_ref, gup_ref, kk_ref, ka_ref,
                      rk_ref, ones_ref,
                      r_out, kk_out, v_out, g_out, bonus_out, lw_out, kd_out, bd_out, ext_scr):
    tt = z_ref.shape[1]
    z = z_ref[0]
    _fill_halo(ext_scr, z, zp_ref[0], zn_ref[0], tt)
    z_prev = ext_scr[pl.ds(SUBLANES - 1, tt), :]
    z_next = ext_scr[pl.ds(SUBLANES + 1, tt), :]
    zs = z + (0.5 * (z_prev + z_next) - z) * mu_ref[...]
    W = A_WIDTH
    zr, zk, zv = zs[:, 0:W], zs[:, W:2 * W], zs[:, 2 * W:3 * W]
    zwa = zs[:, 3 * W:3 * W + LANES]
    zg = zs[:, 3 * W + LANES:3 * W + 2 * LANES]
    ones_bd = ones_ref[...]
    kk_raw = zk * kk_ref[...]
    kk = kk_raw * lax.rsqrt(_seg_sum(kk_raw * kk_raw, ones_bd) + 1e-6)
    lane = lax.broadcasted_iota(jnp.int32, zwa.shape, 1)
    lhs = jnp.where(lane < A_RANK_W, jnp.tanh(zwa), zwa)
    k_sum = jnp.zeros_like(zk)
    for d in range(2):
        up = _dot_hi(lhs, up_ref[d])
        wl = w0_ref[d:d + 1, :] + up[:, 0:W]
        w_log = -_softplus(-wl) - 0.5
        lw_out[d, 0] = -jnp.exp(w_log)
        a = _sigmoid(a0_ref[d:d + 1, :] + up[:, W:2 * W])
        k_d = zk * (1.0 + (a - 1.0) * ka_ref[...])
        kd_out[d, 0] = k_d
        bd_out[d, 0] = kk * a
        k_sum = k_sum + k_d
    r_out[0] = zr
    kk_out[0] = kk
    v_out[0] = zv
    g_out[0] = _dot_hi(_sigmoid(zg), gup_ref[...])
    bonus_out[0] = _seg_sum(zr * k_sum * rk_ref[...], ones_bd) * zv


def _rwkv_prep(z_a, mu, w0, a0, up_comb, g_up, k_k, k_a, r_k, ones_bd):
    B, T, _ = z_a.shape
    tt = min(256, T)
    W = A_WIDTH
    full = lambda shape: pl.BlockSpec(shape, lambda b, i: (0,) * len(shape))
    tok = pl.BlockSpec((1, tt, W), lambda b, i: (b, i, 0))
    tok2 = pl.BlockSpec((2, 1, tt, W), lambda b, i: (0, b, i, 0))
    s1 = jax.ShapeDtypeStruct((B, T, W), F32)
    s2 = jax.ShapeDtypeStruct((2, B, T, W), F32)
    return pl.pallas_call(
        _rwkv_prep_kernel,
        grid=(B, T // tt),
        in_specs=_halo_specs(tt, A_COLS, 0, T) + [
            full((1, A_COLS)), full((2, W)), full((2, W)), full((2, LANES, 2 * W)),
            full((A_RANK_G, W)), full((1, W)), full((1, W)), full((1, W)), full((W, W))],
        out_specs=[tok, tok, tok, tok, tok, tok2, tok2, tok2],
        out_shape=[s1, s1, s1, s1, s1, s2, s2, s2],
        scratch_shapes=[pltpu.VMEM((tt + 2 * SUBLANES, A_COLS), F32)],
        compiler_params=_params(("parallel", "parallel")),
        name="rwkv_prep",
    )(z_a, z_a, z_a, mu, w0, a0, up_comb, g_up, k_k, k_a, r_k, ones_bd)


def _chunk_masks(reverse):
    n = 2 * CHUNK
    row = lax.broadcasted_iota(jnp.int32, (n, n), 0)
    col = lax.broadcasted_iota(jnp.int32, (n, n), 1)
    same = (row // CHUNK) == (col // CHUNK)
    ti, tj = row % CHUNK, col % CHUNK
    if reverse:
        return same & (ti < tj), same & (ti <= tj), row == col
    return same & (ti > tj), same & (ti >= tj), row == col


def _cumsum_matrix(reverse):
    row = lax.broadcasted_iota(jnp.int32, (CHUNK, CHUNK), 0)
    col = lax.broadcasted_iota(jnp.int32, (CHUNK, CHUNK), 1)
    return ((row <= col) if reverse else (row >= col)).astype(F32)


def _rwkv_chunk_operands(r, kk, v, k, b, cs, lw, reverse):
    C = CHUNK
    cs_end = cs[0:1, :] if reverse else cs[C - 1:C, :]
    m0 = lax.broadcasted_iota(jnp.int32, (C, LANES), 1) < A_HEAD_DIM

    def stack(x):
        return jnp.concatenate([jnp.where(m0, x, 0.0), jnp.where(m0, 0.0, x)], axis=0)

    g_inv = jnp.exp(-cs)
    g_tail = jnp.exp(cs_end - cs)
    strict, incl, _ = _chunk_masks(reverse)
    return dict(rg=stack(r * jnp.exp(cs)), kkg=stack(kk * jnp.exp(cs - lw)), ki=stack(k * g_inv),
                bi=stack(b * g_inv), kt=stack(k * g_tail), bt=stack(b * g_tail), vs=stack(v),
                g_end=jnp.exp(cs_end), strict=strict, incl=incl)


def _rwkv_chunks(ops, hts):
    C = CHUNK
    n = 2 * C
    eye = _chunk_masks(False)[2].astype(F32)
    ps = [_dot_nt(jnp.concatenate([o["kkg"], o["rg"]], axis=0), jnp.concatenate([o["bi"], o["ki"]], axis=0))
          for o in ops]
    a_ab = [jnp.where(o["strict"], p[0:n, 0:n], 0.0) for o, p in zip(ops, ps)]
    a_ak = [jnp.where(o["strict"], p[0:n, n:2 * n], 0.0).astype(BF16) for o, p in zip(ops, ps)]
    a_rb = [jnp.where(o["incl"], p[n:2 * n, 0:n], 0.0).astype(BF16) for o, p in zip(ops, ps)]
    a_rk = [jnp.where(o["incl"], p[n:2 * n, n:2 * n], 0.0).astype(BF16) for o, p in zip(ops, ps)]
    vsb = [o["vs"].astype(BF16) for o in ops]
    akv = [jnp.dot(a, v, preferred_element_type=F32) for a, v in zip(a_ak, vsb)]
    arkv = [jnp.dot(a, v, preferred_element_type=F32) for a, v in zip(a_rk, vsb)]
    tinv = _tri_inverse(a_ab, eye)
    wu = [_dot(t, jnp.concatenate([o["kkg"], x], axis=1)).astype(BF16)
          for t, o, x in zip(tinv, ops, akv)]
    r2 = [jnp.dot(a, w, preferred_element_type=F32) for a, w in zip(a_rb, wu)]
    btb = [o["bt"].astype(BF16) for o in ops]
    btw = [_dot_tn(b, w[:, 0:LANES]) for b, w in zip(btb, wu)]
    hloc = [_dot_tn(v, o["kt"]) - _dot_tn(w[:, LANES:2 * LANES], b)
            for v, o, w, b in zip(vsb, ops, wu, btb)]
    ys, hts_new = [], []
    for o, r, yl, ht, bw, hl in zip(ops, r2, arkv, hts, btw, hloc):
        q_s = o["rg"] - r[:, 0:LANES]
        yloc_s = yl - r[:, LANES:2 * LANES]
        ys.append(yloc_s[0:C] + yloc_s[C:n] + _dot_nt(q_s[0:C] + q_s[C:n], ht))
        hts_new.append(ht * o["g_end"] - _dot_nt(ht, bw) + hl)
    return ys, hts_new


def _rwkv_scan_kernel(rf, kkf, vf, lwf, kf, bf, rb, kkb, vb, lwb, kb, bb, yf_out, yb_out, h_scr):
    @pl.when(pl.program_id(1) == 0)
    def _():
        h_scr[...] = jnp.zeros_like(h_scr)

    n_pairs = A_WIDTH // LANES
    sls = [slice(i * LANES, (i + 1) * LANES) for i in range(n_pairs)]
    ops, hts = [], []
    for d, (r_ref, kk_ref, v_ref, lw_ref, k_ref, b_ref) in enumerate(
            ((rf, kkf, vf, lwf, kf, bf), (rb, kkb, vb, lwb, kb, bb))):
        reverse = d == 1
        lw_all = lw_ref[0, 0]
        cs_all = _dot_hi(_cumsum_matrix(reverse), lw_all)
        ops += [_rwkv_chunk_operands(r_ref[0, :, sl], kk_ref[0, :, sl], v_ref[0, :, sl], k_ref[0, 0, :, sl],
                                     b_ref[0, 0, :, sl], cs_all[:, sl], lw_all[:, sl], reverse) for sl in sls]
        hts += [h_scr[d, i] for i in range(n_pairs)]
    ys, hts = _rwkv_chunks(ops, hts)
    for d, y_out in enumerate((yf_out, yb_out)):
        for i, sl in enumerate(sls):
            h_scr[d, i] = hts[d * n_pairs + i]
            y_out[0, :, sl] = ys[d * n_pairs + i]


def _rwkv_scan(r, kk, v, lw, kd, bd):
    B, T, W = r.shape
    N = T // CHUNK
    fwd = pl.BlockSpec((1, CHUNK, W), lambda b, c: (b, c, 0))
    bwd = pl.BlockSpec((1, CHUNK, W), lambda b, c: (b, N - 1 - c, 0))
    fwd2 = pl.BlockSpec((1, 1, CHUNK, W), lambda b, c: (0, b, c, 0))
    bwd2 = pl.BlockSpec((1, 1, CHUNK, W), lambda b, c: (1, b, N - 1 - c, 0))
    out = jax.ShapeDtypeStruct((B, T, W), F32)
    return pl.pallas_call(
        _rwkv_scan_kernel,
        grid=(B, N),
        in_specs=[fwd, fwd, fwd, fwd2, fwd2, fwd2, bwd, bwd, bwd, bwd2, bwd2, bwd2],
        out_specs=[fwd, bwd],
        out_shape=[out, out],
        scratch_shapes=[pltpu.VMEM((2, W // LANES, LANES, LANES), F32)],
        compiler_params=_params(("parallel", "arbitrary")),
        name="rwkv_scan",
    )(r, kk, v, lw, kd, bd, r, kk, v, lw, kd, bd)


def _gdn_prep_kernel(z_ref, zp_ref, zn_ref, small_ref, cw_ref, alog_ref, dtb_ref,
                     q_out, k_out, v_out, bg_out, ext_scr):
    tt = z_ref.shape[1]
    _fill_halo(ext_scr, z_ref[0], zp_ref[0], zn_ref[0], tt)
    half = B_CONV // 2
    acc = None
    for j in range(B_CONV):
        term = ext_scr[pl.ds(SUBLANES - half + j, tt), :] * cw_ref[j:j + 1, :]
        acc = term if acc is None else acc + term
    qkv = _silu(acc)
    for h in range(B_HEADS):
        for idx, out in enumerate((q_out, k_out)):
            x = qkv[:, idx * B_WIDTH + h * B_HEAD_DIM: idx * B_WIDTH + (h + 1) * B_HEAD_DIM]
            xn = x * lax.rsqrt(jnp.sum(x * x, axis=-1, keepdims=True) + 1e-6)
            if idx == 0:
                xn = xn * (B_HEAD_DIM ** -0.5)
            out[0, :, h * B_HEAD_DIM:(h + 1) * B_HEAD_DIM] = xn
    v_out[0] = qkv[:, 2 * B_WIDTH:3 * B_WIDTH]
    s = small_ref[0]
    lane = lax.broadcasted_iota(jnp.int32, s.shape, 1)
    beta = _sigmoid(s)
    g = -jnp.exp(alog_ref[...]) * _softplus(s + dtb_ref[...])
    bg_out[0] = jnp.where(lane < 2 * B_HEADS, beta, g)


def _gdn_prep(z_b, conv_w, alog_vec, dtb_vec):
    B, T, _ = z_b.shape
    tt = min(256, T)
    Wq = 3 * B_WIDTH
    full = lambda shape: pl.BlockSpec(shape, lambda b, i: (0,) * len(shape))
    tok = pl.BlockSpec((1, tt, B_WIDTH), lambda b, i: (b, i, 0))
    small = pl.BlockSpec((1, tt, SMALL_COLS), lambda b, i: (b, i, Wq // SMALL_COLS))
    s1 = jax.ShapeDtypeStruct((B, T, B_WIDTH), F32)
    return pl.pallas_call(
        _gdn_prep_kernel,
        grid=(B, T // tt),
        in_specs=_halo_specs(tt, Wq, 0, T) + [small, full((B_CONV, Wq)), full((1, SMALL_COLS)),
                                            full((1, SMALL_COLS))],
        out_specs=[tok, tok, tok, pl.BlockSpec((1, tt, SMALL_COLS), lambda b, i: (b, i, 0))],
        out_shape=[s1, s1, s1, jax.ShapeDtypeStruct((B, T, SMALL_COLS), F32)],
        scratch_shapes=[pltpu.VMEM((tt + 2 * SUBLANES, Wq), F32)],
        compiler_params=_params(("parallel", "parallel")),
        name="gdn_prep",
    )(z_b, z_b, z_b, z_b, conv_w, alog_vec, dtb_vec)


def _gdn_chunk_operands(q0, q1, k0, k1, v0, v1, gc2, gct2, bg2, j0, j1, reverse):
    C = CHUNK
    n = 2 * C
    strict, incl, _ = _chunk_masks(reverse)
    top = lax.broadcasted_iota(jnp.int32, (n, 1), 0) < C
    left = lax.broadcasted_iota(jnp.int32, (1, n), 1) < C
    g0, g1 = j0 + 2 * B_HEADS, j1 + 2 * B_HEADS
    gcol = jnp.where(top, gc2[:, g0:g0 + 1], gc2[:, g1:g1 + 1])
    grow = jnp.where(left, gct2[g0:g0 + 1, :], gct2[g1:g1 + 1, :])
    beta = jnp.where(top, bg2[:, j0:j0 + 1], bg2[:, j1:j1 + 1])
    e = 0 if reverse else C - 1
    glast = jnp.where(top, gc2[e:e + 1, g0:g0 + 1], gc2[e:e + 1, g1:g1 + 1])
    ks = jnp.concatenate([k0, k1], axis=0)
    qs = jnp.concatenate([q0, q1], axis=0)
    vs = jnp.concatenate([v0, v1], axis=0)
    egc = jnp.exp(gcol)
    kb = ks * beta
    return dict(gam=jnp.exp(jnp.where(incl, gcol - grow, NEG_INF)), strict=strict, ks=ks,
                kbq=jnp.concatenate([kb, qs], axis=0),
                rhs=jnp.concatenate([vs * beta, kb * egc], axis=1),
                qh=(qs * egc).astype(BF16), ktail=(ks * jnp.exp(glast - gcol)).astype(BF16),
                decay=jnp.exp(glast))


def _gdn_chunks(ops, states):
    C = CHUNK
    n = 2 * C
    eye = _chunk_masks(False)[2].astype(F32)
    ps = [_dot3(o["kbq"], o["ks"], (((1,), (1,)), ((), ()))) for o in ops]
    lower = [jnp.where(o["strict"], p[0:n] * o["gam"], 0.0) for o, p in zip(ops, ps)]
    aqk = [(p[n:2 * n] * o["gam"]).astype(BF16) for o, p in zip(ops, ps)]
    tinv = [t.astype(BF16) for t in _tri_inverse(lower, eye)]
    uw = [jnp.dot(t, o["rhs"].astype(BF16), preferred_element_type=F32) for t, o in zip(tinv, ops)]
    resid = [o["rhs"] - x - _dot3(l, x) for o, l, x in zip(ops, lower, uw)]
    uw = [x + jnp.dot(t, r.astype(BF16), preferred_element_type=F32) for x, t, r in zip(uw, tinv, resid)]
    sb = [(s0.astype(BF16), s1.astype(BF16)) for s0, s1 in states]
    ws = [jnp.concatenate([jnp.dot(u[0:C, LANES:].astype(BF16), s0, preferred_element_type=F32),
                           jnp.dot(u[C:n, LANES:].astype(BF16), s1, preferred_element_type=F32)], axis=0)
          for u, (s0, s1) in zip(uw, sb)]
    qss = [jnp.concatenate([jnp.dot(o["qh"][0:C], s0, preferred_element_type=F32),
                            jnp.dot(o["qh"][C:n], s1, preferred_element_type=F32)], axis=0)
           for o, (s0, s1) in zip(ops, sb)]
    v_new = [(u[:, 0:LANES] - w).astype(BF16) for u, w in zip(uw, ws)]
    outs = [q + jnp.dot(a, v, preferred_element_type=F32) for q, a, v in zip(qss, aqk, v_new)]
    new_states = []
    for o, v, (s0, s1) in zip(ops, v_new, states):
        new_states.append((s0 * o["decay"][0:1, :] + _dot_tn(o["ktail"][0:C], v[0:C]),
                           s1 * o["decay"][C:C + 1, :] + _dot_tn(o["ktail"][C:n], v[C:n])))
    return [(x[0:C], x[C:n]) for x in outs], new_states


def _gdn_scan_kernel(qf, kf, vf, bgf, qb, kb, vb, bgb, of_out, ob_out, s_scr):
    @pl.when(pl.program_id(1) == 0)
    def _():
        s_scr[...] = jnp.zeros_like(s_scr)

    D = B_HEAD_DIM
    n_pairs = B_HEADS // 2
    sl = lambda h: slice(h * D, (h + 1) * D)
    ops, states = [], []
    for d, (q_ref, k_ref, v_ref, bg_ref) in enumerate(((qf, kf, vf, bgf), (qb, kb, vb, bgb))):
        reverse = d == 1
        bg = bg_ref[0]
        gc = _dot_hi(_cumsum_matrix(reverse), bg)
        bg2 = jnp.concatenate([bg, bg], axis=0)
        gc2 = jnp.concatenate([gc, gc], axis=0)
        gct2 = gc2.T
        for i in range(n_pairs):
            h0, h1 = 2 * i, 2 * i + 1
            ops.append(_gdn_chunk_operands(
                q_ref[0, :, sl(h0)], q_ref[0, :, sl(h1)], k_ref[0, :, sl(h0)], k_ref[0, :, sl(h1)],
                v_ref[0, :, sl(h0)], v_ref[0, :, sl(h1)], gc2, gct2, bg2,
                d * B_HEADS + h0, d * B_HEADS + h1, reverse))
            states.append((s_scr[d, h0], s_scr[d, h1]))
    outs, states = _gdn_chunks(ops, states)
    for d, o_out in enumerate((of_out, ob_out)):
        for i in range(n_pairs):
            h0, h1 = 2 * i, 2 * i + 1
            (o0, o1), (s0, s1) = outs[d * n_pairs + i], states[d * n_pairs + i]
            s_scr[d, h0] = s0
            s_scr[d, h1] = s1
            o_out[0, :, sl(h0)] = o0
            o_out[0, :, sl(h1)] = o1


def _gdn_scan(q, k, v, bg):
    B, T, W = q.shape
    N = T // CHUNK
    fwd = pl.BlockSpec((1, CHUNK, W), lambda b, c: (b, c, 0))
    bwd = pl.BlockSpec((1, CHUNK, W), lambda b, c: (b, N - 1 - c, 0))
    sfwd = pl.BlockSpec((1, CHUNK, SMALL_COLS), lambda b, c: (b, c, 0))
    sbwd = pl.BlockSpec((1, CHUNK, SMALL_COLS), lambda b, c: (b, N - 1 - c, 0))
    out = jax.ShapeDtypeStruct((B, T, W), F32)
    return pl.pallas_call(
        _gdn_scan_kernel,
        grid=(B, N),
        in_specs=[fwd, fwd, fwd, sfwd, bwd, bwd, bwd, sbwd],
        out_specs=[fwd, bwd],
        out_shape=[out, out],
        scratch_shapes=[pltpu.VMEM((2, B_HEADS, B_HEAD_DIM, B_HEAD_DIM), F32)],
        compiler_params=_params(("parallel", "arbitrary")),
        name="gdn_scan",
    )(q, k, v, bg, q, k, v, bg)


def _pack_bf16_pairs(x):
    n = x.shape[1] // 2
    hi = lax.bitcast_convert_type(x[:, :n].astype(BF16).astype(F32), jnp.int32)
    lo = lax.bitcast_convert_type(x[:, n:].astype(BF16).astype(F32), jnp.int32)
    return hi | lax.shift_right_logical(lo, 16)


def _unpack_bf16_pairs(p):
    a = lax.bitcast_convert_type(p & jnp.int32(-65536), F32)
    b = lax.bitcast_convert_type(lax.shift_left(p, 16), F32)
    return a.astype(BF16), b.astype(BF16)


def _route(scores, biased, base):
    tt = scores.shape[-1]
    shape3 = (N_GROUPS, GROUP_SIZE, tt)
    s3 = scores.reshape(shape3)
    b3 = biased.reshape(shape3)
    jid = lax.broadcasted_iota(jnp.int32, shape3, 1).astype(F32)
    gid = lax.broadcasted_iota(jnp.int32, shape3, 0).astype(F32)
    m1 = jnp.max(b3, axis=1, keepdims=True)
    first = jnp.min(jnp.where(b3 == m1, jid, float(GROUP_SIZE)), axis=1, keepdims=True)
    m2 = jnp.max(jnp.where(jid == first, NEG_INF, b3), axis=1, keepdims=True)
    gs = m1 + m2
    grp = lax.broadcasted_iota(jnp.int32, (N_GROUPS, 1, tt), 0).astype(F32)
    keep = jnp.zeros((N_GROUPS, 1, tt), F32)
    for _ in range(TOPK_GROUPS):
        m = jnp.max(gs, axis=0, keepdims=True)
        pick = grp == jnp.min(jnp.where(gs == m, grp, float(N_GROUPS)), axis=0, keepdims=True)
        keep = jnp.where(pick, 1.0, keep)
        gs = jnp.where(pick, NEG_INF, gs)
    work = jnp.where(keep > 0.0, b3, NEG_INF)
    eid = gid * float(GROUP_SIZE) + jid
    chosen = jnp.zeros(shape3, F32)
    sum01 = lambda a: jnp.sum(jnp.sum(a, axis=1, keepdims=True), axis=0, keepdims=True)
    ids, raw = [], []
    for _ in range(TOP_K):
        m = jnp.max(jnp.max(work, axis=1, keepdims=True), axis=0, keepdims=True)
        cand = jnp.where(work == m, eid, float(N_EXPERTS))
        first = jnp.min(jnp.min(cand, axis=1, keepdims=True), axis=0, keepdims=True)
        pick = eid == first
        ids.append(first)
        raw.append(sum01(jnp.where(pick, s3, 0.0)))
        chosen = jnp.where(pick, 1.0, chosen)
        work = jnp.where(pick, NEG_INF, work)
    total = raw[0]
    for r in raw[1:]:
        total = total + r
    flat = chosen.reshape(N_EXPERTS, tt)
    earlier = (lax.broadcasted_iota(jnp.int32, (tt, tt), 0)
               < lax.broadcasted_iota(jnp.int32, (tt, tt), 1)).astype(BF16)
    prefix = (jnp.dot(flat.astype(BF16), earlier, preferred_element_type=F32) + base).reshape(shape3)
    row = lax.broadcasted_iota(jnp.int32, (SUBLANES, tt), 0)
    idx8 = jnp.zeros((SUBLANES, tt), F32)
    w8 = jnp.zeros((SUBLANES, tt), F32)
    pos8 = jnp.zeros((SUBLANES, tt), F32)
    for k in range(TOP_K):
        rank = sum01(jnp.where(eid == ids[k], prefix, 0.0))
        idx8 = jnp.where(row == k, ids[k].reshape(1, tt), idx8)
        w8 = jnp.where(row == k, (raw[k] / total * ROUTED_SCALE).reshape(1, tt), w8)
        pos8 = jnp.where(row == k, rank.reshape(1, tt), pos8)
    counts = jnp.sum(flat, axis=1, keepdims=True)
    return idx8.astype(jnp.int32), w8, pos8.astype(jnp.int32), counts


def _mixer_post_kernel(yf, yb, bonus, g, of, ob, zz, zga, zgb, x_ref, gt1, sc2, sh2,
                       gnw, gnb, ones_ref, aproj, bng, bproj, wout, n2g, rwt, rbias,
                       x1_out, h2_out, idx_out, w_out_ref, pos_out, counts_out):
    @pl.when((pl.program_id(0) == 0) & (pl.program_id(1) == 0))
    def _():
        counts_out[...] = jnp.zeros_like(counts_out)

    ones_bd = ones_ref[...]
    y = yf[0] + yb[0]
    mean = _seg_sum(y, ones_bd) * (1.0 / A_HEAD_DIM)
    dlt = y - mean
    var = _seg_sum(dlt * dlt, ones_bd) * (1.0 / A_HEAD_DIM)
    yn = dlt * lax.rsqrt(var + A_GN_EPS) * gnw[...] + gnb[...]
    y_a = _dot((yn + bonus[0]) * g[0], aproj[...])

    o = of[0] + ob[0]
    z = zz[0]
    parts = []
    for h in range(B_HEADS):
        oh = o[:, h * B_HEAD_DIM:(h + 1) * B_HEAD_DIM]
        parts.append(oh * lax.rsqrt(jnp.mean(oh * oh, axis=-1, keepdims=True) + EPS))
    on = jnp.concatenate(parts, axis=1) * bng[...] * _silu(z)
    y_b = _dot(on, bproj[...])

    u = _sigmoid(zga[0]) * y_a + _sigmoid(zgb[0]) * y_b
    x1 = x_ref[0] + gt1[0] * _dot(u, wout[...])
    x1_out[0] = x1
    h2 = _rms_mod(x1, n2g[...], sc2[0], sh2[0])
    h2_out[0] = _pack_bf16_pairs(h2)
    logits_t = _dot_nt(rwt[...], h2)
    scores = _sigmoid(logits_t)
    base = counts_out[:, 0:1]
    idx8, w8, pos8, counts = _route(scores, scores + rbias[...], base)
    idx_out[0] = idx8
    w_out_ref[0] = w8
    pos_out[0] = pos8
    counts_out[...] = counts_out[...] + counts


def _mixer_post(yf, yb, bonus, g, of, ob, z_c, x, gt1, sc2, sh2, gn_w, gn_b, ones_bd, a_proj, bng,
                b_proj, w_out, n2g, rwt, rbias):
    B, T, D = x.shape
    tt = min(256, T)
    full = lambda shape: pl.BlockSpec(shape, lambda b, i: (0,) * len(shape))
    tokw = lambda w, col=0: pl.BlockSpec((1, tt, w), lambda b, i: (b, i, col))
    modrow = pl.BlockSpec((1, 1, D), lambda b, i: (b, 0, 0))
    pick = pl.BlockSpec((1, SUBLANES, tt), lambda b, i: (b, 0, i))
    W = A_WIDTH
    return pl.pallas_call(
        _mixer_post_kernel,
        grid=(B, T // tt),
        in_specs=[tokw(W), tokw(W), tokw(W), tokw(W), tokw(D), tokw(D),
                  tokw(D, 0), tokw(D, 1), tokw(D, 2), tokw(D), modrow, modrow, modrow,
                  full((1, W)), full((1, W)), full((W, W)), full((W, D)), full((1, D)), full((D, D)),
                  full((D, D)), full((1, D)), full((N_EXPERTS, D)), full((N_EXPERTS, 1))],
        out_specs=[tokw(D), tokw(D // 2), pick, pick, pick,
                   pl.BlockSpec((N_EXPERTS, LANES), lambda b, i: (0, 0))],
        out_shape=[jax.ShapeDtypeStruct((B, T, D), F32), jax.ShapeDtypeStruct((B, T, D // 2), jnp.int32),
                   jax.ShapeDtypeStruct((B, SUBLANES, T), jnp.int32),
                   jax.ShapeDtypeStruct((B, SUBLANES, T), F32),
                   jax.ShapeDtypeStruct((B, SUBLANES, T), jnp.int32),
                   jax.ShapeDtypeStruct((N_EXPERTS, LANES), F32)],
        compiler_params=_params(("arbitrary", "arbitrary")),
        name="mixer_post",
    )(yf, yb, bonus, g, of, ob, z_c, z_c, z_c, x, gt1, sc2, sh2, gn_w, gn_b, ones_bd, a_proj, bng,
      b_proj, w_out, n2g, rwt, rbias)


def _gather_rows(table, idx):
    n_rows, width = idx.shape[0], table.shape[1]
    per_worker = n_rows // SC_WORKERS
    n_windows = per_worker // GATHER_WINDOW
    mesh = plsc.VectorSubcoreMesh(core_axis_name="c", subcore_axis_name="s")

    @functools.partial(
        pl.kernel, mesh=mesh, out_type=jax.ShapeDtypeStruct((n_rows, width), table.dtype),
        scratch_types=[pltpu.VMEM((GATHER_WINDOW,), jnp.int32),
                       pltpu.VMEM((GATHER_WINDOW, width), table.dtype),
                       pltpu.SemaphoreType.DMA])
    def gather(table_hbm, idx_hbm, out_hbm, idx_v, rows_v, sem):
        worker = lax.axis_index("s") * SC_CORES + lax.axis_index("c")

        @pl.loop(0, n_windows)
        def _(j):
            off = pl.multiple_of(worker * per_worker + j * GATHER_WINDOW, GATHER_WINDOW)
            pltpu.sync_copy(idx_hbm.at[pl.ds(off, GATHER_WINDOW)], idx_v)
            pltpu.async_copy(table_hbm.at[idx_v], rows_v, sem).wait()
            pltpu.sync_copy(rows_v, out_hbm.at[pl.ds(off, GATHER_WINDOW)])

    return gather(table, idx)


def _scatter_rows(rows, dest_w, n_out):
    n_tok, width = rows.shape
    per_worker = n_tok // GATHER_WINDOW // SC_WORKERS
    mesh = plsc.VectorSubcoreMesh(core_axis_name="c", subcore_axis_name="s")

    @functools.partial(
        pl.kernel, mesh=mesh, out_type=jax.ShapeDtypeStruct((n_out, width), rows.dtype),
        scratch_types=[pltpu.VMEM((SUBLANES, GATHER_WINDOW), jnp.int32),
                       pltpu.VMEM((GATHER_WINDOW, width), rows.dtype)])
    def scatter(rows_hbm, dest_hbm, out_hbm, idx_v, rows_v):
        worker = lax.axis_index("s") * SC_CORES + lax.axis_index("c")

        @pl.loop(0, per_worker)
        def _(j):
            win = worker * per_worker + j
            pltpu.sync_copy(dest_hbm.at[win], idx_v)
            pltpu.sync_copy(rows_hbm.at[pl.ds(pl.multiple_of(win * GATHER_WINDOW, GATHER_WINDOW),
                                              GATHER_WINDOW)], rows_v)
            for k in range(TOP_K):
                pltpu.sync_copy(rows_v, out_hbm.at[idx_v.at[k]])

    return scatter(rows, dest_w)


def _dest_kernel(start_ref, idx_ref, pos_ref, dest_ref):
    idx = idx_ref[0]
    dest = jnp.zeros_like(idx)
    for e in range(N_EXPERTS):
        dest = jnp.where(idx == e, start_ref[e], dest)
    dest_ref[0] = dest + pos_ref[0]


def _dest_rows(pad_start, idx, pos):
    B, _, T = idx.shape
    tt = min(2048, T)
    spec = pl.BlockSpec((1, SUBLANES, tt), lambda b, i, start: (b, 0, i))
    return pl.pallas_call(
        _dest_kernel,
        grid_spec=pltpu.PrefetchScalarGridSpec(num_scalar_prefetch=1, grid=(B, T // tt),
                                               in_specs=[spec, spec], out_specs=spec),
        out_shape=jax.ShapeDtypeStruct(idx.shape, jnp.int32),
        compiler_params=_params(("parallel", "parallel")),
        name="dest_rows",
    )(pad_start, idx, pos)


def _expert_kernel(be_ref, valid_ref, xs_ref, wg, wu, wd, ys_ref):
    n_valid = valid_ref[pl.program_id(0)]

    @pl.when(n_valid > 0)
    def _():
        half = D_MODEL // 2
        row = lax.broadcasted_iota(jnp.int32, xs_ref.shape, 0)
        a, b = _unpack_bf16_pairs(jnp.where(row < n_valid, xs_ref[...], 0))
        mm = lambda w: (jnp.dot(a, w[0, 0:half, :], preferred_element_type=F32)
                        + jnp.dot(b, w[0, half:D_MODEL, :], preferred_element_type=F32))
        hid = _silu(mm(wg)) * mm(wu)
        ys_ref[...] = _pack_bf16_pairs(_dot(hid, wd[0]))


def _experts(block_expert, block_valid, xs, wg, wu, wd):
    n_rows, half = xs.shape
    rows = pl.BlockSpec((MOE_BLOCK, half), lambda i, be, valid: (i, 0))
    wspec = lambda shape: pl.BlockSpec(shape, lambda i, be, valid: (be[i], 0, 0))
    return pl.pallas_call(
        _expert_kernel,
        grid_spec=pltpu.PrefetchScalarGridSpec(
            num_scalar_prefetch=2, grid=(n_rows // MOE_BLOCK,),
            in_specs=[rows, wspec((1, D_MODEL, D_EXPERT)), wspec((1, D_MODEL, D_EXPERT)),
                      wspec((1, D_EXPERT, D_MODEL))],
            out_specs=rows),
        out_shape=jax.ShapeDtypeStruct((n_rows, half), jnp.int32),
        compiler_params=_params(("arbitrary",)),
        name="experts",
    )(block_expert, block_valid, xs, wg, wu, wd)


def _moe_final_kernel(yg_ref, w_ref, h_ref, x1_ref, gt2, fg, sg, su, sd, o_ref):
    half = D_MODEL // 2
    a, b = _unpack_bf16_pairs(h_ref[0])
    mm = lambda w: (jnp.dot(a, w[0:half, :], preferred_element_type=F32)
                    + jnp.dot(b, w[half:D_MODEL, :], preferred_element_type=F32))
    shared = _dot(_silu(mm(sg)) * mm(su), sd[...])
    w = w_ref[0]
    lo = hi = None
    for k in range(TOP_K):
        ya, yb = _unpack_bf16_pairs(yg_ref[k, 0])
        wk = w[:, k:k + 1]
        lo = ya.astype(F32) * wk if lo is None else lo + ya.astype(F32) * wk
        hi = yb.astype(F32) * wk if hi is None else hi + yb.astype(F32) * wk
    x2 = x1_ref[0] + gt2[0] * (jnp.concatenate([lo, hi], axis=1) + shared)
    o_ref[0] = x2 * lax.rsqrt(jnp.mean(x2 * x2, axis=-1, keepdims=True) + EPS) * fg[...]


def _moe_final(yg, w_t, h2p, x1, gt2, fg, sg, su, sd):
    B, T, D = x1.shape
    tm = min(512, T)
    tok = lambda w: pl.BlockSpec((1, tm, w), lambda b, i: (b, i, 0))
    full = lambda shape: pl.BlockSpec(shape, lambda b, i: (0,) * len(shape))
    return pl.pallas_call(
        _moe_final_kernel,
        grid=(B, T // tm),
        in_specs=[pl.BlockSpec((TOP_K, 1, tm, D // 2), lambda b, i: (0, b, i, 0)), tok(SUBLANES),
                  tok(D // 2), tok(D), pl.BlockSpec((1, 1, D), lambda b, i: (b, 0, 0)), full((1, D)),
                  full((D, D_EXPERT)), full((D, D_EXPERT)), full((D_EXPERT, D))],
        out_specs=tok(D),
        out_shape=jax.ShapeDtypeStruct((B, T, D), F32),
        compiler_params=_params(("parallel", "parallel")),
        name="moe_final",
    )(yg, w_t, h2p, x1, gt2, fg, sg, su, sd)


def _moe_routed(h2p, idx, w, pos, counts, x1, gt2, p):
    B, T, half = h2p.shape
    n_slots = B * T * TOP_K
    n_rows = n_slots + N_EXPERTS * MOE_BLOCK
    cnt = counts[:, 0].astype(jnp.int32)
    padded = (cnt + MOE_BLOCK - 1) // MOE_BLOCK * MOE_BLOCK
    pad_end = jnp.cumsum(padded)
    pad_start = pad_end - padded
    dest = _dest_rows(pad_start, idx, pos)
    starts = jnp.arange(n_rows // MOE_BLOCK, dtype=jnp.int32) * MOE_BLOCK
    block_expert = jnp.minimum(jnp.sum((pad_end[None, :] <= starts[:, None]).astype(jnp.int32), axis=1),
                               N_EXPERTS - 1)
    block_valid = jnp.clip((pad_start + cnt)[block_expert] - starts, 0, MOE_BLOCK)
    dest_w = jnp.swapaxes(dest.reshape(B, SUBLANES, T // GATHER_WINDOW, GATHER_WINDOW), 1, 2)
    xs = _scatter_rows(h2p.reshape(B * T, half), dest_w.reshape(-1, SUBLANES, GATHER_WINDOW), n_rows)
    ys = _experts(block_expert, block_valid, xs, p["wg"], p["wu"], p["wd"])
    yg = _gather_rows(ys, jnp.swapaxes(dest[:, :TOP_K], 0, 1).reshape(-1)).reshape(TOP_K, B, T, half)
    return _moe_final(yg, jnp.swapaxes(w, 1, 2), h2p, x1, gt2, p["fg"], p["sg"], p["su"], p["sd"])


def _prepare_weights(w_in, shift_mu, a_w0, a_w_up, a_a0, a_a_up, a_g_up, a_k_k, a_k_a, a_r_k, a_gn_w,
                     a_gn_b, a_proj, b_conv_w, b_a_log, b_dt_bias, b_norm_g, b_proj, w_out, router_w,
                     router_bias, exp_gate, exp_up, exp_down, sh_gate, sh_up, sh_down, norm1_g, norm2_g,
                     final_g):
    W = A_WIDTH
    w = w_in[0]
    c0 = A_COLS
    c1 = c0 + 3 * B_WIDTH
    c2 = c1 + B_WIDTH
    c3 = c2 + 4 * B_HEADS
    pad = jnp.zeros((D_MODEL, SMALL_COLS - 4 * B_HEADS), F32)
    w_a = w[:, :c0].astype(BF16)
    w_b = jnp.concatenate([w[:, c0:c1], w[:, c2:c3], pad], axis=1).astype(BF16)
    w_c = jnp.concatenate([w[:, c1:c2], w[:, c3:]], axis=1).astype(BF16)
    zeros = jnp.zeros((2, A_RANK_W, W), F32)
    up_comb = jnp.concatenate([jnp.concatenate([a_w_up[0], zeros], axis=2),
                               jnp.concatenate([zeros, a_a_up[0]], axis=2)], axis=1)
    head = jnp.arange(W) // A_HEAD_DIM
    ones_bd = (head[:, None] == head[None, :]).astype(BF16)
    small = lambda v: jnp.zeros((1, SMALL_COLS), F32).at[0, 2 * B_HEADS:4 * B_HEADS].set(v.reshape(-1))
    return dict(
        w_a=w_a, w_b=w_b, w_c=w_c, mu=shift_mu[0].reshape(1, A_COLS), w0=a_w0[0], a0=a_a0[0],
        up_comb=up_comb, g_up=a_g_up[0], k_k=a_k_k[0].reshape(1, W), k_a=a_k_a[0].reshape(1, W),
        r_k=a_r_k[0].reshape(1, W), ones_bd=ones_bd, gn_w=a_gn_w[0].reshape(1, W),
        gn_b=a_gn_b[0].reshape(1, W), a_proj=a_proj[0].astype(BF16), conv_w=b_conv_w[0],
        alog=small(b_a_log[0]), dtb=small(b_dt_bias[0]),
        bng=jnp.tile(b_norm_g[0], B_HEADS).reshape(1, B_WIDTH), b_proj=b_proj[0].astype(BF16),
        w_out=w_out[0].astype(BF16), rwt=router_w[0].T, rbias=router_bias[0].reshape(N_EXPERTS, 1),
        wg=exp_gate[0].astype(BF16), wu=exp_up[0].astype(BF16), wd=exp_down[0].astype(BF16),
        sg=sh_gate[0].astype(BF16), su=sh_up[0].astype(BF16), sd=sh_down[0].astype(BF16),
        n1g=norm1_g[0].reshape(1, D_MODEL), n2g=norm2_g[0].reshape(1, D_MODEL),
        fg=final_g.reshape(1, D_MODEL))


def _layer(x, mod, p):
    B = x.shape[0]
    sh1, sc1, gt1, sh2, sc2, gt2 = (m.reshape(B, 1, D_MODEL) for m in jnp.split(mod, 6, axis=-1))
    z_a = _inproj(x, p["n1g"], sc1, sh1, p["w_a"], 896)
    z_b = _inproj(x, p["n1g"], sc1, sh1, p["w_b"], 640)
    z_c = _inproj(x, p["n1g"], sc1, sh1, p["w_c"], 1024)
    r, kk, v, g, bonus, lw, kd, bd = _rwkv_prep(z_a, p["mu"], p["w0"], p["a0"], p["up_comb"], p["g_up"],
                                                p["k_k"], p["k_a"], p["r_k"], p["ones_bd"])
    yf, yb = _rwkv_scan(r, kk, v, lw, kd, bd)
    q, k, vv, bg = _gdn_prep(z_b, p["conv_w"], p["alog"], p["dtb"])
    of, ob = _gdn_scan(q, k, vv, bg)
    x1, h2p, idx, w, pos, counts = _mixer_post(yf, yb, bonus, g, of, ob, z_c, x, gt1, sc2, sh2, p["gn_w"],
                                               p["gn_b"], p["ones_bd"], p["a_proj"], p["bng"], p["b_proj"],
                                               p["w_out"], p["n2g"], p["rwt"], p["rbias"])
    return _moe_routed(h2p, idx, w, pos, counts, x1, gt2, p)


def kernel(x_prompt, x_sample, c_prompt, c_sample, ada_w, ada_b, norm1_g, norm2_g, w_in, shift_mu, a_w0, a_w_up, a_a0, a_a_up, a_g_up, a_k_k, a_k_a, a_r_k, a_gn_w, a_gn_b, a_proj, b_conv_w, b_a_log, b_dt_bias, b_norm_g, b_proj, w_out, router_w, router_bias, exp_gate, exp_up, exp_down, sh_gate, sh_up, sh_down, final_g):
    p = _prepare_weights(w_in, shift_mu, a_w0, a_w_up, a_a0, a_a_up, a_g_up, a_k_k, a_k_a, a_r_k, a_gn_w,
                         a_gn_b, a_proj, b_conv_w, b_a_log, b_dt_bias, b_norm_g, b_proj, w_out, router_w,
                         router_bias, exp_gate, exp_up, exp_down, sh_gate, sh_up, sh_down, norm1_g,
                         norm2_g, final_g)
    nb_p, nb_s = c_prompt.shape[0], c_sample.shape[0]
    rows = -(-(nb_p + nb_s) // SUBLANES) * SUBLANES
    c_all = jnp.concatenate([c_prompt, c_sample, jnp.zeros((rows - nb_p - nb_s, D_MODEL), F32)], axis=0)
    mod = _adaln_mod(c_all, ada_w[0], ada_b[0])
    y_prompt = _layer(x_prompt, mod[:nb_p], p)
    y_sample = _layer(x_sample, mod[nb_p:nb_p + nb_s], p)
    return (y_prompt, y_sample)
```

```python
import functools

import jax
import jax.numpy as jnp
from jax import lax
from jax.experimental import pallas as pl
from jax.experimental.pallas import tpu as pltpu
from jax.experimental.pallas import tpu_sc as plsc

F32 = jnp.float32
BF16 = jnp.bfloat16
HIGHEST = lax.Precision.HIGHEST

D_MODEL = 1024
A_HEADS = 8
A_HEAD_DIM = 64
A_WIDTH = A_HEADS * A_HEAD_DIM
A_RANK_W = 64
A_RANK_A = 64
A_RANK_G = 128
A_GN_EPS = 64e-5
A_COLS = 3 * A_WIDTH + A_RANK_W + A_RANK_A + A_RANK_G
B_HEADS = 8
B_HEAD_DIM = 128
B_WIDTH = B_HEADS * B_HEAD_DIM
B_CONV = 5
CHUNK = 64
N_EXPERTS = 64
TOP_K = 6
N_GROUPS = 8
TOPK_GROUPS = 4
GROUP_SIZE = N_EXPERTS // N_GROUPS
D_EXPERT = 256
ROUTED_SCALE = 2.5
EPS = 1e-6
LANES = 128
SUBLANES = 8
SMALL_COLS = LANES
MOE_BLOCK = 512
SC_CORES = 2
SC_SUBCORES = 16
SC_WORKERS = SC_CORES * SC_SUBCORES
GATHER_WINDOW = 64
VMEM_LIMIT = 56 * 1024 * 1024
NEG_INF = float("-inf")


def _dot(a, b):
    return jnp.dot(a.astype(BF16), b.astype(BF16), preferred_element_type=F32)


def _dot_nt(a, b):
    return lax.dot_general(a.astype(BF16), b.astype(BF16), (((1,), (1,)), ((), ())),
                           preferred_element_type=F32)


def _dot_tn(a, b):
    return lax.dot_general(a.astype(BF16), b.astype(BF16), (((0,), (0,)), ((), ())),
                           preferred_element_type=F32)


def _dot_hi(a, b):
    return jnp.dot(a, b, precision=HIGHEST, preferred_element_type=F32)


def _split(x):
    hi = x.astype(BF16)
    return hi, (x - hi.astype(F32)).astype(BF16)


def _dot3(a, b, dims=(((1,), (0,)), ((), ()))):
    ah, al = _split(a)
    bh, bl = _split(b)
    d = lambda u, v: lax.dot_general(u, v, dims, preferred_element_type=F32)
    return d(ah, bh) + (d(ah, bl) + d(al, bh))


def _seg_sum(x, ones_bd):
    hi = x.astype(BF16)
    lo = (x - hi.astype(F32)).astype(BF16)
    return (jnp.dot(hi, ones_bd, preferred_element_type=F32)
            + jnp.dot(lo, ones_bd, preferred_element_type=F32))


def _softplus(x):
    return jnp.maximum(x, 0.0) + jnp.log1p(jnp.exp(-jnp.abs(x)))


def _sigmoid(x):
    return 1.0 / (1.0 + jnp.exp(-x))


def _silu(x):
    return x * _sigmoid(x)


def _tri_inverse(a_list, eye):
    n = a_list[0].shape[0]
    xs = [(-a).astype(BF16) for a in a_list]
    ts = [eye - a for a in a_list]
    xs = [jnp.dot(x, x, preferred_element_type=F32).astype(BF16) for x in xs]
    power = 2
    while 2 * power < CHUNK:
        both = [jnp.dot(x, jnp.concatenate([t.astype(BF16), x], axis=1), preferred_element_type=F32)
                for t, x in zip(ts, xs)]
        ts = [t + b[:, 0:n] for t, b in zip(ts, both)]
        xs = [b[:, n:2 * n].astype(BF16) for b in both]
        power *= 2
    return [t + jnp.dot(x, t.astype(BF16), preferred_element_type=F32) for t, x in zip(ts, xs)]


def _params(sem):
    return pltpu.CompilerParams(dimension_semantics=sem, vmem_limit_bytes=VMEM_LIMIT)


def _mod_kernel(c_ref, w_ref, b_ref, o_ref):
    c = c_ref[...]
    o_ref[...] = _dot_hi(_silu(c), w_ref[...]) + b_ref[...]


def _adaln_mod(c, ada_w, ada_b):
    rows = c.shape[0]
    n = ada_w.shape[1]
    return pl.pallas_call(
        _mod_kernel,
        grid=(n // D_MODEL,),
        in_specs=[pl.BlockSpec((rows, D_MODEL), lambda j: (0, 0)),
                  pl.BlockSpec((D_MODEL, D_MODEL), lambda j: (0, j)),
                  pl.BlockSpec((1, D_MODEL), lambda j: (0, j))],
        out_specs=pl.BlockSpec((rows, D_MODEL), lambda j: (0, j)),
        out_shape=jax.ShapeDtypeStruct((rows, n), F32),
        compiler_params=_params(("arbitrary",)),
        name="adaln_mod",
    )(c, ada_w, ada_b.reshape(1, n))


def _rms_mod(x, g, sc, sh):
    y = x * lax.rsqrt(jnp.mean(x * x, axis=-1, keepdims=True) + EPS)
    return (y * g) * (1.0 + sc) + sh


def _inproj_kernel(x_ref, g_ref, sc_ref, sh_ref, w_ref, o_ref, h_scr):
    @pl.when(pl.program_id(2) == 0)
    def _():
        h_scr[...] = _rms_mod(x_ref[0], g_ref[...], sc_ref[0], sh_ref[0]).astype(BF16)

    o_ref[0] = jnp.dot(h_scr[...], w_ref[...], preferred_element_type=F32)


def _inproj(x, g, sc, sh, w, tn):
    B, T, D = x.shape
    n = w.shape[1]
    tm = min(512, T)
    return pl.pallas_call(
        _inproj_kernel,
        grid=(B, T // tm, n // tn),
        in_specs=[pl.BlockSpec((1, tm, D), lambda b, i, j: (b, i, 0)),
                  pl.BlockSpec((1, D), lambda b, i, j: (0, 0)),
                  pl.BlockSpec((1, 1, D), lambda b, i, j: (b, 0, 0)),
                  pl.BlockSpec((1, 1, D), lambda b, i, j: (b, 0, 0)),
                  pl.BlockSpec((D, tn), lambda b, i, j: (0, j))],
        out_specs=pl.BlockSpec((1, tm, tn), lambda b, i, j: (b, i, j)),
        out_shape=jax.ShapeDtypeStruct((B, T, n), F32),
        scratch_shapes=[pltpu.VMEM((tm, D), BF16)],
        compiler_params=_params(("parallel", "parallel", "arbitrary")),
        name="in_proj",
    )(x, g, sc, sh, w)


def _fill_halo(ext_ref, cur, prev8, next8, tt):
    i = pl.program_id(1)
    last = pl.num_programs(1) - 1
    ext_ref[pl.ds(0, SUBLANES), :] = jnp.where(i == 0, 0.0, prev8)
    ext_ref[pl.ds(SUBLANES, tt), :] = cur
    ext_ref[pl.ds(SUBLANES + tt, SUBLANES), :] = jnp.where(i == last, 0.0, next8)


def _halo_specs(tt, width, col_block, seq_len):
    nb = tt // SUBLANES
    last = seq_len // SUBLANES - 1
    return [
        pl.BlockSpec((1, tt, width), lambda b, i: (b, i, col_block)),
        pl.BlockSpec((1, SUBLANES, width), lambda b, i: (b, jnp.maximum(i * nb - 1, 0), col_block)),
        pl.BlockSpec((1, SUBLANES, width), lambda b, i: (b, jnp.minimum((i + 1) * nb, last), col_block)),
    ]


def _rwkv_prep_kernel(z_ref, zp_ref, zn_ref, mu_ref, w0_ref, a0_ref, up_ref, gup_ref, kk_ref, ka_ref,
                      rk_ref, ones_ref,
                      r_out, kk_out, v_out, g_out, bonus_out, lw_out, kd_out, bd_out, ext_scr):
    tt = z_ref.shape[1]
    z = z_ref[0]
    _fill_halo(ext_scr, z, zp_ref[0], zn_ref[0], tt)
    z_prev = ext_scr[pl.ds(SUBLANES - 1, tt), :]
    z_next = ext_scr[pl.ds(SUBLANES + 1, tt), :]
    zs = z + (0.5 * (z_prev + z_next) - z) * mu_ref[...]
    W = A_WIDTH
    zr, zk, zv = zs[:, 0:W], zs[:, W:2 * W], zs[:, 2 * W:3 * W]
    zwa = zs[:, 3 * W:3 * W + LANES]
    zg = zs[:, 3 * W + LANES:3 * W + 2 * LANES]
    ones_bd = ones_ref[...]
    kk_raw = zk * kk_ref[...]
    kk = kk_raw * lax.rsqrt(_seg_sum(kk_raw * kk_raw, ones_bd) + 1e-6)
    lane = lax.broadcasted_iota(jnp.int32, zwa.shape, 1)
    lhs = jnp.where(lane < A_RANK_W, jnp.tanh(zwa), zwa)
    k_sum = jnp.zeros_like(zk)
    for d in range(2):
        up = _dot_hi(lhs, up_ref[d])
        wl = w0_ref[d:d + 1, :] + up[:, 0:W]
        w_log = -_softplus(-wl) - 0.5
        lw_out[d, 0] = -jnp.exp(w_log)
        a = _sigmoid(a0_ref[d:d + 1, :] + up[:, W:2 * W])
        k_d = zk * (1.0 + (a - 1.0) * ka_ref[...])
        kd_out[d, 0] = k_d
        bd_out[d, 0] = kk * a
        k_sum = k_sum + k_d
    r_out[0] = zr
    kk_out[0] = kk
    v_out[0] = zv
    g_out[0] = _dot_hi(_sigmoid(zg), gup_ref[...])
    bonus_out[0] = _seg_sum(zr * k_sum * rk_ref[...], ones_bd) * zv


def _rwkv_prep(z_a, mu, w0, a0, up_comb, g_up, k_k, k_a, r_k, ones_bd):
    B, T, _ = z_a.shape
    tt = min(256, T)
    W = A_WIDTH
    full = lambda shape: pl.BlockSpec(shape, lambda b, i: (0,) * len(shape))
    tok = pl.BlockSpec((1, tt, W), lambda b, i: (b, i, 0))
    tok2 = pl.BlockSpec((2, 1, tt, W), lambda b, i: (0, b, i, 0))
    s1 = jax.ShapeDtypeStruct((B, T, W), F32)
    s2 = jax.ShapeDtypeStruct((2, B, T, W), F32)
    return pl.pallas_call(
        _rwkv_prep_kernel,
        grid=(B, T // tt),
        in_specs=_halo_specs(tt, A_COLS, 0, T) + [
            full((1, A_COLS)), full((2, W)), full((2, W)), full((2, LANES, 2 * W)),
            full((A_RANK_G, W)), full((1, W)), full((1, W)), full((1, W)), full((W, W))],
        out_specs=[tok, tok, tok, tok, tok, tok2, tok2, tok2],
        out_shape=[s1, s1, s1, s1, s1, s2, s2, s2],
        scratch_shapes=[pltpu.VMEM((tt + 2 * SUBLANES, A_COLS), F32)],
        compiler_params=_params(("parallel", "parallel")),
        name="rwkv_prep",
    )(z_a, z_a, z_a, mu, w0, a0, up_comb, g_up, k_k, k_a, r_k, ones_bd)


def _chunk_masks(reverse):
    n = 2 * CHUNK
    row = lax.broadcasted_iota(jnp.int32, (n, n), 0)
    col = lax.broadcasted_iota(jnp.int32, (n, n), 1)
    same = (row // CHUNK) == (col // CHUNK)
    ti, tj = row % CHUNK, col % CHUNK
    if reverse:
        return same & (ti < tj), same & (ti <= tj), row == col
    return same & (ti > tj), same & (ti >= tj), row == col


def _cumsum_matrix(reverse):
    row = lax.broadcasted_iota(jnp.int32, (CHUNK, CHUNK), 0)
    col = lax.broadcasted_iota(jnp.int32, (CHUNK, CHUNK), 1)
    return ((row <= col) if reverse else (row >= col)).astype(F32)


def _rwkv_chunk_operands(r, kk, v, k, b, cs, lw, reverse):
    C = CHUNK
    cs_end = cs[0:1, :] if reverse else cs[C - 1:C, :]
    m0 = lax.broadcasted_iota(jnp.int32, (C, LANES), 1) < A_HEAD_DIM

    def stack(x):
        return jnp.concatenate([jnp.where(m0, x, 0.0), jnp.where(m0, 0.0, x)], axis=0)

    g_inv = jnp.exp(-cs)
    g_tail = jnp.exp(cs_end - cs)
    strict, incl, _ = _chunk_masks(reverse)
    return dict(rg=stack(r * jnp.exp(cs)), kkg=stack(kk * jnp.exp(cs - lw)), ki=stack(k * g_inv),
                bi=stack(b * g_inv), kt=stack(k * g_tail), bt=stack(b * g_tail), vs=stack(v),
                g_end=jnp.exp(cs_end), strict=strict, incl=incl)


def _rwkv_chunks(ops, hts):
    C = CHUNK
    n = 2 * C
    eye = _chunk_masks(False)[2].astype(F32)
    ps = [_dot_nt(jnp.concatenate([o["kkg"], o["rg"]], axis=0), jnp.concatenate([o["bi"], o["ki"]], axis=0))
          for o in ops]
    a_ab = [jnp.where(o["strict"], p[0:n, 0:n], 0.0) for o, p in zip(ops, ps)]
    a_rb = [jnp.where(o["incl"], p[n:2 * n, 0:n], 0.0).astype(BF16) for o, p in zip(ops, ps)]
    a_k = [jnp.concatenate([jnp.where(o["strict"], p[0:n, n:2 * n], 0.0),
                            jnp.where(o["incl"], p[n:2 * n, n:2 * n], 0.0)], axis=0).astype(BF16)
           for o, p in zip(ops, ps)]
    vsb = [o["vs"].astype(BF16) for o in ops]
    a_kv = [jnp.dot(a, v, preferred_element_type=F32) for a, v in zip(a_k, vsb)]
    akv = [x[0:n] for x in a_kv]
    arkv = [x[n:2 * n] for x in a_kv]
    tinv = _tri_inverse(a_ab, eye)
    wu = [_dot(t, jnp.concatenate([o["kkg"], x], axis=1)).astype(BF16)
          for t, o, x in zip(tinv, ops, akv)]
    r2 = [jnp.dot(a, w, preferred_element_type=F32) for a, w in zip(a_rb, wu)]
    btb = [o["bt"].astype(BF16) for o in ops]
    btw = [_dot_tn(b, w[:, 0:LANES]) for b, w in zip(btb, wu)]
    hloc = [_dot_tn(jnp.concatenate([v, -w[:, LANES:2 * LANES]], axis=0),
                    jnp.concatenate([o["kt"].astype(BF16), b], axis=0))
            for v, o, w, b in zip(vsb, ops, wu, btb)]
    ys, hts_new = [], []
    for o, r, yl, ht, bw, hl in zip(ops, r2, arkv, hts, btw, hloc):
        q_s = o["rg"] - r[:, 0:LANES]
        yloc_s = yl - r[:, LANES:2 * LANES]
        ys.append(yloc_s[0:C] + yloc_s[C:n] + _dot_nt(q_s[0:C] + q_s[C:n], ht))
        hts_new.append(ht * o["g_end"] - _dot_nt(ht, bw) + hl)
    return ys, hts_new


def _rwkv_scan_kernel(rf, kkf, vf, lwf, kf, bf, rb, kkb, vb, lwb, kb, bb, yf_out, yb_out, h_scr):
    @pl.when(pl.program_id(1) == 0)
    def _():
        h_scr[...] = jnp.zeros_like(h_scr)

    n_pairs = A_WIDTH // LANES
    sls = [slice(i * LANES, (i + 1) * LANES) for i in range(n_pairs)]
    ops, hts = [], []
    for d, (r_ref, kk_ref, v_ref, lw_ref, k_ref, b_ref) in enumerate(
            ((rf, kkf, vf, lwf, kf, bf), (rb, kkb, vb, lwb, kb, bb))):
        reverse = d == 1
        lw_all = lw_ref[0, 0]
        cs_all = _dot_hi(_cumsum_matrix(reverse), lw_all)
        ops += [_rwkv_chunk_operands(r_ref[0, :, sl], kk_ref[0, :, sl], v_ref[0, :, sl], k_ref[0, 0, :, sl],
                                     b_ref[0, 0, :, sl], cs_all[:, sl], lw_all[:, sl], reverse) for sl in sls]
        hts += [h_scr[d, i] for i in range(n_pairs)]
    ys, hts = _rwkv_chunks(ops, hts)
    for d, y_out in enumerate((yf_out, yb_out)):
        for i, sl in enumerate(sls):
            h_scr[d, i] = hts[d * n_pairs + i]
            y_out[0, :, sl] = ys[d * n_pairs + i]


def _rwkv_scan(r, kk, v, lw, kd, bd):
    B, T, W = r.shape
    N = T // CHUNK
    fwd = pl.BlockSpec((1, CHUNK, W), lambda b, c: (b, c, 0))
    bwd = pl.BlockSpec((1, CHUNK, W), lambda b, c: (b, N - 1 - c, 0))
    fwd2 = pl.BlockSpec((1, 1, CHUNK, W), lambda b, c: (0, b, c, 0))
    bwd2 = pl.BlockSpec((1, 1, CHUNK, W), lambda b, c: (1, b, N - 1 - c, 0))
    out = jax.ShapeDtypeStruct((B, T, W), F32)
    return pl.pallas_call(
        _rwkv_scan_kernel,
        grid=(B, N),
        in_specs=[fwd, fwd, fwd, fwd2, fwd2, fwd2, bwd, bwd, bwd, bwd2, bwd2, bwd2],
        out_specs=[fwd, bwd],
        out_shape=[out, out],
        scratch_shapes=[pltpu.VMEM((2, W // LANES, LANES, LANES), F32)],
        compiler_params=_params(("parallel", "arbitrary")),
        name="rwkv_scan",
    )(r, kk, v, lw, kd, bd, r, kk, v, lw, kd, bd)


def _gdn_prep_kernel(z_ref, zp_ref, zn_ref, small_ref, cw_ref, alog_ref, dtb_ref,
                     q_out, k_out, v_out, bg_out, ext_scr):
    tt = z_ref.shape[1]
    _fill_halo(ext_scr, z_ref[0], zp_ref[0], zn_ref[0], tt)
    half = B_CONV // 2
    acc = None
    for j in range(B_CONV):
        term = ext_scr[pl.ds(SUBLANES - half + j, tt), :] * cw_ref[j:j + 1, :]
        acc = term if acc is None else acc + term
    qkv = _silu(acc)
    for h in range(B_HEADS):
        for idx, out in enumerate((q_out, k_out)):
            x = qkv[:, idx * B_WIDTH + h * B_HEAD_DIM: idx * B_WIDTH + (h + 1) * B_HEAD_DIM]
            xn = x * lax.rsqrt(jnp.sum(x * x, axis=-1, keepdims=True) + 1e-6)
            if idx == 0:
                xn = xn * (B_HEAD_DIM ** -0.5)
            out[0, :, h * B_HEAD_DIM:(h + 1) * B_HEAD_DIM] = xn
    v_out[0] = qkv[:, 2 * B_WIDTH:3 * B_WIDTH]
    s = small_ref[0]
    lane = lax.broadcasted_iota(jnp.int32, s.shape, 1)
    beta = _sigmoid(s)
    g = -jnp.exp(alog_ref[...]) * _softplus(s + dtb_ref[...])
    bg_out[0] = jnp.where(lane < 2 * B_HEADS, beta, g)


def _gdn_prep(z_b, conv_w, alog_vec, dtb_vec):
    B, T, _ = z_b.shape
    tt = min(256, T)
    Wq = 3 * B_WIDTH
    full = lambda shape: pl.BlockSpec(shape, lambda b, i: (0,) * len(shape))
    tok = pl.BlockSpec((1, tt, B_WIDTH), lambda b, i: (b, i, 0))
    small = pl.BlockSpec((1, tt, SMALL_COLS), lambda b, i: (b, i, Wq // SMALL_COLS))
    s1 = jax.ShapeDtypeStruct((B, T, B_WIDTH), F32)
    return pl.pallas_call(
        _gdn_prep_kernel,
        grid=(B, T // tt),
        in_specs=_halo_specs(tt, Wq, 0, T) + [small, full((B_CONV, Wq)), full((1, SMALL_COLS)),
                                            full((1, SMALL_COLS))],
        out_specs=[tok, tok, tok, pl.BlockSpec((1, tt, SMALL_COLS), lambda b, i: (b, i, 0))],
        out_shape=[s1, s1, s1, jax.ShapeDtypeStruct((B, T, SMALL_COLS), F32)],
        scratch_shapes=[pltpu.VMEM((tt + 2 * SUBLANES, Wq), F32)],
        compiler_params=_params(("parallel", "parallel")),
        name="gdn_prep",
    )(z_b, z_b, z_b, z_b, conv_w, alog_vec, dtb_vec)


def _gdn_chunk_operands(q0, q1, k0, k1, v0, v1, gc2, gct2, bg2, j0, j1, reverse):
    C = CHUNK
    n = 2 * C
    strict, incl, _ = _chunk_masks(reverse)
    top = lax.broadcasted_iota(jnp.int32, (n, 1), 0) < C
    left = lax.broadcasted_iota(jnp.int32, (1, n), 1) < C
    g0, g1 = j0 + 2 * B_HEADS, j1 + 2 * B_HEADS
    gcol = jnp.where(top, gc2[:, g0:g0 + 1], gc2[:, g1:g1 + 1])
    grow = jnp.where(left, gct2[g0:g0 + 1, :], gct2[g1:g1 + 1, :])
    beta = jnp.where(top, bg2[:, j0:j0 + 1], bg2[:, j1:j1 + 1])
    e = 0 if reverse else C - 1
    glast = jnp.where(top, gc2[e:e + 1, g0:g0 + 1], gc2[e:e + 1, g1:g1 + 1])
    ks = jnp.concatenate([k0, k1], axis=0)
    qs = jnp.concatenate([q0, q1], axis=0)
    vs = jnp.concatenate([v0, v1], axis=0)
    egc = jnp.exp(gcol)
    kb = ks * beta
    return dict(gam=jnp.exp(jnp.where(incl, gcol - grow, NEG_INF)), strict=strict, ks=ks,
                kbq=jnp.concatenate([kb, qs], axis=0),
                rhs=jnp.concatenate([vs * beta, kb * egc], axis=1),
                qh=(qs * egc).astype(BF16), ktail=(ks * jnp.exp(glast - gcol)).astype(BF16),
                decay=jnp.exp(glast))


def _gdn_chunks(ops, states):
    C = CHUNK
    n = 2 * C
    eye = _chunk_masks(False)[2].astype(F32)
    ps = [_dot_nt(o["kbq"], o["ks"]) for o in ops]
    lower = [jnp.where(o["strict"], p[0:n] * o["gam"], 0.0) for o, p in zip(ops, ps)]
    aqk = [(p[n:2 * n] * o["gam"]).astype(BF16) for o, p in zip(ops, ps)]
    tinv = [t.astype(BF16) for t in _tri_inverse(lower, eye)]
    uw = [jnp.dot(t, o["rhs"].astype(BF16), preferred_element_type=F32) for t, o in zip(tinv, ops)]
    lx = []
    for l, x in zip(lower, uw):
        lh, ll = _split(l)
        xh, xl = _split(x)
        first = jnp.dot(lh, jnp.concatenate([xh, xl], axis=1), preferred_element_type=F32)
        w2 = x.shape[1]
        lx.append(first[:, 0:w2] + (first[:, w2:2 * w2] + jnp.dot(ll, xh, preferred_element_type=F32)))
    resid = [o["rhs"] - x - y for o, x, y in zip(ops, uw, lx)]
    uw = [x + jnp.dot(t, r.astype(BF16), preferred_element_type=F32) for x, t, r in zip(uw, tinv, resid)]
    sb = [(s0.astype(BF16), s1.astype(BF16)) for s0, s1 in states]
    wq = [[jnp.dot(jnp.concatenate([u[h * C:(h + 1) * C, LANES:].astype(BF16), o["qh"][h * C:(h + 1) * C]],
                                   axis=0), s[h], preferred_element_type=F32) for h in range(2)]
          for u, o, s in zip(uw, ops, sb)]
    ws = [jnp.concatenate([x[0][0:C], x[1][0:C]], axis=0) for x in wq]
    qss = [jnp.concatenate([x[0][C:n], x[1][C:n]], axis=0) for x in wq]
    v_new = [(u[:, 0:LANES] - w).astype(BF16) for u, w in zip(uw, ws)]
    outs = [q + jnp.dot(a, v, preferred_element_type=F32) for q, a, v in zip(qss, aqk, v_new)]
    new_states = []
    for o, v, (s0, s1) in zip(ops, v_new, states):
        new_states.append((s0 * o["decay"][0:1, :] + _dot_tn(o["ktail"][0:C], v[0:C]),
                           s1 * o["decay"][C:C + 1, :] + _dot_tn(o["ktail"][C:n], v[C:n])))
    return [(x[0:C], x[C:n]) for x in outs], new_states


def _gdn_scan_kernel(qf, kf, vf, bgf, qb, kb, vb, bgb, of_out, ob_out, s_scr):
    @pl.when(pl.program_id(1) == 0)
    def _():
        s_scr[...] = jnp.zeros_like(s_scr)

    D = B_HEAD_DIM
    n_pairs = B_HEADS // 2
    sl = lambda h: slice(h * D, (h + 1) * D)
    ops, states = [], []
    for d, (q_ref, k_ref, v_ref, bg_ref) in enumerate(((qf, kf, vf, bgf), (qb, kb, vb, bgb))):
        reverse = d == 1
        bg = bg_ref[0]
        gc = _dot_hi(_cumsum_matrix(reverse), bg)
        bg2 = jnp.concatenate([bg, bg], axis=0)
        gc2 = jnp.concatenate([gc, gc], axis=0)
        gct2 = gc2.T
        for i in range(n_pairs):
            h0, h1 = 2 * i, 2 * i + 1
            ops.append(_gdn_chunk_operands(
                q_ref[0, :, sl(h0)], q_ref[0, :, sl(h1)], k_ref[0, :, sl(h0)], k_ref[0, :, sl(h1)],
                v_ref[0, :, sl(h0)], v_ref[0, :, sl(h1)], gc2, gct2, bg2,
                d * B_HEADS + h0, d * B_HEADS + h1, reverse))
            states.append((s_scr[d, h0], s_scr[d, h1]))
    outs, states = _gdn_chunks(ops, states)
    for d, o_out in enumerate((of_out, ob_out)):
        for i in range(n_pairs):
            h0, h1 = 2 * i, 2 * i + 1
            (o0, o1), (s0, s1) = outs[d * n_pairs + i], states[d * n_pairs + i]
            s_scr[d, h0] = s0
            s_scr[d, h1] = s1
            o_out[0, :, sl(h0)] = o0
            o_out[0, :, sl(h1)] = o1


def _gdn_scan(q, k, v, bg):
    B, T, W = q.shape
    N = T // CHUNK
    fwd = pl.BlockSpec((1, CHUNK, W), lambda b, c: (b, c, 0))
    bwd = pl.BlockSpec((1, CHUNK, W), lambda b, c: (b, N - 1 - c, 0))
    sfwd = pl.BlockSpec((1, CHUNK, SMALL_COLS), lambda b, c: (b, c, 0))
    sbwd = pl.BlockSpec((1, CHUNK, SMALL_COLS), lambda b, c: (b, N - 1 - c, 0))
    out = jax.ShapeDtypeStruct((B, T, W), F32)
    return pl.pallas_call(
        _gdn_scan_kernel,
        grid=(B, N),
        in_specs=[fwd, fwd, fwd, sfwd, bwd, bwd, bwd, sbwd],
        out_specs=[fwd, bwd],
        out_shape=[out, out],
        scratch_shapes=[pltpu.VMEM((2, B_HEADS, B_HEAD_DIM, B_HEAD_DIM), F32)],
        compiler_params=_params(("parallel", "arbitrary")),
        name="gdn_scan",
    )(q, k, v, bg, q, k, v, bg)


def _pack_bf16_pairs(x):
    n = x.shape[1] // 2
    hi = lax.bitcast_convert_type(x[:, :n].astype(BF16).astype(F32), jnp.int32)
    lo = lax.bitcast_convert_type(x[:, n:].astype(BF16).astype(F32), jnp.int32)
    return hi | lax.shift_right_logical(lo, 16)


def _unpack_bf16_pairs(p):
    a = lax.bitcast_convert_type(p & jnp.int32(-65536), F32)
    b = lax.bitcast_convert_type(lax.shift_left(p, 16), F32)
    return a.astype(BF16), b.astype(BF16)


def _route(scores, biased, base):
    tt = scores.shape[-1]
    shape3 = (N_GROUPS, GROUP_SIZE, tt)
    s3 = scores.reshape(shape3)
    b3 = biased.reshape(shape3)
    jid = lax.broadcasted_iota(jnp.int32, shape3, 1).astype(F32)
    gid = lax.broadcasted_iota(jnp.int32, shape3, 0).astype(F32)
    m1 = jnp.max(b3, axis=1, keepdims=True)
    first = jnp.min(jnp.where(b3 == m1, jid, float(GROUP_SIZE)), axis=1, keepdims=True)
    m2 = jnp.max(jnp.where(jid == first, NEG_INF, b3), axis=1, keepdims=True)
    gs = m1 + m2
    grp = lax.broadcasted_iota(jnp.int32, (N_GROUPS, 1, tt), 0).astype(F32)
    keep = jnp.zeros((N_GROUPS, 1, tt), F32)
    for _ in range(TOPK_GROUPS):
        m = jnp.max(gs, axis=0, keepdims=True)
        pick = grp == jnp.min(jnp.where(gs == m, grp, float(N_GROUPS)), axis=0, keepdims=True)
        keep = jnp.where(pick, 1.0, keep)
        gs = jnp.where(pick, NEG_INF, gs)
    work = jnp.where(keep > 0.0, b3, NEG_INF)
    eid = gid * float(GROUP_SIZE) + jid
    chosen = jnp.zeros(shape3, F32)
    sum01 = lambda a: jnp.sum(jnp.sum(a, axis=1, keepdims=True), axis=0, keepdims=True)
    ids, raw = [], []
    for _ in range(TOP_K):
        m = jnp.max(jnp.max(work, axis=1, keepdims=True), axis=0, keepdims=True)
        cand = jnp.where(work == m, eid, float(N_EXPERTS))
        first = jnp.min(jnp.min(cand, axis=1, keepdims=True), axis=0, keepdims=True)
        pick = eid == first
        ids.append(first)
        raw.append(sum01(jnp.where(pick, s3, 0.0)))
        chosen = jnp.where(pick, 1.0, chosen)
        work = jnp.where(pick, NEG_INF, work)
    total = raw[0]
    for r in raw[1:]:
        total = total + r
    flat = chosen.reshape(N_EXPERTS, tt)
    earlier = (lax.broadcasted_iota(jnp.int32, (tt, tt), 0)
               < lax.broadcasted_iota(jnp.int32, (tt, tt), 1)).astype(BF16)
    prefix = (jnp.dot(flat.astype(BF16), earlier, preferred_element_type=F32) + base).reshape(shape3)
    row = lax.broadcasted_iota(jnp.int32, (SUBLANES, tt), 0)
    idx8 = jnp.zeros((SUBLANES, tt), F32)
    w8 = jnp.zeros((SUBLANES, tt), F32)
    pos8 = jnp.zeros((SUBLANES, tt), F32)
    for k in range(TOP_K):
        rank = sum01(jnp.where(eid == ids[k], prefix, 0.0))
        idx8 = jnp.where(row == k, ids[k].reshape(1, tt), idx8)
        w8 = jnp.where(row == k, (raw[k] / total * ROUTED_SCALE).reshape(1, tt), w8)
        pos8 = jnp.where(row == k, rank.reshape(1, tt), pos8)
    counts = jnp.sum(flat, axis=1, keepdims=True)
    return idx8.astype(jnp.int32), w8, pos8.astype(jnp.int32), counts


def _mixer_post_kernel(yf, yb, bonus, g, of, ob, zz, zga, zgb, x_ref, gt1, sc2, sh2,
                       gnw, gnb, ones_ref, aproj, bng, bproj, wout, n2g, rwt, rbias,
                       x1_out, h2_out, idx_out, w_out_ref, pos_out, counts_out):
    @pl.when((pl.program_id(0) == 0) & (pl.program_id(1) == 0))
    def _():
        counts_out[...] = jnp.zeros_like(counts_out)

    ones_bd = ones_ref[...]
    y = yf[0] + yb[0]
    mean = _seg_sum(y, ones_bd) * (1.0 / A_HEAD_DIM)
    dlt = y - mean
    var = _seg_sum(dlt * dlt, ones_bd) * (1.0 / A_HEAD_DIM)
    yn = dlt * lax.rsqrt(var + A_GN_EPS) * gnw[...] + gnb[...]
    y_a = _dot((yn + bonus[0]) * g[0], aproj[...])

    o = of[0] + ob[0]
    z = zz[0]
    parts = []
    for h in range(B_HEADS):
        oh = o[:, h * B_HEAD_DIM:(h + 1) * B_HEAD_DIM]
        parts.append(oh * lax.rsqrt(jnp.mean(oh * oh, axis=-1, keepdims=True) + EPS))
    on = jnp.concatenate(parts, axis=1) * bng[...] * _silu(z)
    y_b = _dot(on, bproj[...])

    u = _sigmoid(zga[0]) * y_a + _sigmoid(zgb[0]) * y_b
    x1 = x_ref[0] + gt1[0] * _dot(u, wout[...])
    x1_out[0] = x1
    h2 = _rms_mod(x1, n2g[...], sc2[0], sh2[0])
    h2_out[0] = _pack_bf16_pairs(h2)
    logits_t = _dot_nt(rwt[...], h2)
    scores = _sigmoid(logits_t)
    base = counts_out[:, 0:1]
    idx8, w8, pos8, counts = _route(scores, scores + rbias[...], base)
    idx_out[0] = idx8
    w_out_ref[0] = w8
    pos_out[0] = pos8
    counts_out[...] = counts_out[...] + counts


def _mixer_post(yf, yb, bonus, g, of, ob, z_c, x, gt1, sc2, sh2, gn_w, gn_b, ones_bd, a_proj, bng,
                b_proj, w_out, n2g, rwt, rbias):
    B, T, D = x.shape
    tt = min(256, T)
    full = lambda shape: pl.BlockSpec(shape, lambda b, i: (0,) * len(shape))
    tokw = lambda w, col=0: pl.BlockSpec((1, tt, w), lambda b, i: (b, i, col))
    modrow = pl.BlockSpec((1, 1, D), lambda b, i: (b, 0, 0))
    pick = pl.BlockSpec((1, SUBLANES, tt), lambda b, i: (b, 0, i))
    W = A_WIDTH
    return pl.pallas_call(
        _mixer_post_kernel,
        grid=(B, T // tt),
        in_specs=[tokw(W), tokw(W), tokw(W), tokw(W), tokw(D), tokw(D),
                  tokw(D, 0), tokw(D, 1), tokw(D, 2), tokw(D), modrow, modrow, modrow,
                  full((1, W)), full((1, W)), full((W, W)), full((W, D)), full((1, D)), full((D, D)),
                  full((D, D)), full((1, D)), full((N_EXPERTS, D)), full((N_EXPERTS, 1))],
        out_specs=[tokw(D), tokw(D // 2), pick, pick, pick,
                   pl.BlockSpec((N_EXPERTS, LANES), lambda b, i: (0, 0))],
        out_shape=[jax.ShapeDtypeStruct((B, T, D), F32), jax.ShapeDtypeStruct((B, T, D // 2), jnp.int32),
                   jax.ShapeDtypeStruct((B, SUBLANES, T), jnp.int32),
                   jax.ShapeDtypeStruct((B, SUBLANES, T), F32),
                   jax.ShapeDtypeStruct((B, SUBLANES, T), jnp.int32),
                   jax.ShapeDtypeStruct((N_EXPERTS, LANES), F32)],
        compiler_params=_params(("arbitrary", "arbitrary")),
        name="mixer_post",
    )(yf, yb, bonus, g, of, ob, z_c, z_c, z_c, x, gt1, sc2, sh2, gn_w, gn_b, ones_bd, a_proj, bng,
      b_proj, w_out, n2g, rwt, rbias)


def _gather_rows(table, idx):
    n_rows, width = idx.shape[0], table.shape[1]
    per_worker = n_rows // SC_WORKERS
    n_windows = per_worker // GATHER_WINDOW
    mesh = plsc.VectorSubcoreMesh(core_axis_name="c", subcore_axis_name="s")

    @functools.partial(
        pl.kernel, mesh=mesh, out_type=jax.ShapeDtypeStruct((n_rows, width), table.dtype),
        scratch_types=[pltpu.VMEM((GATHER_WINDOW,), jnp.int32),
                       pltpu.VMEM((GATHER_WINDOW, width), table.dtype),
                       pltpu.SemaphoreType.DMA])
    def gather(table_hbm, idx_hbm, out_hbm, idx_v, rows_v, sem):
        worker = lax.axis_index("s") * SC_CORES + lax.axis_index("c")

        @pl.loop(0, n_windows)
        def _(j):
            off = pl.multiple_of(worker * per_worker + j * GATHER_WINDOW, GATHER_WINDOW)
            pltpu.sync_copy(idx_hbm.at[pl.ds(off, GATHER_WINDOW)], idx_v)
            pltpu.async_copy(table_hbm.at[idx_v], rows_v, sem).wait()
            pltpu.sync_copy(rows_v, out_hbm.at[pl.ds(off, GATHER_WINDOW)])

    return gather(table, idx)


def _scatter_rows(rows, dest_w, n_out):
    n_tok, width = rows.shape
    per_worker = n_tok // GATHER_WINDOW // SC_WORKERS
    mesh = plsc.VectorSubcoreMesh(core_axis_name="c", subcore_axis_name="s")

    @functools.partial(
        pl.kernel, mesh=mesh, out_type=jax.ShapeDtypeStruct((n_out, width), rows.dtype),
        scratch_types=[pltpu.VMEM((SUBLANES, GATHER_WINDOW), jnp.int32),
                       pltpu.VMEM((GATHER_WINDOW, width), rows.dtype)])
    def scatter(rows_hbm, dest_hbm, out_hbm, idx_v, rows_v):
        worker = lax.axis_index("s") * SC_CORES + lax.axis_index("c")

        @pl.loop(0, per_worker)
        def _(j):
            win = worker * per_worker + j
            pltpu.sync_copy(dest_hbm.at[win], idx_v)
            pltpu.sync_copy(rows_hbm.at[pl.ds(pl.multiple_of(win * GATHER_WINDOW, GATHER_WINDOW),
                                              GATHER_WINDOW)], rows_v)
            for k in range(TOP_K):
                pltpu.sync_copy(rows_v, out_hbm.at[idx_v.at[k]])

    return scatter(rows, dest_w)


def _dest_kernel(start_ref, idx_ref, pos_ref, dest_ref):
    idx = idx_ref[0]
    dest = jnp.zeros_like(idx)
    for e in range(N_EXPERTS):
        dest = jnp.where(idx == e, start_ref[e], dest)
    dest_ref[0] = dest + pos_ref[0]


def _dest_rows(pad_start, idx, pos):
    B, _, T = idx.shape
    tt = min(2048, T)
    spec = pl.BlockSpec((1, SUBLANES, tt), lambda b, i, start: (b, 0, i))
    return pl.pallas_call(
        _dest_kernel,
        grid_spec=pltpu.PrefetchScalarGridSpec(num_scalar_prefetch=1, grid=(B, T // tt),
                                               in_specs=[spec, spec], out_specs=spec),
        out_shape=jax.ShapeDtypeStruct(idx.shape, jnp.int32),
        compiler_params=_params(("parallel", "parallel")),
        name="dest_rows",
    )(pad_start, idx, pos)


def _expert_kernel(be_ref, valid_ref, xs_ref, wg, wu, wd, ys_ref):
    n_valid = valid_ref[pl.program_id(0)]

    @pl.when(n_valid > 0)
    def _():
        half = D_MODEL // 2
        row = lax.broadcasted_iota(jnp.int32, xs_ref.shape, 0)
        a, b = _unpack_bf16_pairs(jnp.where(row < n_valid, xs_ref[...], 0))
        mm = lambda w: (jnp.dot(a, w[0, 0:half, :], preferred_element_type=F32)
                        + jnp.dot(b, w[0, half:D_MODEL, :], preferred_element_type=F32))
        hid = _silu(mm(wg)) * mm(wu)
        ys_ref[...] = _pack_bf16_pairs(_dot(hid, wd[0]))


def _experts(block_expert, block_valid, xs, wg, wu, wd):
    n_rows, half = xs.shape
    rows = pl.BlockSpec((MOE_BLOCK, half), lambda i, be, valid: (i, 0))
    wspec = lambda shape: pl.BlockSpec(shape, lambda i, be, valid: (be[i], 0, 0))
    return pl.pallas_call(
        _expert_kernel,
        grid_spec=pltpu.PrefetchScalarGridSpec(
            num_scalar_prefetch=2, grid=(n_rows // MOE_BLOCK,),
            in_specs=[rows, wspec((1, D_MODEL, D_EXPERT)), wspec((1, D_MODEL, D_EXPERT)),
                      wspec((1, D_EXPERT, D_MODEL))],
            out_specs=rows),
        out_shape=jax.ShapeDtypeStruct((n_rows, half), jnp.int32),
        compiler_params=_params(("arbitrary",)),
        name="experts",
    )(block_expert, block_valid, xs, wg, wu, wd)


def _moe_final_kernel(yg_ref, w_ref, h_ref, x1_ref, gt2, fg, sg, su, sd, o_ref):
    half = D_MODEL // 2
    a, b = _unpack_bf16_pairs(h_ref[0])
    mm = lambda w: (jnp.dot(a, w[0:half, :], preferred_element_type=F32)
                    + jnp.dot(b, w[half:D_MODEL, :], preferred_element_type=F32))
    shared = _dot(_silu(mm(sg)) * mm(su), sd[...])
    w = w_ref[0]
    lo = hi = None
    for k in range(TOP_K):
        ya, yb = _unpack_bf16_pairs(yg_ref[k, 0])
        wk = w[:, k:k + 1]
        lo = ya.astype(F32) * wk if lo is None else lo + ya.astype(F32) * wk
        hi = yb.astype(F32) * wk if hi is None else hi + yb.astype(F32) * wk
    x2 = x1_ref[0] + gt2[0] * (jnp.concatenate([lo, hi], axis=1) + shared)
    o_ref[0] = x2 * lax.rsqrt(jnp.mean(x2 * x2, axis=-1, keepdims=True) + EPS) * fg[...]


def _moe_final(yg, w_t, h2p, x1, gt2, fg, sg, su, sd):
    B, T, D = x1.shape
    tm = min(512, T)
    tok = lambda w: pl.BlockSpec((1, tm, w), lambda b, i: (b, i, 0))
    full = lambda shape: pl.BlockSpec(shape, lambda b, i: (0,) * len(shape))
    return pl.pallas_call(
        _moe_final_kernel,
        grid=(B, T // tm),
        in_specs=[pl.BlockSpec((TOP_K, 1, tm, D // 2), lambda b, i: (0, b, i, 0)), tok(SUBLANES),
                  tok(D // 2), tok(D), pl.BlockSpec((1, 1, D), lambda b, i: (b, 0, 0)), full((1, D)),
                  full((D, D_EXPERT)), full((D, D_EXPERT)), full((D_EXPERT, D))],
        out_specs=tok(D),
        out_shape=jax.ShapeDtypeStruct((B, T, D), F32),
        compiler_params=_params(("parallel", "parallel")),
        name="moe_final",
    )(yg, w_t, h2p, x1, gt2, fg, sg, su, sd)


def _moe_routed(h2p, idx, w, pos, counts, x1, gt2, p):
    B, T, half = h2p.shape
    n_slots = B * T * TOP_K
    n_rows = n_slots + N_EXPERTS * MOE_BLOCK
    cnt = counts[:, 0].astype(jnp.int32)
    padded = (cnt + MOE_BLOCK - 1) // MOE_BLOCK * MOE_BLOCK
    pad_end = jnp.cumsum(padded)
    pad_start = pad_end - padded
    dest = _dest_rows(pad_start, idx, pos)
    starts = jnp.arange(n_rows // MOE_BLOCK, dtype=jnp.int32) * MOE_BLOCK
    block_expert = jnp.minimum(jnp.sum((pad_end[None, :] <= starts[:, None]).astype(jnp.int32), axis=1),
                               N_EXPERTS - 1)
    block_valid = jnp.clip((pad_start + cnt)[block_expert] - starts, 0, MOE_BLOCK)
    dest_w = jnp.swapaxes(dest.reshape(B, SUBLANES, T // GATHER_WINDOW, GATHER_WINDOW), 1, 2)
    xs = _scatter_rows(h2p.reshape(B * T, half), dest_w.reshape(-1, SUBLANES, GATHER_WINDOW), n_rows)
    ys = _experts(block_expert, block_valid, xs, p["wg"], p["wu"], p["wd"])
    yg = _gather_rows(ys, jnp.swapaxes(dest[:, :TOP_K], 0, 1).reshape(-1)).reshape(TOP_K, B, T, half)
    return _moe_final(yg, jnp.swapaxes(w, 1, 2), h2p, x1, gt2, p["fg"], p["sg"], p["su"], p["sd"])


def _prepare_weights(w_in, shift_mu, a_w0, a_w_up, a_a0, a_a_up, a_g_up, a_k_k, a_k_a, a_r_k, a_gn_w,
                     a_gn_b, a_proj, b_conv_w, b_a_log, b_dt_bias, b_norm_g, b_proj, w_out, router_w,
                     router_bias, exp_gate, exp_up, exp_down, sh_gate, sh_up, sh_down, norm1_g, norm2_g,
                     final_g):
    W = A_WIDTH
    w = w_in[0]
    c0 = A_COLS
    c1 = c0 + 3 * B_WIDTH
    c2 = c1 + B_WIDTH
    c3 = c2 + 4 * B_HEADS
    pad = jnp.zeros((D_MODEL, SMALL_COLS - 4 * B_HEADS), F32)
    w_a = w[:, :c0].astype(BF16)
    w_b = jnp.concatenate([w[:, c0:c1], w[:, c2:c3], pad], axis=1).astype(BF16)
    w_c = jnp.concatenate([w[:, c1:c2], w[:, c3:]], axis=1).astype(BF16)
    zeros = jnp.zeros((2, A_RANK_W, W), F32)
    up_comb = jnp.concatenate([jnp.concatenate([a_w_up[0], zeros], axis=2),
                               jnp.concatenate([zeros, a_a_up[0]], axis=2)], axis=1)
    head = jnp.arange(W) // A_HEAD_DIM
    ones_bd = (head[:, None] == head[None, :]).astype(BF16)
    small = lambda v: jnp.zeros((1, SMALL_COLS), F32).at[0, 2 * B_HEADS:4 * B_HEADS].set(v.reshape(-1))
    return dict(
        w_a=w_a, w_b=w_b, w_c=w_c, mu=shift_mu[0].reshape(1, A_COLS), w0=a_w0[0], a0=a_a0[0],
        up_comb=up_comb, g_up=a_g_up[0], k_k=a_k_k[0].reshape(1, W), k_a=a_k_a[0].reshape(1, W),
        r_k=a_r_k[0].reshape(1, W), ones_bd=ones_bd, gn_w=a_gn_w[0].reshape(1, W),
        gn_b=a_gn_b[0].reshape(1, W), a_proj=a_proj[0].astype(BF16), conv_w=b_conv_w[0],
        alog=small(b_a_log[0]), dtb=small(b_dt_bias[0]),
        bng=jnp.tile(b_norm_g[0], B_HEADS).reshape(1, B_WIDTH), b_proj=b_proj[0].astype(BF16),
        w_out=w_out[0].astype(BF16), rwt=router_w[0].T, rbias=router_bias[0].reshape(N_EXPERTS, 1),
        wg=exp_gate[0].astype(BF16), wu=exp_up[0].astype(BF16), wd=exp_down[0].astype(BF16),
        sg=sh_gate[0].astype(BF16), su=sh_up[0].astype(BF16), sd=sh_down[0].astype(BF16),
        n1g=norm1_g[0].reshape(1, D_MODEL), n2g=norm2_g[0].reshape(1, D_MODEL),
        fg=final_g.reshape(1, D_MODEL))


def _layer(x, mod, p):
    B = x.shape[0]
    sh1, sc1, gt1, sh2, sc2, gt2 = (m.reshape(B, 1, D_MODEL) for m in jnp.split(mod, 6, axis=-1))
    z_a = _inproj(x, p["n1g"], sc1, sh1, p["w_a"], p["w_a"].shape[1])
    z_b = _inproj(x, p["n1g"], sc1, sh1, p["w_b"], p["w_b"].shape[1])
    z_c = _inproj(x, p["n1g"], sc1, sh1, p["w_c"], p["w_c"].shape[1])
    r, kk, v, g, bonus, lw, kd, bd = _rwkv_prep(z_a, p["mu"], p["w0"], p["a0"], p["up_comb"], p["g_up"],
                                                p["k_k"], p["k_a"], p["r_k"], p["ones_bd"])
    yf, yb = _rwkv_scan(r, kk, v, lw, kd, bd)
    q, k, vv, bg = _gdn_prep(z_b, p["conv_w"], p["alog"], p["dtb"])
    of, ob = _gdn_scan(q, k, vv, bg)
    x1, h2p, idx, w, pos, counts = _mixer_post(yf, yb, bonus, g, of, ob, z_c, x, gt1, sc2, sh2, p["gn_w"],
                                               p["gn_b"], p["ones_bd"], p["a_proj"], p["bng"], p["b_proj"],
                                               p["w_out"], p["n2g"], p["rwt"], p["rbias"])
    return _moe_routed(h2p, idx, w, pos, counts, x1, gt2, p)


def kernel(x_prompt, x_sample, c_prompt, c_sample, ada_w, ada_b, norm1_g, norm2_g, w_in, shift_mu, a_w0, a_w_up, a_a0, a_a_up, a_g_up, a_k_k, a_k_a, a_r_k, a_gn_w, a_gn_b, a_proj, b_conv_w, b_a_log, b_dt_bias, b_norm_g, b_proj, w_out, router_w, router_bias, exp_gate, exp_up, exp_down, sh_gate, sh_up, sh_down, final_g):
    p = _prepare_weights(w_in, shift_mu, a_w0, a_w_up, a_a0, a_a_up, a_g_up, a_k_k, a_k_a, a_r_k, a_gn_w,
                         a_gn_b, a_proj, b_conv_w, b_a_log, b_dt_bias, b_norm_g, b_proj, w_out, router_w,
                         router_bias, exp_gate, exp_up, exp_down, sh_gate, sh_up, sh_down, norm1_g,
                         norm2_g, final_g)
    nb_p, nb_s = c_prompt.shape[0], c_sample.shape[0]
    rows = -(-(nb_p + nb_s) // SUBLANES) * SUBLANES
    c_all = jnp.concatenate([c_prompt, c_sample, jnp.zeros((rows - nb_p - nb_s, D_MODEL), F32)], axis=0)
    mod = _adaln_mod(c_all, ada_w[0], ada_b[0])
    y_prompt = _layer(x_prompt, mod[:nb_p], p)
    y_sample = _layer(x_sample, mod[nb_p:nb_p + nb_s], p)
    return (y_prompt, y_sample)
```

```python
import functools

import jax
import jax.numpy as jnp
from jax import lax
from jax.experimental import pallas as pl
from jax.experimental.pallas import tpu as pltpu
from jax.experimental.pallas import tpu_sc as plsc

F32 = jnp.float32
BF16 = jnp.bfloat16
HIGHEST = lax.Precision.HIGHEST

D_MODEL = 1024
A_HEADS = 8
A_HEAD_DIM = 64
A_WIDTH = A_HEADS * A_HEAD_DIM
A_RANK_W = 64
A_RANK_A = 64
A_RANK_G = 128
A_GN_EPS = 64e-5
A_COLS = 3 * A_WIDTH + A_RANK_W + A_RANK_A + A_RANK_G
B_HEADS = 8
B_HEAD_DIM = 128
B_WIDTH = B_HEADS * B_HEAD_DIM
B_CONV = 5
CHUNK = 64
N_EXPERTS = 64
TOP_K = 6
N_GROUPS = 8
TOPK_GROUPS = 4
GROUP_SIZE = N_EXPERTS // N_GROUPS
D_EXPERT = 256
ROUTED_SCALE = 2.5
EPS = 1e-6
LANES = 128
SUBLANES = 8
SMALL_COLS = LANES
MOE_BLOCK = 512
SC_CORES = 2
SC_SUBCORES = 16
SC_WORKERS = SC_CORES * SC_SUBCORES
GATHER_WINDOW = 64
VMEM_LIMIT = 56 * 1024 * 1024
NEG_INF = float("-inf")


def _dot(a, b):
    return jnp.dot(a.astype(BF16), b.astype(BF16), preferred_element_type=F32)


def _dot_nt(a, b):
    return lax.dot_general(a.astype(BF16), b.astype(BF16), (((1,), (1,)), ((), ())),
                           preferred_element_type=F32)


def _dot_tn(a, b):
    return lax.dot_general(a.astype(BF16), b.astype(BF16), (((0,), (0,)), ((), ())),
                           preferred_element_type=F32)


def _dot_hi(a, b):
    return jnp.dot(a, b, precision=HIGHEST, preferred_element_type=F32)


def _split(x):
    hi = x.astype(BF16)
    return hi, (x - hi.astype(F32)).astype(BF16)


def _dot3(a, b, dims=(((1,), (0,)), ((), ()))):
    ah, al = _split(a)
    bh, bl = _split(b)
    d = lambda u, v: lax.dot_general(u, v, dims, preferred_element_type=F32)
    return d(ah, bh) + (d(ah, bl) + d(al, bh))


def _seg_sum(x, ones_bd):
    hi = x.astype(BF16)
    lo = (x - hi.astype(F32)).astype(BF16)
    return (jnp.dot(hi, ones_bd, preferred_element_type=F32)
            + jnp.dot(lo, ones_bd, preferred_element_type=F32))


def _softplus(x):
    return jnp.maximum(x, 0.0) + jnp.log1p(jnp.exp(-jnp.abs(x)))


def _sigmoid(x):
    return 1.0 / (1.0 + jnp.exp(-x))


def _silu(x):
    return x * _sigmoid(x)


def _tri_inverse(a_list, eye):
    n = a_list[0].shape[0]
    xs = [(-a).astype(BF16) for a in a_list]
    ts = [eye - a for a in a_list]
    xs = [jnp.dot(x, x, preferred_element_type=F32).astype(BF16) for x in xs]
    power = 2
    while 2 * power < CHUNK:
        both = [jnp.dot(x, jnp.concatenate([t.astype(BF16), x], axis=1), preferred_element_type=F32)
                for t, x in zip(ts, xs)]
        ts = [t + b[:, 0:n] for t, b in zip(ts, both)]
        xs = [b[:, n:2 * n].astype(BF16) for b in both]
        power *= 2
    return [t + jnp.dot(x, t.astype(BF16), preferred_element_type=F32) for t, x in zip(ts, xs)]


def _params(sem):
    return pltpu.CompilerParams(dimension_semantics=sem, vmem_limit_bytes=VMEM_LIMIT)


def _mod_kernel(c_ref, w_ref, b_ref, o_ref):
    c = c_ref[...]
    o_ref[...] = _dot_hi(_silu(c), w_ref[...]) + b_ref[...]


def _adaln_mod(c, ada_w, ada_b):
    rows = c.shape[0]
    n = ada_w.shape[1]
    return pl.pallas_call(
        _mod_kernel,
        grid=(n // D_MODEL,),
        in_specs=[pl.BlockSpec((rows, D_MODEL), lambda j: (0, 0)),
                  pl.BlockSpec((D_MODEL, D_MODEL), lambda j: (0, j)),
                  pl.BlockSpec((1, D_MODEL), lambda j: (0, j))],
        out_specs=pl.BlockSpec((rows, D_MODEL), lambda j: (0, j)),
        out_shape=jax.ShapeDtypeStruct((rows, n), F32),
        compiler_params=_params(("arbitrary",)),
        name="adaln_mod",
    )(c, ada_w, ada_b.reshape(1, n))


def _rms_mod(x, g, sc, sh):
    y = x * lax.rsqrt(jnp.mean(x * x, axis=-1, keepdims=True) + EPS)
    return (y * g) * (1.0 + sc) + sh


def _inproj_kernel(x_ref, g_ref, sc_ref, sh_ref, w_ref, o_ref, h_scr):
    @pl.when(pl.program_id(2) == 0)
    def _():
        h_scr[...] = _rms_mod(x_ref[0], g_ref[...], sc_ref[0], sh_ref[0]).astype(BF16)

    o_ref[0] = jnp.dot(h_scr[...], w_ref[...], preferred_element_type=F32)


def _inproj(x, g, sc, sh, w, tn):
    B, T, D = x.shape
    n = w.shape[1]
    tm = min(512, T)
    return pl.pallas_call(
        _inproj_kernel,
        grid=(B, T // tm, n // tn),
        in_specs=[pl.BlockSpec((1, tm, D), lambda b, i, j: (b, i, 0)),
                  pl.BlockSpec((1, D), lambda b, i, j: (0, 0)),
                  pl.BlockSpec((1, 1, D), lambda b, i, j: (b, 0, 0)),
                  pl.BlockSpec((1, 1, D), lambda b, i, j: (b, 0, 0)),
                  pl.BlockSpec((D, tn), lambda b, i, j: (0, j))],
        out_specs=pl.BlockSpec((1, tm, tn), lambda b, i, j: (b, i, j)),
        out_shape=jax.ShapeDtypeStruct((B, T, n), F32),
        scratch_shapes=[pltpu.VMEM((tm, D), BF16)],
        compiler_params=_params(("parallel", "parallel", "arbitrary")),
        name="in_proj",
    )(x, g, sc, sh, w)


def _fill_halo(ext_ref, cur, prev8, next8, tt):
    i = pl.program_id(1)
    last = pl.num_programs(1) - 1
    ext_ref[pl.ds(0, SUBLANES), :] = jnp.where(i == 0, 0.0, prev8)
    ext_ref[pl.ds(SUBLANES, tt), :] = cur
    ext_ref[pl.ds(SUBLANES + tt, SUBLANES), :] = jnp.where(i == last, 0.0, next8)


def _halo_specs(tt, width, col_block, seq_len):
    nb = tt // SUBLANES
    last = seq_len // SUBLANES - 1
    return [
        pl.BlockSpec((1, tt, width), lambda b, i: (b, i, col_block)),
        pl.BlockSpec((1, SUBLANES, width), lambda b, i: (b, jnp.maximum(i * nb - 1, 0), col_block)),
        pl.BlockSpec((1, SUBLANES, width), lambda b, i: (b, jnp.minimum((i + 1) * nb, last), col_block)),
    ]


def _rwkv_prep_kernel(z_ref, zp_ref, zn_ref, mu_ref, w0_ref, a0_ref, up_ref, gup_ref, kk_ref, ka_ref,
                      rk_ref, ones_ref,
                      r_out, kk_out, v_out, g_out, bonus_out, lw_out, kd_out, bd_out, ext_scr):
    tt = z_ref.shape[1]
    z = z_ref[0]
    _fill_halo(ext_scr, z, zp_ref[0], zn_ref[0], tt)
    z_prev = ext_scr[pl.ds(SUBLANES - 1, tt), :]
    z_next = ext_scr[pl.ds(SUBLANES + 1, tt), :]
    zs = z + (0.5 * (z_prev + z_next) - z) * mu_ref[...]
    W = A_WIDTH
    zr, zk, zv = zs[:, 0:W], zs[:, W:2 * W], zs[:, 2 * W:3 * W]
    zwa = zs[:, 3 * W:3 * W + LANES]
    zg = zs[:, 3 * W + LANES:3 * W + 2 * LANES]
    ones_bd = ones_ref[...]
    kk_raw = zk * kk_ref[...]
    kk = kk_raw * lax.rsqrt(_seg_sum(kk_raw * kk_raw, ones_bd) + 1e-6)
    lane = lax.broadcasted_iota(jnp.int32, zwa.shape, 1)
    lhs = jnp.where(lane < A_RANK_W, jnp.tanh(zwa), zwa)
    k_sum = jnp.zeros_like(zk)
    for d in range(2):
        up = _dot(lhs, up_ref[d])
        wl = w0_ref[d:d + 1, :] + up[:, 0:W]
        w_log = -_softplus(-wl) - 0.5
        lw_out[d, 0] = -jnp.exp(w_log)
        a = _sigmoid(a0_ref[d:d + 1, :] + up[:, W:2 * W])
        k_d = zk * (1.0 + (a - 1.0) * ka_ref[...])
        kd_out[d, 0] = k_d
        bd_out[d, 0] = kk * a
        k_sum = k_sum + k_d
    r_out[0] = zr
    kk_out[0] = kk
    v_out[0] = zv
    g_out[0] = _dot(_sigmoid(zg), gup_ref[...])
    bonus_out[0] = _seg_sum(zr * k_sum * rk_ref[...], ones_bd) * zv


def _rwkv_prep(z_a, mu, w0, a0, up_comb, g_up, k_k, k_a, r_k, ones_bd):
    B, T, _ = z_a.shape
    tt = min(256, T)
    W = A_WIDTH
    full = lambda shape: pl.BlockSpec(shape, lambda b, i: (0,) * len(shape))
    tok = pl.BlockSpec((1, tt, W), lambda b, i: (b, i, 0))
    tok2 = pl.BlockSpec((2, 1, tt, W), lambda b, i: (0, b, i, 0))
    s1 = jax.ShapeDtypeStruct((B, T, W), F32)
    s2 = jax.ShapeDtypeStruct((2, B, T, W), F32)
    return pl.pallas_call(
        _rwkv_prep_kernel,
        grid=(B, T // tt),
        in_specs=_halo_specs(tt, A_COLS, 0, T) + [
            full((1, A_COLS)), full((2, W)), full((2, W)), full((2, LANES, 2 * W)),
            full((A_RANK_G, W)), full((1, W)), full((1, W)), full((1, W)), full((W, W))],
        out_specs=[tok, tok, tok, tok, tok, tok2, tok2, tok2],
        out_shape=[s1, s1, s1, s1, s1, s2, s2, s2],
        scratch_shapes=[pltpu.VMEM((tt + 2 * SUBLANES, A_COLS), F32)],
        compiler_params=_params(("parallel", "parallel")),
        name="rwkv_prep",
    )(z_a, z_a, z_a, mu, w0, a0, up_comb, g_up, k_k, k_a, r_k, ones_bd)


def _chunk_masks(reverse):
    n = 2 * CHUNK
    row = lax.broadcasted_iota(jnp.int32, (n, n), 0)
    col = lax.broadcasted_iota(jnp.int32, (n, n), 1)
    same = (row // CHUNK) == (col // CHUNK)
    ti, tj = row % CHUNK, col % CHUNK
    if reverse:
        return same & (ti < tj), same & (ti <= tj), row == col
    return same & (ti > tj), same & (ti >= tj), row == col


def _cumsum_matrix(reverse):
    row = lax.broadcasted_iota(jnp.int32, (CHUNK, CHUNK), 0)
    col = lax.broadcasted_iota(jnp.int32, (CHUNK, CHUNK), 1)
    return ((row <= col) if reverse else (row >= col)).astype(F32)


def _rwkv_chunk_operands(r, kk, v, k, b, cs, lw, reverse):
    C = CHUNK
    cs_end = cs[0:1, :] if reverse else cs[C - 1:C, :]
    m0 = lax.broadcasted_iota(jnp.int32, (C, LANES), 1) < A_HEAD_DIM

    def stack(x):
        return jnp.concatenate([jnp.where(m0, x, 0.0), jnp.where(m0, 0.0, x)], axis=0)

    g_inv = jnp.exp(-cs)
    g_tail = jnp.exp(cs_end - cs)
    strict, incl, _ = _chunk_masks(reverse)
    return dict(rg=stack(r * jnp.exp(cs)), kkg=stack(kk * jnp.exp(cs - lw)), ki=stack(k * g_inv),
                bi=stack(b * g_inv), kt=stack(k * g_tail), bt=stack(b * g_tail), vs=stack(v),
                g_end=jnp.exp(cs_end), strict=strict, incl=incl)


def _rwkv_chunks(ops, hts):
    C = CHUNK
    n = 2 * C
    eye = _chunk_masks(False)[2].astype(F32)
    ps = [_dot_nt(jnp.concatenate([o["kkg"], o["rg"]], axis=0), jnp.concatenate([o["bi"], o["ki"]], axis=0))
          for o in ops]
    a_ab = [jnp.where(o["strict"], p[0:n, 0:n], 0.0) for o, p in zip(ops, ps)]
    a_rb = [jnp.where(o["incl"], p[n:2 * n, 0:n], 0.0).astype(BF16) for o, p in zip(ops, ps)]
    a_k = [jnp.concatenate([jnp.where(o["strict"], p[0:n, n:2 * n], 0.0),
                            jnp.where(o["incl"], p[n:2 * n, n:2 * n], 0.0)], axis=0).astype(BF16)
           for o, p in zip(ops, ps)]
    vsb = [o["vs"].astype(BF16) for o in ops]
    a_kv = [jnp.dot(a, v, preferred_element_type=F32) for a, v in zip(a_k, vsb)]
    akv = [x[0:n] for x in a_kv]
    arkv = [x[n:2 * n] for x in a_kv]
    tinv = _tri_inverse(a_ab, eye)
    wu = [_dot(t, jnp.concatenate([o["kkg"], x], axis=1)).astype(BF16)
          for t, o, x in zip(tinv, ops, akv)]
    r2 = [jnp.dot(a, w, preferred_element_type=F32) for a, w in zip(a_rb, wu)]
    btb = [o["bt"].astype(BF16) for o in ops]
    btw = [_dot_tn(b, w[:, 0:LANES]) for b, w in zip(btb, wu)]
    hloc = [_dot_tn(jnp.concatenate([v, -w[:, LANES:2 * LANES]], axis=0),
                    jnp.concatenate([o["kt"].astype(BF16), b], axis=0))
            for v, o, w, b in zip(vsb, ops, wu, btb)]
    ys, hts_new = [], []
    for o, r, yl, ht, bw, hl in zip(ops, r2, arkv, hts, btw, hloc):
        q_s = o["rg"] - r[:, 0:LANES]
        yloc_s = yl - r[:, LANES:2 * LANES]
        ys.append(yloc_s[0:C] + yloc_s[C:n] + _dot_nt(q_s[0:C] + q_s[C:n], ht))
        hts_new.append(ht * o["g_end"] - _dot_nt(ht, bw) + hl)
    return ys, hts_new


def _rwkv_scan_kernel(rf, kkf, vf, lwf, kf, bf, rb, kkb, vb, lwb, kb, bb, yf_out, yb_out, h_scr):
    @pl.when(pl.program_id(1) == 0)
    def _():
        h_scr[...] = jnp.zeros_like(h_scr)

    n_pairs = A_WIDTH // LANES
    sls = [slice(i * LANES, (i + 1) * LANES) for i in range(n_pairs)]
    ops, hts = [], []
    for d, (r_ref, kk_ref, v_ref, lw_ref, k_ref, b_ref) in enumerate(
            ((rf, kkf, vf, lwf, kf, bf), (rb, kkb, vb, lwb, kb, bb))):
        reverse = d == 1
        lw_all = lw_ref[0, 0]
        cs_all = _dot_hi(_cumsum_matrix(reverse), lw_all)
        ops += [_rwkv_chunk_operands(r_ref[0, :, sl], kk_ref[0, :, sl], v_ref[0, :, sl], k_ref[0, 0, :, sl],
                                     b_ref[0, 0, :, sl], cs_all[:, sl], lw_all[:, sl], reverse) for sl in sls]
        hts += [h_scr[d, i] for i in range(n_pairs)]
    ys, hts = _rwkv_chunks(ops, hts)
    for d, y_out in enumerate((yf_out, yb_out)):
        for i, sl in enumerate(sls):
            h_scr[d, i] = hts[d * n_pairs + i]
            y_out[0, :, sl] = ys[d * n_pairs + i]


def _rwkv_scan(r, kk, v, lw, kd, bd):
    B, T, W = r.shape
    N = T // CHUNK
    fwd = pl.BlockSpec((1, CHUNK, W), lambda b, c: (b, c, 0))
    bwd = pl.BlockSpec((1, CHUNK, W), lambda b, c: (b, N - 1 - c, 0))
    fwd2 = pl.BlockSpec((1, 1, CHUNK, W), lambda b, c: (0, b, c, 0))
    bwd2 = pl.BlockSpec((1, 1, CHUNK, W), lambda b, c: (1, b, N - 1 - c, 0))
    out = jax.ShapeDtypeStruct((B, T, W), F32)
    return pl.pallas_call(
        _rwkv_scan_kernel,
        grid=(B, N),
        in_specs=[fwd, fwd, fwd, fwd2, fwd2, fwd2, bwd, bwd, bwd, bwd2, bwd2, bwd2],
        out_specs=[fwd, bwd],
        out_shape=[out, out],
        scratch_shapes=[pltpu.VMEM((2, W // LANES, LANES, LANES), F32)],
        compiler_params=_params(("parallel", "arbitrary")),
        name="rwkv_scan",
    )(r, kk, v, lw, kd, bd, r, kk, v, lw, kd, bd)


def _gdn_prep_kernel(z_ref, zp_ref, zn_ref, small_ref, cw_ref, alog_ref, dtb_ref,
                     q_out, k_out, v_out, bg_out, acc_scr):
    tt = z_ref.shape[1]
    i = pl.program_id(1)
    last = pl.num_programs(1) - 1
    half = B_CONV // 2
    zb = z_ref[0].astype(BF16)
    offsets = [j - half for j in range(B_CONV) if j != half]
    row = lax.broadcasted_iota(jnp.int32, (len(offsets) * tt, tt), 0)
    col = lax.broadcasted_iota(jnp.int32, (len(offsets) * tt, tt), 1)
    blk = row // tt
    off = jnp.where(blk < half, blk - half, blk - half + 1)
    shifted = jnp.dot((col == row - blk * tt + off).astype(BF16), zb, preferred_element_type=F32)
    acc = zb.astype(F32) * cw_ref[half:half + 1, :]
    for m, o in enumerate(offsets):
        acc = acc + shifted[m * tt:(m + 1) * tt] * cw_ref[o + half:o + half + 1, :]
    acc_scr[...] = acc
    prev8 = jnp.where(i == 0, 0.0, zp_ref[0]).astype(BF16).astype(F32)
    next8 = jnp.where(i == last, 0.0, zn_ref[0]).astype(BF16).astype(F32)
    r8 = lax.broadcasted_iota(jnp.int32, (SUBLANES, 1), 0)
    top = bot = None
    for o in offsets:
        w = cw_ref[o + half:o + half + 1, :]
        if o < 0:
            f = jnp.where(r8 < -o, pltpu.roll(prev8, -o, axis=0), 0.0) * w
            top = f if top is None else top + f
        else:
            f = jnp.where(r8 >= SUBLANES - o, pltpu.roll(next8, SUBLANES - o, axis=0), 0.0) * w
            bot = f if bot is None else bot + f
    acc_scr[pl.ds(0, SUBLANES), :] += top
    acc_scr[pl.ds(tt - SUBLANES, SUBLANES), :] += bot
    qkv = _silu(acc_scr[...])
    for h in range(B_HEADS):
        for idx, out in enumerate((q_out, k_out)):
            x = qkv[:, idx * B_WIDTH + h * B_HEAD_DIM: idx * B_WIDTH + (h + 1) * B_HEAD_DIM]
            xn = x * lax.rsqrt(jnp.sum(x * x, axis=-1, keepdims=True) + 1e-6)
            if idx == 0:
                xn = xn * (B_HEAD_DIM ** -0.5)
            out[0, :, h * B_HEAD_DIM:(h + 1) * B_HEAD_DIM] = xn
    v_out[0] = qkv[:, 2 * B_WIDTH:3 * B_WIDTH]
    s = small_ref[0]
    lane = lax.broadcasted_iota(jnp.int32, s.shape, 1)
    beta = _sigmoid(s)
    g = -jnp.exp(alog_ref[...]) * _softplus(s + dtb_ref[...])
    bg_out[0] = jnp.where(lane < 2 * B_HEADS, beta, g)


def _gdn_prep(z_b, conv_w, alog_vec, dtb_vec):
    B, T, _ = z_b.shape
    tt = min(256, T)
    Wq = 3 * B_WIDTH
    full = lambda shape: pl.BlockSpec(shape, lambda b, i: (0,) * len(shape))
    tok = pl.BlockSpec((1, tt, B_WIDTH), lambda b, i: (b, i, 0))
    small = pl.BlockSpec((1, tt, SMALL_COLS), lambda b, i: (b, i, Wq // SMALL_COLS))
    s1 = jax.ShapeDtypeStruct((B, T, B_WIDTH), F32)
    return pl.pallas_call(
        _gdn_prep_kernel,
        grid=(B, T // tt),
        in_specs=_halo_specs(tt, Wq, 0, T) + [small, full((B_CONV, Wq)), full((1, SMALL_COLS)),
                                            full((1, SMALL_COLS))],
        out_specs=[tok, tok, tok, pl.BlockSpec((1, tt, SMALL_COLS), lambda b, i: (b, i, 0))],
        out_shape=[s1, s1, s1, jax.ShapeDtypeStruct((B, T, SMALL_COLS), F32)],
        scratch_shapes=[pltpu.VMEM((tt, Wq), F32)],
        compiler_params=_params(("parallel", "parallel")),
        name="gdn_prep",
    )(z_b, z_b, z_b, z_b, conv_w, alog_vec, dtb_vec)


def _gdn_chunk_operands(q0, q1, k0, k1, v0, v1, gc2, gct2, bg2, j0, j1, reverse):
    C = CHUNK
    n = 2 * C
    strict, incl, _ = _chunk_masks(reverse)
    top = lax.broadcasted_iota(jnp.int32, (n, 1), 0) < C
    left = lax.broadcasted_iota(jnp.int32, (1, n), 1) < C
    g0, g1 = j0 + 2 * B_HEADS, j1 + 2 * B_HEADS
    gcol = jnp.where(top, gc2[:, g0:g0 + 1], gc2[:, g1:g1 + 1])
    grow = jnp.where(left, gct2[g0:g0 + 1, :], gct2[g1:g1 + 1, :])
    beta = jnp.where(top, bg2[:, j0:j0 + 1], bg2[:, j1:j1 + 1])
    e = 0 if reverse else C - 1
    glast = jnp.where(top, gc2[e:e + 1, g0:g0 + 1], gc2[e:e + 1, g1:g1 + 1])
    ks = jnp.concatenate([k0, k1], axis=0)
    qs = jnp.concatenate([q0, q1], axis=0)
    vs = jnp.concatenate([v0, v1], axis=0)
    egc = jnp.exp(gcol)
    kb = ks * beta
    return dict(gam=jnp.exp(jnp.where(incl, gcol - grow, NEG_INF)), strict=strict, ks=ks,
                kbq=jnp.concatenate([kb, qs], axis=0),
                rhs=jnp.concatenate([vs * beta, kb * egc], axis=1),
                qh=(qs * egc).astype(BF16), ktail=(ks * jnp.exp(glast - gcol)).astype(BF16),
                decay=jnp.exp(glast))


def _gdn_chunks(ops, states):
    C = CHUNK
    n = 2 * C
    eye = _chunk_masks(False)[2].astype(F32)
    ps = [_dot_nt(o["kbq"], o["ks"]) for o in ops]
    lower = [jnp.where(o["strict"], p[0:n] * o["gam"], 0.0) for o, p in zip(ops, ps)]
    aqk = [(p[n:2 * n] * o["gam"]).astype(BF16) for o, p in zip(ops, ps)]
    tinv = [t.astype(BF16) for t in _tri_inverse(lower, eye)]
    uw = [jnp.dot(t, o["rhs"].astype(BF16), preferred_element_type=F32) for t, o in zip(tinv, ops)]
    lx = []
    for l, x in zip(lower, uw):
        lh, ll = _split(l)
        xh, xl = _split(x)
        first = jnp.dot(lh, jnp.concatenate([xh, xl], axis=1), preferred_element_type=F32)
        w2 = x.shape[1]
        lx.append(first[:, 0:w2] + (first[:, w2:2 * w2] + jnp.dot(ll, xh, preferred_element_type=F32)))
    resid = [o["rhs"] - x - y for o, x, y in zip(ops, uw, lx)]
    uw = [x + jnp.dot(t, r.astype(BF16), preferred_element_type=F32) for x, t, r in zip(uw, tinv, resid)]
    sb = [(s0.astype(BF16), s1.astype(BF16)) for s0, s1 in states]
    wq = [[jnp.dot(jnp.concatenate([u[h * C:(h + 1) * C, LANES:].astype(BF16), o["qh"][h * C:(h + 1) * C]],
                                   axis=0), s[h], preferred_element_type=F32) for h in range(2)]
          for u, o, s in zip(uw, ops, sb)]
    ws = [jnp.concatenate([x[0][0:C], x[1][0:C]], axis=0) for x in wq]
    qss = [jnp.concatenate([x[0][C:n], x[1][C:n]], axis=0) for x in wq]
    v_new = [(u[:, 0:LANES] - w).astype(BF16) for u, w in zip(uw, ws)]
    outs = [q + jnp.dot(a, v, preferred_element_type=F32) for q, a, v in zip(qss, aqk, v_new)]
    new_states = []
    for o, v, (s0, s1) in zip(ops, v_new, states):
        new_states.append((s0 * o["decay"][0:1, :] + _dot_tn(o["ktail"][0:C], v[0:C]),
                           s1 * o["decay"][C:C + 1, :] + _dot_tn(o["ktail"][C:n], v[C:n])))
    return [(x[0:C], x[C:n]) for x in outs], new_states


def _gdn_scan_kernel(qf, kf, vf, bgf, qb, kb, vb, bgb, of_out, ob_out, s_scr):
    @pl.when(pl.program_id(1) == 0)
    def _():
        s_scr[...] = jnp.zeros_like(s_scr)

    D = B_HEAD_DIM
    n_pairs = B_HEADS // 2
    sl = lambda h: slice(h * D, (h + 1) * D)
    ops, states = [], []
    for d, (q_ref, k_ref, v_ref, bg_ref) in enumerate(((qf, kf, vf, bgf), (qb, kb, vb, bgb))):
        reverse = d == 1
        bg = bg_ref[0]
        gc = _dot_hi(_cumsum_matrix(reverse), bg)
        bg2 = jnp.concatenate([bg, bg], axis=0)
        gc2 = jnp.concatenate([gc, gc], axis=0)
        gct2 = gc2.T
        for i in range(n_pairs):
            h0, h1 = 2 * i, 2 * i + 1
            ops.append(_gdn_chunk_operands(
                q_ref[0, :, sl(h0)], q_ref[0, :, sl(h1)], k_ref[0, :, sl(h0)], k_ref[0, :, sl(h1)],
                v_ref[0, :, sl(h0)], v_ref[0, :, sl(h1)], gc2, gct2, bg2,
                d * B_HEADS + h0, d * B_HEADS + h1, reverse))
            states.append((s_scr[d, h0], s_scr[d, h1]))
    outs, states = _gdn_chunks(ops, states)
    for d, o_out in enumerate((of_out, ob_out)):
        for i in range(n_pairs):
            h0, h1 = 2 * i, 2 * i + 1
            (o0, o1), (s0, s1) = outs[d * n_pairs + i], states[d * n_pairs + i]
            s_scr[d, h0] = s0
            s_scr[d, h1] = s1
            o_out[0, :, sl(h0)] = o0
            o_out[0, :, sl(h1)] = o1


def _gdn_scan(q, k, v, bg):
    B, T, W = q.shape
    N = T // CHUNK
    fwd = pl.BlockSpec((1, CHUNK, W), lambda b, c: (b, c, 0))
    bwd = pl.BlockSpec((1, CHUNK, W), lambda b, c: (b, N - 1 - c, 0))
    sfwd = pl.BlockSpec((1, CHUNK, SMALL_COLS), lambda b, c: (b, c, 0))
    sbwd = pl.BlockSpec((1, CHUNK, SMALL_COLS), lambda b, c: (b, N - 1 - c, 0))
    out = jax.ShapeDtypeStruct((B, T, W), F32)
    return pl.pallas_call(
        _gdn_scan_kernel,
        grid=(B, N),
        in_specs=[fwd, fwd, fwd, sfwd, bwd, bwd, bwd, sbwd],
        out_specs=[fwd, bwd],
        out_shape=[out, out],
        scratch_shapes=[pltpu.VMEM((2, B_HEADS, B_HEAD_DIM, B_HEAD_DIM), F32)],
        compiler_params=_params(("parallel", "arbitrary")),
        name="gdn_scan",
    )(q, k, v, bg, q, k, v, bg)


def _pack_bf16_pairs(x):
    n = x.shape[1] // 2
    hi = lax.bitcast_convert_type(x[:, :n].astype(BF16).astype(F32), jnp.int32)
    lo = lax.bitcast_convert_type(x[:, n:].astype(BF16).astype(F32), jnp.int32)
    return hi | lax.shift_right_logical(lo, 16)


def _unpack_bf16_pairs(p):
    a = lax.bitcast_convert_type(p & jnp.int32(-65536), F32)
    b = lax.bitcast_convert_type(lax.shift_left(p, 16), F32)
    return a.astype(BF16), b.astype(BF16)


def _route(scores, biased, base):
    tt = scores.shape[-1]
    shape3 = (N_GROUPS, GROUP_SIZE, tt)
    s3 = scores.reshape(shape3)
    b3 = biased.reshape(shape3)
    jid = lax.broadcasted_iota(jnp.int32, shape3, 1).astype(F32)
    gid = lax.broadcasted_iota(jnp.int32, shape3, 0).astype(F32)
    m1 = jnp.max(b3, axis=1, keepdims=True)
    first = jnp.min(jnp.where(b3 == m1, jid, float(GROUP_SIZE)), axis=1, keepdims=True)
    m2 = jnp.max(jnp.where(jid == first, NEG_INF, b3), axis=1, keepdims=True)
    gs = m1 + m2
    grp = lax.broadcasted_iota(jnp.int32, (N_GROUPS, 1, tt), 0).astype(F32)
    keep = jnp.zeros((N_GROUPS, 1, tt), F32)
    for _ in range(TOPK_GROUPS):
        m = jnp.max(gs, axis=0, keepdims=True)
        pick = grp == jnp.min(jnp.where(gs == m, grp, float(N_GROUPS)), axis=0, keepdims=True)
        keep = jnp.where(pick, 1.0, keep)
        gs = jnp.where(pick, NEG_INF, gs)
    work = jnp.where(keep > 0.0, b3, NEG_INF)
    eid = gid * float(GROUP_SIZE) + jid
    chosen = jnp.zeros(shape3, F32)
    sum01 = lambda a: jnp.sum(jnp.sum(a, axis=1, keepdims=True), axis=0, keepdims=True)
    ids, raw = [], []
    for _ in range(TOP_K):
        m = jnp.max(jnp.max(work, axis=1, keepdims=True), axis=0, keepdims=True)
        cand = jnp.where(work == m, eid, float(N_EXPERTS))
        first = jnp.min(jnp.min(cand, axis=1, keepdims=True), axis=0, keepdims=True)
        pick = eid == first
        ids.append(first)
        raw.append(sum01(jnp.where(pick, s3, 0.0)))
        chosen = jnp.where(pick, 1.0, chosen)
        work = jnp.where(pick, NEG_INF, work)
    total = raw[0]
    for r in raw[1:]:
        total = total + r
    flat = chosen.reshape(N_EXPERTS, tt)
    earlier = (lax.broadcasted_iota(jnp.int32, (tt, tt), 0)
               < lax.broadcasted_iota(jnp.int32, (tt, tt), 1)).astype(BF16)
    prefix = (jnp.dot(flat.astype(BF16), earlier, preferred_element_type=F32) + base).reshape(shape3)
    row = lax.broadcasted_iota(jnp.int32, (SUBLANES, tt), 0)
    idx8 = jnp.zeros((SUBLANES, tt), F32)
    w8 = jnp.zeros((SUBLANES, tt), F32)
    pos8 = jnp.zeros((SUBLANES, tt), F32)
    for k in range(TOP_K):
        rank = sum01(jnp.where(eid == ids[k], prefix, 0.0))
        idx8 = jnp.where(row == k, ids[k].reshape(1, tt), idx8)
        w8 = jnp.where(row == k, (raw[k] / total * ROUTED_SCALE).reshape(1, tt), w8)
        pos8 = jnp.where(row == k, rank.reshape(1, tt), pos8)
    counts = jnp.sum(flat, axis=1, keepdims=True)
    return idx8.astype(jnp.int32), w8, pos8.astype(jnp.int32), counts


def _mixer_post_kernel(yf, yb, bonus, g, of, ob, zz, zga, zgb, x_ref, gt1, sc2, sh2,
                       gnw, gnb, ones_ref, aproj, bng, bproj, wout, n2g, rwt, rbias,
                       x1_out, h2_out, idx_out, w_out_ref, pos_out, counts_out):
    @pl.when((pl.program_id(0) == 0) & (pl.program_id(1) == 0))
    def _():
        counts_out[...] = jnp.zeros_like(counts_out)

    ones_bd = ones_ref[...]
    y = yf[0] + yb[0]
    mean = _seg_sum(y, ones_bd) * (1.0 / A_HEAD_DIM)
    dlt = y - mean
    var = _seg_sum(dlt * dlt, ones_bd) * (1.0 / A_HEAD_DIM)
    yn = dlt * lax.rsqrt(var + A_GN_EPS) * gnw[...] + gnb[...]
    y_a = _dot((yn + bonus[0]) * g[0], aproj[...])

    o = of[0] + ob[0]
    z = zz[0]
    parts = []
    for h in range(B_HEADS):
        oh = o[:, h * B_HEAD_DIM:(h + 1) * B_HEAD_DIM]
        parts.append(oh * lax.rsqrt(jnp.mean(oh * oh, axis=-1, keepdims=True) + EPS))
    on = jnp.concatenate(parts, axis=1) * bng[...] * _silu(z)
    y_b = _dot(on, bproj[...])

    u = _sigmoid(zga[0]) * y_a + _sigmoid(zgb[0]) * y_b
    x1 = x_ref[0] + gt1[0] * _dot(u, wout[...])
    x1_out[0] = x1
    h2 = _rms_mod(x1, n2g[...], sc2[0], sh2[0])
    h2_out[0] = _pack_bf16_pairs(h2)
    logits_t = _dot_nt(rwt[...], h2)
    scores = _sigmoid(logits_t)
    base = counts_out[:, 0:1]
    idx8, w8, pos8, counts = _route(scores, scores + rbias[...], base)
    idx_out[0] = idx8
    w_out_ref[0] = w8
    pos_out[0] = pos8
    counts_out[...] = counts_out[...] + counts


def _mixer_post(yf, yb, bonus, g, of, ob, z_c, x, gt1, sc2, sh2, gn_w, gn_b, ones_bd, a_proj, bng,
                b_proj, w_out, n2g, rwt, rbias):
    B, T, D = x.shape
    tt = min(256, T)
    full = lambda shape: pl.BlockSpec(shape, lambda b, i: (0,) * len(shape))
    tokw = lambda w, col=0: pl.BlockSpec((1, tt, w), lambda b, i: (b, i, col))
    modrow = pl.BlockSpec((1, 1, D), lambda b, i: (b, 0, 0))
    pick = pl.BlockSpec((1, SUBLANES, tt), lambda b, i: (b, 0, i))
    W = A_WIDTH
    return pl.pallas_call(
        _mixer_post_kernel,
        grid=(B, T // tt),
        in_specs=[tokw(W), tokw(W), tokw(W), tokw(W), tokw(D), tokw(D),
                  tokw(D, 0), tokw(D, 1), tokw(D, 2), tokw(D), modrow, modrow, modrow,
                  full((1, W)), full((1, W)), full((W, W)), full((W, D)), full((1, D)), full((D, D)),
                  full((D, D)), full((1, D)), full((N_EXPERTS, D)), full((N_EXPERTS, 1))],
        out_specs=[tokw(D), tokw(D // 2), pick, pick, pick,
                   pl.BlockSpec((N_EXPERTS, LANES), lambda b, i: (0, 0))],
        out_shape=[jax.ShapeDtypeStruct((B, T, D), F32), jax.ShapeDtypeStruct((B, T, D // 2), jnp.int32),
                   jax.ShapeDtypeStruct((B, SUBLANES, T), jnp.int32),
                   jax.ShapeDtypeStruct((B, SUBLANES, T), F32),
                   jax.ShapeDtypeStruct((B, SUBLANES, T), jnp.int32),
                   jax.ShapeDtypeStruct((N_EXPERTS, LANES), F32)],
        compiler_params=_params(("arbitrary", "arbitrary")),
        name="mixer_post",
    )(yf, yb, bonus, g, of, ob, z_c, z_c, z_c, x, gt1, sc2, sh2, gn_w, gn_b, ones_bd, a_proj, bng,
      b_proj, w_out, n2g, rwt, rbias)


def _gather_rows(table, idx):
    n_rows, width = idx.shape[0], table.shape[1]
    per_worker = n_rows // SC_WORKERS
    n_windows = per_worker // GATHER_WINDOW
    mesh = plsc.VectorSubcoreMesh(core_axis_name="c", subcore_axis_name="s")

    @functools.partial(
        pl.kernel, mesh=mesh, out_type=jax.ShapeDtypeStruct((n_rows, width), table.dtype),
        scratch_types=[pltpu.VMEM((GATHER_WINDOW,), jnp.int32),
                       pltpu.VMEM((GATHER_WINDOW, width), table.dtype),
                       pltpu.SemaphoreType.DMA])
    def gather(table_hbm, idx_hbm, out_hbm, idx_v, rows_v, sem):
        worker = lax.axis_index("s") * SC_CORES + lax.axis_index("c")

        @pl.loop(0, n_windows)
        def _(j):
            off = pl.multiple_of(worker * per_worker + j * GATHER_WINDOW, GATHER_WINDOW)
            pltpu.sync_copy(idx_hbm.at[pl.ds(off, GATHER_WINDOW)], idx_v)
            pltpu.async_copy(table_hbm.at[idx_v], rows_v, sem).wait()
            pltpu.sync_copy(rows_v, out_hbm.at[pl.ds(off, GATHER_WINDOW)])

    return gather(table, idx)


def _scatter_rows(rows, dest_w, n_out):
    n_tok, width = rows.shape
    per_worker = n_tok // GATHER_WINDOW // SC_WORKERS
    mesh = plsc.VectorSubcoreMesh(core_axis_name="c", subcore_axis_name="s")

    @functools.partial(
        pl.kernel, mesh=mesh, out_type=jax.ShapeDtypeStruct((n_out, width), rows.dtype),
        scratch_types=[pltpu.VMEM((SUBLANES, GATHER_WINDOW), jnp.int32),
                       pltpu.VMEM((GATHER_WINDOW, width), rows.dtype)])
    def scatter(rows_hbm, dest_hbm, out_hbm, idx_v, rows_v):
        worker = lax.axis_index("s") * SC_CORES + lax.axis_index("c")

        @pl.loop(0, per_worker)
        def _(j):
            win = worker * per_worker + j
            pltpu.sync_copy(dest_hbm.at[win], idx_v)
            pltpu.sync_copy(rows_hbm.at[pl.ds(pl.multiple_of(win * GATHER_WINDOW, GATHER_WINDOW),
                                              GATHER_WINDOW)], rows_v)
            for k in range(TOP_K):
                pltpu.sync_copy(rows_v, out_hbm.at[idx_v.at[k]])

    return scatter(rows, dest_w)


def _dest_kernel(start_ref, idx_ref, pos_ref, dest_ref):
    idx = idx_ref[0]
    dest = jnp.zeros_like(idx)
    for e in range(N_EXPERTS):
        dest = jnp.where(idx == e, start_ref[e], dest)
    dest_ref[0] = dest + pos_ref[0]


def _dest_rows(pad_start, idx, pos):
    B, _, T = idx.shape
    tt = min(2048, T)
    spec = pl.BlockSpec((1, SUBLANES, tt), lambda b, i, start: (b, 0, i))
    return pl.pallas_call(
        _dest_kernel,
        grid_spec=pltpu.PrefetchScalarGridSpec(num_scalar_prefetch=1, grid=(B, T // tt),
                                               in_specs=[spec, spec], out_specs=spec),
        out_shape=jax.ShapeDtypeStruct(idx.shape, jnp.int32),
        compiler_params=_params(("parallel", "parallel")),
        name="dest_rows",
    )(pad_start, idx, pos)


def _expert_kernel(be_ref, valid_ref, xs_ref, wg, wu, wd, ys_ref):
    n_valid = valid_ref[pl.program_id(0)]

    @pl.when(n_valid > 0)
    def _():
        half = D_MODEL // 2
        row = lax.broadcasted_iota(jnp.int32, xs_ref.shape, 0)
        a, b = _unpack_bf16_pairs(jnp.where(row < n_valid, xs_ref[...], 0))
        mm = lambda w: (jnp.dot(a, w[0, 0:half, :], preferred_element_type=F32)
                        + jnp.dot(b, w[0, half:D_MODEL, :], preferred_element_type=F32))
        hid = _silu(mm(wg)) * mm(wu)
        ys_ref[...] = _pack_bf16_pairs(_dot(hid, wd[0]))


def _experts(block_expert, block_valid, xs, wg, wu, wd):
    n_rows, half = xs.shape
    rows = pl.BlockSpec((MOE_BLOCK, half), lambda i, be, valid: (i, 0))
    wspec = lambda shape: pl.BlockSpec(shape, lambda i, be, valid: (be[i], 0, 0))
    return pl.pallas_call(
        _expert_kernel,
        grid_spec=pltpu.PrefetchScalarGridSpec(
            num_scalar_prefetch=2, grid=(n_rows // MOE_BLOCK,),
            in_specs=[rows, wspec((1, D_MODEL, D_EXPERT)), wspec((1, D_MODEL, D_EXPERT)),
                      wspec((1, D_EXPERT, D_MODEL))],
            out_specs=rows),
        out_shape=jax.ShapeDtypeStruct((n_rows, half), jnp.int32),
        compiler_params=_params(("arbitrary",)),
        name="experts",
    )(block_expert, block_valid, xs, wg, wu, wd)


def _moe_final_kernel(yg_ref, w_ref, h_ref, x1_ref, gt2, fg, sg, su, sd, o_ref):
    half = D_MODEL // 2
    a, b = _unpack_bf16_pairs(h_ref[0])
    mm = lambda w: (jnp.dot(a, w[0:half, :], preferred_element_type=F32)
                    + jnp.dot(b, w[half:D_MODEL, :], preferred_element_type=F32))
    shared = _dot(_silu(mm(sg)) * mm(su), sd[...])
    w = w_ref[0]
    lo = hi = None
    for k in range(TOP_K):
        ya, yb = _unpack_bf16_pairs(yg_ref[k, 0])
        wk = w[:, k:k + 1]
        lo = ya.astype(F32) * wk if lo is None else lo + ya.astype(F32) * wk
        hi = yb.astype(F32) * wk if hi is None else hi + yb.astype(F32) * wk
    x2 = x1_ref[0] + gt2[0] * (jnp.concatenate([lo, hi], axis=1) + shared)
    o_ref[0] = x2 * lax.rsqrt(jnp.mean(x2 * x2, axis=-1, keepdims=True) + EPS) * fg[...]


def _moe_final(yg, w_t, h2p, x1, gt2, fg, sg, su, sd):
    B, T, D = x1.shape
    tm = min(512, T)
    tok = lambda w: pl.BlockSpec((1, tm, w), lambda b, i: (b, i, 0))
    full = lambda shape: pl.BlockSpec(shape, lambda b, i: (0,) * len(shape))
    return pl.pallas_call(
        _moe_final_kernel,
        grid=(B, T // tm),
        in_specs=[pl.BlockSpec((TOP_K, 1, tm, D // 2), lambda b, i: (0, b, i, 0)), tok(SUBLANES),
                  tok(D // 2), tok(D), pl.BlockSpec((1, 1, D), lambda b, i: (b, 0, 0)), full((1, D)),
                  full((D, D_EXPERT)), full((D, D_EXPERT)), full((D_EXPERT, D))],
        out_specs=tok(D),
        out_shape=jax.ShapeDtypeStruct((B, T, D), F32),
        compiler_params=_params(("parallel", "parallel")),
        name="moe_final",
    )(yg, w_t, h2p, x1, gt2, fg, sg, su, sd)


def _moe_routed(h2p, idx, w, pos, counts, x1, gt2, p):
    B, T, half = h2p.shape
    n_slots = B * T * TOP_K
    n_rows = n_slots + N_EXPERTS * MOE_BLOCK
    cnt = counts[:, 0].astype(jnp.int32)
    padded = (cnt + MOE_BLOCK - 1) // MOE_BLOCK * MOE_BLOCK
    pad_end = jnp.cumsum(padded)
    pad_start = pad_end - padded
    dest = _dest_rows(pad_start, idx, pos)
    starts = jnp.arange(n_rows // MOE_BLOCK, dtype=jnp.int32) * MOE_BLOCK
    block_expert = jnp.minimum(jnp.sum((pad_end[None, :] <= starts[:, None]).astype(jnp.int32), axis=1),
                               N_EXPERTS - 1)
    block_valid = jnp.clip((pad_start + cnt)[block_expert] - starts, 0, MOE_BLOCK)
    dest_w = jnp.swapaxes(dest.reshape(B, SUBLANES, T // GATHER_WINDOW, GATHER_WINDOW), 1, 2)
    xs = _scatter_rows(h2p.reshape(B * T, half), dest_w.reshape(-1, SUBLANES, GATHER_WINDOW), n_rows)
    ys = _experts(block_expert, block_valid, xs, p["wg"], p["wu"], p["wd"])
    yg = _gather_rows(ys, jnp.swapaxes(dest[:, :TOP_K], 0, 1).reshape(-1)).reshape(TOP_K, B, T, half)
    return _moe_final(yg, jnp.swapaxes(w, 1, 2), h2p, x1, gt2, p["fg"], p["sg"], p["su"], p["sd"])


def _prepare_weights(w_in, shift_mu, a_w0, a_w_up, a_a0, a_a_up, a_g_up, a_k_k, a_k_a, a_r_k, a_gn_w,
                     a_gn_b, a_proj, b_conv_w, b_a_log, b_dt_bias, b_norm_g, b_proj, w_out, router_w,
                     router_bias, exp_gate, exp_up, exp_down, sh_gate, sh_up, sh_down, norm1_g, norm2_g,
                     final_g):
    W = A_WIDTH
    w = w_in[0]
    c0 = A_COLS
    c1 = c0 + 3 * B_WIDTH
    c2 = c1 + B_WIDTH
    c3 = c2 + 4 * B_HEADS
    pad = jnp.zeros((D_MODEL, SMALL_COLS - 4 * B_HEADS), F32)
    w_a = w[:, :c0].astype(BF16)
    w_b = jnp.concatenate([w[:, c0:c1], w[:, c2:c3], pad], axis=1).astype(BF16)
    w_c = jnp.concatenate([w[:, c1:c2], w[:, c3:]], axis=1).astype(BF16)
    zeros = jnp.zeros((2, A_RANK_W, W), F32)
    up_comb = jnp.concatenate([jnp.concatenate([a_w_up[0], zeros], axis=2),
                               jnp.concatenate([zeros, a_a_up[0]], axis=2)], axis=1)
    head = jnp.arange(W) // A_HEAD_DIM
    ones_bd = (head[:, None] == head[None, :]).astype(BF16)
    small = lambda v: jnp.zeros((1, SMALL_COLS), F32).at[0, 2 * B_HEADS:4 * B_HEADS].set(v.reshape(-1))
    return dict(
        w_a=w_a, w_b=w_b, w_c=w_c, mu=shift_mu[0].reshape(1, A_COLS), w0=a_w0[0], a0=a_a0[0],
        up_comb=up_comb, g_up=a_g_up[0], k_k=a_k_k[0].reshape(1, W), k_a=a_k_a[0].reshape(1, W),
        r_k=a_r_k[0].reshape(1, W), ones_bd=ones_bd, gn_w=a_gn_w[0].reshape(1, W),
        gn_b=a_gn_b[0].reshape(1, W), a_proj=a_proj[0].astype(BF16),
        conv_w=b_conv_w[0].astype(BF16).astype(F32),
        alog=small(b_a_log[0]), dtb=small(b_dt_bias[0]),
        bng=jnp.tile(b_norm_g[0], B_HEADS).reshape(1, B_WIDTH), b_proj=b_proj[0].astype(BF16),
        w_out=w_out[0].astype(BF16), rwt=router_w[0].T, rbias=router_bias[0].reshape(N_EXPERTS, 1),
        wg=exp_gate[0].astype(BF16), wu=exp_up[0].astype(BF16), wd=exp_down[0].astype(BF16),
        sg=sh_gate[0].astype(BF16), su=sh_up[0].astype(BF16), sd=sh_down[0].astype(BF16),
        n1g=norm1_g[0].reshape(1, D_MODEL), n2g=norm2_g[0].reshape(1, D_MODEL),
        fg=final_g.reshape(1, D_MODEL))


def _layer(x, mod, p):
    B = x.shape[0]
    sh1, sc1, gt1, sh2, sc2, gt2 = (m.reshape(B, 1, D_MODEL) for m in jnp.split(mod, 6, axis=-1))
    z_a = _inproj(x, p["n1g"], sc1, sh1, p["w_a"], p["w_a"].shape[1])
    z_b = _inproj(x, p["n1g"], sc1, sh1, p["w_b"], p["w_b"].shape[1])
    z_c = _inproj(x, p["n1g"], sc1, sh1, p["w_c"], p["w_c"].shape[1])
    r, kk, v, g, bonus, lw, kd, bd = _rwkv_prep(z_a, p["mu"], p["w0"], p["a0"], p["up_comb"], p["g_up"],
                                                p["k_k"], p["k_a"], p["r_k"], p["ones_bd"])
    yf, yb = _rwkv_scan(r, kk, v, lw, kd, bd)
    q, k, vv, bg = _gdn_prep(z_b, p["conv_w"], p["alog"], p["dtb"])
    of, ob = _gdn_scan(q, k, vv, bg)
    x1, h2p, idx, w, pos, counts = _mixer_post(yf, yb, bonus, g, of, ob, z_c, x, gt1, sc2, sh2, p["gn_w"],
                                               p["gn_b"], p["ones_bd"], p["a_proj"], p["bng"], p["b_proj"],
                                               p["w_out"], p["n2g"], p["rwt"], p["rbias"])
    return _moe_routed(h2p, idx, w, pos, counts, x1, gt2, p)


def kernel(x_prompt, x_sample, c_prompt, c_sample, ada_w, ada_b, norm1_g, norm2_g, w_in, shift_mu, a_w0, a_w_up, a_a0, a_a_up, a_g_up, a_k_k, a_k_a, a_r_k, a_gn_w, a_gn_b, a_proj, b_conv_w, b_a_log, b_dt_bias, b_norm_g, b_proj, w_out, router_w, router_bias, exp_gate, exp_up, exp_down, sh_gate, sh_up, sh_down, final_g):
    p = _prepare_weights(w_in, shift_mu, a_w0, a_w_up, a_a0, a_a_up, a_g_up, a_k_k, a_k_a, a_r_k, a_gn_w,
                         a_gn_b, a_proj, b_conv_w, b_a_log, b_dt_bias, b_norm_g, b_proj, w_out, router_w,
                         router_bias, exp_gate, exp_up, exp_down, sh_gate, sh_up, sh_down, norm1_g,
                         norm2_g, final_g)
    nb_p, nb_s = c_prompt.shape[0], c_sample.shape[0]
    rows = -(-(nb_p + nb_s) // SUBLANES) * SUBLANES
    c_all = jnp.concatenate([c_prompt, c_sample, jnp.zeros((rows - nb_p - nb_s, D_MODEL), F32)], axis=0)
    mod = _adaln_mod(c_all, ada_w[0], ada_b[0])
    y_prompt = _layer(x_prompt, mod[:nb_p], p)
    y_sample = _layer(x_sample, mod[nb_p:nb_p + nb_s], p)
    return (y_prompt, y_sample)
```

```python
import functools

import jax
import jax.numpy as jnp
from jax import lax
from jax.experimental import pallas as pl
from jax.experimental.pallas import tpu as pltpu
from jax.experimental.pallas import tpu_sc as plsc

F32 = jnp.float32
BF16 = jnp.bfloat16
ACT = jnp.bfloat16
HIGHEST = lax.Precision.HIGHEST

D_MODEL = 1024
A_HEADS = 8
A_HEAD_DIM = 64
A_WIDTH = A_HEADS * A_HEAD_DIM
A_RANK_W = 64
A_RANK_A = 64
A_RANK_G = 128
A_GN_EPS = 64e-5
A_COLS = 3 * A_WIDTH + A_RANK_W + A_RANK_A + A_RANK_G
B_HEADS = 8
B_HEAD_DIM = 128
B_WIDTH = B_HEADS * B_HEAD_DIM
B_CONV = 5
CHUNK = 64
N_EXPERTS = 64
TOP_K = 6
N_GROUPS = 8
TOPK_GROUPS = 4
GROUP_SIZE = N_EXPERTS // N_GROUPS
D_EXPERT = 256
ROUTED_SCALE = 2.5
EPS = 1e-6
LANES = 128
SUBLANES = 8
HALO = 16
SMALL_COLS = LANES
MOE_BLOCK = 512
SC_CORES = 2
SC_SUBCORES = 16
SC_WORKERS = SC_CORES * SC_SUBCORES
GATHER_WINDOW = 64
VMEM_LIMIT = 56 * 1024 * 1024
NEG_INF = float("-inf")


def _dot(a, b):
    return jnp.dot(a.astype(BF16), b.astype(BF16), preferred_element_type=F32)


def _dot_nt(a, b):
    return lax.dot_general(a.astype(BF16), b.astype(BF16), (((1,), (1,)), ((), ())),
                           preferred_element_type=F32)


def _dot_tn(a, b):
    return lax.dot_general(a.astype(BF16), b.astype(BF16), (((0,), (0,)), ((), ())),
                           preferred_element_type=F32)


def _dot_hi(a, b):
    return jnp.dot(a, b, precision=HIGHEST, preferred_element_type=F32)


def _split(x):
    hi = x.astype(BF16)
    return hi, (x - hi.astype(F32)).astype(BF16)


def _dot3(a, b, dims=(((1,), (0,)), ((), ()))):
    ah, al = _split(a)
    bh, bl = _split(b)
    d = lambda u, v: lax.dot_general(u, v, dims, preferred_element_type=F32)
    return d(ah, bh) + (d(ah, bl) + d(al, bh))


def _seg_sum(x, ones_bd):
    hi = x.astype(BF16)
    lo = (x - hi.astype(F32)).astype(BF16)
    return (jnp.dot(hi, ones_bd, preferred_element_type=F32)
            + jnp.dot(lo, ones_bd, preferred_element_type=F32))


def _softplus(x):
    return jnp.maximum(x, 0.0) + jnp.log1p(jnp.exp(-jnp.abs(x)))


def _sigmoid(x):
    return 1.0 / (1.0 + jnp.exp(-x))


def _silu(x):
    return x * _sigmoid(x)


def _tri_inverse(a_list, eye):
    n = a_list[0].shape[0]
    xs = [(-a).astype(BF16) for a in a_list]
    ts = [eye - a for a in a_list]
    xs = [jnp.dot(x, x, preferred_element_type=F32).astype(BF16) for x in xs]
    power = 2
    while 2 * power < CHUNK:
        both = [jnp.dot(x, jnp.concatenate([t.astype(BF16), x], axis=1), preferred_element_type=F32)
                for t, x in zip(ts, xs)]
        ts = [t + b[:, 0:n] for t, b in zip(ts, both)]
        xs = [b[:, n:2 * n].astype(BF16) for b in both]
        power *= 2
    return [t + jnp.dot(x, t.astype(BF16), preferred_element_type=F32) for t, x in zip(ts, xs)]


def _params(sem):
    return pltpu.CompilerParams(dimension_semantics=sem, vmem_limit_bytes=VMEM_LIMIT)


def _mod_kernel(c_ref, w_ref, b_ref, o_ref):
    c = c_ref[...]
    o_ref[...] = _dot_hi(_silu(c), w_ref[...]) + b_ref[...]


def _adaln_mod(c, ada_w, ada_b):
    rows = c.shape[0]
    n = ada_w.shape[1]
    return pl.pallas_call(
        _mod_kernel,
        grid=(n // D_MODEL,),
        in_specs=[pl.BlockSpec((rows, D_MODEL), lambda j: (0, 0)),
                  pl.BlockSpec((D_MODEL, D_MODEL), lambda j: (0, j)),
                  pl.BlockSpec((1, D_MODEL), lambda j: (0, j))],
        out_specs=pl.BlockSpec((rows, D_MODEL), lambda j: (0, j)),
        out_shape=jax.ShapeDtypeStruct((rows, n), F32),
        compiler_params=_params(("arbitrary",)),
        name="adaln_mod",
    )(c, ada_w, ada_b.reshape(1, n))


def _rms_mod(x, g, sc, sh):
    y = x * lax.rsqrt(jnp.mean(x * x, axis=-1, keepdims=True) + EPS)
    return (y * g) * (1.0 + sc) + sh


def _inproj_kernel(x_ref, g_ref, sc_ref, sh_ref, w_ref, *o_refs):
    h = _rms_mod(x_ref[0], g_ref[...], sc_ref[0], sh_ref[0]).astype(BF16)
    col = 0
    for o_ref in o_refs:
        n = o_ref.shape[2]
        o_ref[0] = jnp.dot(h, w_ref[:, col:col + n], preferred_element_type=F32).astype(o_ref.dtype)
        col += n


def _inproj(x, g, sc, sh, w, tail=0):
    B, T, D = x.shape
    n = w.shape[1]
    tm = min(512, T)
    widths = [(n - tail, ACT)] + ([(tail, F32)] if tail else [])
    outs = pl.pallas_call(
        _inproj_kernel,
        grid=(B, T // tm),
        in_specs=[pl.BlockSpec((1, tm, D), lambda b, i: (b, i, 0)),
                  pl.BlockSpec((1, D), lambda b, i: (0, 0)),
                  pl.BlockSpec((1, 1, D), lambda b, i: (b, 0, 0)),
                  pl.BlockSpec((1, 1, D), lambda b, i: (b, 0, 0)),
                  pl.BlockSpec((D, n), lambda b, i: (0, 0))],
        out_specs=[pl.BlockSpec((1, tm, wd), lambda b, i: (b, i, 0)) for wd, _ in widths],
        out_shape=[jax.ShapeDtypeStruct((B, T, wd), dt) for wd, dt in widths],
        compiler_params=_params(("parallel", "parallel")),
        name="in_proj",
    )(x, g, sc, sh, w)
    return outs if tail else outs[0]


def _fill_halo(ext_ref, cur, prev8, next8, tt):
    i = pl.program_id(1)
    last = pl.num_programs(1) - 1
    ext_ref[pl.ds(0, SUBLANES), :] = jnp.where(i == 0, 0.0, prev8)
    ext_ref[pl.ds(SUBLANES, tt), :] = cur
    ext_ref[pl.ds(SUBLANES + tt, SUBLANES), :] = jnp.where(i == last, 0.0, next8)


def _halo_specs(tt, width, col_block, seq_len):
    nb = tt // HALO
    last = seq_len // HALO - 1
    return [
        pl.BlockSpec((1, tt, width), lambda b, i: (b, i, col_block)),
        pl.BlockSpec((1, HALO, width), lambda b, i: (b, jnp.maximum(i * nb - 1, 0), col_block)),
        pl.BlockSpec((1, HALO, width), lambda b, i: (b, jnp.minimum((i + 1) * nb, last), col_block)),
    ]


def _halo_rows(zp_ref, zn_ref):
    return zp_ref[0].astype(F32)[HALO - SUBLANES:HALO], zn_ref[0].astype(F32)[0:SUBLANES]


def _rwkv_prep_kernel(z_ref, zp_ref, zn_ref, mu_ref, w0_ref, a0_ref, up_ref, gup_ref, kk_ref, ka_ref,
                      rk_ref, ones_ref,
                      r_out, kk_out, v_out, g_out, bonus_out, lw_out, kd_out, bd_out, ext_scr):
    tt = z_ref.shape[1]
    z = z_ref[0].astype(F32)
    _fill_halo(ext_scr, z, *_halo_rows(zp_ref, zn_ref), tt)
    z_prev = ext_scr[pl.ds(SUBLANES - 1, tt), :]
    z_next = ext_scr[pl.ds(SUBLANES + 1, tt), :]
    zs = z + (0.5 * (z_prev + z_next) - z) * mu_ref[...]
    W = A_WIDTH
    zr, zk, zv = zs[:, 0:W], zs[:, W:2 * W], zs[:, 2 * W:3 * W]
    zwa = zs[:, 3 * W:3 * W + LANES]
    zg = zs[:, 3 * W + LANES:3 * W + 2 * LANES]
    ones_bd = ones_ref[...]
    kk_raw = zk * kk_ref[...]
    kk = kk_raw * lax.rsqrt(_seg_sum(kk_raw * kk_raw, ones_bd) + 1e-6)
    lane = lax.broadcasted_iota(jnp.int32, zwa.shape, 1)
    lhs = jnp.where(lane < A_RANK_W, jnp.tanh(zwa), zwa)
    k_sum = jnp.zeros_like(zk)
    for d in range(2):
        up = _dot(lhs, up_ref[d])
        wl = w0_ref[d:d + 1, :] + up[:, 0:W]
        w_log = -_softplus(-wl) - 0.5
        lw_out[d, 0] = -jnp.exp(w_log)
        a = _sigmoid(a0_ref[d:d + 1, :] + up[:, W:2 * W])
        k_d = zk * (1.0 + (a - 1.0) * ka_ref[...])
        kd_out[d, 0] = k_d.astype(kd_out.dtype)
        bd_out[d, 0] = (kk * a).astype(bd_out.dtype)
        k_sum = k_sum + k_d
    r_out[0] = zr.astype(r_out.dtype)
    kk_out[0] = kk.astype(kk_out.dtype)
    v_out[0] = zv.astype(v_out.dtype)
    g_out[0] = _dot(_sigmoid(zg), gup_ref[...]).astype(g_out.dtype)
    bonus_out[0] = (_seg_sum(zr * k_sum * rk_ref[...], ones_bd) * zv).astype(bonus_out.dtype)


def _rwkv_prep(z_a, mu, w0, a0, up_comb, g_up, k_k, k_a, r_k, ones_bd):
    B, T, _ = z_a.shape
    tt = min(256, T)
    W = A_WIDTH
    full = lambda shape: pl.BlockSpec(shape, lambda b, i: (0,) * len(shape))
    tok = pl.BlockSpec((1, tt, W), lambda b, i: (b, i, 0))
    tok2 = pl.BlockSpec((2, 1, tt, W), lambda b, i: (0, b, i, 0))
    s1 = jax.ShapeDtypeStruct((B, T, W), ACT)
    s2 = jax.ShapeDtypeStruct((2, B, T, W), ACT)
    lw = jax.ShapeDtypeStruct((2, B, T, W), F32)
    return pl.pallas_call(
        _rwkv_prep_kernel,
        grid=(B, T // tt),
        in_specs=_halo_specs(tt, A_COLS, 0, T) + [
            full((1, A_COLS)), full((2, W)), full((2, W)), full((2, LANES, 2 * W)),
            full((A_RANK_G, W)), full((1, W)), full((1, W)), full((1, W)), full((W, W))],
        out_specs=[tok, tok, tok, tok, tok, tok2, tok2, tok2],
        out_shape=[s1, s1, s1, s1, s1, lw, s2, s2],
        scratch_shapes=[pltpu.VMEM((tt + 2 * SUBLANES, A_COLS), F32)],
        compiler_params=_params(("parallel", "parallel")),
        name="rwkv_prep",
    )(z_a, z_a, z_a, mu, w0, a0, up_comb, g_up, k_k, k_a, r_k, ones_bd)


def _chunk_masks(reverse):
    n = 2 * CHUNK
    row = lax.broadcasted_iota(jnp.int32, (n, n), 0)
    col = lax.broadcasted_iota(jnp.int32, (n, n), 1)
    same = (row // CHUNK) == (col // CHUNK)
    ti, tj = row % CHUNK, col % CHUNK
    if reverse:
        return same & (ti < tj), same & (ti <= tj), row == col
    return same & (ti > tj), same & (ti >= tj), row == col


def _cumsum_matrix(reverse):
    row = lax.broadcasted_iota(jnp.int32, (CHUNK, CHUNK), 0)
    col = lax.broadcasted_iota(jnp.int32, (CHUNK, CHUNK), 1)
    return ((row <= col) if reverse else (row >= col)).astype(F32)


def _rwkv_chunk_operands(r, kk, v, k, b, cs, lw, reverse):
    C = CHUNK
    cs_end = cs[0:1, :] if reverse else cs[C - 1:C, :]
    m0 = lax.broadcasted_iota(jnp.int32, (C, LANES), 1) < A_HEAD_DIM

    def stack(x):
        return jnp.concatenate([jnp.where(m0, x, 0.0), jnp.where(m0, 0.0, x)], axis=0)

    g_inv = jnp.exp(-cs)
    g_tail = jnp.exp(cs_end - cs)
    strict, incl, _ = _chunk_masks(reverse)
    return dict(rg=stack(r * jnp.exp(cs)), kkg=stack(kk * jnp.exp(cs - lw)), ki=stack(k * g_inv),
                bi=stack(b * g_inv), kt=stack(k * g_tail), bt=stack(b * g_tail), vs=stack(v),
                g_end=jnp.exp(cs_end), strict=strict, incl=incl)


def _rwkv_chunks(ops, hts):
    C = CHUNK
    n = 2 * C
    eye = _chunk_masks(False)[2].astype(F32)
    ps = [_dot_nt(jnp.concatenate([o["kkg"], o["rg"]], axis=0), jnp.concatenate([o["bi"], o["ki"]], axis=0))
          for o in ops]
    a_ab = [jnp.where(o["strict"], p[0:n, 0:n], 0.0) for o, p in zip(ops, ps)]
    a_rb = [jnp.where(o["incl"], p[n:2 * n, 0:n], 0.0).astype(BF16) for o, p in zip(ops, ps)]
    a_k = [jnp.concatenate([jnp.where(o["strict"], p[0:n, n:2 * n], 0.0),
                            jnp.where(o["incl"], p[n:2 * n, n:2 * n], 0.0)], axis=0).astype(BF16)
           for o, p in zip(ops, ps)]
    vsb = [o["vs"].astype(BF16) for o in ops]
    a_kv = [jnp.dot(a, v, preferred_element_type=F32) for a, v in zip(a_k, vsb)]
    akv = [x[0:n] for x in a_kv]
    arkv = [x[n:2 * n] for x in a_kv]
    tinv = _tri_inverse(a_ab, eye)
    wu = [_dot(t, jnp.concatenate([o["kkg"], x], axis=1)).astype(BF16)
          for t, o, x in zip(tinv, ops, akv)]
    r2 = [jnp.dot(a, w, preferred_element_type=F32) for a, w in zip(a_rb, wu)]
    btb = [o["bt"].astype(BF16) for o in ops]
    btw = [_dot_tn(b, w[:, 0:LANES]) for b, w in zip(btb, wu)]
    hloc = [_dot_tn(jnp.concatenate([v, -w[:, LANES:2 * LANES]], axis=0),
                    jnp.concatenate([o["kt"].astype(BF16), b], axis=0))
            for v, o, w, b in zip(vsb, ops, wu, btb)]
    ys, hts_new = [], []
    for o, r, yl, ht, bw, hl in zip(ops, r2, arkv, hts, btw, hloc):
        q_s = o["rg"] - r[:, 0:LANES]
        yloc_s = yl - r[:, LANES:2 * LANES]
        ys.append(yloc_s[0:C] + yloc_s[C:n] + _dot_nt(q_s[0:C] + q_s[C:n], ht))
        hts_new.append(ht * o["g_end"] - _dot_nt(ht, bw) + hl)
    return ys, hts_new


def _rwkv_scan_kernel(rf, kkf, vf, lwf, kf, bf, rb, kkb, vb, lwb, kb, bb, yf_out, yb_out, h_scr):
    @pl.when(pl.program_id(1) == 0)
    def _():
        h_scr[...] = jnp.zeros_like(h_scr)

    n_pairs = A_WIDTH // LANES
    sls = [slice(i * LANES, (i + 1) * LANES) for i in range(n_pairs)]
    ops, hts = [], []
    for d, (r_ref, kk_ref, v_ref, lw_ref, k_ref, b_ref) in enumerate(
            ((rf, kkf, vf, lwf, kf, bf), (rb, kkb, vb, lwb, kb, bb))):
        reverse = d == 1
        lw_all = lw_ref[0, 0]
        cs_all = _dot_hi(_cumsum_matrix(reverse), lw_all)
        f32 = lambda a: a.astype(F32)
        ops += [_rwkv_chunk_operands(f32(r_ref[0, :, sl]), f32(kk_ref[0, :, sl]), f32(v_ref[0, :, sl]),
                                     f32(k_ref[0, 0, :, sl]), f32(b_ref[0, 0, :, sl]), cs_all[:, sl],
                                     lw_all[:, sl], reverse) for sl in sls]
        hts += [h_scr[d, i] for i in range(n_pairs)]
    ys, hts = _rwkv_chunks(ops, hts)
    for d, y_out in enumerate((yf_out, yb_out)):
        for i, sl in enumerate(sls):
            h_scr[d, i] = hts[d * n_pairs + i]
            y_out[0, :, sl] = ys[d * n_pairs + i].astype(y_out.dtype)


def _rwkv_scan(r, kk, v, lw, kd, bd):
    B, T, W = r.shape
    N = T // CHUNK
    fwd = pl.BlockSpec((1, CHUNK, W), lambda b, c: (b, c, 0))
    bwd = pl.BlockSpec((1, CHUNK, W), lambda b, c: (b, N - 1 - c, 0))
    fwd2 = pl.BlockSpec((1, 1, CHUNK, W), lambda b, c: (0, b, c, 0))
    bwd2 = pl.BlockSpec((1, 1, CHUNK, W), lambda b, c: (1, b, N - 1 - c, 0))
    out = jax.ShapeDtypeStruct((B, T, W), ACT)
    return pl.pallas_call(
        _rwkv_scan_kernel,
        grid=(B, N),
        in_specs=[fwd, fwd, fwd, fwd2, fwd2, fwd2, bwd, bwd, bwd, bwd2, bwd2, bwd2],
        out_specs=[fwd, bwd],
        out_shape=[out, out],
        scratch_shapes=[pltpu.VMEM((2, W // LANES, LANES, LANES), F32)],
        compiler_params=_params(("parallel", "arbitrary")),
        name="rwkv_scan",
    )(r, kk, v, lw, kd, bd, r, kk, v, lw, kd, bd)


def _gdn_prep_kernel(z_ref, zp_ref, zn_ref, small_ref, cw_ref, alog_ref, dtb_ref,
                     q_out, k_out, v_out, bg_out, acc_scr):
    tt = z_ref.shape[1]
    i = pl.program_id(1)
    last = pl.num_programs(1) - 1
    half = B_CONV // 2
    zb = z_ref[0].astype(BF16)
    offsets = [j - half for j in range(B_CONV) if j != half]
    row = lax.broadcasted_iota(jnp.int32, (len(offsets) * tt, tt), 0)
    col = lax.broadcasted_iota(jnp.int32, (len(offsets) * tt, tt), 1)
    blk = row // tt
    off = jnp.where(blk < half, blk - half, blk - half + 1)
    shifted = jnp.dot((col == row - blk * tt + off).astype(BF16), zb, preferred_element_type=F32)
    acc = zb.astype(F32) * cw_ref[half:half + 1, :]
    for m, o in enumerate(offsets):
        acc = acc + shifted[m * tt:(m + 1) * tt] * cw_ref[o + half:o + half + 1, :]
    acc_scr[...] = acc
    prev8, next8 = _halo_rows(zp_ref, zn_ref)
    prev8 = jnp.where(i == 0, 0.0, prev8).astype(BF16).astype(F32)
    next8 = jnp.where(i == last, 0.0, next8).astype(BF16).astype(F32)
    r8 = lax.broadcasted_iota(jnp.int32, (SUBLANES, 1), 0)
    top = bot = None
    for o in offsets:
        w = cw_ref[o + half:o + half + 1, :]
        if o < 0:
            f = jnp.where(r8 < -o, pltpu.roll(prev8, -o, axis=0), 0.0) * w
            top = f if top is None else top + f
        else:
            f = jnp.where(r8 >= SUBLANES - o, pltpu.roll(next8, SUBLANES - o, axis=0), 0.0) * w
            bot = f if bot is None else bot + f
    acc_scr[pl.ds(0, SUBLANES), :] += top
    acc_scr[pl.ds(tt - SUBLANES, SUBLANES), :] += bot
    qkv = _silu(acc_scr[...])
    for h in range(B_HEADS):
        for idx, out in enumerate((q_out, k_out)):
            x = qkv[:, idx * B_WIDTH + h * B_HEAD_DIM: idx * B_WIDTH + (h + 1) * B_HEAD_DIM]
            xn = x * lax.rsqrt(jnp.sum(x * x, axis=-1, keepdims=True) + 1e-6)
            if idx == 0:
                xn = xn * (B_HEAD_DIM ** -0.5)
            out[0, :, h * B_HEAD_DIM:(h + 1) * B_HEAD_DIM] = xn.astype(out.dtype)
    v_out[0] = qkv[:, 2 * B_WIDTH:3 * B_WIDTH].astype(v_out.dtype)
    s = small_ref[0]
    lane = lax.broadcasted_iota(jnp.int32, s.shape, 1)
    beta = _sigmoid(s)
    g = -jnp.exp(alog_ref[...]) * _softplus(s + dtb_ref[...])
    bg_out[0] = jnp.where(lane < 2 * B_HEADS, beta, g)


def _gdn_prep(z_b, z_small, conv_w, alog_vec, dtb_vec):
    B, T, Wq = z_b.shape
    tt = min(256, T)
    full = lambda shape: pl.BlockSpec(shape, lambda b, i: (0,) * len(shape))
    tok = pl.BlockSpec((1, tt, B_WIDTH), lambda b, i: (b, i, 0))
    small = pl.BlockSpec((1, tt, SMALL_COLS), lambda b, i: (b, i, 0))
    s1 = jax.ShapeDtypeStruct((B, T, B_WIDTH), ACT)
    return pl.pallas_call(
        _gdn_prep_kernel,
        grid=(B, T // tt),
        in_specs=_halo_specs(tt, Wq, 0, T) + [small, full((B_CONV, Wq)), full((1, SMALL_COLS)),
                                            full((1, SMALL_COLS))],
        out_specs=[tok, tok, tok, pl.BlockSpec((1, tt, SMALL_COLS), lambda b, i: (b, i, 0))],
        out_shape=[s1, s1, s1, jax.ShapeDtypeStruct((B, T, SMALL_COLS), F32)],
        scratch_shapes=[pltpu.VMEM((tt, Wq), F32)],
        compiler_params=_params(("parallel", "parallel")),
        name="gdn_prep",
    )(z_b, z_b, z_b, z_small, conv_w, alog_vec, dtb_vec)


def _gdn_chunk_operands(q0, q1, k0, k1, v0, v1, gc2, gct2, bg2, j0, j1, reverse):
    C = CHUNK
    n = 2 * C
    strict, incl, _ = _chunk_masks(reverse)
    top = lax.broadcasted_iota(jnp.int32, (n, 1), 0) < C
    left = lax.broadcasted_iota(jnp.int32, (1, n), 1) < C
    g0, g1 = j0 + 2 * B_HEADS, j1 + 2 * B_HEADS
    gcol = jnp.where(top, gc2[:, g0:g0 + 1], gc2[:, g1:g1 + 1])
    grow = jnp.where(left, gct2[g0:g0 + 1, :], gct2[g1:g1 + 1, :])
    beta = jnp.where(top, bg2[:, j0:j0 + 1], bg2[:, j1:j1 + 1])
    e = 0 if reverse else C - 1
    glast = jnp.where(top, gc2[e:e + 1, g0:g0 + 1], gc2[e:e + 1, g1:g1 + 1])
    ks = jnp.concatenate([k0, k1], axis=0)
    qs = jnp.concatenate([q0, q1], axis=0)
    vs = jnp.concatenate([v0, v1], axis=0)
    egc = jnp.exp(gcol)
    kb = ks * beta
    return dict(gam=jnp.exp(jnp.where(incl, gcol - grow, NEG_INF)), strict=strict, ks=ks,
                kbq=jnp.concatenate([kb, qs], axis=0),
                rhs=jnp.concatenate([vs * beta, kb * egc], axis=1),
                qh=(qs * egc).astype(BF16), ktail=(ks * jnp.exp(glast - gcol)).astype(BF16),
                decay=jnp.exp(glast))


def _gdn_chunks(ops, states):
    C = CHUNK
    n = 2 * C
    eye = _chunk_masks(False)[2].astype(F32)
    ps = [_dot_nt(o["kbq"], o["ks"]) for o in ops]
    lower = [jnp.where(o["strict"], p[0:n] * o["gam"], 0.0) for o, p in zip(ops, ps)]
    aqk = [(p[n:2 * n] * o["gam"]).astype(BF16) for o, p in zip(ops, ps)]
    tinv = [t.astype(BF16) for t in _tri_inverse(lower, eye)]
    uw = [jnp.dot(t, o["rhs"].astype(BF16), preferred_element_type=F32) for t, o in zip(tinv, ops)]
    lx = []
    for l, x in zip(lower, uw):
        lh, ll = _split(l)
        xh, xl = _split(x)
        first = jnp.dot(lh, jnp.concatenate([xh, xl], axis=1), preferred_element_type=F32)
        w2 = x.shape[1]
        lx.append(first[:, 0:w2] + (first[:, w2:2 * w2] + jnp.dot(ll, xh, preferred_element_type=F32)))
    resid = [o["rhs"] - x - y for o, x, y in zip(ops, uw, lx)]
    uw = [x + jnp.dot(t, r.astype(BF16), preferred_element_type=F32) for x, t, r in zip(uw, tinv, resid)]
    sb = [(s0.astype(BF16), s1.astype(BF16)) for s0, s1 in states]
    wq = [[jnp.dot(jnp.concatenate([u[h * C:(h + 1) * C, LANES:].astype(BF16), o["qh"][h * C:(h + 1) * C]],
                                   axis=0), s[h], preferred_element_type=F32) for h in range(2)]
          for u, o, s in zip(uw, ops, sb)]
    ws = [jnp.concatenate([x[0][0:C], x[1][0:C]], axis=0) for x in wq]
    qss = [jnp.concatenate([x[0][C:n], x[1][C:n]], axis=0) for x in wq]
    v_new = [(u[:, 0:LANES] - w).astype(BF16) for u, w in zip(uw, ws)]
    outs = [q + jnp.dot(a, v, preferred_element_type=F32) for q, a, v in zip(qss, aqk, v_new)]
    new_states = []
    for o, v, (s0, s1) in zip(ops, v_new, states):
        new_states.append((s0 * o["decay"][0:1, :] + _dot_tn(o["ktail"][0:C], v[0:C]),
                           s1 * o["decay"][C:C + 1, :] + _dot_tn(o["ktail"][C:n], v[C:n])))
    return [(x[0:C], x[C:n]) for x in outs], new_states


def _gdn_scan_kernel(qf, kf, vf, bgf, qb, kb, vb, bgb, of_out, ob_out, s_scr):
    @pl.when(pl.program_id(1) == 0)
    def _():
        s_scr[...] = jnp.zeros_like(s_scr)

    D = B_HEAD_DIM
    n_pairs = B_HEADS // 2
    sl = lambda h: slice(h * D, (h + 1) * D)
    ops, states = [], []
    for d, (q_ref, k_ref, v_ref, bg_ref) in enumerate(((qf, kf, vf, bgf), (qb, kb, vb, bgb))):
        reverse = d == 1
        bg = bg_ref[0]
        gc = _dot_hi(_cumsum_matrix(reverse), bg)
        bg2 = jnp.concatenate([bg, bg], axis=0)
        gc2 = jnp.concatenate([gc, gc], axis=0)
        gct2 = gc2.T
        for i in range(n_pairs):
            h0, h1 = 2 * i, 2 * i + 1
            f32 = lambda a: a.astype(F32)
            ops.append(_gdn_chunk_operands(
                f32(q_ref[0, :, sl(h0)]), f32(q_ref[0, :, sl(h1)]), f32(k_ref[0, :, sl(h0)]),
                f32(k_ref[0, :, sl(h1)]), f32(v_ref[0, :, sl(h0)]), f32(v_ref[0, :, sl(h1)]), gc2, gct2, bg2,
                d * B_HEADS + h0, d * B_HEADS + h1, reverse))
            states.append((s_scr[d, h0], s_scr[d, h1]))
    outs, states = _gdn_chunks(ops, states)
    for d, o_out in enumerate((of_out, ob_out)):
        for i in range(n_pairs):
            h0, h1 = 2 * i, 2 * i + 1
            (o0, o1), (s0, s1) = outs[d * n_pairs + i], states[d * n_pairs + i]
            s_scr[d, h0] = s0
            s_scr[d, h1] = s1
            o_out[0, :, sl(h0)] = o0.astype(o_out.dtype)
            o_out[0, :, sl(h1)] = o1.astype(o_out.dtype)


def _gdn_scan(q, k, v, bg):
    B, T, W = q.shape
    N = T // CHUNK
    fwd = pl.BlockSpec((1, CHUNK, W), lambda b, c: (b, c, 0))
    bwd = pl.BlockSpec((1, CHUNK, W), lambda b, c: (b, N - 1 - c, 0))
    sfwd = pl.BlockSpec((1, CHUNK, SMALL_COLS), lambda b, c: (b, c, 0))
    sbwd = pl.BlockSpec((1, CHUNK, SMALL_COLS), lambda b, c: (b, N - 1 - c, 0))
    out = jax.ShapeDtypeStruct((B, T, W), ACT)
    return pl.pallas_call(
        _gdn_scan_kernel,
        grid=(B, N),
        in_specs=[fwd, fwd, fwd, sfwd, bwd, bwd, bwd, sbwd],
        out_specs=[fwd, bwd],
        out_shape=[out, out],
        scratch_shapes=[pltpu.VMEM((2, B_HEADS, B_HEAD_DIM, B_HEAD_DIM), F32)],
        compiler_params=_params(("parallel", "arbitrary")),
        name="gdn_scan",
    )(q, k, v, bg, q, k, v, bg)


def _pack_bf16_pairs(x):
    n = x.shape[1] // 2
    hi = lax.bitcast_convert_type(x[:, :n].astype(BF16).astype(F32), jnp.int32)
    lo = lax.bitcast_convert_type(x[:, n:].astype(BF16).astype(F32), jnp.int32)
    return hi | lax.shift_right_logical(lo, 16)


def _unpack_bf16_pairs(p):
    a = lax.bitcast_convert_type(p & jnp.int32(-65536), F32)
    b = lax.bitcast_convert_type(lax.shift_left(p, 16), F32)
    return a.astype(BF16), b.astype(BF16)


def _route(scores, biased, base):
    tt = scores.shape[-1]
    shape3 = (N_GROUPS, GROUP_SIZE, tt)
    s3 = scores.reshape(shape3)
    b3 = biased.reshape(shape3)
    jid = lax.broadcasted_iota(jnp.int32, shape3, 1).astype(F32)
    gid = lax.broadcasted_iota(jnp.int32, shape3, 0).astype(F32)
    m1 = jnp.max(b3, axis=1, keepdims=True)
    first = jnp.min(jnp.where(b3 == m1, jid, float(GROUP_SIZE)), axis=1, keepdims=True)
    m2 = jnp.max(jnp.where(jid == first, NEG_INF, b3), axis=1, keepdims=True)
    gs = m1 + m2
    grp = lax.broadcasted_iota(jnp.int32, (N_GROUPS, 1, tt), 0).astype(F32)
    keep = jnp.zeros((N_GROUPS, 1, tt), F32)
    for _ in range(TOPK_GROUPS):
        m = jnp.max(gs, axis=0, keepdims=True)
        pick = grp == jnp.min(jnp.where(gs == m, grp, float(N_GROUPS)), axis=0, keepdims=True)
        keep = jnp.where(pick, 1.0, keep)
        gs = jnp.where(pick, NEG_INF, gs)
    work = jnp.where(keep > 0.0, b3, NEG_INF)
    eid = gid * float(GROUP_SIZE) + jid
    chosen = jnp.zeros(shape3, F32)
    sum01 = lambda a: jnp.sum(jnp.sum(a, axis=1, keepdims=True), axis=0, keepdims=True)
    ids, raw = [], []
    for _ in range(TOP_K):
        m = jnp.max(jnp.max(work, axis=1, keepdims=True), axis=0, keepdims=True)
        cand = jnp.where(work == m, eid, float(N_EXPERTS))
        first = jnp.min(jnp.min(cand, axis=1, keepdims=True), axis=0, keepdims=True)
        pick = eid == first
        ids.append(first)
        raw.append(sum01(jnp.where(pick, s3, 0.0)))
        chosen = jnp.where(pick, 1.0, chosen)
        work = jnp.where(pick, NEG_INF, work)
    total = raw[0]
    for r in raw[1:]:
        total = total + r
    flat = chosen.reshape(N_EXPERTS, tt)
    earlier = (lax.broadcasted_iota(jnp.int32, (tt, tt), 0)
               < lax.broadcasted_iota(jnp.int32, (tt, tt), 1)).astype(BF16)
    prefix = (jnp.dot(flat.astype(BF16), earlier, preferred_element_type=F32) + base).reshape(shape3)
    row = lax.broadcasted_iota(jnp.int32, (SUBLANES, tt), 0)
    idx8 = jnp.zeros((SUBLANES, tt), F32)
    w8 = jnp.zeros((SUBLANES, tt), F32)
    pos8 = jnp.zeros((SUBLANES, tt), F32)
    for k in range(TOP_K):
        rank = sum01(jnp.where(eid == ids[k], prefix, 0.0))
        idx8 = jnp.where(row == k, ids[k].reshape(1, tt), idx8)
        w8 = jnp.where(row == k, (raw[k] / total * ROUTED_SCALE).reshape(1, tt), w8)
        pos8 = jnp.where(row == k, rank.reshape(1, tt), pos8)
    counts = jnp.sum(flat, axis=1, keepdims=True)
    return idx8.astype(jnp.int32), w8, pos8.astype(jnp.int32), counts


def _mixer_post_kernel(yf, yb, bonus, g, of, ob, zz, zga, zgb, x_ref, gt1, sc2, sh2,
                       gnw, gnb, ones_ref, aproj, bng, bproj, wout, n2g, rwt, rbias,
                       x1_out, h2_out, idx_out, w_out_ref, pos_out, counts_out):
    @pl.when((pl.program_id(0) == 0) & (pl.program_id(1) == 0))
    def _():
        counts_out[...] = jnp.zeros_like(counts_out)

    ones_bd = ones_ref[...]
    f32 = lambda ref: ref[0].astype(F32)
    y = f32(yf) + f32(yb)
    mean = _seg_sum(y, ones_bd) * (1.0 / A_HEAD_DIM)
    dlt = y - mean
    var = _seg_sum(dlt * dlt, ones_bd) * (1.0 / A_HEAD_DIM)
    yn = dlt * lax.rsqrt(var + A_GN_EPS) * gnw[...] + gnb[...]
    y_a = _dot((yn + f32(bonus)) * f32(g), aproj[...])

    o = f32(of) + f32(ob)
    z = f32(zz)
    parts = []
    for h in range(B_HEADS):
        oh = o[:, h * B_HEAD_DIM:(h + 1) * B_HEAD_DIM]
        parts.append(oh * lax.rsqrt(jnp.mean(oh * oh, axis=-1, keepdims=True) + EPS))
    on = jnp.concatenate(parts, axis=1) * bng[...] * _silu(z)
    y_b = _dot(on, bproj[...])

    u = _sigmoid(f32(zga)) * y_a + _sigmoid(f32(zgb)) * y_b
    x1 = x_ref[0] + gt1[0] * _dot(u, wout[...])
    x1_out[0] = x1
    h2 = _rms_mod(x1, n2g[...], sc2[0], sh2[0])
    h2_out[0] = _pack_bf16_pairs(h2)
    logits_t = _dot_nt(rwt[...], h2)
    scores = _sigmoid(logits_t)
    base = counts_out[:, 0:1]
    idx8, w8, pos8, counts = _route(scores, scores + rbias[...], base)
    idx_out[0] = idx8
    w_out_ref[0] = w8
    pos_out[0] = pos8
    counts_out[...] = counts_out[...] + counts


def _mixer_post(yf, yb, bonus, g, of, ob, z_c, x, gt1, sc2, sh2, gn_w, gn_b, ones_bd, a_proj, bng,
                b_proj, w_out, n2g, rwt, rbias):
    B, T, D = x.shape
    tt = min(256, T)
    full = lambda shape: pl.BlockSpec(shape, lambda b, i: (0,) * len(shape))
    tokw = lambda w, col=0: pl.BlockSpec((1, tt, w), lambda b, i: (b, i, col))
    modrow = pl.BlockSpec((1, 1, D), lambda b, i: (b, 0, 0))
    pick = pl.BlockSpec((1, SUBLANES, tt), lambda b, i: (b, 0, i))
    W = A_WIDTH
    return pl.pallas_call(
        _mixer_post_kernel,
        grid=(B, T // tt),
        in_specs=[tokw(W), tokw(W), tokw(W), tokw(W), tokw(D), tokw(D),
                  tokw(D, 0), tokw(D, 1), tokw(D, 2), tokw(D), modrow, modrow, modrow,
                  full((1, W)), full((1, W)), full((W, W)), full((W, D)), full((1, D)), full((D, D)),
                  full((D, D)), full((1, D)), full((N_EXPERTS, D)), full((N_EXPERTS, 1))],
        out_specs=[tokw(D), tokw(D // 2), pick, pick, pick,
                   pl.BlockSpec((N_EXPERTS, LANES), lambda b, i: (0, 0))],
        out_shape=[jax.ShapeDtypeStruct((B, T, D), F32), jax.ShapeDtypeStruct((B, T, D // 2), jnp.int32),
                   jax.ShapeDtypeStruct((B, SUBLANES, T), jnp.int32),
                   jax.ShapeDtypeStruct((B, SUBLANES, T), F32),
                   jax.ShapeDtypeStruct((B, SUBLANES, T), jnp.int32),
                   jax.ShapeDtypeStruct((N_EXPERTS, LANES), F32)],
        compiler_params=_params(("arbitrary", "arbitrary")),
        name="mixer_post",
    )(yf, yb, bonus, g, of, ob, z_c, z_c, z_c, x, gt1, sc2, sh2, gn_w, gn_b, ones_bd, a_proj, bng,
      b_proj, w_out, n2g, rwt, rbias)


def _gather_rows(table, idx):
    n_rows, width = idx.shape[0], table.shape[1]
    per_worker = n_rows // SC_WORKERS
    n_windows = per_worker // GATHER_WINDOW
    mesh = plsc.VectorSubcoreMesh(core_axis_name="c", subcore_axis_name="s")

    @functools.partial(
        pl.kernel, mesh=mesh, out_type=jax.ShapeDtypeStruct((n_rows, width), table.dtype),
        scratch_types=[pltpu.VMEM((GATHER_WINDOW,), jnp.int32),
                       pltpu.VMEM((GATHER_WINDOW, width), table.dtype),
                       pltpu.SemaphoreType.DMA])
    def gather(table_hbm, idx_hbm, out_hbm, idx_v, rows_v, sem):
        worker = lax.axis_index("s") * SC_CORES + lax.axis_index("c")

        @pl.loop(0, n_windows)
        def _(j):
            off = pl.multiple_of(worker * per_worker + j * GATHER_WINDOW, GATHER_WINDOW)
            pltpu.sync_copy(idx_hbm.at[pl.ds(off, GATHER_WINDOW)], idx_v)
            pltpu.async_copy(table_hbm.at[idx_v], rows_v, sem).wait()
            pltpu.sync_copy(rows_v, out_hbm.at[pl.ds(off, GATHER_WINDOW)])

    return gather(table, idx)


def _scatter_rows(rows, dest_w, n_out):
    n_tok, width = rows.shape
    per_worker = n_tok // GATHER_WINDOW // SC_WORKERS
    mesh = plsc.VectorSubcoreMesh(core_axis_name="c", subcore_axis_name="s")

    @functools.partial(
        pl.kernel, mesh=mesh, out_type=jax.ShapeDtypeStruct((n_out, width), rows.dtype),
        scratch_types=[pltpu.VMEM((SUBLANES, GATHER_WINDOW), jnp.int32),
                       pltpu.VMEM((GATHER_WINDOW, width), rows.dtype)])
    def scatter(rows_hbm, dest_hbm, out_hbm, idx_v, rows_v):
        worker = lax.axis_index("s") * SC_CORES + lax.axis_index("c")

        @pl.loop(0, per_worker)
        def _(j):
            win = worker * per_worker + j
            pltpu.sync_copy(dest_hbm.at[win], idx_v)
            pltpu.sync_copy(rows_hbm.at[pl.ds(pl.multiple_of(win * GATHER_WINDOW, GATHER_WINDOW),
                                              GATHER_WINDOW)], rows_v)
            for k in range(TOP_K):
                pltpu.sync_copy(rows_v, out_hbm.at[idx_v.at[k]])

    return scatter(rows, dest_w)


def _dest_kernel(start_ref, idx_ref, pos_ref, dest_ref):
    idx = idx_ref[0]
    dest = jnp.zeros_like(idx)
    for e in range(N_EXPERTS):
        dest = jnp.where(idx == e, start_ref[e], dest)
    dest_ref[0] = dest + pos_ref[0]


def _dest_rows(pad_start, idx, pos):
    B, _, T = idx.shape
    tt = min(2048, T)
    spec = pl.BlockSpec((1, SUBLANES, tt), lambda b, i, start: (b, 0, i))
    return pl.pallas_call(
        _dest_kernel,
        grid_spec=pltpu.PrefetchScalarGridSpec(num_scalar_prefetch=1, grid=(B, T // tt),
                                               in_specs=[spec, spec], out_specs=spec),
        out_shape=jax.ShapeDtypeStruct(idx.shape, jnp.int32),
        compiler_params=_params(("parallel", "parallel")),
        name="dest_rows",
    )(pad_start, idx, pos)


def _expert_kernel(be_ref, valid_ref, xs_ref, wg, wu, wd, ys_ref):
    n_valid = valid_ref[pl.program_id(0)]

    @pl.when(n_valid > 0)
    def _():
        half = D_MODEL // 2
        row = lax.broadcasted_iota(jnp.int32, xs_ref.shape, 0)
        a, b = _unpack_bf16_pairs(jnp.where(row < n_valid, xs_ref[...], 0))
        mm = lambda w: (jnp.dot(a, w[0, 0:half, :], preferred_element_type=F32)
                        + jnp.dot(b, w[0, half:D_MODEL, :], preferred_element_type=F32))
        hid = _silu(mm(wg)) * mm(wu)
        ys_ref[...] = _pack_bf16_pairs(_dot(hid, wd[0]))


def _experts(block_expert, block_valid, xs, wg, wu, wd):
    n_rows, half = xs.shape
    rows = pl.BlockSpec((MOE_BLOCK, half), lambda i, be, valid: (i, 0))
    wspec = lambda shape: pl.BlockSpec(shape, lambda i, be, valid: (be[i], 0, 0))
    return pl.pallas_call(
        _expert_kernel,
        grid_spec=pltpu.PrefetchScalarGridSpec(
            num_scalar_prefetch=2, grid=(n_rows // MOE_BLOCK,),
            in_specs=[rows, wspec((1, D_MODEL, D_EXPERT)), wspec((1, D_MODEL, D_EXPERT)),
                      wspec((1, D_EXPERT, D_MODEL))],
            out_specs=rows),
        out_shape=jax.ShapeDtypeStruct((n_rows, half), jnp.int32),
        compiler_params=_params(("arbitrary",)),
        name="experts",
    )(block_expert, block_valid, xs, wg, wu, wd)


def _moe_final_kernel(yg_ref, w_ref, h_ref, x1_ref, gt2, fg, sg, su, sd, o_ref):
    half = D_MODEL // 2
    a, b = _unpack_bf16_pairs(h_ref[0])
    mm = lambda w: (jnp.dot(a, w[0:half, :], preferred_element_type=F32)
                    + jnp.dot(b, w[half:D_MODEL, :], preferred_element_type=F32))
    shared = _dot(_silu(mm(sg)) * mm(su), sd[...])
    w = w_ref[0]
    lo = hi = None
    for k in range(TOP_K):
        ya, yb = _unpack_bf16_pairs(yg_ref[k, 0])
        wk = w[:, k:k + 1]
        lo = ya.astype(F32) * wk if lo is None else lo + ya.astype(F32) * wk
        hi = yb.astype(F32) * wk if hi is None else hi + yb.astype(F32) * wk
    x2 = x1_ref[0] + gt2[0] * (jnp.concatenate([lo, hi], axis=1) + shared)
    o_ref[0] = x2 * lax.rsqrt(jnp.mean(x2 * x2, axis=-1, keepdims=True) + EPS) * fg[...]


def _moe_final(yg, w_t, h2p, x1, gt2, fg, sg, su, sd):
    B, T, D = x1.shape
    tm = min(512, T)
    tok = lambda w: pl.BlockSpec((1, tm, w), lambda b, i: (b, i, 0))
    full = lambda shape: pl.BlockSpec(shape, lambda b, i: (0,) * len(shape))
    return pl.pallas_call(
        _moe_final_kernel,
        grid=(B, T // tm),
        in_specs=[pl.BlockSpec((TOP_K, 1, tm, D // 2), lambda b, i: (0, b, i, 0)), tok(SUBLANES),
                  tok(D // 2), tok(D), pl.BlockSpec((1, 1, D), lambda b, i: (b, 0, 0)), full((1, D)),
                  full((D, D_EXPERT)), full((D, D_EXPERT)), full((D_EXPERT, D))],
        out_specs=tok(D),
        out_shape=jax.ShapeDtypeStruct((B, T, D), F32),
        compiler_params=_params(("parallel", "parallel")),
        name="moe_final",
    )(yg, w_t, h2p, x1, gt2, fg, sg, su, sd)


def _moe_routed(h2p, idx, w, pos, counts, x1, gt2, p):
    B, T, half = h2p.shape
    n_slots = B * T * TOP_K
    n_rows = n_slots + N_EXPERTS * MOE_BLOCK
    cnt = counts[:, 0].astype(jnp.int32)
    padded = (cnt + MOE_BLOCK - 1) // MOE_BLOCK * MOE_BLOCK
    pad_end = jnp.cumsum(padded)
    pad_start = pad_end - padded
    dest = _dest_rows(pad_start, idx, pos)
    starts = jnp.arange(n_rows // MOE_BLOCK, dtype=jnp.int32) * MOE_BLOCK
    block_expert = jnp.minimum(jnp.sum((pad_end[None, :] <= starts[:, None]).astype(jnp.int32), axis=1),
                               N_EXPERTS - 1)
    block_valid = jnp.clip((pad_start + cnt)[block_expert] - starts, 0, MOE_BLOCK)
    dest_w = jnp.swapaxes(dest.reshape(B, SUBLANES, T // GATHER_WINDOW, GATHER_WINDOW), 1, 2)
    xs = _scatter_rows(h2p.reshape(B * T, half), dest_w.reshape(-1, SUBLANES, GATHER_WINDOW), n_rows)
    ys = _experts(block_expert, block_valid, xs, p["wg"], p["wu"], p["wd"])
    yg = _gather_rows(ys, jnp.swapaxes(dest[:, :TOP_K], 0, 1).reshape(-1)).reshape(TOP_K, B, T, half)
    return _moe_final(yg, jnp.swapaxes(w, 1, 2), h2p, x1, gt2, p["fg"], p["sg"], p["su"], p["sd"])


def _prepare_weights(w_in, shift_mu, a_w0, a_w_up, a_a0, a_a_up, a_g_up, a_k_k, a_k_a, a_r_k, a_gn_w,
                     a_gn_b, a_proj, b_conv_w, b_a_log, b_dt_bias, b_norm_g, b_proj, w_out, router_w,
                     router_bias, exp_gate, exp_up, exp_down, sh_gate, sh_up, sh_down, norm1_g, norm2_g,
                     final_g):
    W = A_WIDTH
    w = w_in[0]
    c0 = A_COLS
    c1 = c0 + 3 * B_WIDTH
    c2 = c1 + B_WIDTH
    c3 = c2 + 4 * B_HEADS
    pad = jnp.zeros((D_MODEL, SMALL_COLS - 4 * B_HEADS), F32)
    w_a = w[:, :c0].astype(BF16)
    w_b = jnp.concatenate([w[:, c0:c1], w[:, c2:c3], pad], axis=1).astype(BF16)
    w_c = jnp.concatenate([w[:, c1:c2], w[:, c3:]], axis=1).astype(BF16)
    zeros = jnp.zeros((2, A_RANK_W, W), F32)
    up_comb = jnp.concatenate([jnp.concatenate([a_w_up[0], zeros], axis=2),
                               jnp.concatenate([zeros, a_a_up[0]], axis=2)], axis=1)
    head = jnp.arange(W) // A_HEAD_DIM
    ones_bd = (head[:, None] == head[None, :]).astype(BF16)
    small = lambda v: jnp.zeros((1, SMALL_COLS), F32).at[0, 2 * B_HEADS:4 * B_HEADS].set(v.reshape(-1))
    return dict(
        w_a=w_a, w_b=w_b, w_c=w_c, mu=shift_mu[0].reshape(1, A_COLS), w0=a_w0[0], a0=a_a0[0],
        up_comb=up_comb, g_up=a_g_up[0], k_k=a_k_k[0].reshape(1, W), k_a=a_k_a[0].reshape(1, W),
        r_k=a_r_k[0].reshape(1, W), ones_bd=ones_bd, gn_w=a_gn_w[0].reshape(1, W),
        gn_b=a_gn_b[0].reshape(1, W), a_proj=a_proj[0].astype(BF16),
        conv_w=b_conv_w[0].astype(BF16).astype(F32),
        alog=small(b_a_log[0]), dtb=small(b_dt_bias[0]),
        bng=jnp.tile(b_norm_g[0], B_HEADS).reshape(1, B_WIDTH), b_proj=b_proj[0].astype(BF16),
        w_out=w_out[0].astype(BF16), rwt=router_w[0].T, rbias=router_bias[0].reshape(N_EXPERTS, 1),
        wg=exp_gate[0].astype(BF16), wu=exp_up[0].astype(BF16), wd=exp_down[0].astype(BF16),
        sg=sh_gate[0].astype(BF16), su=sh_up[0].astype(BF16), sd=sh_down[0].astype(BF16),
        n1g=norm1_g[0].reshape(1, D_MODEL), n2g=norm2_g[0].reshape(1, D_MODEL),
        fg=final_g.reshape(1, D_MODEL))


def _layer(x, mod, p):
    B = x.shape[0]
    sh1, sc1, gt1, sh2, sc2, gt2 = (m.reshape(B, 1, D_MODEL) for m in jnp.split(mod, 6, axis=-1))
    z_a = _inproj(x, p["n1g"], sc1, sh1, p["w_a"])
    z_b, z_small = _inproj(x, p["n1g"], sc1, sh1, p["w_b"], tail=SMALL_COLS)
    z_c = _inproj(x, p["n1g"], sc1, sh1, p["w_c"])
    r, kk, v, g, bonus, lw, kd, bd = _rwkv_prep(z_a, p["mu"], p["w0"], p["a0"], p["up_comb"], p["g_up"],
                                                p["k_k"], p["k_a"], p["r_k"], p["ones_bd"])
    yf, yb = _rwkv_scan(r, kk, v, lw, kd, bd)
    q, k, vv, bg = _gdn_prep(z_b, z_small, p["conv_w"], p["alog"], p["dtb"])
    of, ob = _gdn_scan(q, k, vv, bg)
    x1, h2p, idx, w, pos, counts = _mixer_post(yf, yb, bonus, g, of, ob, z_c, x, gt1, sc2, sh2, p["gn_w"],
                                               p["gn_b"], p["ones_bd"], p["a_proj"], p["bng"], p["b_proj"],
                                               p["w_out"], p["n2g"], p["rwt"], p["rbias"])
    return _moe_routed(h2p, idx, w, pos, counts, x1, gt2, p)


def kernel(x_prompt, x_sample, c_prompt, c_sample, ada_w, ada_b, norm1_g, norm2_g, w_in, shift_mu, a_w0, a_w_up, a_a0, a_a_up, a_g_up, a_k_k, a_k_a, a_r_k, a_gn_w, a_gn_b, a_proj, b_conv_w, b_a_log, b_dt_bias, b_norm_g, b_proj, w_out, router_w, router_bias, exp_gate, exp_up, exp_down, sh_gate, sh_up, sh_down, final_g):
    p = _prepare_weights(w_in, shift_mu, a_w0, a_w_up, a_a0, a_a_up, a_g_up, a_k_k, a_k_a, a_r_k, a_gn_w,
                         a_gn_b, a_proj, b_conv_w, b_a_log, b_dt_bias, b_norm_g, b_proj, w_out, router_w,
                         router_bias, exp_gate, exp_up, exp_down, sh_gate, sh_up, sh_down, norm1_g,
                         norm2_g, final_g)
    nb_p, nb_s = c_prompt.shape[0], c_sample.shape[0]
    rows = -(-(nb_p + nb_s) // SUBLANES) * SUBLANES
    c_all = jnp.concatenate([c_prompt, c_sample, jnp.zeros((rows - nb_p - nb_s, D_MODEL), F32)], axis=0)
    mod = _adaln_mod(c_all, ada_w[0], ada_b[0])
    y_prompt = _layer(x_prompt, mod[:nb_p], p)
    y_sample = _layer(x_sample, mod[nb_p:nb_p + nb_s], p)
    return (y_prompt, y_sample)
```

```python
import functools

import jax
import jax.numpy as jnp
from jax import lax
from jax.experimental import pallas as pl
from jax.experimental.pallas import tpu as pltpu
from jax.experimental.pallas import tpu_sc as plsc

F32 = jnp.float32
BF16 = jnp.bfloat16
ACT = jnp.bfloat16
HIGHEST = lax.Precision.HIGHEST

D_MODEL = 1024
A_HEADS = 8
A_HEAD_DIM = 64
A_WIDTH = A_HEADS * A_HEAD_DIM
A_RANK_W = 64
A_RANK_A = 64
A_RANK_G = 128
A_GN_EPS = 64e-5
A_COLS = 3 * A_WIDTH + A_RANK_W + A_RANK_A + A_RANK_G
B_HEADS = 8
B_HEAD_DIM = 128
B_WIDTH = B_HEADS * B_HEAD_DIM
B_CONV = 5
CHUNK = 64
SCAN_SUB = 4
N_EXPERTS = 64
TOP_K = 6
N_GROUPS = 8
TOPK_GROUPS = 4
GROUP_SIZE = N_EXPERTS // N_GROUPS
D_EXPERT = 256
ROUTED_SCALE = 2.5
EPS = 1e-6
LANES = 128
SUBLANES = 8
HALO = 16
SMALL_COLS = LANES
MOE_BLOCK = 512
SC_CORES = 2
SC_SUBCORES = 16
SC_WORKERS = SC_CORES * SC_SUBCORES
GATHER_WINDOW = 64
VMEM_LIMIT = 56 * 1024 * 1024
NEG_INF = float("-inf")


def _dot(a, b):
    return jnp.dot(a.astype(BF16), b.astype(BF16), preferred_element_type=F32)


def _dot_nt(a, b):
    return lax.dot_general(a.astype(BF16), b.astype(BF16), (((1,), (1,)), ((), ())),
                           preferred_element_type=F32)


def _dot_tn(a, b):
    return lax.dot_general(a.astype(BF16), b.astype(BF16), (((0,), (0,)), ((), ())),
                           preferred_element_type=F32)


def _dot_hi(a, b):
    return jnp.dot(a, b, precision=HIGHEST, preferred_element_type=F32)


def _split(x):
    hi = x.astype(BF16)
    return hi, (x - hi.astype(F32)).astype(BF16)


def _dot3(a, b, dims=(((1,), (0,)), ((), ()))):
    ah, al = _split(a)
    bh, bl = _split(b)
    d = lambda u, v: lax.dot_general(u, v, dims, preferred_element_type=F32)
    return d(ah, bh) + (d(ah, bl) + d(al, bh))


def _seg_sum(x, ones_bd):
    hi = x.astype(BF16)
    lo = (x - hi.astype(F32)).astype(BF16)
    return (jnp.dot(hi, ones_bd, preferred_element_type=F32)
            + jnp.dot(lo, ones_bd, preferred_element_type=F32))


def _softplus(x):
    return jnp.maximum(x, 0.0) + jnp.log1p(jnp.exp(-jnp.abs(x)))


def _sigmoid(x):
    return 1.0 / (1.0 + jnp.exp(-x))


def _silu(x):
    return x * _sigmoid(x)


def _tri_inverse(a_list, eye):
    n = a_list[0].shape[0]
    xs = [(-a).astype(BF16) for a in a_list]
    ts = [eye - a for a in a_list]
    xs = [jnp.dot(x, x, preferred_element_type=F32).astype(BF16) for x in xs]
    power = 2
    while 2 * power < CHUNK:
        both = [jnp.dot(x, jnp.concatenate([t.astype(BF16), x], axis=1), preferred_element_type=F32)
                for t, x in zip(ts, xs)]
        ts = [t + b[:, 0:n] for t, b in zip(ts, both)]
        xs = [b[:, n:2 * n].astype(BF16) for b in both]
        power *= 2
    return [t + jnp.dot(x, t.astype(BF16), preferred_element_type=F32) for t, x in zip(ts, xs)]


def _params(sem, **extra):
    return pltpu.CompilerParams(dimension_semantics=sem, vmem_limit_bytes=VMEM_LIMIT, **extra)


def _mod_kernel(c_ref, w_ref, b_ref, o_ref):
    c = c_ref[...]
    o_ref[...] = _dot_hi(_silu(c), w_ref[...]) + b_ref[...]


def _adaln_mod(c, ada_w, ada_b):
    rows = c.shape[0]
    n = ada_w.shape[1]
    return pl.pallas_call(
        _mod_kernel,
        grid=(n // D_MODEL,),
        in_specs=[pl.BlockSpec((rows, D_MODEL), lambda j: (0, 0)),
                  pl.BlockSpec((D_MODEL, D_MODEL), lambda j: (0, j)),
                  pl.BlockSpec((1, D_MODEL), lambda j: (0, j))],
        out_specs=pl.BlockSpec((rows, D_MODEL), lambda j: (0, j)),
        out_shape=jax.ShapeDtypeStruct((rows, n), F32),
        compiler_params=_params(("arbitrary",)),
        name="adaln_mod",
    )(c, ada_w, ada_b.reshape(1, n))


def _rms_mod(x, g, sc, sh):
    y = x * lax.rsqrt(jnp.mean(x * x, axis=-1, keepdims=True) + EPS)
    return (y * g) * (1.0 + sc) + sh


def _inproj_kernel(x_ref, g_ref, sc_ref, sh_ref, w_ref, *o_refs):
    h = _rms_mod(x_ref[0], g_ref[...], sc_ref[0], sh_ref[0]).astype(BF16)
    col = 0
    for o_ref in o_refs:
        n = o_ref.shape[2]
        o_ref[0] = jnp.dot(h, w_ref[:, col:col + n], preferred_element_type=F32).astype(o_ref.dtype)
        col += n


def _inproj(x, g, sc, sh, w, tail=0):
    B, T, D = x.shape
    n = w.shape[1]
    tm = min(512, T)
    widths = [(n - tail, ACT)] + ([(tail, F32)] if tail else [])
    outs = pl.pallas_call(
        _inproj_kernel,
        grid=(B, T // tm),
        in_specs=[pl.BlockSpec((1, tm, D), lambda b, i: (b, i, 0)),
                  pl.BlockSpec((1, D), lambda b, i: (0, 0)),
                  pl.BlockSpec((1, 1, D), lambda b, i: (b, 0, 0)),
                  pl.BlockSpec((1, 1, D), lambda b, i: (b, 0, 0)),
                  pl.BlockSpec((D, n), lambda b, i: (0, 0))],
        out_specs=[pl.BlockSpec((1, tm, wd), lambda b, i: (b, i, 0)) for wd, _ in widths],
        out_shape=[jax.ShapeDtypeStruct((B, T, wd), dt) for wd, dt in widths],
        compiler_params=_params(("parallel", "parallel")),
        name="in_proj",
    )(x, g, sc, sh, w)
    return outs if tail else outs[0]


def _fill_halo(ext_ref, cur, prev8, next8, tt):
    i = pl.program_id(1)
    last = pl.num_programs(1) - 1
    ext_ref[pl.ds(0, SUBLANES), :] = jnp.where(i == 0, 0.0, prev8)
    ext_ref[pl.ds(SUBLANES, tt), :] = cur
    ext_ref[pl.ds(SUBLANES + tt, SUBLANES), :] = jnp.where(i == last, 0.0, next8)


def _halo_specs(tt, width, col_block, seq_len):
    nb = tt // HALO
    last = seq_len // HALO - 1
    return [
        pl.BlockSpec((1, tt, width), lambda b, i: (b, i, col_block)),
        pl.BlockSpec((1, HALO, width), lambda b, i: (b, jnp.maximum(i * nb - 1, 0), col_block)),
        pl.BlockSpec((1, HALO, width), lambda b, i: (b, jnp.minimum((i + 1) * nb, last), col_block)),
    ]


def _halo_rows(zp_ref, zn_ref):
    return zp_ref[0].astype(F32)[HALO - SUBLANES:HALO], zn_ref[0].astype(F32)[0:SUBLANES]


def _rwkv_prep_kernel(z_ref, zp_ref, zn_ref, mu_ref, w0_ref, a0_ref, up_ref, gup_ref, kk_ref, ka_ref,
                      rk_ref, ones_ref,
                      r_out, kk_out, v_out, g_out, bonus_out, lw_out, kd_out, bd_out, ext_scr):
    tt = z_ref.shape[1]
    z = z_ref[0].astype(F32)
    _fill_halo(ext_scr, z, *_halo_rows(zp_ref, zn_ref), tt)
    z_prev = ext_scr[pl.ds(SUBLANES - 1, tt), :]
    z_next = ext_scr[pl.ds(SUBLANES + 1, tt), :]
    zs = z + (0.5 * (z_prev + z_next) - z) * mu_ref[...]
    W = A_WIDTH
    zr, zk, zv = zs[:, 0:W], zs[:, W:2 * W], zs[:, 2 * W:3 * W]
    zwa = zs[:, 3 * W:3 * W + LANES]
    zg = zs[:, 3 * W + LANES:3 * W + 2 * LANES]
    ones_bd = ones_ref[...]
    kk_raw = zk * kk_ref[...]
    kk = kk_raw * lax.rsqrt(_seg_sum(kk_raw * kk_raw, ones_bd) + 1e-6)
    lane = lax.broadcasted_iota(jnp.int32, zwa.shape, 1)
    lhs = jnp.where(lane < A_RANK_W, jnp.tanh(zwa), zwa)
    k_sum = jnp.zeros_like(zk)
    for d in range(2):
        up = _dot(lhs, up_ref[d])
        wl = w0_ref[d:d + 1, :] + up[:, 0:W]
        w_log = -_softplus(-wl) - 0.5
        lw_out[d, 0] = -jnp.exp(w_log)
        a = _sigmoid(a0_ref[d:d + 1, :] + up[:, W:2 * W])
        k_d = zk * (1.0 + (a - 1.0) * ka_ref[...])
        kd_out[d, 0] = k_d.astype(kd_out.dtype)
        bd_out[d, 0] = (kk * a).astype(bd_out.dtype)
        k_sum = k_sum + k_d
    r_out[0] = zr.astype(r_out.dtype)
    kk_out[0] = kk.astype(kk_out.dtype)
    v_out[0] = zv.astype(v_out.dtype)
    g_out[0] = _dot(_sigmoid(zg), gup_ref[...]).astype(g_out.dtype)
    bonus_out[0] = (_seg_sum(zr * k_sum * rk_ref[...], ones_bd) * zv).astype(bonus_out.dtype)


def _rwkv_prep(z_a, mu, w0, a0, up_comb, g_up, k_k, k_a, r_k, ones_bd):
    B, T, _ = z_a.shape
    tt = min(256, T)
    W = A_WIDTH
    full = lambda shape: pl.BlockSpec(shape, lambda b, i: (0,) * len(shape))
    tok = pl.BlockSpec((1, tt, W), lambda b, i: (b, i, 0))
    tok2 = pl.BlockSpec((2, 1, tt, W), lambda b, i: (0, b, i, 0))
    s1 = jax.ShapeDtypeStruct((B, T, W), ACT)
    s2 = jax.ShapeDtypeStruct((2, B, T, W), ACT)
    lw = jax.ShapeDtypeStruct((2, B, T, W), F32)
    return pl.pallas_call(
        _rwkv_prep_kernel,
        grid=(B, T // tt),
        in_specs=_halo_specs(tt, A_COLS, 0, T) + [
            full((1, A_COLS)), full((2, W)), full((2, W)), full((2, LANES, 2 * W)),
            full((A_RANK_G, W)), full((1, W)), full((1, W)), full((1, W)), full((W, W))],
        out_specs=[tok, tok, tok, tok, tok, tok2, tok2, tok2],
        out_shape=[s1, s1, s1, s1, s1, lw, s2, s2],
        scratch_shapes=[pltpu.VMEM((tt + 2 * SUBLANES, A_COLS), F32)],
        compiler_params=_params(("parallel", "parallel")),
        name="rwkv_prep",
    )(z_a, z_a, z_a, mu, w0, a0, up_comb, g_up, k_k, k_a, r_k, ones_bd)


def _chunk_masks(reverse):
    n = 2 * CHUNK
    row = lax.broadcasted_iota(jnp.int32, (n, n), 0)
    col = lax.broadcasted_iota(jnp.int32, (n, n), 1)
    same = (row // CHUNK) == (col // CHUNK)
    ti, tj = row % CHUNK, col % CHUNK
    if reverse:
        return same & (ti < tj), same & (ti <= tj), row == col
    return same & (ti > tj), same & (ti >= tj), row == col


def _cumsum_matrix(reverse):
    row = lax.broadcasted_iota(jnp.int32, (CHUNK, CHUNK), 0)
    col = lax.broadcasted_iota(jnp.int32, (CHUNK, CHUNK), 1)
    return ((row <= col) if reverse else (row >= col)).astype(F32)


def _rwkv_chunk_operands(r, kk, v, k, b, cs, lw, reverse):
    C = CHUNK
    cs_end = cs[0:1, :] if reverse else cs[C - 1:C, :]
    m0 = lax.broadcasted_iota(jnp.int32, (C, LANES), 1) < A_HEAD_DIM

    def stack(x):
        return jnp.concatenate([jnp.where(m0, x, 0.0), jnp.where(m0, 0.0, x)], axis=0)

    g_inv = jnp.exp(-cs)
    g_tail = jnp.exp(cs_end - cs)
    strict, incl, _ = _chunk_masks(reverse)
    return dict(rg=stack(r * jnp.exp(cs)), kkg=stack(kk * jnp.exp(cs - lw)), ki=stack(k * g_inv),
                bi=stack(b * g_inv), kt=stack(k * g_tail), bt=stack(b * g_tail), vs=stack(v),
                g_end=jnp.exp(cs_end), strict=strict, incl=incl)


def _rwkv_chunks(ops, hts):
    C = CHUNK
    n = 2 * C
    eye = _chunk_masks(False)[2].astype(F32)
    ps = [_dot_nt(jnp.concatenate([o["kkg"], o["rg"]], axis=0), jnp.concatenate([o["bi"], o["ki"]], axis=0))
          for o in ops]
    a_ab = [jnp.where(o["strict"], p[0:n, 0:n], 0.0) for o, p in zip(ops, ps)]
    a_rb = [jnp.where(o["incl"], p[n:2 * n, 0:n], 0.0).astype(BF16) for o, p in zip(ops, ps)]
    a_k = [jnp.concatenate([jnp.where(o["strict"], p[0:n, n:2 * n], 0.0),
                            jnp.where(o["incl"], p[n:2 * n, n:2 * n], 0.0)], axis=0).astype(BF16)
           for o, p in zip(ops, ps)]
    vsb = [o["vs"].astype(BF16) for o in ops]
    a_kv = [jnp.dot(a, v, preferred_element_type=F32) for a, v in zip(a_k, vsb)]
    akv = [x[0:n] for x in a_kv]
    arkv = [x[n:2 * n] for x in a_kv]
    tinv = _tri_inverse(a_ab, eye)
    wu = [_dot(t, jnp.concatenate([o["kkg"], x], axis=1)).astype(BF16)
          for t, o, x in zip(tinv, ops, akv)]
    r2 = [jnp.dot(a, w, preferred_element_type=F32) for a, w in zip(a_rb, wu)]
    btb = [o["bt"].astype(BF16) for o in ops]
    btw = [_dot_tn(b, w[:, 0:LANES]) for b, w in zip(btb, wu)]
    hloc = [_dot_tn(jnp.concatenate([v, -w[:, LANES:2 * LANES]], axis=0),
                    jnp.concatenate([o["kt"].astype(BF16), b], axis=0))
            for v, o, w, b in zip(vsb, ops, wu, btb)]
    ys, hts_new = [], []
    for o, r, yl, ht, bw, hl in zip(ops, r2, arkv, hts, btw, hloc):
        q_s = o["rg"] - r[:, 0:LANES]
        yloc_s = yl - r[:, LANES:2 * LANES]
        ys.append(yloc_s[0:C] + yloc_s[C:n] + _dot_nt(q_s[0:C] + q_s[C:n], ht))
        hts_new.append(ht * o["g_end"] - _dot_nt(ht, bw) + hl)
    return ys, hts_new


def _rwkv_scan_kernel(rf, kkf, vf, lwf, kf, bf, rb, kkb, vb, lwb, kb, bb, yf_out, yb_out, h_scr):
    @pl.when(pl.program_id(1) == 0)
    def _():
        h_scr[...] = jnp.zeros_like(h_scr)

    n_pairs = A_WIDTH // LANES
    sls = [slice(i * LANES, (i + 1) * LANES) for i in range(n_pairs)]
    f32 = lambda a: a.astype(F32)
    hts = [h_scr[d, i] for d in range(2) for i in range(n_pairs)]
    for j in range(SCAN_SUB):
        ops = []
        for d, (r_ref, kk_ref, v_ref, lw_ref, k_ref, b_ref) in enumerate(
                ((rf, kkf, vf, lwf, kf, bf), (rb, kkb, vb, lwb, kb, bb))):
            reverse = d == 1
            rows = pl.ds((SCAN_SUB - 1 - j if reverse else j) * CHUNK, CHUNK)
            lw_all = lw_ref[0, 0, rows, :]
            cs_all = _dot_hi(_cumsum_matrix(reverse), lw_all)
            ops += [_rwkv_chunk_operands(f32(r_ref[0, rows, sl]), f32(kk_ref[0, rows, sl]),
                                         f32(v_ref[0, rows, sl]), f32(k_ref[0, 0, rows, sl]),
                                         f32(b_ref[0, 0, rows, sl]), cs_all[:, sl], lw_all[:, sl], reverse)
                    for sl in sls]
        ys, hts = _rwkv_chunks(ops, hts)
        for d, y_out in enumerate((yf_out, yb_out)):
            rows = pl.ds((SCAN_SUB - 1 - j if d == 1 else j) * CHUNK, CHUNK)
            for i, sl in enumerate(sls):
                y_out[0, rows, sl] = ys[d * n_pairs + i].astype(y_out.dtype)
    for d in range(2):
        for i in range(n_pairs):
            h_scr[d, i] = hts[d * n_pairs + i]


def _rwkv_scan(r, kk, v, lw, kd, bd):
    B, T, W = r.shape
    rows = CHUNK * SCAN_SUB
    N = T // rows
    fwd = pl.BlockSpec((1, rows, W), lambda b, c: (b, c, 0))
    bwd = pl.BlockSpec((1, rows, W), lambda b, c: (b, N - 1 - c, 0))
    fwd2 = pl.BlockSpec((1, 1, rows, W), lambda b, c: (0, b, c, 0))
    bwd2 = pl.BlockSpec((1, 1, rows, W), lambda b, c: (1, b, N - 1 - c, 0))
    out = jax.ShapeDtypeStruct((B, T, W), ACT)
    return pl.pallas_call(
        _rwkv_scan_kernel,
        grid=(B, N),
        in_specs=[fwd, fwd, fwd, fwd2, fwd2, fwd2, bwd, bwd, bwd, bwd2, bwd2, bwd2],
        out_specs=[fwd, bwd],
        out_shape=[out, out],
        scratch_shapes=[pltpu.VMEM((2, W // LANES, LANES, LANES), F32)],
        compiler_params=_params(("parallel", "arbitrary")),
        name="rwkv_scan",
    )(r, kk, v, lw, kd, bd, r, kk, v, lw, kd, bd)


def _gdn_prep_kernel(z_ref, zp_ref, zn_ref, small_ref, cw_ref, alog_ref, dtb_ref,
                     q_out, k_out, v_out, bg_out, acc_scr):
    tt = z_ref.shape[1]
    i = pl.program_id(1)
    last = pl.num_programs(1) - 1
    half = B_CONV // 2
    zb = z_ref[0].astype(BF16)
    offsets = [j - half for j in range(B_CONV) if j != half]
    row = lax.broadcasted_iota(jnp.int32, (len(offsets) * tt, tt), 0)
    col = lax.broadcasted_iota(jnp.int32, (len(offsets) * tt, tt), 1)
    blk = row // tt
    off = jnp.where(blk < half, blk - half, blk - half + 1)
    shifted = jnp.dot((col == row - blk * tt + off).astype(BF16), zb, preferred_element_type=F32)
    acc = zb.astype(F32) * cw_ref[half:half + 1, :]
    for m, o in enumerate(offsets):
        acc = acc + shifted[m * tt:(m + 1) * tt] * cw_ref[o + half:o + half + 1, :]
    acc_scr[...] = acc
    prev8, next8 = _halo_rows(zp_ref, zn_ref)
    prev8 = jnp.where(i == 0, 0.0, prev8).astype(BF16).astype(F32)
    next8 = jnp.where(i == last, 0.0, next8).astype(BF16).astype(F32)
    r8 = lax.broadcasted_iota(jnp.int32, (SUBLANES, 1), 0)
    top = bot = None
    for o in offsets:
        w = cw_ref[o + half:o + half + 1, :]
        if o < 0:
            f = jnp.where(r8 < -o, pltpu.roll(prev8, -o, axis=0), 0.0) * w
            top = f if top is None else top + f
        else:
            f = jnp.where(r8 >= SUBLANES - o, pltpu.roll(next8, SUBLANES - o, axis=0), 0.0) * w
            bot = f if bot is None else bot + f
    acc_scr[pl.ds(0, SUBLANES), :] += top
    acc_scr[pl.ds(tt - SUBLANES, SUBLANES), :] += bot
    qkv = _silu(acc_scr[...])
    for h in range(B_HEADS):
        for idx, out in enumerate((q_out, k_out)):
            x = qkv[:, idx * B_WIDTH + h * B_HEAD_DIM: idx * B_WIDTH + (h + 1) * B_HEAD_DIM]
            xn = x * lax.rsqrt(jnp.sum(x * x, axis=-1, keepdims=True) + 1e-6)
            if idx == 0:
                xn = xn * (B_HEAD_DIM ** -0.5)
            out[0, :, h * B_HEAD_DIM:(h + 1) * B_HEAD_DIM] = xn.astype(out.dtype)
    v_out[0] = qkv[:, 2 * B_WIDTH:3 * B_WIDTH].astype(v_out.dtype)
    s = small_ref[0]
    lane = lax.broadcasted_iota(jnp.int32, s.shape, 1)
    beta = _sigmoid(s)
    g = -jnp.exp(alog_ref[...]) * _softplus(s + dtb_ref[...])
    bg_out[0] = jnp.where(lane < 2 * B_HEADS, beta, g)


def _gdn_prep(z_b, z_small, conv_w, alog_vec, dtb_vec):
    B, T, Wq = z_b.shape
    tt = min(256, T)
    full = lambda shape: pl.BlockSpec(shape, lambda b, i: (0,) * len(shape))
    tok = pl.BlockSpec((1, tt, B_WIDTH), lambda b, i: (b, i, 0))
    small = pl.BlockSpec((1, tt, SMALL_COLS), lambda b, i: (b, i, 0))
    s1 = jax.ShapeDtypeStruct((B, T, B_WIDTH), ACT)
    return pl.pallas_call(
        _gdn_prep_kernel,
        grid=(B, T // tt),
        in_specs=_halo_specs(tt, Wq, 0, T) + [small, full((B_CONV, Wq)), full((1, SMALL_COLS)),
                                            full((1, SMALL_COLS))],
        out_specs=[tok, tok, tok, pl.BlockSpec((1, tt, SMALL_COLS), lambda b, i: (b, i, 0))],
        out_shape=[s1, s1, s1, jax.ShapeDtypeStruct((B, T, SMALL_COLS), F32)],
        scratch_shapes=[pltpu.VMEM((tt, Wq), F32)],
        compiler_params=_params(("parallel", "parallel")),
        name="gdn_prep",
    )(z_b, z_b, z_b, z_small, conv_w, alog_vec, dtb_vec)


def _gdn_chunk_operands(q0, q1, k0, k1, v0, v1, gc2, gct2, bg2, j0, j1, reverse):
    C = CHUNK
    n = 2 * C
    strict, incl, _ = _chunk_masks(reverse)
    top = lax.broadcasted_iota(jnp.int32, (n, 1), 0) < C
    left = lax.broadcasted_iota(jnp.int32, (1, n), 1) < C
    g0, g1 = j0 + 2 * B_HEADS, j1 + 2 * B_HEADS
    gcol = jnp.where(top, gc2[:, g0:g0 + 1], gc2[:, g1:g1 + 1])
    grow = jnp.where(left, gct2[g0:g0 + 1, :], gct2[g1:g1 + 1, :])
    beta = jnp.where(top, bg2[:, j0:j0 + 1], bg2[:, j1:j1 + 1])
    e = 0 if reverse else C - 1
    glast = jnp.where(top, gc2[e:e + 1, g0:g0 + 1], gc2[e:e + 1, g1:g1 + 1])
    ks = jnp.concatenate([k0, k1], axis=0)
    qs = jnp.concatenate([q0, q1], axis=0)
    vs = jnp.concatenate([v0, v1], axis=0)
    egc = jnp.exp(gcol)
    kb = ks * beta
    return dict(gam=jnp.exp(jnp.where(incl, gcol - grow, NEG_INF)), strict=strict, ks=ks,
                kbq=jnp.concatenate([kb, qs], axis=0),
                rhs=jnp.concatenate([vs * beta, kb * egc], axis=1),
                qh=(qs * egc).astype(BF16), ktail=(ks * jnp.exp(glast - gcol)).astype(BF16),
                decay=jnp.exp(glast))


def _gdn_chunks(ops, states):
    C = CHUNK
    n = 2 * C
    eye = _chunk_masks(False)[2].astype(F32)
    ps = [_dot_nt(o["kbq"], o["ks"]) for o in ops]
    lower = [jnp.where(o["strict"], p[0:n] * o["gam"], 0.0) for o, p in zip(ops, ps)]
    aqk = [(p[n:2 * n] * o["gam"]).astype(BF16) for o, p in zip(ops, ps)]
    tinv = [t.astype(BF16) for t in _tri_inverse(lower, eye)]
    uw = [jnp.dot(t, o["rhs"].astype(BF16), preferred_element_type=F32) for t, o in zip(tinv, ops)]
    lx = []
    for l, x in zip(lower, uw):
        lh, ll = _split(l)
        xh, xl = _split(x)
        first = jnp.dot(lh, jnp.concatenate([xh, xl], axis=1), preferred_element_type=F32)
        w2 = x.shape[1]
        lx.append(first[:, 0:w2] + (first[:, w2:2 * w2] + jnp.dot(ll, xh, preferred_element_type=F32)))
    resid = [o["rhs"] - x - y for o, x, y in zip(ops, uw, lx)]
    uw = [x + jnp.dot(t, r.astype(BF16), preferred_element_type=F32) for x, t, r in zip(uw, tinv, resid)]
    sb = [(s0.astype(BF16), s1.astype(BF16)) for s0, s1 in states]
    wq = [[jnp.dot(jnp.concatenate([u[h * C:(h + 1) * C, LANES:].astype(BF16), o["qh"][h * C:(h + 1) * C]],
                                   axis=0), s[h], preferred_element_type=F32) for h in range(2)]
          for u, o, s in zip(uw, ops, sb)]
    ws = [jnp.concatenate([x[0][0:C], x[1][0:C]], axis=0) for x in wq]
    qss = [jnp.concatenate([x[0][C:n], x[1][C:n]], axis=0) for x in wq]
    v_new = [(u[:, 0:LANES] - w).astype(BF16) for u, w in zip(uw, ws)]
    outs = [q + jnp.dot(a, v, preferred_element_type=F32) for q, a, v in zip(qss, aqk, v_new)]
    new_states = []
    for o, v, (s0, s1) in zip(ops, v_new, states):
        new_states.append((s0 * o["decay"][0:1, :] + _dot_tn(o["ktail"][0:C], v[0:C]),
                           s1 * o["decay"][C:C + 1, :] + _dot_tn(o["ktail"][C:n], v[C:n])))
    return [(x[0:C], x[C:n]) for x in outs], new_states


def _gdn_scan_kernel(qf, kf, vf, bgf, qb, kb, vb, bgb, of_out, ob_out, s_scr):
    @pl.when(pl.program_id(1) == 0)
    def _():
        s_scr[...] = jnp.zeros_like(s_scr)

    D = B_HEAD_DIM
    n_pairs = B_HEADS // 2
    sl = lambda h: slice(h * D, (h + 1) * D)
    f32 = lambda a: a.astype(F32)
    states = [(s_scr[d, 2 * i], s_scr[d, 2 * i + 1]) for d in range(2) for i in range(n_pairs)]
    for j in range(SCAN_SUB):
        ops = []
        for d, (q_ref, k_ref, v_ref, bg_ref) in enumerate(((qf, kf, vf, bgf), (qb, kb, vb, bgb))):
            reverse = d == 1
            rows = pl.ds((SCAN_SUB - 1 - j if reverse else j) * CHUNK, CHUNK)
            bg = bg_ref[0, rows, :]
            gc = _dot_hi(_cumsum_matrix(reverse), bg)
            bg2 = jnp.concatenate([bg, bg], axis=0)
            gc2 = jnp.concatenate([gc, gc], axis=0)
            gct2 = gc2.T
            for i in range(n_pairs):
                h0, h1 = 2 * i, 2 * i + 1
                ops.append(_gdn_chunk_operands(
                    f32(q_ref[0, rows, sl(h0)]), f32(q_ref[0, rows, sl(h1)]), f32(k_ref[0, rows, sl(h0)]),
                    f32(k_ref[0, rows, sl(h1)]), f32(v_ref[0, rows, sl(h0)]), f32(v_ref[0, rows, sl(h1)]),
                    gc2, gct2, bg2, d * B_HEADS + h0, d * B_HEADS + h1, reverse))
        outs, states = _gdn_chunks(ops, states)
        for d, o_out in enumerate((of_out, ob_out)):
            rows = pl.ds((SCAN_SUB - 1 - j if d == 1 else j) * CHUNK, CHUNK)
            for i in range(n_pairs):
                o0, o1 = outs[d * n_pairs + i]
                o_out[0, rows, sl(2 * i)] = o0.astype(o_out.dtype)
                o_out[0, rows, sl(2 * i + 1)] = o1.astype(o_out.dtype)
    for d in range(2):
        for i in range(n_pairs):
            s_scr[d, 2 * i], s_scr[d, 2 * i + 1] = states[d * n_pairs + i]


def _gdn_scan(q, k, v, bg):
    B, T, W = q.shape
    rows = CHUNK * SCAN_SUB
    N = T // rows
    fwd = pl.BlockSpec((1, rows, W), lambda b, c: (b, c, 0))
    bwd = pl.BlockSpec((1, rows, W), lambda b, c: (b, N - 1 - c, 0))
    sfwd = pl.BlockSpec((1, rows, SMALL_COLS), lambda b, c: (b, c, 0))
    sbwd = pl.BlockSpec((1, rows, SMALL_COLS), lambda b, c: (b, N - 1 - c, 0))
    out = jax.ShapeDtypeStruct((B, T, W), ACT)
    return pl.pallas_call(
        _gdn_scan_kernel,
        grid=(B, N),
        in_specs=[fwd, fwd, fwd, sfwd, bwd, bwd, bwd, sbwd],
        out_specs=[fwd, bwd],
        out_shape=[out, out],
        scratch_shapes=[pltpu.VMEM((2, B_HEADS, B_HEAD_DIM, B_HEAD_DIM), F32)],
        compiler_params=_params(("parallel", "arbitrary")),
        name="gdn_scan",
    )(q, k, v, bg, q, k, v, bg)


def _pack_bf16_pairs(x):
    n = x.shape[1] // 2
    hi = lax.bitcast_convert_type(x[:, :n].astype(BF16).astype(F32), jnp.int32)
    lo = lax.bitcast_convert_type(x[:, n:].astype(BF16).astype(F32), jnp.int32)
    return hi | lax.shift_right_logical(lo, 16)


def _unpack_bf16_pairs(p):
    a = lax.bitcast_convert_type(p & jnp.int32(-65536), F32)
    b = lax.bitcast_convert_type(lax.shift_left(p, 16), F32)
    return a.astype(BF16), b.astype(BF16)


def _route(scores, biased, base):
    tt = scores.shape[-1]
    shape3 = (N_GROUPS, GROUP_SIZE, tt)
    s3 = scores.reshape(shape3)
    b3 = biased.reshape(shape3)
    jid = lax.broadcasted_iota(jnp.int32, shape3, 1).astype(F32)
    gid = lax.broadcasted_iota(jnp.int32, shape3, 0).astype(F32)
    m1 = jnp.max(b3, axis=1, keepdims=True)
    first = jnp.min(jnp.where(b3 == m1, jid, float(GROUP_SIZE)), axis=1, keepdims=True)
    m2 = jnp.max(jnp.where(jid == first, NEG_INF, b3), axis=1, keepdims=True)
    gs = m1 + m2
    grp = lax.broadcasted_iota(jnp.int32, (N_GROUPS, 1, tt), 0).astype(F32)
    keep = jnp.zeros((N_GROUPS, 1, tt), F32)
    for _ in range(TOPK_GROUPS):
        m = jnp.max(gs, axis=0, keepdims=True)
        pick = grp == jnp.min(jnp.where(gs == m, grp, float(N_GROUPS)), axis=0, keepdims=True)
        keep = jnp.where(pick, 1.0, keep)
        gs = jnp.where(pick, NEG_INF, gs)
    work = jnp.where(keep > 0.0, b3, NEG_INF)
    eid = gid * float(GROUP_SIZE) + jid
    chosen = jnp.zeros(shape3, F32)
    sum01 = lambda a: jnp.sum(jnp.sum(a, axis=1, keepdims=True), axis=0, keepdims=True)
    ids, raw = [], []
    for _ in range(TOP_K):
        m = jnp.max(jnp.max(work, axis=1, keepdims=True), axis=0, keepdims=True)
        cand = jnp.where(work == m, eid, float(N_EXPERTS))
        first = jnp.min(jnp.min(cand, axis=1, keepdims=True), axis=0, keepdims=True)
        pick = eid == first
        ids.append(first)
        raw.append(sum01(jnp.where(pick, s3, 0.0)))
        chosen = jnp.where(pick, 1.0, chosen)
        work = jnp.where(pick, NEG_INF, work)
    total = raw[0]
    for r in raw[1:]:
        total = total + r
    flat = chosen.reshape(N_EXPERTS, tt)
    earlier = (lax.broadcasted_iota(jnp.int32, (tt, tt), 0)
               < lax.broadcasted_iota(jnp.int32, (tt, tt), 1)).astype(BF16)
    prefix = (jnp.dot(flat.astype(BF16), earlier, preferred_element_type=F32) + base).reshape(shape3)
    row = lax.broadcasted_iota(jnp.int32, (SUBLANES, tt), 0)
    idx8 = jnp.zeros((SUBLANES, tt), F32)
    w8 = jnp.zeros((SUBLANES, tt), F32)
    pos8 = jnp.zeros((SUBLANES, tt), F32)
    for k in range(TOP_K):
        rank = sum01(jnp.where(eid == ids[k], prefix, 0.0))
        idx8 = jnp.where(row == k, ids[k].reshape(1, tt), idx8)
        w8 = jnp.where(row == k, (raw[k] / total * ROUTED_SCALE).reshape(1, tt), w8)
        pos8 = jnp.where(row == k, rank.reshape(1, tt), pos8)
    counts = jnp.sum(flat, axis=1, keepdims=True)
    return idx8.astype(jnp.int32), w8, pos8.astype(jnp.int32), counts


def _mixer_post_kernel(yf, yb, bonus, g, of, ob, zz, zga, zgb, x_ref, gt1, sc2, sh2,
                       gnw, gnb, ones_ref, aproj, bng, bproj, wout, n2g, rwt, rbias,
                       x1_out, h2_out, idx_out, w_out_ref, pos_out, counts_out):
    @pl.when((pl.program_id(0) == 0) & (pl.program_id(1) == 0))
    def _():
        counts_out[...] = jnp.zeros_like(counts_out)

    ones_bd = ones_ref[...]
    f32 = lambda ref: ref[0].astype(F32)
    y = f32(yf) + f32(yb)
    mean = _seg_sum(y, ones_bd) * (1.0 / A_HEAD_DIM)
    dlt = y - mean
    var = _seg_sum(dlt * dlt, ones_bd) * (1.0 / A_HEAD_DIM)
    yn = dlt * lax.rsqrt(var + A_GN_EPS) * gnw[...] + gnb[...]
    y_a = _dot((yn + f32(bonus)) * f32(g), aproj[...])

    o = f32(of) + f32(ob)
    z = f32(zz)
    parts = []
    for h in range(B_HEADS):
        oh = o[:, h * B_HEAD_DIM:(h + 1) * B_HEAD_DIM]
        parts.append(oh * lax.rsqrt(jnp.mean(oh * oh, axis=-1, keepdims=True) + EPS))
    on = jnp.concatenate(parts, axis=1) * bng[...] * _silu(z)
    y_b = _dot(on, bproj[...])

    u = _sigmoid(f32(zga)) * y_a + _sigmoid(f32(zgb)) * y_b
    x1 = x_ref[0] + gt1[0] * _dot(u, wout[...])
    x1_out[0] = x1
    h2 = _rms_mod(x1, n2g[...], sc2[0], sh2[0])
    h2_out[0] = _pack_bf16_pairs(h2)
    logits_t = _dot_nt(rwt[...], h2)
    scores = _sigmoid(logits_t)
    base = counts_out[:, 0:1]
    idx8, w8, pos8, counts = _route(scores, scores + rbias[...], base)
    idx_out[0] = idx8
    w_out_ref[0] = w8
    pos_out[0] = pos8
    counts_out[...] = counts_out[...] + counts


def _mixer_post(yf, yb, bonus, g, of, ob, z_c, x, gt1, sc2, sh2, gn_w, gn_b, ones_bd, a_proj, bng,
                b_proj, w_out, n2g, rwt, rbias):
    B, T, D = x.shape
    tt = min(256, T)
    full = lambda shape: pl.BlockSpec(shape, lambda b, i: (0,) * len(shape))
    tokw = lambda w, col=0: pl.BlockSpec((1, tt, w), lambda b, i: (b, i, col))
    modrow = pl.BlockSpec((1, 1, D), lambda b, i: (b, 0, 0))
    pick = pl.BlockSpec((1, SUBLANES, tt), lambda b, i: (b, 0, i))
    W = A_WIDTH
    return pl.pallas_call(
        _mixer_post_kernel,
        grid=(B, T // tt),
        in_specs=[tokw(W), tokw(W), tokw(W), tokw(W), tokw(D), tokw(D),
                  tokw(D, 0), tokw(D, 1), tokw(D, 2), tokw(D), modrow, modrow, modrow,
                  full((1, W)), full((1, W)), full((W, W)), full((W, D)), full((1, D)), full((D, D)),
                  full((D, D)), full((1, D)), full((N_EXPERTS, D)), full((N_EXPERTS, 1))],
        out_specs=[tokw(D), tokw(D // 2), pick, pick, pick,
                   pl.BlockSpec((N_EXPERTS, LANES), lambda b, i: (0, 0))],
        out_shape=[jax.ShapeDtypeStruct((B, T, D), F32), jax.ShapeDtypeStruct((B, T, D // 2), jnp.int32),
                   jax.ShapeDtypeStruct((B, SUBLANES, T), jnp.int32),
                   jax.ShapeDtypeStruct((B, SUBLANES, T), F32),
                   jax.ShapeDtypeStruct((B, SUBLANES, T), jnp.int32),
                   jax.ShapeDtypeStruct((N_EXPERTS, LANES), F32)],
        compiler_params=_params(("arbitrary", "arbitrary")),
        name="mixer_post",
    )(yf, yb, bonus, g, of, ob, z_c, z_c, z_c, x, gt1, sc2, sh2, gn_w, gn_b, ones_bd, a_proj, bng,
      b_proj, w_out, n2g, rwt, rbias)


def _gather_rows(table, idx):
    n_rows, width = idx.shape[0], table.shape[1]
    per_worker = n_rows // SC_WORKERS
    n_windows = per_worker // GATHER_WINDOW
    mesh = plsc.VectorSubcoreMesh(core_axis_name="c", subcore_axis_name="s")

    @functools.partial(
        pl.kernel, mesh=mesh, out_type=jax.ShapeDtypeStruct((n_rows, width), table.dtype),
        scratch_types=[pltpu.VMEM((GATHER_WINDOW,), jnp.int32),
                       pltpu.VMEM((GATHER_WINDOW, width), table.dtype),
                       pltpu.SemaphoreType.DMA])
    def gather(table_hbm, idx_hbm, out_hbm, idx_v, rows_v, sem):
        worker = lax.axis_index("s") * SC_CORES + lax.axis_index("c")

        @pl.loop(0, n_windows)
        def _(j):
            off = pl.multiple_of(worker * per_worker + j * GATHER_WINDOW, GATHER_WINDOW)
            pltpu.sync_copy(idx_hbm.at[pl.ds(off, GATHER_WINDOW)], idx_v)
            pltpu.async_copy(table_hbm.at[idx_v], rows_v, sem).wait()
            pltpu.sync_copy(rows_v, out_hbm.at[pl.ds(off, GATHER_WINDOW)])

    return gather(table, idx)


def _scatter_rows(rows, dest_w, n_out):
    n_tok, width = rows.shape
    per_worker = n_tok // GATHER_WINDOW // SC_WORKERS
    mesh = plsc.VectorSubcoreMesh(core_axis_name="c", subcore_axis_name="s")

    @functools.partial(
        pl.kernel, mesh=mesh, out_type=jax.ShapeDtypeStruct((n_out, width), rows.dtype),
        scratch_types=[pltpu.VMEM((SUBLANES, GATHER_WINDOW), jnp.int32),
                       pltpu.VMEM((GATHER_WINDOW, width), rows.dtype)])
    def scatter(rows_hbm, dest_hbm, out_hbm, idx_v, rows_v):
        worker = lax.axis_index("s") * SC_CORES + lax.axis_index("c")

        @pl.loop(0, per_worker)
        def _(j):
            win = worker * per_worker + j
            pltpu.sync_copy(dest_hbm.at[win], idx_v)
            pltpu.sync_copy(rows_hbm.at[pl.ds(pl.multiple_of(win * GATHER_WINDOW, GATHER_WINDOW),
                                              GATHER_WINDOW)], rows_v)
            for k in range(TOP_K):
                pltpu.sync_copy(rows_v, out_hbm.at[idx_v.at[k]])

    return scatter(rows, dest_w)


def _dest_kernel(start_ref, idx_ref, pos_ref, dest_ref):
    idx = idx_ref[0]
    dest = jnp.zeros_like(idx)
    for e in range(N_EXPERTS):
        dest = jnp.where(idx == e, start_ref[e], dest)
    dest_ref[0] = dest + pos_ref[0]


def _dest_rows(pad_start, idx, pos):
    B, _, T = idx.shape
    tt = min(2048, T)
    spec = pl.BlockSpec((1, SUBLANES, tt), lambda b, i, start: (b, 0, i))
    return pl.pallas_call(
        _dest_kernel,
        grid_spec=pltpu.PrefetchScalarGridSpec(num_scalar_prefetch=1, grid=(B, T // tt),
                                               in_specs=[spec, spec], out_specs=spec),
        out_shape=jax.ShapeDtypeStruct(idx.shape, jnp.int32),
        compiler_params=_params(("parallel", "parallel")),
        name="dest_rows",
    )(pad_start, idx, pos)


def _expert_kernel(be_ref, valid_ref, xs_ref, wg, wu, wd, ys_ref):
    n_valid = valid_ref[pl.program_id(0)]

    @pl.when(n_valid > 0)
    def _():
        half = D_MODEL // 2
        row = lax.broadcasted_iota(jnp.int32, xs_ref.shape, 0)
        a, b = _unpack_bf16_pairs(jnp.where(row < n_valid, xs_ref[...], 0))
        mm = lambda w: (jnp.dot(a, w[0, 0:half, :], preferred_element_type=F32)
                        + jnp.dot(b, w[0, half:D_MODEL, :], preferred_element_type=F32))
        hid = _silu(mm(wg)) * mm(wu)
        ys_ref[...] = _pack_bf16_pairs(_dot(hid, wd[0]))


def _experts(block_expert, block_valid, xs, wg, wu, wd):
    n_rows, half = xs.shape
    rows = pl.BlockSpec((MOE_BLOCK, half), lambda i, be, valid: (i, 0))
    wspec = lambda shape: pl.BlockSpec(shape, lambda i, be, valid: (be[i], 0, 0))
    return pl.pallas_call(
        _expert_kernel,
        grid_spec=pltpu.PrefetchScalarGridSpec(
            num_scalar_prefetch=2, grid=(n_rows // MOE_BLOCK,),
            in_specs=[rows, wspec((1, D_MODEL, D_EXPERT)), wspec((1, D_MODEL, D_EXPERT)),
                      wspec((1, D_EXPERT, D_MODEL))],
            out_specs=rows),
        out_shape=jax.ShapeDtypeStruct((n_rows, half), jnp.int32),
        compiler_params=_params(("arbitrary",)),
        name="experts",
    )(block_expert, block_valid, xs, wg, wu, wd)


def _moe_final_kernel(yg_ref, w_ref, h_ref, x1_ref, gt2, fg, sg, su, sd, o_ref):
    half = D_MODEL // 2
    a, b = _unpack_bf16_pairs(h_ref[0])
    mm = lambda w: (jnp.dot(a, w[0:half, :], preferred_element_type=F32)
                    + jnp.dot(b, w[half:D_MODEL, :], preferred_element_type=F32))
    shared = _dot(_silu(mm(sg)) * mm(su), sd[...])
    w = w_ref[0]
    lo = hi = None
    for k in range(TOP_K):
        ya, yb = _unpack_bf16_pairs(yg_ref[k, 0])
        wk = w[:, k:k + 1]
        lo = ya.astype(F32) * wk if lo is None else lo + ya.astype(F32) * wk
        hi = yb.astype(F32) * wk if hi is None else hi + yb.astype(F32) * wk
    x2 = x1_ref[0] + gt2[0] * (jnp.concatenate([lo, hi], axis=1) + shared)
    o_ref[0] = x2 * lax.rsqrt(jnp.mean(x2 * x2, axis=-1, keepdims=True) + EPS) * fg[...]


def _moe_final(yg, w_t, h2p, x1, gt2, fg, sg, su, sd):
    B, T, D = x1.shape
    tm = min(512, T)
    tok = lambda w: pl.BlockSpec((1, tm, w), lambda b, i: (b, i, 0))
    full = lambda shape: pl.BlockSpec(shape, lambda b, i: (0,) * len(shape))
    return pl.pallas_call(
        _moe_final_kernel,
        grid=(B, T // tm),
        in_specs=[pl.BlockSpec((TOP_K, 1, tm, D // 2), lambda b, i: (0, b, i, 0)), tok(SUBLANES),
                  tok(D // 2), tok(D), pl.BlockSpec((1, 1, D), lambda b, i: (b, 0, 0)), full((1, D)),
                  full((D, D_EXPERT)), full((D, D_EXPERT)), full((D_EXPERT, D))],
        out_specs=tok(D),
        out_shape=jax.ShapeDtypeStruct((B, T, D), F32),
        compiler_params=_params(("parallel", "parallel")),
        name="moe_final",
    )(yg, w_t, h2p, x1, gt2, fg, sg, su, sd)


def _moe_routed(h2p, idx, w, pos, counts, x1, gt2, p):
    B, T, half = h2p.shape
    n_slots = B * T * TOP_K
    n_rows = n_slots + N_EXPERTS * MOE_BLOCK
    cnt = counts[:, 0].astype(jnp.int32)
    padded = (cnt + MOE_BLOCK - 1) // MOE_BLOCK * MOE_BLOCK
    pad_end = jnp.cumsum(padded)
    pad_start = pad_end - padded
    dest = _dest_rows(pad_start, idx, pos)
    starts = jnp.arange(n_rows // MOE_BLOCK, dtype=jnp.int32) * MOE_BLOCK
    block_expert = jnp.minimum(jnp.sum((pad_end[None, :] <= starts[:, None]).astype(jnp.int32), axis=1),
                               N_EXPERTS - 1)
    block_valid = jnp.clip((pad_start + cnt)[block_expert] - starts, 0, MOE_BLOCK)
    dest_w = jnp.swapaxes(dest.reshape(B, SUBLANES, T // GATHER_WINDOW, GATHER_WINDOW), 1, 2)
    xs = _scatter_rows(h2p.reshape(B * T, half), dest_w.reshape(-1, SUBLANES, GATHER_WINDOW), n_rows)
    ys = _experts(block_expert, block_valid, xs, p["wg"], p["wu"], p["wd"])
    yg = _gather_rows(ys, jnp.swapaxes(dest[:, :TOP_K], 0, 1).reshape(-1)).reshape(TOP_K, B, T, half)
    return _moe_final(yg, jnp.swapaxes(w, 1, 2), h2p, x1, gt2, p["fg"], p["sg"], p["su"], p["sd"])


def _prepare_weights(w_in, shift_mu, a_w0, a_w_up, a_a0, a_a_up, a_g_up, a_k_k, a_k_a, a_r_k, a_gn_w,
                     a_gn_b, a_proj, b_conv_w, b_a_log, b_dt_bias, b_norm_g, b_proj, w_out, router_w,
                     router_bias, exp_gate, exp_up, exp_down, sh_gate, sh_up, sh_down, norm1_g, norm2_g,
                     final_g):
    W = A_WIDTH
    w = w_in[0]
    c0 = A_COLS
    c1 = c0 + 3 * B_WIDTH
    c2 = c1 + B_WIDTH
    c3 = c2 + 4 * B_HEADS
    pad = jnp.zeros((D_MODEL, SMALL_COLS - 4 * B_HEADS), F32)
    w_a = w[:, :c0].astype(BF16)
    w_b = jnp.concatenate([w[:, c0:c1], w[:, c2:c3], pad], axis=1).astype(BF16)
    w_c = jnp.concatenate([w[:, c1:c2], w[:, c3:]], axis=1).astype(BF16)
    zeros = jnp.zeros((2, A_RANK_W, W), F32)
    up_comb = jnp.concatenate([jnp.concatenate([a_w_up[0], zeros], axis=2),
                               jnp.concatenate([zeros, a_a_up[0]], axis=2)], axis=1)
    head = jnp.arange(W) // A_HEAD_DIM
    ones_bd = (head[:, None] == head[None, :]).astype(BF16)
    small = lambda v: jnp.zeros((1, SMALL_COLS), F32).at[0, 2 * B_HEADS:4 * B_HEADS].set(v.reshape(-1))
    return dict(
        w_a=w_a, w_b=w_b, w_c=w_c, mu=shift_mu[0].reshape(1, A_COLS), w0=a_w0[0], a0=a_a0[0],
        up_comb=up_comb, g_up=a_g_up[0], k_k=a_k_k[0].reshape(1, W), k_a=a_k_a[0].reshape(1, W),
        r_k=a_r_k[0].reshape(1, W), ones_bd=ones_bd, gn_w=a_gn_w[0].reshape(1, W),
        gn_b=a_gn_b[0].reshape(1, W), a_proj=a_proj[0].astype(BF16),
        conv_w=b_conv_w[0].astype(BF16).astype(F32),
        alog=small(b_a_log[0]), dtb=small(b_dt_bias[0]),
        bng=jnp.tile(b_norm_g[0], B_HEADS).reshape(1, B_WIDTH), b_proj=b_proj[0].astype(BF16),
        w_out=w_out[0].astype(BF16), rwt=router_w[0].T, rbias=router_bias[0].reshape(N_EXPERTS, 1),
        wg=exp_gate[0].astype(BF16), wu=exp_up[0].astype(BF16), wd=exp_down[0].astype(BF16),
        sg=sh_gate[0].astype(BF16), su=sh_up[0].astype(BF16), sd=sh_down[0].astype(BF16),
        n1g=norm1_g[0].reshape(1, D_MODEL), n2g=norm2_g[0].reshape(1, D_MODEL),
        fg=final_g.reshape(1, D_MODEL))


def _layer(x, mod, p):
    B = x.shape[0]
    sh1, sc1, gt1, sh2, sc2, gt2 = (m.reshape(B, 1, D_MODEL) for m in jnp.split(mod, 6, axis=-1))
    z_a = _inproj(x, p["n1g"], sc1, sh1, p["w_a"])
    z_b, z_small = _inproj(x, p["n1g"], sc1, sh1, p["w_b"], tail=SMALL_COLS)
    z_c = _inproj(x, p["n1g"], sc1, sh1, p["w_c"])
    r, kk, v, g, bonus, lw, kd, bd = _rwkv_prep(z_a, p["mu"], p["w0"], p["a0"], p["up_comb"], p["g_up"],
                                                p["k_k"], p["k_a"], p["r_k"], p["ones_bd"])
    yf, yb = _rwkv_scan(r, kk, v, lw, kd, bd)
    q, k, vv, bg = _gdn_prep(z_b, z_small, p["conv_w"], p["alog"], p["dtb"])
    of, ob = _gdn_scan(q, k, vv, bg)
    x1, h2p, idx, w, pos, counts = _mixer_post(yf, yb, bonus, g, of, ob, z_c, x, gt1, sc2, sh2, p["gn_w"],
                                               p["gn_b"], p["ones_bd"], p["a_proj"], p["bng"], p["b_proj"],
                                               p["w_out"], p["n2g"], p["rwt"], p["rbias"])
    return _moe_routed(h2p, idx, w, pos, counts, x1, gt2, p)


def kernel(x_prompt, x_sample, c_prompt, c_sample, ada_w, ada_b, norm1_g, norm2_g, w_in, shift_mu, a_w0, a_w_up, a_a0, a_a_up, a_g_up, a_k_k, a_k_a, a_r_k, a_gn_w, a_gn_b, a_proj, b_conv_w, b_a_log, b_dt_bias, b_norm_g, b_proj, w_out, router_w, router_bias, exp_gate, exp_up, exp_down, sh_gate, sh_up, sh_down, final_g):
    p = _prepare_weights(w_in, shift_mu, a_w0, a_w_up, a_a0, a_a_up, a_g_up, a_k_k, a_k_a, a_r_k, a_gn_w,
                         a_gn_b, a_proj, b_conv_w, b_a_log, b_dt_bias, b_norm_g, b_proj, w_out, router_w,
                         router_bias, exp_gate, exp_up, exp_down, sh_gate, sh_up, sh_down, norm1_g,
                         norm2_g, final_g)
    nb_p, nb_s = c_prompt.shape[0], c_sample.shape[0]
    rows = -(-(nb_p + nb_s) // SUBLANES) * SUBLANES
    c_all = jnp.concatenate([c_prompt, c_sample, jnp.zeros((rows - nb_p - nb_s, D_MODEL), F32)], axis=0)
    mod = _adaln_mod(c_all, ada_w[0], ada_b[0])
    y_prompt = _layer(x_prompt, mod[:nb_p], p)
    y_sample = _layer(x_sample, mod[nb_p:nb_p + nb_s], p)
    return (y_prompt, y_sample)
```

```python
import functools

import jax
import jax.numpy as jnp
from jax import lax
from jax.experimental import pallas as pl
from jax.experimental.pallas import tpu as pltpu
from jax.experimental.pallas import tpu_sc as plsc

F32 = jnp.float32
BF16 = jnp.bfloat16
ACT = jnp.bfloat16
HIGHEST = lax.Precision.HIGHEST

D_MODEL = 1024
A_HEADS = 8
A_HEAD_DIM = 64
A_WIDTH = A_HEADS * A_HEAD_DIM
A_RANK_W = 64
A_RANK_A = 64
A_RANK_G = 128
A_GN_EPS = 64e-5
A_COLS = 3 * A_WIDTH + A_RANK_W + A_RANK_A + A_RANK_G
B_HEADS = 8
B_HEAD_DIM = 128
B_WIDTH = B_HEADS * B_HEAD_DIM
B_CONV = 5
CHUNK = 64
SCAN_SUB = 4
N_EXPERTS = 64
TOP_K = 6
N_GROUPS = 8
TOPK_GROUPS = 4
GROUP_SIZE = N_EXPERTS // N_GROUPS
D_EXPERT = 256
ROUTED_SCALE = 2.5
EPS = 1e-6
LANES = 128
SUBLANES = 8
HALO = 16
SMALL_COLS = LANES
MOE_BLOCK = 512
SC_CORES = 2
SC_SUBCORES = 16
SC_WORKERS = SC_CORES * SC_SUBCORES
GATHER_WINDOW = 64
VMEM_LIMIT = 56 * 1024 * 1024
NEG_INF = float("-inf")


def _dot(a, b):
    return jnp.dot(a.astype(BF16), b.astype(BF16), preferred_element_type=F32)


def _dot_nt(a, b):
    return lax.dot_general(a.astype(BF16), b.astype(BF16), (((1,), (1,)), ((), ())),
                           preferred_element_type=F32)


def _dot_tn(a, b):
    return lax.dot_general(a.astype(BF16), b.astype(BF16), (((0,), (0,)), ((), ())),
                           preferred_element_type=F32)


def _dot_hi(a, b):
    return jnp.dot(a, b, precision=HIGHEST, preferred_element_type=F32)


def _split(x):
    hi = x.astype(BF16)
    return hi, (x - hi.astype(F32)).astype(BF16)


def _dot3(a, b, dims=(((1,), (0,)), ((), ()))):
    ah, al = _split(a)
    bh, bl = _split(b)
    d = lambda u, v: lax.dot_general(u, v, dims, preferred_element_type=F32)
    return d(ah, bh) + (d(ah, bl) + d(al, bh))


def _seg_sum(x, ones_bd):
    hi = x.astype(BF16)
    lo = (x - hi.astype(F32)).astype(BF16)
    return (jnp.dot(hi, ones_bd, preferred_element_type=F32)
            + jnp.dot(lo, ones_bd, preferred_element_type=F32))


def _softplus(x):
    return jnp.maximum(x, 0.0) + jnp.log1p(jnp.exp(-jnp.abs(x)))


def _sigmoid(x):
    return 1.0 / (1.0 + jnp.exp(-x))


def _silu(x):
    return x * _sigmoid(x)


def _tri_inverse(a_list, eye):
    n = a_list[0].shape[0]
    xs = [(-a).astype(BF16) for a in a_list]
    ts = [eye - a for a in a_list]
    xs = [jnp.dot(x, x, preferred_element_type=F32).astype(BF16) for x in xs]
    power = 2
    while 2 * power < CHUNK:
        both = [jnp.dot(x, jnp.concatenate([t.astype(BF16), x], axis=1), preferred_element_type=F32)
                for t, x in zip(ts, xs)]
        ts = [t + b[:, 0:n] for t, b in zip(ts, both)]
        xs = [b[:, n:2 * n].astype(BF16) for b in both]
        power *= 2
    return [t + jnp.dot(x, t.astype(BF16), preferred_element_type=F32) for t, x in zip(ts, xs)]


def _params(sem, **extra):
    return pltpu.CompilerParams(dimension_semantics=sem, vmem_limit_bytes=VMEM_LIMIT, **extra)


def _mod_kernel(c_ref, w_ref, b_ref, o_ref):
    c = c_ref[...]
    o_ref[...] = _dot_hi(_silu(c), w_ref[...]) + b_ref[...]


def _adaln_mod(c, ada_w, ada_b):
    rows = c.shape[0]
    n = ada_w.shape[1]
    return pl.pallas_call(
        _mod_kernel,
        grid=(n // D_MODEL,),
        in_specs=[pl.BlockSpec((rows, D_MODEL), lambda j: (0, 0)),
                  pl.BlockSpec((D_MODEL, D_MODEL), lambda j: (0, j)),
                  pl.BlockSpec((1, D_MODEL), lambda j: (0, j))],
        out_specs=pl.BlockSpec((rows, D_MODEL), lambda j: (0, j)),
        out_shape=jax.ShapeDtypeStruct((rows, n), F32),
        compiler_params=_params(("arbitrary",)),
        name="adaln_mod",
    )(c, ada_w, ada_b.reshape(1, n))


def _rms_mod(x, g, sc, sh):
    y = x * lax.rsqrt(jnp.mean(x * x, axis=-1, keepdims=True) + EPS)
    return (y * g) * (1.0 + sc) + sh


def _inproj_kernel(x_ref, g_ref, sc_ref, sh_ref, w_ref, *o_refs):
    h = _rms_mod(x_ref[0], g_ref[...], sc_ref[0], sh_ref[0]).astype(BF16)
    col = 0
    for o_ref in o_refs:
        n = o_ref.shape[2]
        o_ref[0] = jnp.dot(h, w_ref[:, col:col + n], preferred_element_type=F32).astype(o_ref.dtype)
        col += n


def _inproj(x, g, sc, sh, w, tail=0):
    B, T, D = x.shape
    n = w.shape[1]
    tm = min(512, T)
    widths = [(n - tail, ACT)] + ([(tail, F32)] if tail else [])
    outs = pl.pallas_call(
        _inproj_kernel,
        grid=(B, T // tm),
        in_specs=[pl.BlockSpec((1, tm, D), lambda b, i: (b, i, 0)),
                  pl.BlockSpec((1, D), lambda b, i: (0, 0)),
                  pl.BlockSpec((1, 1, D), lambda b, i: (b, 0, 0)),
                  pl.BlockSpec((1, 1, D), lambda b, i: (b, 0, 0)),
                  pl.BlockSpec((D, n), lambda b, i: (0, 0))],
        out_specs=[pl.BlockSpec((1, tm, wd), lambda b, i: (b, i, 0)) for wd, _ in widths],
        out_shape=[jax.ShapeDtypeStruct((B, T, wd), dt) for wd, dt in widths],
        compiler_params=_params(("parallel", "parallel")),
        name="in_proj",
    )(x, g, sc, sh, w)
    return outs if tail else outs[0]


def _fill_halo(ext_ref, cur, prev8, next8, tt):
    i = pl.program_id(1)
    last = pl.num_programs(1) - 1
    ext_ref[pl.ds(0, SUBLANES), :] = jnp.where(i == 0, 0.0, prev8)
    ext_ref[pl.ds(SUBLANES, tt), :] = cur
    ext_ref[pl.ds(SUBLANES + tt, SUBLANES), :] = jnp.where(i == last, 0.0, next8)


def _halo_specs(tt, width, col_block, seq_len):
    nb = tt // HALO
    last = seq_len // HALO - 1
    return [
        pl.BlockSpec((1, tt, width), lambda b, i: (b, i, col_block)),
        pl.BlockSpec((1, HALO, width), lambda b, i: (b, jnp.maximum(i * nb - 1, 0), col_block)),
        pl.BlockSpec((1, HALO, width), lambda b, i: (b, jnp.minimum((i + 1) * nb, last), col_block)),
    ]


def _halo_rows(zp_ref, zn_ref):
    return zp_ref[0].astype(F32)[HALO - SUBLANES:HALO], zn_ref[0].astype(F32)[0:SUBLANES]


def _project_with_halo(x_ref, xp_ref, xn_ref, g_ref, sc_ref, sh_ref, w_ref):
    prev8, next8 = _halo_rows(xp_ref, xn_ref)
    x_ext = jnp.concatenate([prev8, x_ref[0], next8], axis=0)
    h = _rms_mod(x_ext, g_ref[...], sc_ref[0], sh_ref[0]).astype(BF16)
    return jnp.dot(h, w_ref[...], preferred_element_type=F32)


def _rwkv_prep_kernel(x_ref, xp_ref, xn_ref, g1_ref, sc_ref, sh_ref, w_ref,
                      mu_ref, w0_ref, a0_ref, up_ref, gup_ref, kk_ref, ka_ref, rk_ref, ones_ref,
                      r_out, kk_out, v_out, g_out, bonus_out, lw_out, kd_out, bd_out, ext_scr):
    tt = x_ref.shape[1]
    i = pl.program_id(1)
    last = pl.num_programs(1) - 1
    z_ext = _project_with_halo(x_ref, xp_ref, xn_ref, g1_ref, sc_ref, sh_ref, w_ref).astype(ACT).astype(F32)
    row = lax.broadcasted_iota(jnp.int32, (tt + 2 * SUBLANES, 1), 0)
    inside = ((row >= SUBLANES) | (i > 0)) & ((row < tt + SUBLANES) | (i < last))
    ext_scr[...] = jnp.where(inside, z_ext, 0.0)
    z = ext_scr[pl.ds(SUBLANES, tt), :]
    z_prev = ext_scr[pl.ds(SUBLANES - 1, tt), :]
    z_next = ext_scr[pl.ds(SUBLANES + 1, tt), :]
    zs = z + (0.5 * (z_prev + z_next) - z) * mu_ref[...]
    W = A_WIDTH
    zr, zk, zv = zs[:, 0:W], zs[:, W:2 * W], zs[:, 2 * W:3 * W]
    zwa = zs[:, 3 * W:3 * W + LANES]
    zg = zs[:, 3 * W + LANES:3 * W + 2 * LANES]
    ones_bd = ones_ref[...]
    kk_raw = zk * kk_ref[...]
    kk = kk_raw * lax.rsqrt(_seg_sum(kk_raw * kk_raw, ones_bd) + 1e-6)
    lane = lax.broadcasted_iota(jnp.int32, zwa.shape, 1)
    lhs = jnp.where(lane < A_RANK_W, jnp.tanh(zwa), zwa)
    k_sum = jnp.zeros_like(zk)
    for d in range(2):
        up = _dot(lhs, up_ref[d])
        wl = w0_ref[d:d + 1, :] + up[:, 0:W]
        w_log = -_softplus(-wl) - 0.5
        lw_out[d, 0] = -jnp.exp(w_log)
        a = _sigmoid(a0_ref[d:d + 1, :] + up[:, W:2 * W])
        k_d = zk * (1.0 + (a - 1.0) * ka_ref[...])
        kd_out[d, 0] = k_d.astype(kd_out.dtype)
        bd_out[d, 0] = (kk * a).astype(bd_out.dtype)
        k_sum = k_sum + k_d
    r_out[0] = zr.astype(r_out.dtype)
    kk_out[0] = kk.astype(kk_out.dtype)
    v_out[0] = zv.astype(v_out.dtype)
    g_out[0] = _dot(_sigmoid(zg), gup_ref[...]).astype(g_out.dtype)
    bonus_out[0] = (_seg_sum(zr * k_sum * rk_ref[...], ones_bd) * zv).astype(bonus_out.dtype)


def _front_specs(tt, seq_len, n_cols):
    modrow = pl.BlockSpec((1, 1, D_MODEL), lambda b, i: (b, 0, 0))
    return _halo_specs(tt, D_MODEL, 0, seq_len) + [
        pl.BlockSpec((1, D_MODEL), lambda b, i: (0, 0)), modrow, modrow,
        pl.BlockSpec((D_MODEL, n_cols), lambda b, i: (0, 0))]


def _rwkv_prep(x, g1, sc, sh, w_a, mu, w0, a0, up_comb, g_up, k_k, k_a, r_k, ones_bd):
    B, T, _ = x.shape
    tt = min(256, T)
    W = A_WIDTH
    full = lambda shape: pl.BlockSpec(shape, lambda b, i: (0,) * len(shape))
    tok = pl.BlockSpec((1, tt, W), lambda b, i: (b, i, 0))
    tok2 = pl.BlockSpec((2, 1, tt, W), lambda b, i: (0, b, i, 0))
    s1 = jax.ShapeDtypeStruct((B, T, W), ACT)
    s2 = jax.ShapeDtypeStruct((2, B, T, W), ACT)
    lw = jax.ShapeDtypeStruct((2, B, T, W), F32)
    return pl.pallas_call(
        _rwkv_prep_kernel,
        grid=(B, T // tt),
        in_specs=_front_specs(tt, T, A_COLS) + [
            full((1, A_COLS)), full((2, W)), full((2, W)), full((2, LANES, 2 * W)),
            full((A_RANK_G, W)), full((1, W)), full((1, W)), full((1, W)), full((W, W))],
        out_specs=[tok, tok, tok, tok, tok, tok2, tok2, tok2],
        out_shape=[s1, s1, s1, s1, s1, lw, s2, s2],
        scratch_shapes=[pltpu.VMEM((tt + 2 * SUBLANES, A_COLS), F32)],
        compiler_params=_params(("parallel", "parallel")),
        name="rwkv_prep",
    )(x, x, x, g1, sc, sh, w_a, mu, w0, a0, up_comb, g_up, k_k, k_a, r_k, ones_bd)


def _chunk_masks(reverse):
    n = 2 * CHUNK
    row = lax.broadcasted_iota(jnp.int32, (n, n), 0)
    col = lax.broadcasted_iota(jnp.int32, (n, n), 1)
    same = (row // CHUNK) == (col // CHUNK)
    ti, tj = row % CHUNK, col % CHUNK
    if reverse:
        return same & (ti < tj), same & (ti <= tj), row == col
    return same & (ti > tj), same & (ti >= tj), row == col


def _cumsum_matrix(reverse):
    row = lax.broadcasted_iota(jnp.int32, (CHUNK, CHUNK), 0)
    col = lax.broadcasted_iota(jnp.int32, (CHUNK, CHUNK), 1)
    return ((row <= col) if reverse else (row >= col)).astype(F32)


def _rwkv_chunk_operands(r, kk, v, k, b, cs, lw, reverse):
    C = CHUNK
    cs_end = cs[0:1, :] if reverse else cs[C - 1:C, :]
    m0 = lax.broadcasted_iota(jnp.int32, (C, LANES), 1) < A_HEAD_DIM

    def stack(x):
        return jnp.concatenate([jnp.where(m0, x, 0.0), jnp.where(m0, 0.0, x)], axis=0)

    g_inv = jnp.exp(-cs)
    g_tail = jnp.exp(cs_end - cs)
    strict, incl, _ = _chunk_masks(reverse)
    return dict(rg=stack(r * jnp.exp(cs)), kkg=stack(kk * jnp.exp(cs - lw)), ki=stack(k * g_inv),
                bi=stack(b * g_inv), kt=stack(k * g_tail), bt=stack(b * g_tail), vs=stack(v),
                g_end=jnp.exp(cs_end), strict=strict, incl=incl)


def _rwkv_chunks(ops, hts):
    C = CHUNK
    n = 2 * C
    eye = _chunk_masks(False)[2].astype(F32)
    ps = [_dot_nt(jnp.concatenate([o["kkg"], o["rg"]], axis=0), jnp.concatenate([o["bi"], o["ki"]], axis=0))
          for o in ops]
    a_ab = [jnp.where(o["strict"], p[0:n, 0:n], 0.0) for o, p in zip(ops, ps)]
    a_rb = [jnp.where(o["incl"], p[n:2 * n, 0:n], 0.0).astype(BF16) for o, p in zip(ops, ps)]
    a_k = [jnp.concatenate([jnp.where(o["strict"], p[0:n, n:2 * n], 0.0),
                            jnp.where(o["incl"], p[n:2 * n, n:2 * n], 0.0)], axis=0).astype(BF16)
           for o, p in zip(ops, ps)]
    vsb = [o["vs"].astype(BF16) for o in ops]
    a_kv = [jnp.dot(a, v, preferred_element_type=F32) for a, v in zip(a_k, vsb)]
    akv = [x[0:n] for x in a_kv]
    arkv = [x[n:2 * n] for x in a_kv]
    tinv = _tri_inverse(a_ab, eye)
    wu = [_dot(t, jnp.concatenate([o["kkg"], x], axis=1)).astype(BF16)
          for t, o, x in zip(tinv, ops, akv)]
    r2 = [jnp.dot(a, w, preferred_element_type=F32) for a, w in zip(a_rb, wu)]
    btb = [o["bt"].astype(BF16) for o in ops]
    btw = [_dot_tn(b, w[:, 0:LANES]) for b, w in zip(btb, wu)]
    hloc = [_dot_tn(jnp.concatenate([v, -w[:, LANES:2 * LANES]], axis=0),
                    jnp.concatenate([o["kt"].astype(BF16), b], axis=0))
            for v, o, w, b in zip(vsb, ops, wu, btb)]
    ys, hts_new = [], []
    for o, r, yl, ht, bw, hl in zip(ops, r2, arkv, hts, btw, hloc):
        q_s = o["rg"] - r[:, 0:LANES]
        yloc_s = yl - r[:, LANES:2 * LANES]
        ys.append(yloc_s[0:C] + yloc_s[C:n] + _dot_nt(q_s[0:C] + q_s[C:n], ht))
        hts_new.append(ht * o["g_end"] - _dot_nt(ht, bw) + hl)
    return ys, hts_new


def _rwkv_scan_kernel(rf, kkf, vf, lwf, kf, bf, rb, kkb, vb, lwb, kb, bb, yf_out, yb_out, h_scr):
    @pl.when(pl.program_id(1) == 0)
    def _():
        h_scr[...] = jnp.zeros_like(h_scr)

    n_pairs = A_WIDTH // LANES
    sls = [slice(i * LANES, (i + 1) * LANES) for i in range(n_pairs)]
    f32 = lambda a: a.astype(F32)
    hts = [h_scr[d, i] for d in range(2) for i in range(n_pairs)]
    for j in range(SCAN_SUB):
        ops = []
        for d, (r_ref, kk_ref, v_ref, lw_ref, k_ref, b_ref) in enumerate(
                ((rf, kkf, vf, lwf, kf, bf), (rb, kkb, vb, lwb, kb, bb))):
            reverse = d == 1
            rows = pl.ds((SCAN_SUB - 1 - j if reverse else j) * CHUNK, CHUNK)
            lw_all = lw_ref[0, 0, rows, :]
            cs_all = _dot_hi(_cumsum_matrix(reverse), lw_all)
            ops += [_rwkv_chunk_operands(f32(r_ref[0, rows, sl]), f32(kk_ref[0, rows, sl]),
                                         f32(v_ref[0, rows, sl]), f32(k_ref[0, 0, rows, sl]),
                                         f32(b_ref[0, 0, rows, sl]), cs_all[:, sl], lw_all[:, sl], reverse)
                    for sl in sls]
        ys, hts = _rwkv_chunks(ops, hts)
        for d, y_out in enumerate((yf_out, yb_out)):
            rows = pl.ds((SCAN_SUB - 1 - j if d == 1 else j) * CHUNK, CHUNK)
            for i, sl in enumerate(sls):
                y_out[0, rows, sl] = ys[d * n_pairs + i].astype(y_out.dtype)
    for d in range(2):
        for i in range(n_pairs):
            h_scr[d, i] = hts[d * n_pairs + i]


def _rwkv_scan(r, kk, v, lw, kd, bd):
    B, T, W = r.shape
    rows = CHUNK * SCAN_SUB
    N = T // rows
    fwd = pl.BlockSpec((1, rows, W), lambda b, c: (b, c, 0))
    bwd = pl.BlockSpec((1, rows, W), lambda b, c: (b, N - 1 - c, 0))
    fwd2 = pl.BlockSpec((1, 1, rows, W), lambda b, c: (0, b, c, 0))
    bwd2 = pl.BlockSpec((1, 1, rows, W), lambda b, c: (1, b, N - 1 - c, 0))
    out = jax.ShapeDtypeStruct((B, T, W), ACT)
    return pl.pallas_call(
        _rwkv_scan_kernel,
        grid=(B, N),
        in_specs=[fwd, fwd, fwd, fwd2, fwd2, fwd2, bwd, bwd, bwd, bwd2, bwd2, bwd2],
        out_specs=[fwd, bwd],
        out_shape=[out, out],
        scratch_shapes=[pltpu.VMEM((2, W // LANES, LANES, LANES), F32)],
        compiler_params=_params(("parallel", "arbitrary")),
        name="rwkv_scan",
    )(r, kk, v, lw, kd, bd, r, kk, v, lw, kd, bd)


def _gdn_prep_kernel(x_ref, xp_ref, xn_ref, g1_ref, sc_ref, sh_ref, w_ref, cw_ref, alog_ref, dtb_ref,
                     q_out, k_out, v_out, bg_out, acc_scr):
    tt = x_ref.shape[1]
    i = pl.program_id(1)
    last = pl.num_programs(1) - 1
    half = B_CONV // 2
    Wq = 3 * B_WIDTH
    z_ext = _project_with_halo(x_ref, xp_ref, xn_ref, g1_ref, sc_ref, sh_ref, w_ref)
    zb = z_ext[SUBLANES:SUBLANES + tt, 0:Wq].astype(BF16)
    offsets = [j - half for j in range(B_CONV) if j != half]
    row = lax.broadcasted_iota(jnp.int32, (len(offsets) * tt, tt), 0)
    col = lax.broadcasted_iota(jnp.int32, (len(offsets) * tt, tt), 1)
    blk = row // tt
    off = jnp.where(blk < half, blk - half, blk - half + 1)
    shifted = jnp.dot((col == row - blk * tt + off).astype(BF16), zb, preferred_element_type=F32)
    acc = zb.astype(F32) * cw_ref[half:half + 1, :]
    for m, o in enumerate(offsets):
        acc = acc + shifted[m * tt:(m + 1) * tt] * cw_ref[o + half:o + half + 1, :]
    acc_scr[...] = acc
    prev8 = jnp.where(i == 0, 0.0, z_ext[0:SUBLANES, 0:Wq]).astype(BF16).astype(F32)
    next8 = jnp.where(i == last, 0.0, z_ext[SUBLANES + tt:, 0:Wq]).astype(BF16).astype(F32)
    r8 = lax.broadcasted_iota(jnp.int32, (SUBLANES, 1), 0)
    top = bot = None
    for o in offsets:
        w = cw_ref[o + half:o + half + 1, :]
        if o < 0:
            f = jnp.where(r8 < -o, pltpu.roll(prev8, -o, axis=0), 0.0) * w
            top = f if top is None else top + f
        else:
            f = jnp.where(r8 >= SUBLANES - o, pltpu.roll(next8, SUBLANES - o, axis=0), 0.0) * w
            bot = f if bot is None else bot + f
    acc_scr[pl.ds(0, SUBLANES), :] += top
    acc_scr[pl.ds(tt - SUBLANES, SUBLANES), :] += bot
    qkv = _silu(acc_scr[...])
    for h in range(B_HEADS):
        for idx, out in enumerate((q_out, k_out)):
            x = qkv[:, idx * B_WIDTH + h * B_HEAD_DIM: idx * B_WIDTH + (h + 1) * B_HEAD_DIM]
            xn = x * lax.rsqrt(jnp.sum(x * x, axis=-1, keepdims=True) + 1e-6)
            if idx == 0:
                xn = xn * (B_HEAD_DIM ** -0.5)
            out[0, :, h * B_HEAD_DIM:(h + 1) * B_HEAD_DIM] = xn.astype(out.dtype)
    v_out[0] = qkv[:, 2 * B_WIDTH:3 * B_WIDTH].astype(v_out.dtype)
    s = z_ext[SUBLANES:SUBLANES + tt, Wq:Wq + SMALL_COLS]
    lane = lax.broadcasted_iota(jnp.int32, s.shape, 1)
    beta = _sigmoid(s)
    g = -jnp.exp(alog_ref[...]) * _softplus(s + dtb_ref[...])
    bg_out[0] = jnp.where(lane < 2 * B_HEADS, beta, g)


def _gdn_prep(x, g1, sc, sh, w_b, conv_w, alog_vec, dtb_vec):
    B, T, _ = x.shape
    Wq = 3 * B_WIDTH
    tt = min(256, T)
    full = lambda shape: pl.BlockSpec(shape, lambda b, i: (0,) * len(shape))
    tok = pl.BlockSpec((1, tt, B_WIDTH), lambda b, i: (b, i, 0))
    s1 = jax.ShapeDtypeStruct((B, T, B_WIDTH), ACT)
    return pl.pallas_call(
        _gdn_prep_kernel,
        grid=(B, T // tt),
        in_specs=_front_specs(tt, T, Wq + SMALL_COLS) + [full((B_CONV, Wq)), full((1, SMALL_COLS)),
                                                          full((1, SMALL_COLS))],
        out_specs=[tok, tok, tok, pl.BlockSpec((1, tt, SMALL_COLS), lambda b, i: (b, i, 0))],
        out_shape=[s1, s1, s1, jax.ShapeDtypeStruct((B, T, SMALL_COLS), F32)],
        scratch_shapes=[pltpu.VMEM((tt, Wq), F32)],
        compiler_params=_params(("parallel", "parallel")),
        name="gdn_prep",
    )(x, x, x, g1, sc, sh, w_b, conv_w, alog_vec, dtb_vec)


def _gdn_chunk_operands(q0, q1, k0, k1, v0, v1, gc2, gct2, bg2, j0, j1, reverse):
    C = CHUNK
    n = 2 * C
    strict, incl, _ = _chunk_masks(reverse)
    top = lax.broadcasted_iota(jnp.int32, (n, 1), 0) < C
    left = lax.broadcasted_iota(jnp.int32, (1, n), 1) < C
    g0, g1 = j0 + 2 * B_HEADS, j1 + 2 * B_HEADS
    gcol = jnp.where(top, gc2[:, g0:g0 + 1], gc2[:, g1:g1 + 1])
    grow = jnp.where(left, gct2[g0:g0 + 1, :], gct2[g1:g1 + 1, :])
    beta = jnp.where(top, bg2[:, j0:j0 + 1], bg2[:, j1:j1 + 1])
    e = 0 if reverse else C - 1
    glast = jnp.where(top, gc2[e:e + 1, g0:g0 + 1], gc2[e:e + 1, g1:g1 + 1])
    ks = jnp.concatenate([k0, k1], axis=0)
    qs = jnp.concatenate([q0, q1], axis=0)
    vs = jnp.concatenate([v0, v1], axis=0)
    egc = jnp.exp(gcol)
    kb = ks * beta
    return dict(gam=jnp.exp(jnp.where(incl, gcol - grow, NEG_INF)), strict=strict, ks=ks,
                kbq=jnp.concatenate([kb, qs], axis=0),
                rhs=jnp.concatenate([vs * beta, kb * egc], axis=1),
                qh=(qs * egc).astype(BF16), ktail=(ks * jnp.exp(glast - gcol)).astype(BF16),
                decay=jnp.exp(glast))


def _gdn_chunks(ops, states):
    C = CHUNK
    n = 2 * C
    eye = _chunk_masks(False)[2].astype(F32)
    ps = [_dot_nt(o["kbq"], o["ks"]) for o in ops]
    lower = [jnp.where(o["strict"], p[0:n] * o["gam"], 0.0) for o, p in zip(ops, ps)]
    aqk = [(p[n:2 * n] * o["gam"]).astype(BF16) for o, p in zip(ops, ps)]
    tinv = [t.astype(BF16) for t in _tri_inverse(lower, eye)]
    uw = [jnp.dot(t, o["rhs"].astype(BF16), preferred_element_type=F32) for t, o in zip(tinv, ops)]
    lx = []
    for l, x in zip(lower, uw):
        lh, ll = _split(l)
        xh, xl = _split(x)
        first = jnp.dot(lh, jnp.concatenate([xh, xl], axis=1), preferred_element_type=F32)
        w2 = x.shape[1]
        lx.append(first[:, 0:w2] + (first[:, w2:2 * w2] + jnp.dot(ll, xh, preferred_element_type=F32)))
    resid = [o["rhs"] - x - y for o, x, y in zip(ops, uw, lx)]
    uw = [x + jnp.dot(t, r.astype(BF16), preferred_element_type=F32) for x, t, r in zip(uw, tinv, resid)]
    sb = [(s0.astype(BF16), s1.astype(BF16)) for s0, s1 in states]
    wq = [[jnp.dot(jnp.concatenate([u[h * C:(h + 1) * C, LANES:].astype(BF16), o["qh"][h * C:(h + 1) * C]],
                                   axis=0), s[h], preferred_element_type=F32) for h in range(2)]
          for u, o, s in zip(uw, ops, sb)]
    ws = [jnp.concatenate([x[0][0:C], x[1][0:C]], axis=0) for x in wq]
    qss = [jnp.concatenate([x[0][C:n], x[1][C:n]], axis=0) for x in wq]
    v_new = [(u[:, 0:LANES] - w).astype(BF16) for u, w in zip(uw, ws)]
    outs = [q + jnp.dot(a, v, preferred_element_type=F32) for q, a, v in zip(qss, aqk, v_new)]
    new_states = []
    for o, v, (s0, s1) in zip(ops, v_new, states):
        new_states.append((s0 * o["decay"][0:1, :] + _dot_tn(o["ktail"][0:C], v[0:C]),
                           s1 * o["decay"][C:C + 1, :] + _dot_tn(o["ktail"][C:n], v[C:n])))
    return [(x[0:C], x[C:n]) for x in outs], new_states


def _gdn_scan_kernel(qf, kf, vf, bgf, qb, kb, vb, bgb, of_out, ob_out, s_scr):
    @pl.when(pl.program_id(1) == 0)
    def _():
        s_scr[...] = jnp.zeros_like(s_scr)

    D = B_HEAD_DIM
    n_pairs = B_HEADS // 2
    sl = lambda h: slice(h * D, (h + 1) * D)
    f32 = lambda a: a.astype(F32)
    states = [(s_scr[d, 2 * i], s_scr[d, 2 * i + 1]) for d in range(2) for i in range(n_pairs)]
    for j in range(SCAN_SUB):
        ops = []
        for d, (q_ref, k_ref, v_ref, bg_ref) in enumerate(((qf, kf, vf, bgf), (qb, kb, vb, bgb))):
            reverse = d == 1
            rows = pl.ds((SCAN_SUB - 1 - j if reverse else j) * CHUNK, CHUNK)
            bg = bg_ref[0, rows, :]
            gc = _dot_hi(_cumsum_matrix(reverse), bg)
            bg2 = jnp.concatenate([bg, bg], axis=0)
            gc2 = jnp.concatenate([gc, gc], axis=0)
            gct2 = gc2.T
            for i in range(n_pairs):
                h0, h1 = 2 * i, 2 * i + 1
                ops.append(_gdn_chunk_operands(
                    f32(q_ref[0, rows, sl(h0)]), f32(q_ref[0, rows, sl(h1)]), f32(k_ref[0, rows, sl(h0)]),
                    f32(k_ref[0, rows, sl(h1)]), f32(v_ref[0, rows, sl(h0)]), f32(v_ref[0, rows, sl(h1)]),
                    gc2, gct2, bg2, d * B_HEADS + h0, d * B_HEADS + h1, reverse))
        outs, states = _gdn_chunks(ops, states)
        for d, o_out in enumerate((of_out, ob_out)):
            rows = pl.ds((SCAN_SUB - 1 - j if d == 1 else j) * CHUNK, CHUNK)
            for i in range(n_pairs):
                o0, o1 = outs[d * n_pairs + i]
                o_out[0, rows, sl(2 * i)] = o0.astype(o_out.dtype)
                o_out[0, rows, sl(2 * i + 1)] = o1.astype(o_out.dtype)
    for d in range(2):
        for i in range(n_pairs):
            s_scr[d, 2 * i], s_scr[d, 2 * i + 1] = states[d * n_pairs + i]


def _gdn_scan(q, k, v, bg):
    B, T, W = q.shape
    rows = CHUNK * SCAN_SUB
    N = T // rows
    fwd = pl.BlockSpec((1, rows, W), lambda b, c: (b, c, 0))
    bwd = pl.BlockSpec((1, rows, W), lambda b, c: (b, N - 1 - c, 0))
    sfwd = pl.BlockSpec((1, rows, SMALL_COLS), lambda b, c: (b, c, 0))
    sbwd = pl.BlockSpec((1, rows, SMALL_COLS), lambda b, c: (b, N - 1 - c, 0))
    out = jax.ShapeDtypeStruct((B, T, W), ACT)
    return pl.pallas_call(
        _gdn_scan_kernel,
        grid=(B, N),
        in_specs=[fwd, fwd, fwd, sfwd, bwd, bwd, bwd, sbwd],
        out_specs=[fwd, bwd],
        out_shape=[out, out],
        scratch_shapes=[pltpu.VMEM((2, B_HEADS, B_HEAD_DIM, B_HEAD_DIM), F32)],
        compiler_params=_params(("parallel", "arbitrary")),
        name="gdn_scan",
    )(q, k, v, bg, q, k, v, bg)


def _pack_bf16_pairs(x):
    n = x.shape[1] // 2
    hi = lax.bitcast_convert_type(x[:, :n].astype(BF16).astype(F32), jnp.int32)
    lo = lax.bitcast_convert_type(x[:, n:].astype(BF16).astype(F32), jnp.int32)
    return hi | lax.shift_right_logical(lo, 16)


def _unpack_bf16_pairs(p):
    a = lax.bitcast_convert_type(p & jnp.int32(-65536), F32)
    b = lax.bitcast_convert_type(lax.shift_left(p, 16), F32)
    return a.astype(BF16), b.astype(BF16)


def _route(scores, biased, base):
    tt = scores.shape[-1]
    shape3 = (N_GROUPS, GROUP_SIZE, tt)
    s3 = scores.reshape(shape3)
    b3 = biased.reshape(shape3)
    jid = lax.broadcasted_iota(jnp.int32, shape3, 1).astype(F32)
    gid = lax.broadcasted_iota(jnp.int32, shape3, 0).astype(F32)
    m1 = jnp.max(b3, axis=1, keepdims=True)
    first = jnp.min(jnp.where(b3 == m1, jid, float(GROUP_SIZE)), axis=1, keepdims=True)
    m2 = jnp.max(jnp.where(jid == first, NEG_INF, b3), axis=1, keepdims=True)
    gs = m1 + m2
    grp = lax.broadcasted_iota(jnp.int32, (N_GROUPS, 1, tt), 0).astype(F32)
    keep = jnp.zeros((N_GROUPS, 1, tt), F32)
    for _ in range(TOPK_GROUPS):
        m = jnp.max(gs, axis=0, keepdims=True)
        pick = grp == jnp.min(jnp.where(gs == m, grp, float(N_GROUPS)), axis=0, keepdims=True)
        keep = jnp.where(pick, 1.0, keep)
        gs = jnp.where(pick, NEG_INF, gs)
    work = jnp.where(keep > 0.0, b3, NEG_INF)
    eid = gid * float(GROUP_SIZE) + jid
    chosen = jnp.zeros(shape3, F32)
    sum01 = lambda a: jnp.sum(jnp.sum(a, axis=1, keepdims=True), axis=0, keepdims=True)
    ids, raw = [], []
    for _ in range(TOP_K):
        m = jnp.max(jnp.max(work, axis=1, keepdims=True), axis=0, keepdims=True)
        cand = jnp.where(work == m, eid, float(N_EXPERTS))
        first = jnp.min(jnp.min(cand, axis=1, keepdims=True), axis=0, keepdims=True)
        pick = eid == first
        ids.append(first)
        raw.append(sum01(jnp.where(pick, s3, 0.0)))
        chosen = jnp.where(pick, 1.0, chosen)
        work = jnp.where(pick, NEG_INF, work)
    total = raw[0]
    for r in raw[1:]:
        total = total + r
    flat = chosen.reshape(N_EXPERTS, tt)
    earlier = (lax.broadcasted_iota(jnp.int32, (tt, tt), 0)
               < lax.broadcasted_iota(jnp.int32, (tt, tt), 1)).astype(BF16)
    prefix = (jnp.dot(flat.astype(BF16), earlier, preferred_element_type=F32) + base).reshape(shape3)
    row = lax.broadcasted_iota(jnp.int32, (SUBLANES, tt), 0)
    idx8 = jnp.zeros((SUBLANES, tt), F32)
    w8 = jnp.zeros((SUBLANES, tt), F32)
    pos8 = jnp.zeros((SUBLANES, tt), F32)
    for k in range(TOP_K):
        rank = sum01(jnp.where(eid == ids[k], prefix, 0.0))
        idx8 = jnp.where(row == k, ids[k].reshape(1, tt), idx8)
        w8 = jnp.where(row == k, (raw[k] / total * ROUTED_SCALE).reshape(1, tt), w8)
        pos8 = jnp.where(row == k, rank.reshape(1, tt), pos8)
    counts = jnp.sum(flat, axis=1, keepdims=True)
    return idx8.astype(jnp.int32), w8, pos8.astype(jnp.int32), counts


def _mixer_post_kernel(yf, yb, bonus, g, of, ob, zz, zga, zgb, x_ref, gt1, sc2, sh2,
                       gnw, gnb, ones_ref, aproj, bng, bproj, wout, n2g, rwt, rbias,
                       x1_out, h2_out, idx_out, w_out_ref, pos_out, counts_out):
    @pl.when((pl.program_id(0) == 0) & (pl.program_id(1) == 0))
    def _():
        counts_out[...] = jnp.zeros_like(counts_out)

    ones_bd = ones_ref[...]
    f32 = lambda ref: ref[0].astype(F32)
    y = f32(yf) + f32(yb)
    mean = _seg_sum(y, ones_bd) * (1.0 / A_HEAD_DIM)
    dlt = y - mean
    var = _seg_sum(dlt * dlt, ones_bd) * (1.0 / A_HEAD_DIM)
    yn = dlt * lax.rsqrt(var + A_GN_EPS) * gnw[...] + gnb[...]
    y_a = _dot((yn + f32(bonus)) * f32(g), aproj[...])

    o = f32(of) + f32(ob)
    z = f32(zz)
    parts = []
    for h in range(B_HEADS):
        oh = o[:, h * B_HEAD_DIM:(h + 1) * B_HEAD_DIM]
        parts.append(oh * lax.rsqrt(jnp.mean(oh * oh, axis=-1, keepdims=True) + EPS))
    on = jnp.concatenate(parts, axis=1) * bng[...] * _silu(z)
    y_b = _dot(on, bproj[...])

    u = _sigmoid(f32(zga)) * y_a + _sigmoid(f32(zgb)) * y_b
    x1 = x_ref[0] + gt1[0] * _dot(u, wout[...])
    x1_out[0] = x1
    h2 = _rms_mod(x1, n2g[...], sc2[0], sh2[0])
    h2_out[0] = _pack_bf16_pairs(h2)
    logits_t = _dot_nt(rwt[...], h2)
    scores = _sigmoid(logits_t)
    base = counts_out[:, 0:1]
    idx8, w8, pos8, counts = _route(scores, scores + rbias[...], base)
    idx_out[0] = idx8
    w_out_ref[0] = w8
    pos_out[0] = pos8
    counts_out[...] = counts_out[...] + counts


def _mixer_post(yf, yb, bonus, g, of, ob, z_c, x, gt1, sc2, sh2, gn_w, gn_b, ones_bd, a_proj, bng,
                b_proj, w_out, n2g, rwt, rbias):
    B, T, D = x.shape
    tt = min(256, T)
    full = lambda shape: pl.BlockSpec(shape, lambda b, i: (0,) * len(shape))
    tokw = lambda w, col=0: pl.BlockSpec((1, tt, w), lambda b, i: (b, i, col))
    modrow = pl.BlockSpec((1, 1, D), lambda b, i: (b, 0, 0))
    pick = pl.BlockSpec((1, SUBLANES, tt), lambda b, i: (b, 0, i))
    W = A_WIDTH
    return pl.pallas_call(
        _mixer_post_kernel,
        grid=(B, T // tt),
        in_specs=[tokw(W), tokw(W), tokw(W), tokw(W), tokw(D), tokw(D),
                  tokw(D, 0), tokw(D, 1), tokw(D, 2), tokw(D), modrow, modrow, modrow,
                  full((1, W)), full((1, W)), full((W, W)), full((W, D)), full((1, D)), full((D, D)),
                  full((D, D)), full((1, D)), full((N_EXPERTS, D)), full((N_EXPERTS, 1))],
        out_specs=[tokw(D), tokw(D // 2), pick, pick, pick,
                   pl.BlockSpec((N_EXPERTS, LANES), lambda b, i: (0, 0))],
        out_shape=[jax.ShapeDtypeStruct((B, T, D), F32), jax.ShapeDtypeStruct((B, T, D // 2), jnp.int32),
                   jax.ShapeDtypeStruct((B, SUBLANES, T), jnp.int32),
                   jax.ShapeDtypeStruct((B, SUBLANES, T), F32),
                   jax.ShapeDtypeStruct((B, SUBLANES, T), jnp.int32),
                   jax.ShapeDtypeStruct((N_EXPERTS, LANES), F32)],
        compiler_params=_params(("arbitrary", "arbitrary")),
        name="mixer_post",
    )(yf, yb, bonus, g, of, ob, z_c, z_c, z_c, x, gt1, sc2, sh2, gn_w, gn_b, ones_bd, a_proj, bng,
      b_proj, w_out, n2g, rwt, rbias)


def _gather_rows(table, idx):
    n_rows, width = idx.shape[0], table.shape[1]
    per_worker = n_rows // SC_WORKERS
    n_windows = per_worker // GATHER_WINDOW
    mesh = plsc.VectorSubcoreMesh(core_axis_name="c", subcore_axis_name="s")

    @functools.partial(
        pl.kernel, mesh=mesh, out_type=jax.ShapeDtypeStruct((n_rows, width), table.dtype),
        scratch_types=[pltpu.VMEM((GATHER_WINDOW,), jnp.int32),
                       pltpu.VMEM((GATHER_WINDOW, width), table.dtype),
                       pltpu.SemaphoreType.DMA])
    def gather(table_hbm, idx_hbm, out_hbm, idx_v, rows_v, sem):
        worker = lax.axis_index("s") * SC_CORES + lax.axis_index("c")

        @pl.loop(0, n_windows)
        def _(j):
            off = pl.multiple_of(worker * per_worker + j * GATHER_WINDOW, GATHER_WINDOW)
            pltpu.sync_copy(idx_hbm.at[pl.ds(off, GATHER_WINDOW)], idx_v)
            pltpu.async_copy(table_hbm.at[idx_v], rows_v, sem).wait()
            pltpu.sync_copy(rows_v, out_hbm.at[pl.ds(off, GATHER_WINDOW)])

    return gather(table, idx)


def _scatter_rows(rows, dest_w, n_out):
    n_tok, width = rows.shape
    per_worker = n_tok // GATHER_WINDOW // SC_WORKERS
    mesh = plsc.VectorSubcoreMesh(core_axis_name="c", subcore_axis_name="s")

    @functools.partial(
        pl.kernel, mesh=mesh, out_type=jax.ShapeDtypeStruct((n_out, width), rows.dtype),
        scratch_types=[pltpu.VMEM((SUBLANES, GATHER_WINDOW), jnp.int32),
                       pltpu.VMEM((GATHER_WINDOW, width), rows.dtype)])
    def scatter(rows_hbm, dest_hbm, out_hbm, idx_v, rows_v):
        worker = lax.axis_index("s") * SC_CORES + lax.axis_index("c")

        @pl.loop(0, per_worker)
        def _(j):
            win = worker * per_worker + j
            pltpu.sync_copy(dest_hbm.at[win], idx_v)
            pltpu.sync_copy(rows_hbm.at[pl.ds(pl.multiple_of(win * GATHER_WINDOW, GATHER_WINDOW),
                                              GATHER_WINDOW)], rows_v)
            for k in range(TOP_K):
                pltpu.sync_copy(rows_v, out_hbm.at[idx_v.at[k]])

    return scatter(rows, dest_w)


def _dest_kernel(start_ref, idx_ref, pos_ref, dest_ref):
    idx = idx_ref[0]
    dest = jnp.zeros_like(idx)
    for e in range(N_EXPERTS):
        dest = jnp.where(idx == e, start_ref[e], dest)
    dest_ref[0] = dest + pos_ref[0]


def _dest_rows(pad_start, idx, pos):
    B, _, T = idx.shape
    tt = min(2048, T)
    spec = pl.BlockSpec((1, SUBLANES, tt), lambda b, i, start: (b, 0, i))
    return pl.pallas_call(
        _dest_kernel,
        grid_spec=pltpu.PrefetchScalarGridSpec(num_scalar_prefetch=1, grid=(B, T // tt),
                                               in_specs=[spec, spec], out_specs=spec),
        out_shape=jax.ShapeDtypeStruct(idx.shape, jnp.int32),
        compiler_params=_params(("parallel", "parallel")),
        name="dest_rows",
    )(pad_start, idx, pos)


def _expert_kernel(be_ref, valid_ref, xs_ref, wg, wu, wd, ys_ref):
    n_valid = valid_ref[pl.program_id(0)]

    @pl.when(n_valid > 0)
    def _():
        half = D_MODEL // 2
        row = lax.broadcasted_iota(jnp.int32, xs_ref.shape, 0)
        a, b = _unpack_bf16_pairs(jnp.where(row < n_valid, xs_ref[...], 0))
        mm = lambda w: (jnp.dot(a, w[0, 0:half, :], preferred_element_type=F32)
                        + jnp.dot(b, w[0, half:D_MODEL, :], preferred_element_type=F32))
        hid = _silu(mm(wg)) * mm(wu)
        ys_ref[...] = _pack_bf16_pairs(_dot(hid, wd[0]))


def _experts(block_expert, block_valid, xs, wg, wu, wd):
    n_rows, half = xs.shape
    rows = pl.BlockSpec((MOE_BLOCK, half), lambda i, be, valid: (i, 0))
    wspec = lambda shape: pl.BlockSpec(shape, lambda i, be, valid: (be[i], 0, 0))
    return pl.pallas_call(
        _expert_kernel,
        grid_spec=pltpu.PrefetchScalarGridSpec(
            num_scalar_prefetch=2, grid=(n_rows // MOE_BLOCK,),
            in_specs=[rows, wspec((1, D_MODEL, D_EXPERT)), wspec((1, D_MODEL, D_EXPERT)),
                      wspec((1, D_EXPERT, D_MODEL))],
            out_specs=rows),
        out_shape=jax.ShapeDtypeStruct((n_rows, half), jnp.int32),
        compiler_params=_params(("arbitrary",)),
        name="experts",
    )(block_expert, block_valid, xs, wg, wu, wd)


def _moe_final_kernel(yg_ref, w_ref, h_ref, x1_ref, gt2, fg, sg, su, sd, o_ref):
    half = D_MODEL // 2
    a, b = _unpack_bf16_pairs(h_ref[0])
    mm = lambda w: (jnp.dot(a, w[0:half, :], preferred_element_type=F32)
                    + jnp.dot(b, w[half:D_MODEL, :], preferred_element_type=F32))
    shared = _dot(_silu(mm(sg)) * mm(su), sd[...])
    w = w_ref[0]
    lo = hi = None
    for k in range(TOP_K):
        ya, yb = _unpack_bf16_pairs(yg_ref[k, 0])
        wk = w[:, k:k + 1]
        lo = ya.astype(F32) * wk if lo is None else lo + ya.astype(F32) * wk
        hi = yb.astype(F32) * wk if hi is None else hi + yb.astype(F32) * wk
    x2 = x1_ref[0] + gt2[0] * (jnp.concatenate([lo, hi], axis=1) + shared)
    o_ref[0] = x2 * lax.rsqrt(jnp.mean(x2 * x2, axis=-1, keepdims=True) + EPS) * fg[...]


def _moe_final(yg, w_t, h2p, x1, gt2, fg, sg, su, sd):
    B, T, D = x1.shape
    tm = min(512, T)
    tok = lambda w: pl.BlockSpec((1, tm, w), lambda b, i: (b, i, 0))
    full = lambda shape: pl.BlockSpec(shape, lambda b, i: (0,) * len(shape))
    return pl.pallas_call(
        _moe_final_kernel,
        grid=(B, T // tm),
        in_specs=[pl.BlockSpec((TOP_K, 1, tm, D // 2), lambda b, i: (0, b, i, 0)), tok(SUBLANES),
                  tok(D // 2), tok(D), pl.BlockSpec((1, 1, D), lambda b, i: (b, 0, 0)), full((1, D)),
                  full((D, D_EXPERT)), full((D, D_EXPERT)), full((D_EXPERT, D))],
        out_specs=tok(D),
        out_shape=jax.ShapeDtypeStruct((B, T, D), F32),
        compiler_params=_params(("parallel", "parallel")),
        name="moe_final",
    )(yg, w_t, h2p, x1, gt2, fg, sg, su, sd)


def _moe_routed(h2p, idx, w, pos, counts, x1, gt2, p):
    B, T, half = h2p.shape
    n_slots = B * T * TOP_K
    n_rows = n_slots + N_EXPERTS * MOE_BLOCK
    cnt = counts[:, 0].astype(jnp.int32)
    padded = (cnt + MOE_BLOCK - 1) // MOE_BLOCK * MOE_BLOCK
    pad_end = jnp.cumsum(padded)
    pad_start = pad_end - padded
    dest = _dest_rows(pad_start, idx, pos)
    starts = jnp.arange(n_rows // MOE_BLOCK, dtype=jnp.int32) * MOE_BLOCK
    block_expert = jnp.minimum(jnp.sum((pad_end[None, :] <= starts[:, None]).astype(jnp.int32), axis=1),
                               N_EXPERTS - 1)
    block_valid = jnp.clip((pad_start + cnt)[block_expert] - starts, 0, MOE_BLOCK)
    dest_w = jnp.swapaxes(dest.reshape(B, SUBLANES, T // GATHER_WINDOW, GATHER_WINDOW), 1, 2)
    xs = _scatter_rows(h2p.reshape(B * T, half), dest_w.reshape(-1, SUBLANES, GATHER_WINDOW), n_rows)
    ys = _experts(block_expert, block_valid, xs, p["wg"], p["wu"], p["wd"])
    yg = _gather_rows(ys, jnp.swapaxes(dest[:, :TOP_K], 0, 1).reshape(-1)).reshape(TOP_K, B, T, half)
    return _moe_final(yg, jnp.swapaxes(w, 1, 2), h2p, x1, gt2, p["fg"], p["sg"], p["su"], p["sd"])


def _prepare_weights(w_in, shift_mu, a_w0, a_w_up, a_a0, a_a_up, a_g_up, a_k_k, a_k_a, a_r_k, a_gn_w,
                     a_gn_b, a_proj, b_conv_w, b_a_log, b_dt_bias, b_norm_g, b_proj, w_out, router_w,
                     router_bias, exp_gate, exp_up, exp_down, sh_gate, sh_up, sh_down, norm1_g, norm2_g,
                     final_g):
    W = A_WIDTH
    w = w_in[0]
    c0 = A_COLS
    c1 = c0 + 3 * B_WIDTH
    c2 = c1 + B_WIDTH
    c3 = c2 + 4 * B_HEADS
    pad = jnp.zeros((D_MODEL, SMALL_COLS - 4 * B_HEADS), F32)
    w_a = w[:, :c0].astype(BF16)
    w_b = jnp.concatenate([w[:, c0:c1], w[:, c2:c3], pad], axis=1).astype(BF16)
    w_c = jnp.concatenate([w[:, c1:c2], w[:, c3:]], axis=1).astype(BF16)
    zeros = jnp.zeros((2, A_RANK_W, W), F32)
    up_comb = jnp.concatenate([jnp.concatenate([a_w_up[0], zeros], axis=2),
                               jnp.concatenate([zeros, a_a_up[0]], axis=2)], axis=1)
    head = jnp.arange(W) // A_HEAD_DIM
    ones_bd = (head[:, None] == head[None, :]).astype(BF16)
    small = lambda v: jnp.zeros((1, SMALL_COLS), F32).at[0, 2 * B_HEADS:4 * B_HEADS].set(v.reshape(-1))
    return dict(
        w_a=w_a, w_b=w_b, w_c=w_c, mu=shift_mu[0].reshape(1, A_COLS), w0=a_w0[0], a0=a_a0[0],
        up_comb=up_comb, g_up=a_g_up[0], k_k=a_k_k[0].reshape(1, W), k_a=a_k_a[0].reshape(1, W),
        r_k=a_r_k[0].reshape(1, W), ones_bd=ones_bd, gn_w=a_gn_w[0].reshape(1, W),
        gn_b=a_gn_b[0].reshape(1, W), a_proj=a_proj[0].astype(BF16),
        conv_w=b_conv_w[0].astype(BF16).astype(F32),
        alog=small(b_a_log[0]), dtb=small(b_dt_bias[0]),
        bng=jnp.tile(b_norm_g[0], B_HEADS).reshape(1, B_WIDTH), b_proj=b_proj[0].astype(BF16),
        w_out=w_out[0].astype(BF16), rwt=router_w[0].T, rbias=router_bias[0].reshape(N_EXPERTS, 1),
        wg=exp_gate[0].astype(BF16), wu=exp_up[0].astype(BF16), wd=exp_down[0].astype(BF16),
        sg=sh_gate[0].astype(BF16), su=sh_up[0].astype(BF16), sd=sh_down[0].astype(BF16),
        n1g=norm1_g[0].reshape(1, D_MODEL), n2g=norm2_g[0].reshape(1, D_MODEL),
        fg=final_g.reshape(1, D_MODEL))


def _layer(x, mod, p):
    B = x.shape[0]
    sh1, sc1, gt1, sh2, sc2, gt2 = (m.reshape(B, 1, D_MODEL) for m in jnp.split(mod, 6, axis=-1))
    z_c = _inproj(x, p["n1g"], sc1, sh1, p["w_c"])
    r, kk, v, g, bonus, lw, kd, bd = _rwkv_prep(x, p["n1g"], sc1, sh1, p["w_a"], p["mu"], p["w0"], p["a0"],
                                                p["up_comb"], p["g_up"], p["k_k"], p["k_a"], p["r_k"],
                                                p["ones_bd"])
    yf, yb = _rwkv_scan(r, kk, v, lw, kd, bd)
    q, k, vv, bg = _gdn_prep(x, p["n1g"], sc1, sh1, p["w_b"], p["conv_w"], p["alog"], p["dtb"])
    of, ob = _gdn_scan(q, k, vv, bg)
    x1, h2p, idx, w, pos, counts = _mixer_post(yf, yb, bonus, g, of, ob, z_c, x, gt1, sc2, sh2, p["gn_w"],
                                               p["gn_b"], p["ones_bd"], p["a_proj"], p["bng"], p["b_proj"],
                                               p["w_out"], p["n2g"], p["rwt"], p["rbias"])
    return _moe_routed(h2p, idx, w, pos, counts, x1, gt2, p)


def kernel(x_prompt, x_sample, c_prompt, c_sample, ada_w, ada_b, norm1_g, norm2_g, w_in, shift_mu, a_w0, a_w_up, a_a0, a_a_up, a_g_up, a_k_k, a_k_a, a_r_k, a_gn_w, a_gn_b, a_proj, b_conv_w, b_a_log, b_dt_bias, b_norm_g, b_proj, w_out, router_w, router_bias, exp_gate, exp_up, exp_down, sh_gate, sh_up, sh_down, final_g):
    p = _prepare_weights(w_in, shift_mu, a_w0, a_w_up, a_a0, a_a_up, a_g_up, a_k_k, a_k_a, a_r_k, a_gn_w,
                         a_gn_b, a_proj, b_conv_w, b_a_log, b_dt_bias, b_norm_g, b_proj, w_out, router_w,
                         router_bias, exp_gate, exp_up, exp_down, sh_gate, sh_up, sh_down, norm1_g,
                         norm2_g, final_g)
    nb_p, nb_s = c_prompt.shape[0], c_sample.shape[0]
    rows = -(-(nb_p + nb_s) // SUBLANES) * SUBLANES
    c_all = jnp.concatenate([c_prompt, c_sample, jnp.zeros((rows - nb_p - nb_s, D_MODEL), F32)], axis=0)
    mod = _adaln_mod(c_all, ada_w[0], ada_b[0])
    y_prompt = _layer(x_prompt, mod[:nb_p], p)
    y_sample = _layer(x_sample, mod[nb_p:nb_p + nb_s], p)
    return (y_prompt, y_sample)
```

```python
import functools

import jax
import jax.numpy as jnp
from jax import lax
from jax.experimental import pallas as pl
from jax.experimental.pallas import tpu as pltpu
from jax.experimental.pallas import tpu_sc as plsc

F32 = jnp.float32
BF16 = jnp.bfloat16
ACT = jnp.bfloat16
HIGHEST = lax.Precision.HIGHEST

D_MODEL = 1024
A_HEADS = 8
A_HEAD_DIM = 64
A_WIDTH = A_HEADS * A_HEAD_DIM
A_RANK_W = 64
A_RANK_A = 64
A_RANK_G = 128
A_GN_EPS = 64e-5
A_COLS = 3 * A_WIDTH + A_RANK_W + A_RANK_A + A_RANK_G
B_HEADS = 8
B_HEAD_DIM = 128
B_WIDTH = B_HEADS * B_HEAD_DIM
B_CONV = 5
CHUNK = 64
SCAN_SUB = 4
GDN_COL_GROUP = 512
N_EXPERTS = 64
TOP_K = 6
N_GROUPS = 8
TOPK_GROUPS = 4
GROUP_SIZE = N_EXPERTS // N_GROUPS
D_EXPERT = 256
ROUTED_SCALE = 2.5
EPS = 1e-6
LANES = 128
SUBLANES = 8
HALO = 16
SMALL_COLS = LANES
MOE_BLOCK = 512
SC_CORES = 2
SC_SUBCORES = 16
SC_WORKERS = SC_CORES * SC_SUBCORES
GATHER_WINDOW = 64
VMEM_LIMIT = 56 * 1024 * 1024
NEG_INF = float("-inf")


def _dot(a, b):
    return jnp.dot(a.astype(BF16), b.astype(BF16), preferred_element_type=F32)


def _dot_nt(a, b):
    return lax.dot_general(a.astype(BF16), b.astype(BF16), (((1,), (1,)), ((), ())),
                           preferred_element_type=F32)


def _dot_tn(a, b):
    return lax.dot_general(a.astype(BF16), b.astype(BF16), (((0,), (0,)), ((), ())),
                           preferred_element_type=F32)


def _dot_hi(a, b):
    return jnp.dot(a, b, precision=HIGHEST, preferred_element_type=F32)


def _split(x):
    hi = x.astype(BF16)
    return hi, (x - hi.astype(F32)).astype(BF16)


def _dot3(a, b, dims=(((1,), (0,)), ((), ()))):
    ah, al = _split(a)
    bh, bl = _split(b)
    d = lambda u, v: lax.dot_general(u, v, dims, preferred_element_type=F32)
    return d(ah, bh) + (d(ah, bl) + d(al, bh))


def _seg_sum(x, ones_bd):
    hi = x.astype(BF16)
    lo = (x - hi.astype(F32)).astype(BF16)
    return (jnp.dot(hi, ones_bd, preferred_element_type=F32)
            + jnp.dot(lo, ones_bd, preferred_element_type=F32))


def _softplus(x):
    return jnp.maximum(x, 0.0) + jnp.log1p(jnp.exp(-jnp.abs(x)))


def _sigmoid(x):
    return 1.0 / (1.0 + jnp.exp(-x))


def _silu(x):
    return x * _sigmoid(x)


def _tri_inverse(a_list, eye):
    n = a_list[0].shape[0]
    xs = [(-a).astype(BF16) for a in a_list]
    ts = [eye - a for a in a_list]
    xs = [jnp.dot(x, x, preferred_element_type=F32).astype(BF16) for x in xs]
    power = 2
    while 2 * power < CHUNK:
        both = [jnp.dot(x, jnp.concatenate([t.astype(BF16), x], axis=1), preferred_element_type=F32)
                for t, x in zip(ts, xs)]
        ts = [t + b[:, 0:n] for t, b in zip(ts, both)]
        xs = [b[:, n:2 * n].astype(BF16) for b in both]
        power *= 2
    return [t + jnp.dot(x, t.astype(BF16), preferred_element_type=F32) for t, x in zip(ts, xs)]


def _params(sem, **extra):
    return pltpu.CompilerParams(dimension_semantics=sem, vmem_limit_bytes=VMEM_LIMIT, **extra)


def _mod_kernel(c_ref, w_ref, b_ref, o_ref):
    c = c_ref[...]
    o_ref[...] = _dot_hi(_silu(c), w_ref[...]) + b_ref[...]


def _adaln_mod(c, ada_w, ada_b):
    rows = c.shape[0]
    n = ada_w.shape[1]
    return pl.pallas_call(
        _mod_kernel,
        grid=(n // D_MODEL,),
        in_specs=[pl.BlockSpec((rows, D_MODEL), lambda j: (0, 0)),
                  pl.BlockSpec((D_MODEL, D_MODEL), lambda j: (0, j)),
                  pl.BlockSpec((1, D_MODEL), lambda j: (0, j))],
        out_specs=pl.BlockSpec((rows, D_MODEL), lambda j: (0, j)),
        out_shape=jax.ShapeDtypeStruct((rows, n), F32),
        compiler_params=_params(("arbitrary",)),
        name="adaln_mod",
    )(c, ada_w, ada_b.reshape(1, n))


def _rms_mod(x, g, sc, sh):
    y = x * lax.rsqrt(jnp.mean(x * x, axis=-1, keepdims=True) + EPS)
    return (y * g) * (1.0 + sc) + sh


def _inproj_kernel(x_ref, g_ref, sc_ref, sh_ref, w_ref, *o_refs):
    h = _rms_mod(x_ref[0], g_ref[...], sc_ref[0], sh_ref[0]).astype(BF16)
    col = 0
    for o_ref in o_refs:
        n = o_ref.shape[2]
        o_ref[0] = jnp.dot(h, w_ref[:, col:col + n], preferred_element_type=F32).astype(o_ref.dtype)
        col += n


def _inproj(x, g, sc, sh, w, tail=0):
    B, T, D = x.shape
    n = w.shape[1]
    tm = min(512, T)
    widths = [(n - tail, ACT)] + ([(tail, F32)] if tail else [])
    outs = pl.pallas_call(
        _inproj_kernel,
        grid=(B, T // tm),
        in_specs=[pl.BlockSpec((1, tm, D), lambda b, i: (b, i, 0)),
                  pl.BlockSpec((1, D), lambda b, i: (0, 0)),
                  pl.BlockSpec((1, 1, D), lambda b, i: (b, 0, 0)),
                  pl.BlockSpec((1, 1, D), lambda b, i: (b, 0, 0)),
                  pl.BlockSpec((D, n), lambda b, i: (0, 0))],
        out_specs=[pl.BlockSpec((1, tm, wd), lambda b, i: (b, i, 0)) for wd, _ in widths],
        out_shape=[jax.ShapeDtypeStruct((B, T, wd), dt) for wd, dt in widths],
        compiler_params=_params(("parallel", "parallel")),
        name="in_proj",
    )(x, g, sc, sh, w)
    return outs if tail else outs[0]


def _fill_halo(ext_ref, cur, prev8, next8, tt):
    i = pl.program_id(1)
    last = pl.num_programs(1) - 1
    ext_ref[pl.ds(0, SUBLANES), :] = jnp.where(i == 0, 0.0, prev8)
    ext_ref[pl.ds(SUBLANES, tt), :] = cur
    ext_ref[pl.ds(SUBLANES + tt, SUBLANES), :] = jnp.where(i == last, 0.0, next8)


def _halo_specs(tt, width, col_block, seq_len):
    nb = tt // HALO
    last = seq_len // HALO - 1
    return [
        pl.BlockSpec((1, tt, width), lambda b, i: (b, i, col_block)),
        pl.BlockSpec((1, HALO, width), lambda b, i: (b, jnp.maximum(i * nb - 1, 0), col_block)),
        pl.BlockSpec((1, HALO, width), lambda b, i: (b, jnp.minimum((i + 1) * nb, last), col_block)),
    ]


def _halo_rows(zp_ref, zn_ref):
    return zp_ref[0].astype(F32)[HALO - SUBLANES:HALO], zn_ref[0].astype(F32)[0:SUBLANES]


def _normed_with_halo(x_ref, xp_ref, xn_ref, g_ref, sc_ref, sh_ref):
    prev8, next8 = _halo_rows(xp_ref, xn_ref)
    x_ext = jnp.concatenate([prev8, x_ref[0], next8], axis=0)
    return _rms_mod(x_ext, g_ref[...], sc_ref[0], sh_ref[0]).astype(BF16)


def _rwkv_prep_kernel(x_ref, xp_ref, xn_ref, g1_ref, sc_ref, sh_ref, w_ref,
                      mu_ref, w0_ref, a0_ref, up_ref, gup_ref, kk_ref, ka_ref, rk_ref, ones_ref,
                      r_out, kk_out, v_out, g_out, bonus_out, lw_out, kd_out, bd_out, ext_scr):
    tt = x_ref.shape[1]
    i = pl.program_id(1)
    last = pl.num_programs(1) - 1
    h = _normed_with_halo(x_ref, xp_ref, xn_ref, g1_ref, sc_ref, sh_ref)
    project = lambda c0, c1: jnp.dot(h, w_ref[:, c0:c1], preferred_element_type=F32)
    row = lax.broadcasted_iota(jnp.int32, (tt + 2 * SUBLANES, 1), 0)
    inside = ((row >= SUBLANES) | (i > 0)) & ((row < tt + SUBLANES) | (i < last))
    W = A_WIDTH

    def shift_mix(z_ext, c0, c1):
        z_ext = z_ext.astype(ACT).astype(F32)
        ext_scr[:, c0:c1] = jnp.where(inside, z_ext, 0.0)
        z = ext_scr[pl.ds(SUBLANES, tt), c0:c1]
        z_prev = ext_scr[pl.ds(SUBLANES - 1, tt), c0:c1]
        z_next = ext_scr[pl.ds(SUBLANES + 1, tt), c0:c1]
        return z + (0.5 * (z_prev + z_next) - z) * mu_ref[:, c0:c1]

    groups = [(3 * W, A_COLS), (W, 2 * W), (0, W), (2 * W, 3 * W)]
    mixed = []
    z_nxt = project(*groups[0])
    for gi, (c0, c1) in enumerate(groups):
        z_cur = z_nxt
        if gi + 1 < len(groups):
            z_nxt = project(*groups[gi + 1])
        mixed.append(shift_mix(z_cur, c0, c1))
    zwag, zk, zr, zv = mixed
    zwa = zwag[:, 0:LANES]
    zg = zwag[:, LANES:2 * LANES]
    ones_bd = ones_ref[...]
    kk_raw = zk * kk_ref[...]
    kk = kk_raw * lax.rsqrt(_seg_sum(kk_raw * kk_raw, ones_bd) + 1e-6)
    lane = lax.broadcasted_iota(jnp.int32, zwa.shape, 1)
    lhs = jnp.where(lane < A_RANK_W, jnp.tanh(zwa), zwa)
    k_sum = jnp.zeros_like(zk)
    for d in range(2):
        up = _dot(lhs, up_ref[d])
        wl = w0_ref[d:d + 1, :] + up[:, 0:W]
        w_log = -_softplus(-wl) - 0.5
        lw_out[d, 0] = -jnp.exp(w_log)
        a = _sigmoid(a0_ref[d:d + 1, :] + up[:, W:2 * W])
        k_d = zk * (1.0 + (a - 1.0) * ka_ref[...])
        kd_out[d, 0] = k_d.astype(kd_out.dtype)
        bd_out[d, 0] = (kk * a).astype(bd_out.dtype)
        k_sum = k_sum + k_d
    r_out[0] = zr.astype(r_out.dtype)
    kk_out[0] = kk.astype(kk_out.dtype)
    v_out[0] = zv.astype(v_out.dtype)
    g_out[0] = _dot(_sigmoid(zg), gup_ref[...]).astype(g_out.dtype)
    bonus_out[0] = (_seg_sum(zr * k_sum * rk_ref[...], ones_bd) * zv).astype(bonus_out.dtype)


def _front_specs(tt, seq_len, n_cols):
    modrow = pl.BlockSpec((1, 1, D_MODEL), lambda b, i: (b, 0, 0))
    return _halo_specs(tt, D_MODEL, 0, seq_len) + [
        pl.BlockSpec((1, D_MODEL), lambda b, i: (0, 0)), modrow, modrow,
        pl.BlockSpec((D_MODEL, n_cols), lambda b, i: (0, 0))]


def _rwkv_prep(x, g1, sc, sh, w_a, mu, w0, a0, up_comb, g_up, k_k, k_a, r_k, ones_bd):
    B, T, _ = x.shape
    tt = min(256, T)
    W = A_WIDTH
    full = lambda shape: pl.BlockSpec(shape, lambda b, i: (0,) * len(shape))
    tok = pl.BlockSpec((1, tt, W), lambda b, i: (b, i, 0))
    tok2 = pl.BlockSpec((2, 1, tt, W), lambda b, i: (0, b, i, 0))
    s1 = jax.ShapeDtypeStruct((B, T, W), ACT)
    s2 = jax.ShapeDtypeStruct((2, B, T, W), ACT)
    lw = jax.ShapeDtypeStruct((2, B, T, W), F32)
    return pl.pallas_call(
        _rwkv_prep_kernel,
        grid=(B, T // tt),
        in_specs=_front_specs(tt, T, A_COLS) + [
            full((1, A_COLS)), full((2, W)), full((2, W)), full((2, LANES, 2 * W)),
            full((A_RANK_G, W)), full((1, W)), full((1, W)), full((1, W)), full((W, W))],
        out_specs=[tok, tok, tok, tok, tok, tok2, tok2, tok2],
        out_shape=[s1, s1, s1, s1, s1, lw, s2, s2],
        scratch_shapes=[pltpu.VMEM((tt + 2 * SUBLANES, A_COLS), F32)],
        compiler_params=_params(("parallel", "parallel")),
        name="rwkv_prep",
    )(x, x, x, g1, sc, sh, w_a, mu, w0, a0, up_comb, g_up, k_k, k_a, r_k, ones_bd)


def _chunk_masks(reverse):
    n = 2 * CHUNK
    row = lax.broadcasted_iota(jnp.int32, (n, n), 0)
    col = lax.broadcasted_iota(jnp.int32, (n, n), 1)
    same = (row // CHUNK) == (col // CHUNK)
    ti, tj = row % CHUNK, col % CHUNK
    if reverse:
        return same & (ti < tj), same & (ti <= tj), row == col
    return same & (ti > tj), same & (ti >= tj), row == col


def _cumsum_matrix(reverse):
    row = lax.broadcasted_iota(jnp.int32, (CHUNK, CHUNK), 0)
    col = lax.broadcasted_iota(jnp.int32, (CHUNK, CHUNK), 1)
    return ((row <= col) if reverse else (row >= col)).astype(F32)


def _rwkv_chunk_operands(r, kk, v, k, b, cs, lw, reverse):
    C = CHUNK
    cs_end = cs[0:1, :] if reverse else cs[C - 1:C, :]
    m0 = lax.broadcasted_iota(jnp.int32, (C, LANES), 1) < A_HEAD_DIM

    def stack(x):
        return jnp.concatenate([jnp.where(m0, x, 0.0), jnp.where(m0, 0.0, x)], axis=0)

    g_inv = jnp.exp(-cs)
    g_tail = jnp.exp(cs_end - cs)
    strict, incl, _ = _chunk_masks(reverse)
    return dict(rg=stack(r * jnp.exp(cs)), kkg=stack(kk * jnp.exp(cs - lw)), ki=stack(k * g_inv),
                bi=stack(b * g_inv), kt=stack(k * g_tail), bt=stack(b * g_tail), vs=stack(v),
                g_end=jnp.exp(cs_end), strict=strict, incl=incl)


def _rwkv_chunks(ops, hts):
    C = CHUNK
    n = 2 * C
    eye = _chunk_masks(False)[2].astype(F32)
    ps = [_dot_nt(jnp.concatenate([o["kkg"], o["rg"]], axis=0), jnp.concatenate([o["bi"], o["ki"]], axis=0))
          for o in ops]
    a_ab = [jnp.where(o["strict"], p[0:n, 0:n], 0.0) for o, p in zip(ops, ps)]
    a_rb = [jnp.where(o["incl"], p[n:2 * n, 0:n], 0.0).astype(BF16) for o, p in zip(ops, ps)]
    a_k = [jnp.concatenate([jnp.where(o["strict"], p[0:n, n:2 * n], 0.0),
                            jnp.where(o["incl"], p[n:2 * n, n:2 * n], 0.0)], axis=0).astype(BF16)
           for o, p in zip(ops, ps)]
    vsb = [o["vs"].astype(BF16) for o in ops]
    a_kv = [jnp.dot(a, v, preferred_element_type=F32) for a, v in zip(a_k, vsb)]
    akv = [x[0:n] for x in a_kv]
    arkv = [x[n:2 * n] for x in a_kv]
    tinv = _tri_inverse(a_ab, eye)
    wu = [_dot(t, jnp.concatenate([o["kkg"], x], axis=1)).astype(BF16)
          for t, o, x in zip(tinv, ops, akv)]
    r2 = [jnp.dot(a, w, preferred_element_type=F32) for a, w in zip(a_rb, wu)]
    btb = [o["bt"].astype(BF16) for o in ops]
    btw = [_dot_tn(b, w[:, 0:LANES]) for b, w in zip(btb, wu)]
    hloc = [_dot_tn(jnp.concatenate([v, -w[:, LANES:2 * LANES]], axis=0),
                    jnp.concatenate([o["kt"].astype(BF16), b], axis=0))
            for v, o, w, b in zip(vsb, ops, wu, btb)]
    ys, hts_new = [], []
    for o, r, yl, ht, bw, hl in zip(ops, r2, arkv, hts, btw, hloc):
        q_s = o["rg"] - r[:, 0:LANES]
        yloc_s = yl - r[:, LANES:2 * LANES]
        ys.append(yloc_s[0:C] + yloc_s[C:n] + _dot_nt(q_s[0:C] + q_s[C:n], ht))
        hts_new.append(ht * o["g_end"] - _dot_nt(ht, bw) + hl)
    return ys, hts_new


def _rwkv_scan_kernel(rf, kkf, vf, lwf, kf, bf, rb, kkb, vb, lwb, kb, bb, yf_out, yb_out, h_scr):
    @pl.when(pl.program_id(1) == 0)
    def _():
        h_scr[...] = jnp.zeros_like(h_scr)

    n_pairs = A_WIDTH // LANES
    sls = [slice(i * LANES, (i + 1) * LANES) for i in range(n_pairs)]
    f32 = lambda a: a.astype(F32)
    hts = [h_scr[d, i] for d in range(2) for i in range(n_pairs)]
    for j in range(SCAN_SUB):
        ops = []
        for d, (r_ref, kk_ref, v_ref, lw_ref, k_ref, b_ref) in enumerate(
                ((rf, kkf, vf, lwf, kf, bf), (rb, kkb, vb, lwb, kb, bb))):
            reverse = d == 1
            rows = pl.ds((SCAN_SUB - 1 - j if reverse else j) * CHUNK, CHUNK)
            lw_all = lw_ref[0, 0, rows, :]
            cs_all = _dot_hi(_cumsum_matrix(reverse), lw_all)
            ops += [_rwkv_chunk_operands(f32(r_ref[0, rows, sl]), f32(kk_ref[0, rows, sl]),
                                         f32(v_ref[0, rows, sl]), f32(k_ref[0, 0, rows, sl]),
                                         f32(b_ref[0, 0, rows, sl]), cs_all[:, sl], lw_all[:, sl], reverse)
                    for sl in sls]
        ys, hts = _rwkv_chunks(ops, hts)
        for d, y_out in enumerate((yf_out, yb_out)):
            rows = pl.ds((SCAN_SUB - 1 - j if d == 1 else j) * CHUNK, CHUNK)
            for i, sl in enumerate(sls):
                y_out[0, rows, sl] = ys[d * n_pairs + i].astype(y_out.dtype)
    for d in range(2):
        for i in range(n_pairs):
            h_scr[d, i] = hts[d * n_pairs + i]


def _rwkv_scan(r, kk, v, lw, kd, bd):
    B, T, W = r.shape
    rows = CHUNK * SCAN_SUB
    N = T // rows
    fwd = pl.BlockSpec((1, rows, W), lambda b, c: (b, c, 0))
    bwd = pl.BlockSpec((1, rows, W), lambda b, c: (b, N - 1 - c, 0))
    fwd2 = pl.BlockSpec((1, 1, rows, W), lambda b, c: (0, b, c, 0))
    bwd2 = pl.BlockSpec((1, 1, rows, W), lambda b, c: (1, b, N - 1 - c, 0))
    out = jax.ShapeDtypeStruct((B, T, W), ACT)
    return pl.pallas_call(
        _rwkv_scan_kernel,
        grid=(B, N),
        in_specs=[fwd, fwd, fwd, fwd2, fwd2, fwd2, bwd, bwd, bwd, bwd2, bwd2, bwd2],
        out_specs=[fwd, bwd],
        out_shape=[out, out],
        scratch_shapes=[pltpu.VMEM((2, W // LANES, LANES, LANES), F32)],
        compiler_params=_params(("parallel", "arbitrary")),
        name="rwkv_scan",
    )(r, kk, v, lw, kd, bd, r, kk, v, lw, kd, bd)


def _gdn_prep_kernel(x_ref, xp_ref, xn_ref, g1_ref, sc_ref, sh_ref, w_ref, cw_ref, alog_ref, dtb_ref,
                     q_out, k_out, v_out, bg_out, acc_scr):
    tt = x_ref.shape[1]
    i = pl.program_id(1)
    last = pl.num_programs(1) - 1
    half = B_CONV // 2
    Wq = 3 * B_WIDTH
    h = _normed_with_halo(x_ref, xp_ref, xn_ref, g1_ref, sc_ref, sh_ref)
    project = lambda c0, c1: jnp.dot(h, w_ref[:, c0:c1], preferred_element_type=F32)
    offsets = [j - half for j in range(B_CONV) if j != half]
    row = lax.broadcasted_iota(jnp.int32, (len(offsets) * tt, tt), 0)
    col = lax.broadcasted_iota(jnp.int32, (len(offsets) * tt, tt), 1)
    blk = row // tt
    off = jnp.where(blk < half, blk - half, blk - half + 1)
    shift_rows = (col == row - blk * tt + off).astype(BF16)
    r8 = lax.broadcasted_iota(jnp.int32, (SUBLANES, 1), 0)

    def conv_group(z_ext, c0, c1):
        zb = z_ext[SUBLANES:SUBLANES + tt].astype(BF16)
        shifted = jnp.dot(shift_rows, zb, preferred_element_type=F32)
        acc = zb.astype(F32) * cw_ref[half:half + 1, c0:c1]
        for m, o in enumerate(offsets):
            acc = acc + shifted[m * tt:(m + 1) * tt] * cw_ref[o + half:o + half + 1, c0:c1]
        acc_scr[:, c0:c1] = acc
        prev8 = jnp.where(i == 0, 0.0, z_ext[0:SUBLANES]).astype(BF16).astype(F32)
        next8 = jnp.where(i == last, 0.0, z_ext[SUBLANES + tt:]).astype(BF16).astype(F32)
        top = bot = None
        for o in offsets:
            w = cw_ref[o + half:o + half + 1, c0:c1]
            if o < 0:
                f = jnp.where(r8 < -o, pltpu.roll(prev8, -o, axis=0), 0.0) * w
                top = f if top is None else top + f
            else:
                f = jnp.where(r8 >= SUBLANES - o, pltpu.roll(next8, SUBLANES - o, axis=0), 0.0) * w
                bot = f if bot is None else bot + f
        acc_scr[pl.ds(0, SUBLANES), c0:c1] += top
        acc_scr[pl.ds(tt - SUBLANES, SUBLANES), c0:c1] += bot
        return _silu(acc_scr[:, c0:c1])

    bounds = list(range(0, Wq + 1, GDN_COL_GROUP))
    z_next = project(bounds[0], bounds[1])
    for gi in range(len(bounds) - 1):
        c0, c1 = bounds[gi], bounds[gi + 1]
        z_cur = z_next
        z_next = project(bounds[gi + 1], bounds[gi + 2]) if gi + 2 < len(bounds) else project(Wq, Wq + SMALL_COLS)
        act = conv_group(z_cur, c0, c1)
        for hh in range(GDN_COL_GROUP // B_HEAD_DIM):
            x = act[:, hh * B_HEAD_DIM:(hh + 1) * B_HEAD_DIM]
            lo = c0 % B_WIDTH + hh * B_HEAD_DIM
            if c0 < 2 * B_WIDTH:
                x = x * lax.rsqrt(jnp.sum(x * x, axis=-1, keepdims=True) + 1e-6)
                if c0 < B_WIDTH:
                    x = x * (B_HEAD_DIM ** -0.5)
            out = (q_out, k_out, v_out)[c0 // B_WIDTH]
            out[0, :, lo:lo + B_HEAD_DIM] = x.astype(out.dtype)
    s = z_next[SUBLANES:SUBLANES + tt]
    lane = lax.broadcasted_iota(jnp.int32, s.shape, 1)
    beta = _sigmoid(s)
    g = -jnp.exp(alog_ref[...]) * _softplus(s + dtb_ref[...])
    bg_out[0] = jnp.where(lane < 2 * B_HEADS, beta, g)


def _gdn_prep(x, g1, sc, sh, w_b, conv_w, alog_vec, dtb_vec):
    B, T, _ = x.shape
    Wq = 3 * B_WIDTH
    tt = min(256, T)
    full = lambda shape: pl.BlockSpec(shape, lambda b, i: (0,) * len(shape))
    tok = pl.BlockSpec((1, tt, B_WIDTH), lambda b, i: (b, i, 0))
    s1 = jax.ShapeDtypeStruct((B, T, B_WIDTH), ACT)
    return pl.pallas_call(
        _gdn_prep_kernel,
        grid=(B, T // tt),
        in_specs=_front_specs(tt, T, Wq + SMALL_COLS) + [full((B_CONV, Wq)), full((1, SMALL_COLS)),
                                                          full((1, SMALL_COLS))],
        out_specs=[tok, tok, tok, pl.BlockSpec((1, tt, SMALL_COLS), lambda b, i: (b, i, 0))],
        out_shape=[s1, s1, s1, jax.ShapeDtypeStruct((B, T, SMALL_COLS), F32)],
        scratch_shapes=[pltpu.VMEM((tt, Wq), F32)],
        compiler_params=_params(("parallel", "parallel")),
        name="gdn_prep",
    )(x, x, x, g1, sc, sh, w_b, conv_w, alog_vec, dtb_vec)


def _gdn_chunk_operands(q0, q1, k0, k1, v0, v1, gc2, gct2, bg2, j0, j1, reverse):
    C = CHUNK
    n = 2 * C
    strict, incl, _ = _chunk_masks(reverse)
    top = lax.broadcasted_iota(jnp.int32, (n, 1), 0) < C
    left = lax.broadcasted_iota(jnp.int32, (1, n), 1) < C
    g0, g1 = j0 + 2 * B_HEADS, j1 + 2 * B_HEADS
    gcol = jnp.where(top, gc2[:, g0:g0 + 1], gc2[:, g1:g1 + 1])
    grow = jnp.where(left, gct2[g0:g0 + 1, :], gct2[g1:g1 + 1, :])
    beta = jnp.where(top, bg2[:, j0:j0 + 1], bg2[:, j1:j1 + 1])
    e = 0 if reverse else C - 1
    glast = jnp.where(top, gc2[e:e + 1, g0:g0 + 1], gc2[e:e + 1, g1:g1 + 1])
    ks = jnp.concatenate([k0, k1], axis=0)
    qs = jnp.concatenate([q0, q1], axis=0)
    vs = jnp.concatenate([v0, v1], axis=0)
    egc = jnp.exp(gcol)
    kb = ks * beta
    return dict(gam=jnp.exp(jnp.where(incl, gcol - grow, NEG_INF)), strict=strict, ks=ks,
                kbq=jnp.concatenate([kb, qs], axis=0),
                rhs=jnp.concatenate([vs * beta, kb * egc], axis=1),
                qh=(qs * egc).astype(BF16), ktail=(ks * jnp.exp(glast - gcol)).astype(BF16),
                decay=jnp.exp(glast))


def _gdn_chunks(ops, states):
    C = CHUNK
    n = 2 * C
    eye = _chunk_masks(False)[2].astype(F32)
    ps = [_dot_nt(o["kbq"], o["ks"]) for o in ops]
    lower = [jnp.where(o["strict"], p[0:n] * o["gam"], 0.0) for o, p in zip(ops, ps)]
    aqk = [(p[n:2 * n] * o["gam"]).astype(BF16) for o, p in zip(ops, ps)]
    tinv = [t.astype(BF16) for t in _tri_inverse(lower, eye)]
    uw = [jnp.dot(t, o["rhs"].astype(BF16), preferred_element_type=F32) for t, o in zip(tinv, ops)]
    lx = []
    for l, x in zip(lower, uw):
        lh, ll = _split(l)
        xh, xl = _split(x)
        first = jnp.dot(lh, jnp.concatenate([xh, xl], axis=1), preferred_element_type=F32)
        w2 = x.shape[1]
        lx.append(first[:, 0:w2] + (first[:, w2:2 * w2] + jnp.dot(ll, xh, preferred_element_type=F32)))
    resid = [o["rhs"] - x - y for o, x, y in zip(ops, uw, lx)]
    uw = [x + jnp.dot(t, r.astype(BF16), preferred_element_type=F32) for x, t, r in zip(uw, tinv, resid)]
    sb = [(s0.astype(BF16), s1.astype(BF16)) for s0, s1 in states]
    wq = [[jnp.dot(jnp.concatenate([u[h * C:(h + 1) * C, LANES:].astype(BF16), o["qh"][h * C:(h + 1) * C]],
                                   axis=0), s[h], preferred_element_type=F32) for h in range(2)]
          for u, o, s in zip(uw, ops, sb)]
    ws = [jnp.concatenate([x[0][0:C], x[1][0:C]], axis=0) for x in wq]
    qss = [jnp.concatenate([x[0][C:n], x[1][C:n]], axis=0) for x in wq]
    v_new = [(u[:, 0:LANES] - w).astype(BF16) for u, w in zip(uw, ws)]
    outs = [q + jnp.dot(a, v, preferred_element_type=F32) for q, a, v in zip(qss, aqk, v_new)]
    new_states = []
    for o, v, (s0, s1) in zip(ops, v_new, states):
        new_states.append((s0 * o["decay"][0:1, :] + _dot_tn(o["ktail"][0:C], v[0:C]),
                           s1 * o["decay"][C:C + 1, :] + _dot_tn(o["ktail"][C:n], v[C:n])))
    return [(x[0:C], x[C:n]) for x in outs], new_states


def _gdn_scan_kernel(qf, kf, vf, bgf, qb, kb, vb, bgb, of_out, ob_out, s_scr):
    @pl.when(pl.program_id(1) == 0)
    def _():
        s_scr[...] = jnp.zeros_like(s_scr)

    D = B_HEAD_DIM
    n_pairs = B_HEADS // 2
    sl = lambda h: slice(h * D, (h + 1) * D)
    f32 = lambda a: a.astype(F32)
    states = [(s_scr[d, 2 * i], s_scr[d, 2 * i + 1]) for d in range(2) for i in range(n_pairs)]
    for j in range(SCAN_SUB):
        ops = []
        for d, (q_ref, k_ref, v_ref, bg_ref) in enumerate(((qf, kf, vf, bgf), (qb, kb, vb, bgb))):
            reverse = d == 1
            rows = pl.ds((SCAN_SUB - 1 - j if reverse else j) * CHUNK, CHUNK)
            bg = bg_ref[0, rows, :]
            gc = _dot_hi(_cumsum_matrix(reverse), bg)
            bg2 = jnp.concatenate([bg, bg], axis=0)
            gc2 = jnp.concatenate([gc, gc], axis=0)
            gct2 = gc2.T
            for i in range(n_pairs):
                h0, h1 = 2 * i, 2 * i + 1
                ops.append(_gdn_chunk_operands(
                    f32(q_ref[0, rows, sl(h0)]), f32(q_ref[0, rows, sl(h1)]), f32(k_ref[0, rows, sl(h0)]),
                    f32(k_ref[0, rows, sl(h1)]), f32(v_ref[0, rows, sl(h0)]), f32(v_ref[0, rows, sl(h1)]),
                    gc2, gct2, bg2, d * B_HEADS + h0, d * B_HEADS + h1, reverse))
        outs, states = _gdn_chunks(ops, states)
        for d, o_out in enumerate((of_out, ob_out)):
            rows = pl.ds((SCAN_SUB - 1 - j if d == 1 else j) * CHUNK, CHUNK)
            for i in range(n_pairs):
                o0, o1 = outs[d * n_pairs + i]
                o_out[0, rows, sl(2 * i)] = o0.astype(o_out.dtype)
                o_out[0, rows, sl(2 * i + 1)] = o1.astype(o_out.dtype)
    for d in range(2):
        for i in range(n_pairs):
            s_scr[d, 2 * i], s_scr[d, 2 * i + 1] = states[d * n_pairs + i]


def _gdn_scan(q, k, v, bg):
    B, T, W = q.shape
    rows = CHUNK * SCAN_SUB
    N = T // rows
    fwd = pl.BlockSpec((1, rows, W), lambda b, c: (b, c, 0))
    bwd = pl.BlockSpec((1, rows, W), lambda b, c: (b, N - 1 - c, 0))
    sfwd = pl.BlockSpec((1, rows, SMALL_COLS), lambda b, c: (b, c, 0))
    sbwd = pl.BlockSpec((1, rows, SMALL_COLS), lambda b, c: (b, N - 1 - c, 0))
    out = jax.ShapeDtypeStruct((B, T, W), ACT)
    return pl.pallas_call(
        _gdn_scan_kernel,
        grid=(B, N),
        in_specs=[fwd, fwd, fwd, sfwd, bwd, bwd, bwd, sbwd],
        out_specs=[fwd, bwd],
        out_shape=[out, out],
        scratch_shapes=[pltpu.VMEM((2, B_HEADS, B_HEAD_DIM, B_HEAD_DIM), F32)],
        compiler_params=_params(("parallel", "arbitrary")),
        name="gdn_scan",
    )(q, k, v, bg, q, k, v, bg)


def _pack_bf16_pairs(x):
    n = x.shape[1] // 2
    hi = lax.bitcast_convert_type(x[:, :n].astype(BF16).astype(F32), jnp.int32)
    lo = lax.bitcast_convert_type(x[:, n:].astype(BF16).astype(F32), jnp.int32)
    return hi | lax.shift_right_logical(lo, 16)


def _unpack_bf16_pairs(p):
    a = lax.bitcast_convert_type(p & jnp.int32(-65536), F32)
    b = lax.bitcast_convert_type(lax.shift_left(p, 16), F32)
    return a.astype(BF16), b.astype(BF16)


def _route(scores, biased, base):
    tt = scores.shape[-1]
    shape3 = (N_GROUPS, GROUP_SIZE, tt)
    s3 = scores.reshape(shape3)
    b3 = biased.reshape(shape3)
    jid = lax.broadcasted_iota(jnp.int32, shape3, 1).astype(F32)
    gid = lax.broadcasted_iota(jnp.int32, shape3, 0).astype(F32)
    m1 = jnp.max(b3, axis=1, keepdims=True)
    first = jnp.min(jnp.where(b3 == m1, jid, float(GROUP_SIZE)), axis=1, keepdims=True)
    m2 = jnp.max(jnp.where(jid == first, NEG_INF, b3), axis=1, keepdims=True)
    gs = m1 + m2
    grp = lax.broadcasted_iota(jnp.int32, (N_GROUPS, 1, tt), 0).astype(F32)
    keep = jnp.zeros((N_GROUPS, 1, tt), F32)
    for _ in range(TOPK_GROUPS):
        m = jnp.max(gs, axis=0, keepdims=True)
        pick = grp == jnp.min(jnp.where(gs == m, grp, float(N_GROUPS)), axis=0, keepdims=True)
        keep = jnp.where(pick, 1.0, keep)
        gs = jnp.where(pick, NEG_INF, gs)
    work = jnp.where(keep > 0.0, b3, NEG_INF)
    eid = gid * float(GROUP_SIZE) + jid
    chosen = jnp.zeros(shape3, F32)
    sum01 = lambda a: jnp.sum(jnp.sum(a, axis=1, keepdims=True), axis=0, keepdims=True)
    ids, raw = [], []
    for _ in range(TOP_K):
        m = jnp.max(jnp.max(work, axis=1, keepdims=True), axis=0, keepdims=True)
        cand = jnp.where(work == m, eid, float(N_EXPERTS))
        first = jnp.min(jnp.min(cand, axis=1, keepdims=True), axis=0, keepdims=True)
        pick = eid == first
        ids.append(first)
        raw.append(sum01(jnp.where(pick, s3, 0.0)))
        chosen = jnp.where(pick, 1.0, chosen)
        work = jnp.where(pick, NEG_INF, work)
    total = raw[0]
    for r in raw[1:]:
        total = total + r
    flat = chosen.reshape(N_EXPERTS, tt)
    earlier = (lax.broadcasted_iota(jnp.int32, (tt, tt), 0)
               < lax.broadcasted_iota(jnp.int32, (tt, tt), 1)).astype(BF16)
    prefix = (jnp.dot(flat.astype(BF16), earlier, preferred_element_type=F32) + base).reshape(shape3)
    row = lax.broadcasted_iota(jnp.int32, (SUBLANES, tt), 0)
    idx8 = jnp.zeros((SUBLANES, tt), F32)
    w8 = jnp.zeros((SUBLANES, tt), F32)
    pos8 = jnp.zeros((SUBLANES, tt), F32)
    for k in range(TOP_K):
        rank = sum01(jnp.where(eid == ids[k], prefix, 0.0))
        idx8 = jnp.where(row == k, ids[k].reshape(1, tt), idx8)
        w8 = jnp.where(row == k, (raw[k] / total * ROUTED_SCALE).reshape(1, tt), w8)
        pos8 = jnp.where(row == k, rank.reshape(1, tt), pos8)
    counts = jnp.sum(flat, axis=1, keepdims=True)
    return idx8.astype(jnp.int32), w8, pos8.astype(jnp.int32), counts


def _mixer_post_kernel(yf, yb, bonus, g, of, ob, zz, zga, zgb, x_ref, gt1, sc2, sh2,
                       gnw, gnb, ones_ref, aproj, bng, bproj, wout, n2g, rwt, rbias,
                       x1_out, h2_out, idx_out, w_out_ref, pos_out, counts_out):
    @pl.when((pl.program_id(0) == 0) & (pl.program_id(1) == 0))
    def _():
        counts_out[...] = jnp.zeros_like(counts_out)

    ones_bd = ones_ref[...]
    f32 = lambda ref: ref[0].astype(F32)
    y = f32(yf) + f32(yb)
    mean = _seg_sum(y, ones_bd) * (1.0 / A_HEAD_DIM)
    dlt = y - mean
    var = _seg_sum(dlt * dlt, ones_bd) * (1.0 / A_HEAD_DIM)
    yn = dlt * lax.rsqrt(var + A_GN_EPS) * gnw[...] + gnb[...]
    y_a = _dot((yn + f32(bonus)) * f32(g), aproj[...])

    o = f32(of) + f32(ob)
    z = f32(zz)
    parts = []
    for h in range(B_HEADS):
        oh = o[:, h * B_HEAD_DIM:(h + 1) * B_HEAD_DIM]
        parts.append(oh * lax.rsqrt(jnp.mean(oh * oh, axis=-1, keepdims=True) + EPS))
    on = jnp.concatenate(parts, axis=1) * bng[...] * _silu(z)
    y_b = _dot(on, bproj[...])

    u = _sigmoid(f32(zga)) * y_a + _sigmoid(f32(zgb)) * y_b
    x1 = x_ref[0] + gt1[0] * _dot(u, wout[...])
    x1_out[0] = x1
    h2 = _rms_mod(x1, n2g[...], sc2[0], sh2[0])
    h2_out[0] = _pack_bf16_pairs(h2)
    logits_t = _dot_nt(rwt[...], h2)
    scores = _sigmoid(logits_t)
    base = counts_out[:, 0:1]
    idx8, w8, pos8, counts = _route(scores, scores + rbias[...], base)
    idx_out[0] = idx8
    w_out_ref[0] = w8
    pos_out[0] = pos8
    counts_out[...] = counts_out[...] + counts


def _mixer_post(yf, yb, bonus, g, of, ob, z_c, x, gt1, sc2, sh2, gn_w, gn_b, ones_bd, a_proj, bng,
                b_proj, w_out, n2g, rwt, rbias):
    B, T, D = x.shape
    tt = min(256, T)
    full = lambda shape: pl.BlockSpec(shape, lambda b, i: (0,) * len(shape))
    tokw = lambda w, col=0: pl.BlockSpec((1, tt, w), lambda b, i: (b, i, col))
    modrow = pl.BlockSpec((1, 1, D), lambda b, i: (b, 0, 0))
    pick = pl.BlockSpec((1, SUBLANES, tt), lambda b, i: (b, 0, i))
    W = A_WIDTH
    return pl.pallas_call(
        _mixer_post_kernel,
        grid=(B, T // tt),
        in_specs=[tokw(W), tokw(W), tokw(W), tokw(W), tokw(D), tokw(D),
                  tokw(D, 0), tokw(D, 1), tokw(D, 2), tokw(D), modrow, modrow, modrow,
                  full((1, W)), full((1, W)), full((W, W)), full((W, D)), full((1, D)), full((D, D)),
                  full((D, D)), full((1, D)), full((N_EXPERTS, D)), full((N_EXPERTS, 1))],
        out_specs=[tokw(D), tokw(D // 2), pick, pick, pick,
                   pl.BlockSpec((N_EXPERTS, LANES), lambda b, i: (0, 0))],
        out_shape=[jax.ShapeDtypeStruct((B, T, D), F32), jax.ShapeDtypeStruct((B, T, D // 2), jnp.int32),
                   jax.ShapeDtypeStruct((B, SUBLANES, T), jnp.int32),
                   jax.ShapeDtypeStruct((B, SUBLANES, T), F32),
                   jax.ShapeDtypeStruct((B, SUBLANES, T), jnp.int32),
                   jax.ShapeDtypeStruct((N_EXPERTS, LANES), F32)],
        compiler_params=_params(("arbitrary", "arbitrary")),
        name="mixer_post",
    )(yf, yb, bonus, g, of, ob, z_c, z_c, z_c, x, gt1, sc2, sh2, gn_w, gn_b, ones_bd, a_proj, bng,
      b_proj, w_out, n2g, rwt, rbias)


def _gather_rows(table, idx):
    n_rows, width = idx.shape[0], table.shape[1]
    per_worker = n_rows // SC_WORKERS
    n_windows = per_worker // GATHER_WINDOW
    mesh = plsc.VectorSubcoreMesh(core_axis_name="c", subcore_axis_name="s")

    @functools.partial(
        pl.kernel, mesh=mesh, out_type=jax.ShapeDtypeStruct((n_rows, width), table.dtype),
        scratch_types=[pltpu.VMEM((GATHER_WINDOW,), jnp.int32),
                       pltpu.VMEM((GATHER_WINDOW, width), table.dtype),
                       pltpu.SemaphoreType.DMA])
    def gather(table_hbm, idx_hbm, out_hbm, idx_v, rows_v, sem):
        worker = lax.axis_index("s") * SC_CORES + lax.axis_index("c")

        @pl.loop(0, n_windows)
        def _(j):
            off = pl.multiple_of(worker * per_worker + j * GATHER_WINDOW, GATHER_WINDOW)
            pltpu.sync_copy(idx_hbm.at[pl.ds(off, GATHER_WINDOW)], idx_v)
            pltpu.async_copy(table_hbm.at[idx_v], rows_v, sem).wait()
            pltpu.sync_copy(rows_v, out_hbm.at[pl.ds(off, GATHER_WINDOW)])

    return gather(table, idx)


def _scatter_rows(rows, dest_w, n_out):
    n_tok, width = rows.shape
    per_worker = n_tok // GATHER_WINDOW // SC_WORKERS
    mesh = plsc.VectorSubcoreMesh(core_axis_name="c", subcore_axis_name="s")

    @functools.partial(
        pl.kernel, mesh=mesh, out_type=jax.ShapeDtypeStruct((n_out, width), rows.dtype),
        scratch_types=[pltpu.VMEM((SUBLANES, GATHER_WINDOW), jnp.int32),
                       pltpu.VMEM((GATHER_WINDOW, width), rows.dtype)])
    def scatter(rows_hbm, dest_hbm, out_hbm, idx_v, rows_v):
        worker = lax.axis_index("s") * SC_CORES + lax.axis_index("c")

        @pl.loop(0, per_worker)
        def _(j):
            win = worker * per_worker + j
            pltpu.sync_copy(dest_hbm.at[win], idx_v)
            pltpu.sync_copy(rows_hbm.at[pl.ds(pl.multiple_of(win * GATHER_WINDOW, GATHER_WINDOW),
                                              GATHER_WINDOW)], rows_v)
            for k in range(TOP_K):
                pltpu.sync_copy(rows_v, out_hbm.at[idx_v.at[k]])

    return scatter(rows, dest_w)


def _dest_kernel(start_ref, idx_ref, pos_ref, dest_ref):
    idx = idx_ref[0]
    dest = jnp.zeros_like(idx)
    for e in range(N_EXPERTS):
        dest = jnp.where(idx == e, start_ref[e], dest)
    dest_ref[0] = dest + pos_ref[0]


def _dest_rows(pad_start, idx, pos):
    B, _, T = idx.shape
    tt = min(2048, T)
    spec = pl.BlockSpec((1, SUBLANES, tt), lambda b, i, start: (b, 0, i))
    return pl.pallas_call(
        _dest_kernel,
        grid_spec=pltpu.PrefetchScalarGridSpec(num_scalar_prefetch=1, grid=(B, T // tt),
                                               in_specs=[spec, spec], out_specs=spec),
        out_shape=jax.ShapeDtypeStruct(idx.shape, jnp.int32),
        compiler_params=_params(("parallel", "parallel")),
        name="dest_rows",
    )(pad_start, idx, pos)


def _expert_kernel(be_ref, valid_ref, xs_ref, wg, wu, wd, ys_ref):
    n_valid = valid_ref[pl.program_id(0)]

    @pl.when(n_valid > 0)
    def _():
        half = D_MODEL // 2
        row = lax.broadcasted_iota(jnp.int32, xs_ref.shape, 0)
        a, b = _unpack_bf16_pairs(jnp.where(row < n_valid, xs_ref[...], 0))
        mm = lambda w: (jnp.dot(a, w[0, 0:half, :], preferred_element_type=F32)
                        + jnp.dot(b, w[0, half:D_MODEL, :], preferred_element_type=F32))
        hid = _silu(mm(wg)) * mm(wu)
        ys_ref[...] = _pack_bf16_pairs(_dot(hid, wd[0]))


def _experts(block_expert, block_valid, xs, wg, wu, wd):
    n_rows, half = xs.shape
    rows = pl.BlockSpec((MOE_BLOCK, half), lambda i, be, valid: (i, 0))
    wspec = lambda shape: pl.BlockSpec(shape, lambda i, be, valid: (be[i], 0, 0))
    return pl.pallas_call(
        _expert_kernel,
        grid_spec=pltpu.PrefetchScalarGridSpec(
            num_scalar_prefetch=2, grid=(n_rows // MOE_BLOCK,),
            in_specs=[rows, wspec((1, D_MODEL, D_EXPERT)), wspec((1, D_MODEL, D_EXPERT)),
                      wspec((1, D_EXPERT, D_MODEL))],
            out_specs=rows),
        out_shape=jax.ShapeDtypeStruct((n_rows, half), jnp.int32),
        compiler_params=_params(("arbitrary",)),
        name="experts",
    )(block_expert, block_valid, xs, wg, wu, wd)


def _moe_final_kernel(yg_ref, w_ref, h_ref, x1_ref, gt2, fg, sg, su, sd, o_ref):
    half = D_MODEL // 2
    a, b = _unpack_bf16_pairs(h_ref[0])
    mm = lambda w: (jnp.dot(a, w[0:half, :], preferred_element_type=F32)
                    + jnp.dot(b, w[half:D_MODEL, :], preferred_element_type=F32))
    shared = _dot(_silu(mm(sg)) * mm(su), sd[...])
    w = w_ref[0]
    lo = hi = None
    for k in range(TOP_K):
        ya, yb = _unpack_bf16_pairs(yg_ref[k, 0])
        wk = w[:, k:k + 1]
        lo = ya.astype(F32) * wk if lo is None else lo + ya.astype(F32) * wk
        hi = yb.astype(F32) * wk if hi is None else hi + yb.astype(F32) * wk
    x2 = x1_ref[0] + gt2[0] * (jnp.concatenate([lo, hi], axis=1) + shared)
    o_ref[0] = x2 * lax.rsqrt(jnp.mean(x2 * x2, axis=-1, keepdims=True) + EPS) * fg[...]


def _moe_final(yg, w_t, h2p, x1, gt2, fg, sg, su, sd):
    B, T, D = x1.shape
    tm = min(512, T)
    tok = lambda w: pl.BlockSpec((1, tm, w), lambda b, i: (b, i, 0))
    full = lambda shape: pl.BlockSpec(shape, lambda b, i: (0,) * len(shape))
    return pl.pallas_call(
        _moe_final_kernel,
        grid=(B, T // tm),
        in_specs=[pl.BlockSpec((TOP_K, 1, tm, D // 2), lambda b, i: (0, b, i, 0)), tok(SUBLANES),
                  tok(D // 2), tok(D), pl.BlockSpec((1, 1, D), lambda b, i: (b, 0, 0)), full((1, D)),
                  full((D, D_EXPERT)), full((D, D_EXPERT)), full((D_EXPERT, D))],
        out_specs=tok(D),
        out_shape=jax.ShapeDtypeStruct((B, T, D), F32),
        compiler_params=_params(("parallel", "parallel")),
        name="moe_final",
    )(yg, w_t, h2p, x1, gt2, fg, sg, su, sd)


def _moe_routed(h2p, idx, w, pos, counts, x1, gt2, p):
    B, T, half = h2p.shape
    n_slots = B * T * TOP_K
    n_rows = n_slots + N_EXPERTS * MOE_BLOCK
    cnt = counts[:, 0].astype(jnp.int32)
    padded = (cnt + MOE_BLOCK - 1) // MOE_BLOCK * MOE_BLOCK
    pad_end = jnp.cumsum(padded)
    pad_start = pad_end - padded
    dest = _dest_rows(pad_start, idx, pos)
    starts = jnp.arange(n_rows // MOE_BLOCK, dtype=jnp.int32) * MOE_BLOCK
    block_expert = jnp.minimum(jnp.sum((pad_end[None, :] <= starts[:, None]).astype(jnp.int32), axis=1),
                               N_EXPERTS - 1)
    block_valid = jnp.clip((pad_start + cnt)[block_expert] - starts, 0, MOE_BLOCK)
    dest_w = jnp.swapaxes(dest.reshape(B, SUBLANES, T // GATHER_WINDOW, GATHER_WINDOW), 1, 2)
    xs = _scatter_rows(h2p.reshape(B * T, half), dest_w.reshape(-1, SUBLANES, GATHER_WINDOW), n_rows)
    ys = _experts(block_expert, block_valid, xs, p["wg"], p["wu"], p["wd"])
    yg = _gather_rows(ys, jnp.swapaxes(dest[:, :TOP_K], 0, 1).reshape(-1)).reshape(TOP_K, B, T, half)
    return _moe_final(yg, jnp.swapaxes(w, 1, 2), h2p, x1, gt2, p["fg"], p["sg"], p["su"], p["sd"])


def _prepare_weights(w_in, shift_mu, a_w0, a_w_up, a_a0, a_a_up, a_g_up, a_k_k, a_k_a, a_r_k, a_gn_w,
                     a_gn_b, a_proj, b_conv_w, b_a_log, b_dt_bias, b_norm_g, b_proj, w_out, router_w,
                     router_bias, exp_gate, exp_up, exp_down, sh_gate, sh_up, sh_down, norm1_g, norm2_g,
                     final_g):
    W = A_WIDTH
    w = w_in[0]
    c0 = A_COLS
    c1 = c0 + 3 * B_WIDTH
    c2 = c1 + B_WIDTH
    c3 = c2 + 4 * B_HEADS
    pad = jnp.zeros((D_MODEL, SMALL_COLS - 4 * B_HEADS), F32)
    w_a = w[:, :c0].astype(BF16)
    w_b = jnp.concatenate([w[:, c0:c1], w[:, c2:c3], pad], axis=1).astype(BF16)
    w_c = jnp.concatenate([w[:, c1:c2], w[:, c3:]], axis=1).astype(BF16)
    zeros = jnp.zeros((2, A_RANK_W, W), F32)
    up_comb = jnp.concatenate([jnp.concatenate([a_w_up[0], zeros], axis=2),
                               jnp.concatenate([zeros, a_a_up[0]], axis=2)], axis=1)
    head = jnp.arange(W) // A_HEAD_DIM
    ones_bd = (head[:, None] == head[None, :]).astype(BF16)
    small = lambda v: jnp.zeros((1, SMALL_COLS), F32).at[0, 2 * B_HEADS:4 * B_HEADS].set(v.reshape(-1))
    return dict(
        w_a=w_a, w_b=w_b, w_c=w_c, mu=shift_mu[0].reshape(1, A_COLS), w0=a_w0[0], a0=a_a0[0],
        up_comb=up_comb, g_up=a_g_up[0], k_k=a_k_k[0].reshape(1, W), k_a=a_k_a[0].reshape(1, W),
        r_k=a_r_k[0].reshape(1, W), ones_bd=ones_bd, gn_w=a_gn_w[0].reshape(1, W),
        gn_b=a_gn_b[0].reshape(1, W), a_proj=a_proj[0].astype(BF16),
        conv_w=b_conv_w[0].astype(BF16).astype(F32),
        alog=small(b_a_log[0]), dtb=small(b_dt_bias[0]),
        bng=jnp.tile(b_norm_g[0], B_HEADS).reshape(1, B_WIDTH), b_proj=b_proj[0].astype(BF16),
        w_out=w_out[0].astype(BF16), rwt=router_w[0].T, rbias=router_bias[0].reshape(N_EXPERTS, 1),
        wg=exp_gate[0].astype(BF16), wu=exp_up[0].astype(BF16), wd=exp_down[0].astype(BF16),
        sg=sh_gate[0].astype(BF16), su=sh_up[0].astype(BF16), sd=sh_down[0].astype(BF16),
        n1g=norm1_g[0].reshape(1, D_MODEL), n2g=norm2_g[0].reshape(1, D_MODEL),
        fg=final_g.reshape(1, D_MODEL))


def _layer(x, mod, p):
    B = x.shape[0]
    sh1, sc1, gt1, sh2, sc2, gt2 = (m.reshape(B, 1, D_MODEL) for m in jnp.split(mod, 6, axis=-1))
    z_c = _inproj(x, p["n1g"], sc1, sh1, p["w_c"])
    r, kk, v, g, bonus, lw, kd, bd = _rwkv_prep(x, p["n1g"], sc1, sh1, p["w_a"], p["mu"], p["w0"], p["a0"],
                                                p["up_comb"], p["g_up"], p["k_k"], p["k_a"], p["r_k"],
                                                p["ones_bd"])
    yf, yb = _rwkv_scan(r, kk, v, lw, kd, bd)
    q, k, vv, bg = _gdn_prep(x, p["n1g"], sc1, sh1, p["w_b"], p["conv_w"], p["alog"], p["dtb"])
    of, ob = _gdn_scan(q, k, vv, bg)
    x1, h2p, idx, w, pos, counts = _mixer_post(yf, yb, bonus, g, of, ob, z_c, x, gt1, sc2, sh2, p["gn_w"],
                                               p["gn_b"], p["ones_bd"], p["a_proj"], p["bng"], p["b_proj"],
                                               p["w_out"], p["n2g"], p["rwt"], p["rbias"])
    return _moe_routed(h2p, idx, w, pos, counts, x1, gt2, p)


def kernel(x_prompt, x_sample, c_prompt, c_sample, ada_w, ada_b, norm1_g, norm2_g, w_in, shift_mu, a_w0, a_w_up, a_a0, a_a_up, a_g_up, a_k_k, a_k_a, a_r_k, a_gn_w, a_gn_b, a_proj, b_conv_w, b_a_log, b_dt_bias, b_norm_g, b_proj, w_out, router_w, router_bias, exp_gate, exp_up, exp_down, sh_gate, sh_up, sh_down, final_g):
    p = _prepare_weights(w_in, shift_mu, a_w0, a_w_up, a_a0, a_a_up, a_g_up, a_k_k, a_k_a, a_r_k, a_gn_w,
                         a_gn_b, a_proj, b_conv_w, b_a_log, b_dt_bias, b_norm_g, b_proj, w_out, router_w,
                         router_bias, exp_gate, exp_up, exp_down, sh_gate, sh_up, sh_down, norm1_g,
                         norm2_g, final_g)
    nb_p, nb_s = c_prompt.shape[0], c_sample.shape[0]
    rows = -(-(nb_p + nb_s) // SUBLANES) * SUBLANES
    c_all = jnp.concatenate([c_prompt, c_sample, jnp.zeros((rows - nb_p - nb_s, D_MODEL), F32)], axis=0)
    mod = _adaln_mod(c_all, ada_w[0], ada_b[0])
    y_prompt = _layer(x_prompt, mod[:nb_p], p)
    y_sample = _layer(x_sample, mod[nb_p:nb_p + nb_s], p)
    return (y_prompt, y_sample)
```

```python
import functools

import jax
import jax.numpy as jnp
from jax import lax
from jax.experimental import pallas as pl
from jax.experimental.pallas import tpu as pltpu
from jax.experimental.pallas import tpu_sc as plsc

F32 = jnp.float32
BF16 = jnp.bfloat16
ACT = jnp.bfloat16
HIGHEST = lax.Precision.HIGHEST

D_MODEL = 1024
A_HEADS = 8
A_HEAD_DIM = 64
A_WIDTH = A_HEADS * A_HEAD_DIM
A_RANK_W = 64
A_RANK_A = 64
A_RANK_G = 128
A_GN_EPS = 64e-5
A_COLS = 3 * A_WIDTH + A_RANK_W + A_RANK_A + A_RANK_G
B_HEADS = 8
B_HEAD_DIM = 128
B_WIDTH = B_HEADS * B_HEAD_DIM
B_CONV = 5
CHUNK = 64
SCAN_SUB = 4
GDN_COL_GROUP = 512
MIXER_SUB = 2
N_EXPERTS = 64
TOP_K = 6
N_GROUPS = 8
TOPK_GROUPS = 4
GROUP_SIZE = N_EXPERTS // N_GROUPS
D_EXPERT = 256
ROUTED_SCALE = 2.5
EPS = 1e-6
LANES = 128
SUBLANES = 8
HALO = 16
SMALL_COLS = LANES
MOE_BLOCK = 512
SC_CORES = 2
SC_SUBCORES = 16
SC_WORKERS = SC_CORES * SC_SUBCORES
GATHER_WINDOW = 64
VMEM_LIMIT = 56 * 1024 * 1024
NEG_INF = float("-inf")


def _dot(a, b):
    return jnp.dot(a.astype(BF16), b.astype(BF16), preferred_element_type=F32)


def _dot_nt(a, b):
    return lax.dot_general(a.astype(BF16), b.astype(BF16), (((1,), (1,)), ((), ())),
                           preferred_element_type=F32)


def _dot_tn(a, b):
    return lax.dot_general(a.astype(BF16), b.astype(BF16), (((0,), (0,)), ((), ())),
                           preferred_element_type=F32)


def _dot_hi(a, b):
    return jnp.dot(a, b, precision=HIGHEST, preferred_element_type=F32)


def _split(x):
    hi = x.astype(BF16)
    return hi, (x - hi.astype(F32)).astype(BF16)


def _dot3(a, b, dims=(((1,), (0,)), ((), ()))):
    ah, al = _split(a)
    bh, bl = _split(b)
    d = lambda u, v: lax.dot_general(u, v, dims, preferred_element_type=F32)
    return d(ah, bh) + (d(ah, bl) + d(al, bh))


def _seg_sum(x, ones_bd):
    hi = x.astype(BF16)
    lo = (x - hi.astype(F32)).astype(BF16)
    return (jnp.dot(hi, ones_bd, preferred_element_type=F32)
            + jnp.dot(lo, ones_bd, preferred_element_type=F32))


def _softplus(x):
    return jnp.maximum(x, 0.0) + jnp.log1p(jnp.exp(-jnp.abs(x)))


def _sigmoid(x):
    return 1.0 / (1.0 + jnp.exp(-x))


def _silu(x):
    return x * _sigmoid(x)


def _tri_inverse(a_list, eye):
    n = a_list[0].shape[0]
    xs = [(-a).astype(BF16) for a in a_list]
    ts = [eye - a for a in a_list]
    xs = [jnp.dot(x, x, preferred_element_type=F32).astype(BF16) for x in xs]
    power = 2
    while 2 * power < CHUNK:
        both = [jnp.dot(x, jnp.concatenate([t.astype(BF16), x], axis=1), preferred_element_type=F32)
                for t, x in zip(ts, xs)]
        ts = [t + b[:, 0:n] for t, b in zip(ts, both)]
        xs = [b[:, n:2 * n].astype(BF16) for b in both]
        power *= 2
    return [t + jnp.dot(x, t.astype(BF16), preferred_element_type=F32) for t, x in zip(ts, xs)]


def _params(sem, **extra):
    return pltpu.CompilerParams(dimension_semantics=sem, vmem_limit_bytes=VMEM_LIMIT, **extra)


def _mod_kernel(c_ref, w_ref, b_ref, o_ref):
    c = c_ref[...]
    o_ref[...] = _dot_hi(_silu(c), w_ref[...]) + b_ref[...]


def _adaln_mod(c, ada_w, ada_b):
    rows = c.shape[0]
    n = ada_w.shape[1]
    return pl.pallas_call(
        _mod_kernel,
        grid=(n // D_MODEL,),
        in_specs=[pl.BlockSpec((rows, D_MODEL), lambda j: (0, 0)),
                  pl.BlockSpec((D_MODEL, D_MODEL), lambda j: (0, j)),
                  pl.BlockSpec((1, D_MODEL), lambda j: (0, j))],
        out_specs=pl.BlockSpec((rows, D_MODEL), lambda j: (0, j)),
        out_shape=jax.ShapeDtypeStruct((rows, n), F32),
        compiler_params=_params(("arbitrary",)),
        name="adaln_mod",
    )(c, ada_w, ada_b.reshape(1, n))


def _rms_mod(x, g, sc, sh):
    y = x * lax.rsqrt(jnp.mean(x * x, axis=-1, keepdims=True) + EPS)
    return (y * g) * (1.0 + sc) + sh


def _inproj_kernel(x_ref, g_ref, sc_ref, sh_ref, w_ref, o_ref):
    h = _rms_mod(x_ref[0], g_ref[...], sc_ref[0], sh_ref[0]).astype(BF16)
    o_ref[0] = jnp.dot(h, w_ref[...], preferred_element_type=F32).astype(o_ref.dtype)


def _inproj(x, g, sc, sh, w):
    B, T, D = x.shape
    n = w.shape[1]
    tm = min(512, T)
    return pl.pallas_call(
        _inproj_kernel,
        grid=(B, T // tm),
        in_specs=[pl.BlockSpec((1, tm, D), lambda b, i: (b, i, 0)),
                  pl.BlockSpec((1, D), lambda b, i: (0, 0)),
                  pl.BlockSpec((1, 1, D), lambda b, i: (b, 0, 0)),
                  pl.BlockSpec((1, 1, D), lambda b, i: (b, 0, 0)),
                  pl.BlockSpec((D, n), lambda b, i: (0, 0))],
        out_specs=pl.BlockSpec((1, tm, n), lambda b, i: (b, i, 0)),
        out_shape=jax.ShapeDtypeStruct((B, T, n), ACT),
        compiler_params=_params(("parallel", "parallel")),
        name="in_proj",
    )(x, g, sc, sh, w)


def _halo_specs(tt, width, col_block, seq_len):
    nb = tt // HALO
    last = seq_len // HALO - 1
    return [
        pl.BlockSpec((1, tt, width), lambda b, i: (b, i, col_block)),
        pl.BlockSpec((1, HALO, width), lambda b, i: (b, jnp.maximum(i * nb - 1, 0), col_block)),
        pl.BlockSpec((1, HALO, width), lambda b, i: (b, jnp.minimum((i + 1) * nb, last), col_block)),
    ]


def _halo_rows(zp_ref, zn_ref):
    return zp_ref[0].astype(F32)[HALO - SUBLANES:HALO], zn_ref[0].astype(F32)[0:SUBLANES]


def _normed_with_halo(x_ref, xp_ref, xn_ref, g_ref, sc_ref, sh_ref):
    prev8, next8 = _halo_rows(xp_ref, xn_ref)
    x_ext = jnp.concatenate([prev8, x_ref[0], next8], axis=0)
    return _rms_mod(x_ext, g_ref[...], sc_ref[0], sh_ref[0]).astype(BF16)


def _rwkv_prep_kernel(x_ref, xp_ref, xn_ref, g1_ref, sc_ref, sh_ref, w_ref,
                      mu_ref, w0_ref, a0_ref, up_ref, gup_ref, kk_ref, ka_ref, rk_ref, ones_ref,
                      r_out, kk_out, v_out, g_out, bonus_out, lw_out, kd_out, bd_out, ext_scr):
    tt = x_ref.shape[1]
    i = pl.program_id(1)
    last = pl.num_programs(1) - 1
    h = _normed_with_halo(x_ref, xp_ref, xn_ref, g1_ref, sc_ref, sh_ref)
    project = lambda c0, c1: jnp.dot(h, w_ref[:, c0:c1], preferred_element_type=F32)
    row = lax.broadcasted_iota(jnp.int32, (tt + 2 * SUBLANES, 1), 0)
    inside = ((row >= SUBLANES) | (i > 0)) & ((row < tt + SUBLANES) | (i < last))
    W = A_WIDTH

    def shift_mix(z_ext, c0, c1):
        z_ext = z_ext.astype(ACT).astype(F32)
        ext_scr[:, c0:c1] = jnp.where(inside, z_ext, 0.0)
        z = ext_scr[pl.ds(SUBLANES, tt), c0:c1]
        z_prev = ext_scr[pl.ds(SUBLANES - 1, tt), c0:c1]
        z_next = ext_scr[pl.ds(SUBLANES + 1, tt), c0:c1]
        return z + (0.5 * (z_prev + z_next) - z) * mu_ref[:, c0:c1]

    groups = [(3 * W, A_COLS), (W, 2 * W), (0, W), (2 * W, 3 * W)]
    mixed = []
    z_nxt = project(*groups[0])
    for gi, (c0, c1) in enumerate(groups):
        z_cur = z_nxt
        if gi + 1 < len(groups):
            z_nxt = project(*groups[gi + 1])
        mixed.append(shift_mix(z_cur, c0, c1))
    zwag, zk, zr, zv = mixed
    zwa = zwag[:, 0:LANES]
    zg = zwag[:, LANES:2 * LANES]
    ones_bd = ones_ref[...]
    kk_raw = zk * kk_ref[...]
    kk = kk_raw * lax.rsqrt(_seg_sum(kk_raw * kk_raw, ones_bd) + 1e-6)
    lane = lax.broadcasted_iota(jnp.int32, zwa.shape, 1)
    lhs = jnp.where(lane < A_RANK_W, jnp.tanh(zwa), zwa)
    k_sum = jnp.zeros_like(zk)
    for d in range(2):
        up = _dot(lhs, up_ref[d])
        wl = w0_ref[d:d + 1, :] + up[:, 0:W]
        w_log = -_softplus(-wl) - 0.5
        lw_out[d, 0] = -jnp.exp(w_log)
        a = _sigmoid(a0_ref[d:d + 1, :] + up[:, W:2 * W])
        k_d = zk * (1.0 + (a - 1.0) * ka_ref[...])
        kd_out[d, 0] = k_d.astype(kd_out.dtype)
        bd_out[d, 0] = (kk * a).astype(bd_out.dtype)
        k_sum = k_sum + k_d
    r_out[0] = zr.astype(r_out.dtype)
    kk_out[0] = kk.astype(kk_out.dtype)
    v_out[0] = zv.astype(v_out.dtype)
    g_out[0] = _dot(_sigmoid(zg), gup_ref[...]).astype(g_out.dtype)
    bonus_out[0] = (_seg_sum(zr * k_sum * rk_ref[...], ones_bd) * zv).astype(bonus_out.dtype)


def _front_specs(tt, seq_len, n_cols):
    modrow = pl.BlockSpec((1, 1, D_MODEL), lambda b, i: (b, 0, 0))
    return _halo_specs(tt, D_MODEL, 0, seq_len) + [
        pl.BlockSpec((1, D_MODEL), lambda b, i: (0, 0)), modrow, modrow,
        pl.BlockSpec((D_MODEL, n_cols), lambda b, i: (0, 0))]


def _rwkv_prep(x, g1, sc, sh, w_a, mu, w0, a0, up_comb, g_up, k_k, k_a, r_k, ones_bd):
    B, T, _ = x.shape
    tt = min(256, T)
    W = A_WIDTH
    full = lambda shape: pl.BlockSpec(shape, lambda b, i: (0,) * len(shape))
    tok = pl.BlockSpec((1, tt, W), lambda b, i: (b, i, 0))
    tok2 = pl.BlockSpec((2, 1, tt, W), lambda b, i: (0, b, i, 0))
    s1 = jax.ShapeDtypeStruct((B, T, W), ACT)
    s2 = jax.ShapeDtypeStruct((2, B, T, W), ACT)
    lw = jax.ShapeDtypeStruct((2, B, T, W), F32)
    return pl.pallas_call(
        _rwkv_prep_kernel,
        grid=(B, T // tt),
        in_specs=_front_specs(tt, T, A_COLS) + [
            full((1, A_COLS)), full((2, W)), full((2, W)), full((2, LANES, 2 * W)),
            full((A_RANK_G, W)), full((1, W)), full((1, W)), full((1, W)), full((W, W))],
        out_specs=[tok, tok, tok, tok, tok, tok2, tok2, tok2],
        out_shape=[s1, s1, s1, s1, s1, lw, s2, s2],
        scratch_shapes=[pltpu.VMEM((tt + 2 * SUBLANES, A_COLS), F32)],
        compiler_params=_params(("parallel", "parallel")),
        name="rwkv_prep",
    )(x, x, x, g1, sc, sh, w_a, mu, w0, a0, up_comb, g_up, k_k, k_a, r_k, ones_bd)


def _chunk_masks(reverse):
    n = 2 * CHUNK
    row = lax.broadcasted_iota(jnp.int32, (n, n), 0)
    col = lax.broadcasted_iota(jnp.int32, (n, n), 1)
    same = (row // CHUNK) == (col // CHUNK)
    ti, tj = row % CHUNK, col % CHUNK
    if reverse:
        return same & (ti < tj), same & (ti <= tj), row == col
    return same & (ti > tj), same & (ti >= tj), row == col


def _cumsum_matrix(reverse):
    row = lax.broadcasted_iota(jnp.int32, (CHUNK, CHUNK), 0)
    col = lax.broadcasted_iota(jnp.int32, (CHUNK, CHUNK), 1)
    return ((row <= col) if reverse else (row >= col)).astype(F32)


def _rwkv_chunk_operands(r, kk, v, k, b, cs, lw, reverse):
    C = CHUNK
    cs_end = cs[0:1, :] if reverse else cs[C - 1:C, :]
    m0 = lax.broadcasted_iota(jnp.int32, (C, LANES), 1) < A_HEAD_DIM

    def stack(x):
        return jnp.concatenate([jnp.where(m0, x, 0.0), jnp.where(m0, 0.0, x)], axis=0)

    g_inv = jnp.exp(-cs)
    g_tail = jnp.exp(cs_end - cs)
    strict, incl, _ = _chunk_masks(reverse)
    return dict(rg=stack(r * jnp.exp(cs)), kkg=stack(kk * jnp.exp(cs - lw)), ki=stack(k * g_inv),
                bi=stack(b * g_inv), kt=stack(k * g_tail), bt=stack(b * g_tail), vs=stack(v),
                g_end=jnp.exp(cs_end), strict=strict, incl=incl)


def _rwkv_chunks(ops, hts):
    C = CHUNK
    n = 2 * C
    eye = _chunk_masks(False)[2].astype(F32)
    ps = [_dot_nt(jnp.concatenate([o["kkg"], o["rg"]], axis=0), jnp.concatenate([o["bi"], o["ki"]], axis=0))
          for o in ops]
    a_ab = [jnp.where(o["strict"], p[0:n, 0:n], 0.0) for o, p in zip(ops, ps)]
    a_rb = [jnp.where(o["incl"], p[n:2 * n, 0:n], 0.0).astype(BF16) for o, p in zip(ops, ps)]
    a_k = [jnp.concatenate([jnp.where(o["strict"], p[0:n, n:2 * n], 0.0),
                            jnp.where(o["incl"], p[n:2 * n, n:2 * n], 0.0)], axis=0).astype(BF16)
           for o, p in zip(ops, ps)]
    vsb = [o["vs"].astype(BF16) for o in ops]
    a_kv = [jnp.dot(a, v, preferred_element_type=F32) for a, v in zip(a_k, vsb)]
    akv = [x[0:n] for x in a_kv]
    arkv = [x[n:2 * n] for x in a_kv]
    tinv = _tri_inverse(a_ab, eye)
    wu = [_dot(t, jnp.concatenate([o["kkg"], x], axis=1)).astype(BF16)
          for t, o, x in zip(tinv, ops, akv)]
    r2 = [jnp.dot(a, w, preferred_element_type=F32) for a, w in zip(a_rb, wu)]
    btb = [o["bt"].astype(BF16) for o in ops]
    btw = [_dot_tn(b, w[:, 0:LANES]) for b, w in zip(btb, wu)]
    hloc = [_dot_tn(jnp.concatenate([v, -w[:, LANES:2 * LANES]], axis=0),
                    jnp.concatenate([o["kt"].astype(BF16), b], axis=0))
            for v, o, w, b in zip(vsb, ops, wu, btb)]
    ys, hts_new = [], []
    for o, r, yl, ht, bw, hl in zip(ops, r2, arkv, hts, btw, hloc):
        q_s = o["rg"] - r[:, 0:LANES]
        yloc_s = yl - r[:, LANES:2 * LANES]
        ys.append(yloc_s[0:C] + yloc_s[C:n] + _dot_nt(q_s[0:C] + q_s[C:n], ht))
        hts_new.append(ht * o["g_end"] - _dot_nt(ht, bw) + hl)
    return ys, hts_new


def _rwkv_scan_kernel(rf, kkf, vf, lwf, kf, bf, rb, kkb, vb, lwb, kb, bb, yf_out, yb_out, h_scr):
    @pl.when(pl.program_id(1) == 0)
    def _():
        h_scr[...] = jnp.zeros_like(h_scr)

    n_pairs = A_WIDTH // LANES
    sls = [slice(i * LANES, (i + 1) * LANES) for i in range(n_pairs)]
    f32 = lambda a: a.astype(F32)
    hts = [h_scr[d, i] for d in range(2) for i in range(n_pairs)]
    for j in range(SCAN_SUB):
        ops = []
        for d, (r_ref, kk_ref, v_ref, lw_ref, k_ref, b_ref) in enumerate(
                ((rf, kkf, vf, lwf, kf, bf), (rb, kkb, vb, lwb, kb, bb))):
            reverse = d == 1
            rows = pl.ds((SCAN_SUB - 1 - j if reverse else j) * CHUNK, CHUNK)
            lw_all = lw_ref[0, 0, rows, :]
            cs_all = _dot_hi(_cumsum_matrix(reverse), lw_all)
            ops += [_rwkv_chunk_operands(f32(r_ref[0, rows, sl]), f32(kk_ref[0, rows, sl]),
                                         f32(v_ref[0, rows, sl]), f32(k_ref[0, 0, rows, sl]),
                                         f32(b_ref[0, 0, rows, sl]), cs_all[:, sl], lw_all[:, sl], reverse)
                    for sl in sls]
        ys, hts = _rwkv_chunks(ops, hts)
        for d, y_out in enumerate((yf_out, yb_out)):
            rows = pl.ds((SCAN_SUB - 1 - j if d == 1 else j) * CHUNK, CHUNK)
            for i, sl in enumerate(sls):
                y_out[0, rows, sl] = ys[d * n_pairs + i].astype(y_out.dtype)
    for d in range(2):
        for i in range(n_pairs):
            h_scr[d, i] = hts[d * n_pairs + i]


def _rwkv_scan(r, kk, v, lw, kd, bd):
    B, T, W = r.shape
    rows = CHUNK * SCAN_SUB
    N = T // rows
    fwd = pl.BlockSpec((1, rows, W), lambda b, c: (b, c, 0))
    bwd = pl.BlockSpec((1, rows, W), lambda b, c: (b, N - 1 - c, 0))
    fwd2 = pl.BlockSpec((1, 1, rows, W), lambda b, c: (0, b, c, 0))
    bwd2 = pl.BlockSpec((1, 1, rows, W), lambda b, c: (1, b, N - 1 - c, 0))
    out = jax.ShapeDtypeStruct((B, T, W), ACT)
    return pl.pallas_call(
        _rwkv_scan_kernel,
        grid=(B, N),
        in_specs=[fwd, fwd, fwd, fwd2, fwd2, fwd2, bwd, bwd, bwd, bwd2, bwd2, bwd2],
        out_specs=[fwd, bwd],
        out_shape=[out, out],
        scratch_shapes=[pltpu.VMEM((2, W // LANES, LANES, LANES), F32)],
        compiler_params=_params(("parallel", "arbitrary")),
        name="rwkv_scan",
    )(r, kk, v, lw, kd, bd, r, kk, v, lw, kd, bd)


def _gdn_prep_kernel(x_ref, xp_ref, xn_ref, g1_ref, sc_ref, sh_ref, w_ref, cw_ref, alog_ref, dtb_ref,
                     q_out, k_out, v_out, bg_out, acc_scr):
    tt = x_ref.shape[1]
    i = pl.program_id(1)
    last = pl.num_programs(1) - 1
    half = B_CONV // 2
    Wq = 3 * B_WIDTH
    h = _normed_with_halo(x_ref, xp_ref, xn_ref, g1_ref, sc_ref, sh_ref)
    project = lambda c0, c1: jnp.dot(h, w_ref[:, c0:c1], preferred_element_type=F32)
    offsets = [j - half for j in range(B_CONV) if j != half]
    row = lax.broadcasted_iota(jnp.int32, (len(offsets) * tt, tt), 0)
    col = lax.broadcasted_iota(jnp.int32, (len(offsets) * tt, tt), 1)
    blk = row // tt
    off = jnp.where(blk < half, blk - half, blk - half + 1)
    shift_rows = (col == row - blk * tt + off).astype(BF16)
    r8 = lax.broadcasted_iota(jnp.int32, (SUBLANES, 1), 0)

    def conv_group(z_ext, c0, c1):
        zb = z_ext[SUBLANES:SUBLANES + tt].astype(BF16)
        shifted = jnp.dot(shift_rows, zb, preferred_element_type=F32)
        acc = zb.astype(F32) * cw_ref[half:half + 1, c0:c1]
        for m, o in enumerate(offsets):
            acc = acc + shifted[m * tt:(m + 1) * tt] * cw_ref[o + half:o + half + 1, c0:c1]
        acc_scr[:, c0:c1] = acc
        prev8 = jnp.where(i == 0, 0.0, z_ext[0:SUBLANES]).astype(BF16).astype(F32)
        next8 = jnp.where(i == last, 0.0, z_ext[SUBLANES + tt:]).astype(BF16).astype(F32)
        top = bot = None
        for o in offsets:
            w = cw_ref[o + half:o + half + 1, c0:c1]
            if o < 0:
                f = jnp.where(r8 < -o, pltpu.roll(prev8, -o, axis=0), 0.0) * w
                top = f if top is None else top + f
            else:
                f = jnp.where(r8 >= SUBLANES - o, pltpu.roll(next8, SUBLANES - o, axis=0), 0.0) * w
                bot = f if bot is None else bot + f
        acc_scr[pl.ds(0, SUBLANES), c0:c1] += top
        acc_scr[pl.ds(tt - SUBLANES, SUBLANES), c0:c1] += bot
        return _silu(acc_scr[:, c0:c1])

    bounds = list(range(0, Wq + 1, GDN_COL_GROUP))
    z_next = project(bounds[0], bounds[1])
    for gi in range(len(bounds) - 1):
        c0, c1 = bounds[gi], bounds[gi + 1]
        z_cur = z_next
        z_next = project(bounds[gi + 1], bounds[gi + 2]) if gi + 2 < len(bounds) else project(Wq, Wq + SMALL_COLS)
        act = conv_group(z_cur, c0, c1)
        for hh in range(GDN_COL_GROUP // B_HEAD_DIM):
            x = act[:, hh * B_HEAD_DIM:(hh + 1) * B_HEAD_DIM]
            lo = c0 % B_WIDTH + hh * B_HEAD_DIM
            if c0 < 2 * B_WIDTH:
                x = x * lax.rsqrt(jnp.sum(x * x, axis=-1, keepdims=True) + 1e-6)
                if c0 < B_WIDTH:
                    x = x * (B_HEAD_DIM ** -0.5)
            out = (q_out, k_out, v_out)[c0 // B_WIDTH]
            out[0, :, lo:lo + B_HEAD_DIM] = x.astype(out.dtype)
    s = z_next[SUBLANES:SUBLANES + tt]
    lane = lax.broadcasted_iota(jnp.int32, s.shape, 1)
    beta = _sigmoid(s)
    g = -jnp.exp(alog_ref[...]) * _softplus(s + dtb_ref[...])
    bg_out[0] = jnp.where(lane < 2 * B_HEADS, beta, g)


def _gdn_prep(x, g1, sc, sh, w_b, conv_w, alog_vec, dtb_vec):
    B, T, _ = x.shape
    Wq = 3 * B_WIDTH
    tt = min(256, T)
    full = lambda shape: pl.BlockSpec(shape, lambda b, i: (0,) * len(shape))
    tok = pl.BlockSpec((1, tt, B_WIDTH), lambda b, i: (b, i, 0))
    s1 = jax.ShapeDtypeStruct((B, T, B_WIDTH), ACT)
    return pl.pallas_call(
        _gdn_prep_kernel,
        grid=(B, T // tt),
        in_specs=_front_specs(tt, T, Wq + SMALL_COLS) + [full((B_CONV, Wq)), full((1, SMALL_COLS)),
                                                          full((1, SMALL_COLS))],
        out_specs=[tok, tok, tok, pl.BlockSpec((1, tt, SMALL_COLS), lambda b, i: (b, i, 0))],
        out_shape=[s1, s1, s1, jax.ShapeDtypeStruct((B, T, SMALL_COLS), F32)],
        scratch_shapes=[pltpu.VMEM((tt, Wq), F32)],
        compiler_params=_params(("parallel", "parallel")),
        name="gdn_prep",
    )(x, x, x, g1, sc, sh, w_b, conv_w, alog_vec, dtb_vec)


def _gdn_chunk_operands(q0, q1, k0, k1, v0, v1, gc2, gct2, bg2, j0, j1, reverse):
    C = CHUNK
    n = 2 * C
    strict, incl, _ = _chunk_masks(reverse)
    top = lax.broadcasted_iota(jnp.int32, (n, 1), 0) < C
    left = lax.broadcasted_iota(jnp.int32, (1, n), 1) < C
    g0, g1 = j0 + 2 * B_HEADS, j1 + 2 * B_HEADS
    gcol = jnp.where(top, gc2[:, g0:g0 + 1], gc2[:, g1:g1 + 1])
    grow = jnp.where(left, gct2[g0:g0 + 1, :], gct2[g1:g1 + 1, :])
    beta = jnp.where(top, bg2[:, j0:j0 + 1], bg2[:, j1:j1 + 1])
    e = 0 if reverse else C - 1
    glast = jnp.where(top, gc2[e:e + 1, g0:g0 + 1], gc2[e:e + 1, g1:g1 + 1])
    ks = jnp.concatenate([k0, k1], axis=0)
    qs = jnp.concatenate([q0, q1], axis=0)
    vs = jnp.concatenate([v0, v1], axis=0)
    egc = jnp.exp(gcol)
    kb = ks * beta
    return dict(gam=jnp.exp(jnp.where(incl, gcol - grow, NEG_INF)), strict=strict, ks=ks,
                kbq=jnp.concatenate([kb, qs], axis=0),
                rhs=jnp.concatenate([vs * beta, kb * egc], axis=1),
                qh=(qs * egc).astype(BF16), ktail=(ks * jnp.exp(glast - gcol)).astype(BF16),
                decay=jnp.exp(glast))


def _gdn_chunks(ops, states):
    C = CHUNK
    n = 2 * C
    eye = _chunk_masks(False)[2].astype(F32)
    ps = [_dot_nt(o["kbq"], o["ks"]) for o in ops]
    lower = [jnp.where(o["strict"], p[0:n] * o["gam"], 0.0) for o, p in zip(ops, ps)]
    aqk = [(p[n:2 * n] * o["gam"]).astype(BF16) for o, p in zip(ops, ps)]
    tinv = [t.astype(BF16) for t in _tri_inverse(lower, eye)]
    uw = [jnp.dot(t, o["rhs"].astype(BF16), preferred_element_type=F32) for t, o in zip(tinv, ops)]
    lx = []
    for l, x in zip(lower, uw):
        lh, ll = _split(l)
        xh, xl = _split(x)
        first = jnp.dot(lh, jnp.concatenate([xh, xl], axis=1), preferred_element_type=F32)
        w2 = x.shape[1]
        lx.append(first[:, 0:w2] + (first[:, w2:2 * w2] + jnp.dot(ll, xh, preferred_element_type=F32)))
    resid = [o["rhs"] - x - y for o, x, y in zip(ops, uw, lx)]
    uw = [x + jnp.dot(t, r.astype(BF16), preferred_element_type=F32) for x, t, r in zip(uw, tinv, resid)]
    sb = [(s0.astype(BF16), s1.astype(BF16)) for s0, s1 in states]
    wq = [[jnp.dot(jnp.concatenate([u[h * C:(h + 1) * C, LANES:].astype(BF16), o["qh"][h * C:(h + 1) * C]],
                                   axis=0), s[h], preferred_element_type=F32) for h in range(2)]
          for u, o, s in zip(uw, ops, sb)]
    ws = [jnp.concatenate([x[0][0:C], x[1][0:C]], axis=0) for x in wq]
    qss = [jnp.concatenate([x[0][C:n], x[1][C:n]], axis=0) for x in wq]
    v_new = [(u[:, 0:LANES] - w).astype(BF16) for u, w in zip(uw, ws)]
    outs = [q + jnp.dot(a, v, preferred_element_type=F32) for q, a, v in zip(qss, aqk, v_new)]
    new_states = []
    for o, v, (s0, s1) in zip(ops, v_new, states):
        new_states.append((s0 * o["decay"][0:1, :] + _dot_tn(o["ktail"][0:C], v[0:C]),
                           s1 * o["decay"][C:C + 1, :] + _dot_tn(o["ktail"][C:n], v[C:n])))
    return [(x[0:C], x[C:n]) for x in outs], new_states


def _gdn_scan_kernel(qf, kf, vf, bgf, qb, kb, vb, bgb, of_out, ob_out, s_scr):
    @pl.when(pl.program_id(1) == 0)
    def _():
        s_scr[...] = jnp.zeros_like(s_scr)

    D = B_HEAD_DIM
    n_pairs = B_HEADS // 2
    sl = lambda h: slice(h * D, (h + 1) * D)
    f32 = lambda a: a.astype(F32)
    states = [(s_scr[d, 2 * i], s_scr[d, 2 * i + 1]) for d in range(2) for i in range(n_pairs)]
    for j in range(SCAN_SUB):
        ops = []
        for d, (q_ref, k_ref, v_ref, bg_ref) in enumerate(((qf, kf, vf, bgf), (qb, kb, vb, bgb))):
            reverse = d == 1
            rows = pl.ds((SCAN_SUB - 1 - j if reverse else j) * CHUNK, CHUNK)
            bg = bg_ref[0, rows, :]
            gc = _dot_hi(_cumsum_matrix(reverse), bg)
            bg2 = jnp.concatenate([bg, bg], axis=0)
            gc2 = jnp.concatenate([gc, gc], axis=0)
            gct2 = gc2.T
            for i in range(n_pairs):
                h0, h1 = 2 * i, 2 * i + 1
                ops.append(_gdn_chunk_operands(
                    f32(q_ref[0, rows, sl(h0)]), f32(q_ref[0, rows, sl(h1)]), f32(k_ref[0, rows, sl(h0)]),
                    f32(k_ref[0, rows, sl(h1)]), f32(v_ref[0, rows, sl(h0)]), f32(v_ref[0, rows, sl(h1)]),
                    gc2, gct2, bg2, d * B_HEADS + h0, d * B_HEADS + h1, reverse))
        outs, states = _gdn_chunks(ops, states)
        for d, o_out in enumerate((of_out, ob_out)):
            rows = pl.ds((SCAN_SUB - 1 - j if d == 1 else j) * CHUNK, CHUNK)
            for i in range(n_pairs):
                o0, o1 = outs[d * n_pairs + i]
                o_out[0, rows, sl(2 * i)] = o0.astype(o_out.dtype)
                o_out[0, rows, sl(2 * i + 1)] = o1.astype(o_out.dtype)
    for d in range(2):
        for i in range(n_pairs):
            s_scr[d, 2 * i], s_scr[d, 2 * i + 1] = states[d * n_pairs + i]


def _gdn_scan(q, k, v, bg):
    B, T, W = q.shape
    rows = CHUNK * SCAN_SUB
    N = T // rows
    fwd = pl.BlockSpec((1, rows, W), lambda b, c: (b, c, 0))
    bwd = pl.BlockSpec((1, rows, W), lambda b, c: (b, N - 1 - c, 0))
    sfwd = pl.BlockSpec((1, rows, SMALL_COLS), lambda b, c: (b, c, 0))
    sbwd = pl.BlockSpec((1, rows, SMALL_COLS), lambda b, c: (b, N - 1 - c, 0))
    out = jax.ShapeDtypeStruct((B, T, W), ACT)
    return pl.pallas_call(
        _gdn_scan_kernel,
        grid=(B, N),
        in_specs=[fwd, fwd, fwd, sfwd, bwd, bwd, bwd, sbwd],
        out_specs=[fwd, bwd],
        out_shape=[out, out],
        scratch_shapes=[pltpu.VMEM((2, B_HEADS, B_HEAD_DIM, B_HEAD_DIM), F32)],
        compiler_params=_params(("parallel", "arbitrary")),
        name="gdn_scan",
    )(q, k, v, bg, q, k, v, bg)


def _pack_bf16_pairs(x):
    n = x.shape[1] // 2
    hi = lax.bitcast_convert_type(x[:, :n].astype(BF16).astype(F32), jnp.int32)
    lo = lax.bitcast_convert_type(x[:, n:].astype(BF16).astype(F32), jnp.int32)
    return hi | lax.shift_right_logical(lo, 16)


def _unpack_bf16_pairs(p):
    a = lax.bitcast_convert_type(p & jnp.int32(-65536), F32)
    b = lax.bitcast_convert_type(lax.shift_left(p, 16), F32)
    return a.astype(BF16), b.astype(BF16)


def _route(scores, biased, base):
    tt = scores.shape[-1]
    shape3 = (N_GROUPS, GROUP_SIZE, tt)
    s3 = scores.reshape(shape3)
    b3 = biased.reshape(shape3)
    jid = lax.broadcasted_iota(jnp.int32, shape3, 1).astype(F32)
    gid = lax.broadcasted_iota(jnp.int32, shape3, 0).astype(F32)
    m1 = jnp.max(b3, axis=1, keepdims=True)
    first = jnp.min(jnp.where(b3 == m1, jid, float(GROUP_SIZE)), axis=1, keepdims=True)
    m2 = jnp.max(jnp.where(jid == first, NEG_INF, b3), axis=1, keepdims=True)
    gs = m1 + m2
    grp = lax.broadcasted_iota(jnp.int32, (N_GROUPS, 1, tt), 0).astype(F32)
    keep = jnp.zeros((N_GROUPS, 1, tt), F32)
    for _ in range(TOPK_GROUPS):
        m = jnp.max(gs, axis=0, keepdims=True)
        pick = grp == jnp.min(jnp.where(gs == m, grp, float(N_GROUPS)), axis=0, keepdims=True)
        keep = jnp.where(pick, 1.0, keep)
        gs = jnp.where(pick, NEG_INF, gs)
    work = jnp.where(keep > 0.0, b3, NEG_INF)
    eid = gid * float(GROUP_SIZE) + jid
    chosen = jnp.zeros(shape3, F32)
    sum01 = lambda a: jnp.sum(jnp.sum(a, axis=1, keepdims=True), axis=0, keepdims=True)
    ids, raw = [], []
    for _ in range(TOP_K):
        m = jnp.max(jnp.max(work, axis=1, keepdims=True), axis=0, keepdims=True)
        cand = jnp.where(work == m, eid, float(N_EXPERTS))
        first = jnp.min(jnp.min(cand, axis=1, keepdims=True), axis=0, keepdims=True)
        pick = eid == first
        ids.append(first)
        raw.append(sum01(jnp.where(pick, s3, 0.0)))
        chosen = jnp.where(pick, 1.0, chosen)
        work = jnp.where(pick, NEG_INF, work)
    total = raw[0]
    for r in raw[1:]:
        total = total + r
    flat = chosen.reshape(N_EXPERTS, tt)
    earlier = (lax.broadcasted_iota(jnp.int32, (tt, tt), 0)
               < lax.broadcasted_iota(jnp.int32, (tt, tt), 1)).astype(BF16)
    prefix = (jnp.dot(flat.astype(BF16), earlier, preferred_element_type=F32) + base).reshape(shape3)
    row = lax.broadcasted_iota(jnp.int32, (SUBLANES, tt), 0)
    idx8 = jnp.zeros((SUBLANES, tt), F32)
    w8 = jnp.zeros((SUBLANES, tt), F32)
    pos8 = jnp.zeros((SUBLANES, tt), F32)
    for k in range(TOP_K):
        rank = sum01(jnp.where(eid == ids[k], prefix, 0.0))
        idx8 = jnp.where(row == k, ids[k].reshape(1, tt), idx8)
        w8 = jnp.where(row == k, (raw[k] / total * ROUTED_SCALE).reshape(1, tt), w8)
        pos8 = jnp.where(row == k, rank.reshape(1, tt), pos8)
    counts = jnp.sum(flat, axis=1, keepdims=True)
    return idx8.astype(jnp.int32), w8, pos8.astype(jnp.int32), counts


def _mixer_post_kernel(yf, yb, bonus, g, of, ob, zz, zga, zgb, x_ref, gt1, sc2, sh2,
                       gnw, gnb, ones_ref, aproj, bng, bproj, wout, n2g, rwt, rbias,
                       x1_out, h2_out, idx_out, w_out_ref, pos_out, counts_out):
    @pl.when((pl.program_id(0) == 0) & (pl.program_id(1) == 0))
    def _():
        counts_out[...] = jnp.zeros_like(counts_out)

    ones_bd = ones_ref[...]
    n_sub = MIXER_SUB
    rs = x_ref.shape[1] // n_sub
    sub = range(n_sub)
    ld = lambda ref, k: ref[0, pl.ds(k * rs, rs), :].astype(F32)
    y = [ld(yf, k) + ld(yb, k) for k in sub]
    mean = [_seg_sum(v, ones_bd) * (1.0 / A_HEAD_DIM) for v in y]
    dlt = [v - m for v, m in zip(y, mean)]
    var = [_seg_sum(v * v, ones_bd) * (1.0 / A_HEAD_DIM) for v in dlt]
    yn = [v * lax.rsqrt(s + A_GN_EPS) * gnw[...] + gnb[...] for v, s in zip(dlt, var)]
    y_a = [_dot((v + ld(bonus, k)) * ld(g, k), aproj[...]) for k, v in zip(sub, yn)]

    def head_norm(o):
        parts = []
        for h in range(B_HEADS):
            oh = o[:, h * B_HEAD_DIM:(h + 1) * B_HEAD_DIM]
            parts.append(oh * lax.rsqrt(jnp.mean(oh * oh, axis=-1, keepdims=True) + EPS))
        return jnp.concatenate(parts, axis=1)

    on = [head_norm(ld(of, k) + ld(ob, k)) * bng[...] * _silu(ld(zz, k)) for k in sub]
    y_b = [_dot(v, bproj[...]) for v in on]
    u = [_sigmoid(ld(zga, k)) * a + _sigmoid(ld(zgb, k)) * b for k, a, b in zip(sub, y_a, y_b)]
    x1 = [ld(x_ref, k) + gt1[0] * _dot(v, wout[...]) for k, v in zip(sub, u)]
    h2 = [_rms_mod(v, n2g[...], sc2[0], sh2[0]) for v in x1]
    scores = [_sigmoid(_dot_nt(rwt[...], v)) for v in h2]
    base = counts_out[:, 0:1]
    for k in sub:
        rows, lanes = pl.ds(k * rs, rs), pl.ds(k * rs, rs)
        x1_out[0, rows, :] = x1[k]
        h2_out[0, rows, :] = _pack_bf16_pairs(h2[k])
        idx8, w8, pos8, counts = _route(scores[k], scores[k] + rbias[...], base)
        idx_out[0, :, lanes] = idx8
        w_out_ref[0, :, lanes] = w8
        pos_out[0, :, lanes] = pos8
        base = base + counts
    counts_out[...] = jnp.broadcast_to(base, counts_out.shape)


def _mixer_post(yf, yb, bonus, g, of, ob, z_c, x, gt1, sc2, sh2, gn_w, gn_b, ones_bd, a_proj, bng,
                b_proj, w_out, n2g, rwt, rbias):
    B, T, D = x.shape
    tt = min(512, T)
    full = lambda shape: pl.BlockSpec(shape, lambda b, i: (0,) * len(shape))
    tokw = lambda w, col=0: pl.BlockSpec((1, tt, w), lambda b, i: (b, i, col))
    modrow = pl.BlockSpec((1, 1, D), lambda b, i: (b, 0, 0))
    pick = pl.BlockSpec((1, SUBLANES, tt), lambda b, i: (b, 0, i))
    W = A_WIDTH
    return pl.pallas_call(
        _mixer_post_kernel,
        grid=(B, T // tt),
        in_specs=[tokw(W), tokw(W), tokw(W), tokw(W), tokw(D), tokw(D),
                  tokw(D, 0), tokw(D, 1), tokw(D, 2), tokw(D), modrow, modrow, modrow,
                  full((1, W)), full((1, W)), full((W, W)), full((W, D)), full((1, D)), full((D, D)),
                  full((D, D)), full((1, D)), full((N_EXPERTS, D)), full((N_EXPERTS, 1))],
        out_specs=[tokw(D), tokw(D // 2), pick, pick, pick,
                   pl.BlockSpec((N_EXPERTS, LANES), lambda b, i: (0, 0))],
        out_shape=[jax.ShapeDtypeStruct((B, T, D), F32), jax.ShapeDtypeStruct((B, T, D // 2), jnp.int32),
                   jax.ShapeDtypeStruct((B, SUBLANES, T), jnp.int32),
                   jax.ShapeDtypeStruct((B, SUBLANES, T), F32),
                   jax.ShapeDtypeStruct((B, SUBLANES, T), jnp.int32),
                   jax.ShapeDtypeStruct((N_EXPERTS, LANES), F32)],
        compiler_params=_params(("arbitrary", "arbitrary")),
        name="mixer_post",
    )(yf, yb, bonus, g, of, ob, z_c, z_c, z_c, x, gt1, sc2, sh2, gn_w, gn_b, ones_bd, a_proj, bng,
      b_proj, w_out, n2g, rwt, rbias)


def _gather_rows(table, idx):
    n_rows, width = idx.shape[0], table.shape[1]
    per_worker = n_rows // SC_WORKERS
    n_windows = per_worker // GATHER_WINDOW
    mesh = plsc.VectorSubcoreMesh(core_axis_name="c", subcore_axis_name="s")

    @functools.partial(
        pl.kernel, mesh=mesh, out_type=jax.ShapeDtypeStruct((n_rows, width), table.dtype),
        scratch_types=[pltpu.VMEM((GATHER_WINDOW,), jnp.int32),
                       pltpu.VMEM((GATHER_WINDOW, width), table.dtype),
                       pltpu.SemaphoreType.DMA])
    def gather(table_hbm, idx_hbm, out_hbm, idx_v, rows_v, sem):
        worker = lax.axis_index("s") * SC_CORES + lax.axis_index("c")

        @pl.loop(0, n_windows)
        def _(j):
            off = pl.multiple_of(worker * per_worker + j * GATHER_WINDOW, GATHER_WINDOW)
            pltpu.sync_copy(idx_hbm.at[pl.ds(off, GATHER_WINDOW)], idx_v)
            pltpu.async_copy(table_hbm.at[idx_v], rows_v, sem).wait()
            pltpu.sync_copy(rows_v, out_hbm.at[pl.ds(off, GATHER_WINDOW)])

    return gather(table, idx)


def _scatter_rows(rows, dest_w, n_out):
    n_tok, width = rows.shape
    per_worker = n_tok // GATHER_WINDOW // SC_WORKERS
    mesh = plsc.VectorSubcoreMesh(core_axis_name="c", subcore_axis_name="s")

    @functools.partial(
        pl.kernel, mesh=mesh, out_type=jax.ShapeDtypeStruct((n_out, width), rows.dtype),
        scratch_types=[pltpu.VMEM((SUBLANES, GATHER_WINDOW), jnp.int32),
                       pltpu.VMEM((GATHER_WINDOW, width), rows.dtype)])
    def scatter(rows_hbm, dest_hbm, out_hbm, idx_v, rows_v):
        worker = lax.axis_index("s") * SC_CORES + lax.axis_index("c")

        @pl.loop(0, per_worker)
        def _(j):
            win = worker * per_worker + j
            pltpu.sync_copy(dest_hbm.at[win], idx_v)
            pltpu.sync_copy(rows_hbm.at[pl.ds(pl.multiple_of(win * GATHER_WINDOW, GATHER_WINDOW),
                                              GATHER_WINDOW)], rows_v)
            for k in range(TOP_K):
                pltpu.sync_copy(rows_v, out_hbm.at[idx_v.at[k]])

    return scatter(rows, dest_w)


def _dest_kernel(start_ref, idx_ref, pos_ref, dest_ref):
    idx = idx_ref[0]
    dest = jnp.zeros_like(idx)
    for e in range(N_EXPERTS):
        dest = jnp.where(idx == e, start_ref[e], dest)
    dest_ref[0] = dest + pos_ref[0]


def _dest_rows(pad_start, idx, pos):
    B, _, T = idx.shape
    tt = min(2048, T)
    spec = pl.BlockSpec((1, SUBLANES, tt), lambda b, i, start: (b, 0, i))
    return pl.pallas_call(
        _dest_kernel,
        grid_spec=pltpu.PrefetchScalarGridSpec(num_scalar_prefetch=1, grid=(B, T // tt),
                                               in_specs=[spec, spec], out_specs=spec),
        out_shape=jax.ShapeDtypeStruct(idx.shape, jnp.int32),
        compiler_params=_params(("parallel", "parallel")),
        name="dest_rows",
    )(pad_start, idx, pos)


def _expert_kernel(be_ref, valid_ref, xs_ref, wg, wu, wd, ys_ref):
    n_valid = valid_ref[pl.program_id(0)]

    @pl.when(n_valid > 0)
    def _():
        half = D_MODEL // 2
        row = lax.broadcasted_iota(jnp.int32, xs_ref.shape, 0)
        a, b = _unpack_bf16_pairs(jnp.where(row < n_valid, xs_ref[...], 0))
        mm = lambda w: (jnp.dot(a, w[0, 0:half, :], preferred_element_type=F32)
                        + jnp.dot(b, w[0, half:D_MODEL, :], preferred_element_type=F32))
        hid = _silu(mm(wg)) * mm(wu)
        ys_ref[...] = _pack_bf16_pairs(_dot(hid, wd[0]))


def _experts(block_expert, block_valid, xs, wg, wu, wd):
    n_rows, half = xs.shape
    rows = pl.BlockSpec((MOE_BLOCK, half), lambda i, be, valid: (i, 0))
    wspec = lambda shape: pl.BlockSpec(shape, lambda i, be, valid: (be[i], 0, 0))
    return pl.pallas_call(
        _expert_kernel,
        grid_spec=pltpu.PrefetchScalarGridSpec(
            num_scalar_prefetch=2, grid=(n_rows // MOE_BLOCK,),
            in_specs=[rows, wspec((1, D_MODEL, D_EXPERT)), wspec((1, D_MODEL, D_EXPERT)),
                      wspec((1, D_EXPERT, D_MODEL))],
            out_specs=rows),
        out_shape=jax.ShapeDtypeStruct((n_rows, half), jnp.int32),
        compiler_params=_params(("arbitrary",)),
        name="experts",
    )(block_expert, block_valid, xs, wg, wu, wd)


def _moe_final_kernel(yg_ref, w_ref, h_ref, x1_ref, gt2, fg, sg, su, sd, o_ref):
    half = D_MODEL // 2
    a, b = _unpack_bf16_pairs(h_ref[0])
    mm = lambda w: (jnp.dot(a, w[0:half, :], preferred_element_type=F32)
                    + jnp.dot(b, w[half:D_MODEL, :], preferred_element_type=F32))
    shared = _dot(_silu(mm(sg)) * mm(su), sd[...])
    w = w_ref[0]
    lo = hi = None
    for k in range(TOP_K):
        ya, yb = _unpack_bf16_pairs(yg_ref[k, 0])
        wk = w[:, k:k + 1]
        lo = ya.astype(F32) * wk if lo is None else lo + ya.astype(F32) * wk
        hi = yb.astype(F32) * wk if hi is None else hi + yb.astype(F32) * wk
    x2 = x1_ref[0] + gt2[0] * (jnp.concatenate([lo, hi], axis=1) + shared)
    o_ref[0] = x2 * lax.rsqrt(jnp.mean(x2 * x2, axis=-1, keepdims=True) + EPS) * fg[...]


def _moe_final(yg, w_t, h2p, x1, gt2, fg, sg, su, sd):
    B, T, D = x1.shape
    tm = min(512, T)
    tok = lambda w: pl.BlockSpec((1, tm, w), lambda b, i: (b, i, 0))
    full = lambda shape: pl.BlockSpec(shape, lambda b, i: (0,) * len(shape))
    return pl.pallas_call(
        _moe_final_kernel,
        grid=(B, T // tm),
        in_specs=[pl.BlockSpec((TOP_K, 1, tm, D // 2), lambda b, i: (0, b, i, 0)), tok(SUBLANES),
                  tok(D // 2), tok(D), pl.BlockSpec((1, 1, D), lambda b, i: (b, 0, 0)), full((1, D)),
                  full((D, D_EXPERT)), full((D, D_EXPERT)), full((D_EXPERT, D))],
        out_specs=tok(D),
        out_shape=jax.ShapeDtypeStruct((B, T, D), F32),
        compiler_params=_params(("parallel", "parallel")),
        name="moe_final",
    )(yg, w_t, h2p, x1, gt2, fg, sg, su, sd)


def _moe_routed(h2p, idx, w, pos, counts, x1, gt2, p):
    B, T, half = h2p.shape
    n_slots = B * T * TOP_K
    n_rows = n_slots + N_EXPERTS * MOE_BLOCK
    cnt = counts[:, 0].astype(jnp.int32)
    padded = (cnt + MOE_BLOCK - 1) // MOE_BLOCK * MOE_BLOCK
    pad_end = jnp.cumsum(padded)
    pad_start = pad_end - padded
    dest = _dest_rows(pad_start, idx, pos)
    starts = jnp.arange(n_rows // MOE_BLOCK, dtype=jnp.int32) * MOE_BLOCK
    block_expert = jnp.minimum(jnp.sum((pad_end[None, :] <= starts[:, None]).astype(jnp.int32), axis=1),
                               N_EXPERTS - 1)
    block_valid = jnp.clip((pad_start + cnt)[block_expert] - starts, 0, MOE_BLOCK)
    dest_w = jnp.swapaxes(dest.reshape(B, SUBLANES, T // GATHER_WINDOW, GATHER_WINDOW), 1, 2)
    xs = _scatter_rows(h2p.reshape(B * T, half), dest_w.reshape(-1, SUBLANES, GATHER_WINDOW), n_rows)
    ys = _experts(block_expert, block_valid, xs, p["wg"], p["wu"], p["wd"])
    yg = _gather_rows(ys, jnp.swapaxes(dest[:, :TOP_K], 0, 1).reshape(-1)).reshape(TOP_K, B, T, half)
    return _moe_final(yg, jnp.swapaxes(w, 1, 2), h2p, x1, gt2, p["fg"], p["sg"], p["su"], p["sd"])


def _prepare_weights(w_in, shift_mu, a_w0, a_w_up, a_a0, a_a_up, a_g_up, a_k_k, a_k_a, a_r_k, a_gn_w,
                     a_gn_b, a_proj, b_conv_w, b_a_log, b_dt_bias, b_norm_g, b_proj, w_out, router_w,
                     router_bias, exp_gate, exp_up, exp_down, sh_gate, sh_up, sh_down, norm1_g, norm2_g,
                     final_g):
    W = A_WIDTH
    w = w_in[0]
    c0 = A_COLS
    c1 = c0 + 3 * B_WIDTH
    c2 = c1 + B_WIDTH
    c3 = c2 + 4 * B_HEADS
    pad = jnp.zeros((D_MODEL, SMALL_COLS - 4 * B_HEADS), F32)
    w_a = w[:, :c0].astype(BF16)
    w_b = jnp.concatenate([w[:, c0:c1], w[:, c2:c3], pad], axis=1).astype(BF16)
    w_c = jnp.concatenate([w[:, c1:c2], w[:, c3:]], axis=1).astype(BF16)
    zeros = jnp.zeros((2, A_RANK_W, W), F32)
    up_comb = jnp.concatenate([jnp.concatenate([a_w_up[0], zeros], axis=2),
                               jnp.concatenate([zeros, a_a_up[0]], axis=2)], axis=1)
    head = jnp.arange(W) // A_HEAD_DIM
    ones_bd = (head[:, None] == head[None, :]).astype(BF16)
    small = lambda v: jnp.zeros((1, SMALL_COLS), F32).at[0, 2 * B_HEADS:4 * B_HEADS].set(v.reshape(-1))
    return dict(
        w_a=w_a, w_b=w_b, w_c=w_c, mu=shift_mu[0].reshape(1, A_COLS), w0=a_w0[0], a0=a_a0[0],
        up_comb=up_comb, g_up=a_g_up[0], k_k=a_k_k[0].reshape(1, W), k_a=a_k_a[0].reshape(1, W),
        r_k=a_r_k[0].reshape(1, W), ones_bd=ones_bd, gn_w=a_gn_w[0].reshape(1, W),
        gn_b=a_gn_b[0].reshape(1, W), a_proj=a_proj[0].astype(BF16),
        conv_w=b_conv_w[0].astype(BF16).astype(F32),
        alog=small(b_a_log[0]), dtb=small(b_dt_bias[0]),
        bng=jnp.tile(b_norm_g[0], B_HEADS).reshape(1, B_WIDTH), b_proj=b_proj[0].astype(BF16),
        w_out=w_out[0].astype(BF16), rwt=router_w[0].T, rbias=router_bias[0].reshape(N_EXPERTS, 1),
        wg=exp_gate[0].astype(BF16), wu=exp_up[0].astype(BF16), wd=exp_down[0].astype(BF16),
        sg=sh_gate[0].astype(BF16), su=sh_up[0].astype(BF16), sd=sh_down[0].astype(BF16),
        n1g=norm1_g[0].reshape(1, D_MODEL), n2g=norm2_g[0].reshape(1, D_MODEL),
        fg=final_g.reshape(1, D_MODEL))


def _layer(x, mod, p):
    B = x.shape[0]
    sh1, sc1, gt1, sh2, sc2, gt2 = (m.reshape(B, 1, D_MODEL) for m in jnp.split(mod, 6, axis=-1))
    z_c = _inproj(x, p["n1g"], sc1, sh1, p["w_c"])
    r, kk, v, g, bonus, lw, kd, bd = _rwkv_prep(x, p["n1g"], sc1, sh1, p["w_a"], p["mu"], p["w0"], p["a0"],
                                                p["up_comb"], p["g_up"], p["k_k"], p["k_a"], p["r_k"],
                                                p["ones_bd"])
    yf, yb = _rwkv_scan(r, kk, v, lw, kd, bd)
    q, k, vv, bg = _gdn_prep(x, p["n1g"], sc1, sh1, p["w_b"], p["conv_w"], p["alog"], p["dtb"])
    of, ob = _gdn_scan(q, k, vv, bg)
    x1, h2p, idx, w, pos, counts = _mixer_post(yf, yb, bonus, g, of, ob, z_c, x, gt1, sc2, sh2, p["gn_w"],
                                               p["gn_b"], p["ones_bd"], p["a_proj"], p["bng"], p["b_proj"],
                                               p["w_out"], p["n2g"], p["rwt"], p["rbias"])
    return _moe_routed(h2p, idx, w, pos, counts, x1, gt2, p)


def kernel(x_prompt, x_sample, c_prompt, c_sample, ada_w, ada_b, norm1_g, norm2_g, w_in, shift_mu, a_w0, a_w_up, a_a0, a_a_up, a_g_up, a_k_k, a_k_a, a_r_k, a_gn_w, a_gn_b, a_proj, b_conv_w, b_a_log, b_dt_bias, b_norm_g, b_proj, w_out, router_w, router_bias, exp_gate, exp_up, exp_down, sh_gate, sh_up, sh_down, final_g):
    p = _prepare_weights(w_in, shift_mu, a_w0, a_w_up, a_a0, a_a_up, a_g_up, a_k_k, a_k_a, a_r_k, a_gn_w,
                         a_gn_b, a_proj, b_conv_w, b_a_log, b_dt_bias, b_norm_g, b_proj, w_out, router_w,
                         router_bias, exp_gate, exp_up, exp_down, sh_gate, sh_up, sh_down, norm1_g,
                         norm2_g, final_g)
    nb_p, nb_s = c_prompt.shape[0], c_sample.shape[0]
    rows = -(-(nb_p + nb_s) // SUBLANES) * SUBLANES
    c_all = jnp.concatenate([c_prompt, c_sample, jnp.zeros((rows - nb_p - nb_s, D_MODEL), F32)], axis=0)
    mod = _adaln_mod(c_all, ada_w[0], ada_b[0])
    y_prompt = _layer(x_prompt, mod[:nb_p], p)
    y_sample = _layer(x_sample, mod[nb_p:nb_p + nb_s], p)
    return (y_prompt, y_sample)
```

```python
import functools

import jax
import jax.numpy as jnp
from jax import lax
from jax.experimental import pallas as pl
from jax.experimental.pallas import tpu as pltpu
from jax.experimental.pallas import tpu_sc as plsc

F32 = jnp.float32
BF16 = jnp.bfloat16
ACT = jnp.bfloat16
HIGHEST = lax.Precision.HIGHEST

D_MODEL = 1024
A_HEADS = 8
A_HEAD_DIM = 64
A_WIDTH = A_HEADS * A_HEAD_DIM
A_RANK_W = 64
A_RANK_A = 64
A_RANK_G = 128
A_GN_EPS = 64e-5
A_COLS = 3 * A_WIDTH + A_RANK_W + A_RANK_A + A_RANK_G
B_HEADS = 8
B_HEAD_DIM = 128
B_WIDTH = B_HEADS * B_HEAD_DIM
B_CONV = 5
CHUNK = 64
SCAN_SUB = 4
GDN_COL_GROUP = 512
MIXER_SUB = 2
N_EXPERTS = 64
TOP_K = 6
N_GROUPS = 8
TOPK_GROUPS = 4
GROUP_SIZE = N_EXPERTS // N_GROUPS
D_EXPERT = 256
ROUTED_SCALE = 2.5
EPS = 1e-6
LANES = 128
SUBLANES = 8
HALO = 16
SMALL_COLS = LANES
MOE_BLOCK = 512
SC_CORES = 2
SC_SUBCORES = 16
SC_WORKERS = SC_CORES * SC_SUBCORES
GATHER_WINDOW = 64
VMEM_LIMIT = 56 * 1024 * 1024
NEG_INF = float("-inf")


def _dot(a, b):
    return jnp.dot(a.astype(BF16), b.astype(BF16), preferred_element_type=F32)


def _dot_nt(a, b):
    return lax.dot_general(a.astype(BF16), b.astype(BF16), (((1,), (1,)), ((), ())),
                           preferred_element_type=F32)


def _dot_tn(a, b):
    return lax.dot_general(a.astype(BF16), b.astype(BF16), (((0,), (0,)), ((), ())),
                           preferred_element_type=F32)


def _dot_hi(a, b):
    return jnp.dot(a, b, precision=HIGHEST, preferred_element_type=F32)


def _split(x):
    hi = x.astype(BF16)
    return hi, (x - hi.astype(F32)).astype(BF16)


def _dot3(a, b, dims=(((1,), (0,)), ((), ()))):
    ah, al = _split(a)
    bh, bl = _split(b)
    d = lambda u, v: lax.dot_general(u, v, dims, preferred_element_type=F32)
    return d(ah, bh) + (d(ah, bl) + d(al, bh))


def _seg_sum(x, ones_bd):
    hi = x.astype(BF16)
    lo = (x - hi.astype(F32)).astype(BF16)
    return (jnp.dot(hi, ones_bd, preferred_element_type=F32)
            + jnp.dot(lo, ones_bd, preferred_element_type=F32))


def _softplus(x):
    return jnp.maximum(x, 0.0) + jnp.log1p(jnp.exp(-jnp.abs(x)))


def _sigmoid(x):
    return 1.0 / (1.0 + jnp.exp(-x))


def _silu(x):
    return x * _sigmoid(x)


def _tri_inverse(a_list, eye):
    n = a_list[0].shape[0]
    xs = [(-a).astype(BF16) for a in a_list]
    ts = [eye - a for a in a_list]
    xs = [jnp.dot(x, x, preferred_element_type=F32).astype(BF16) for x in xs]
    power = 2
    while 2 * power < CHUNK:
        both = [jnp.dot(x, jnp.concatenate([t.astype(BF16), x], axis=1), preferred_element_type=F32)
                for t, x in zip(ts, xs)]
        ts = [t + b[:, 0:n] for t, b in zip(ts, both)]
        xs = [b[:, n:2 * n].astype(BF16) for b in both]
        power *= 2
    return [t + jnp.dot(x, t.astype(BF16), preferred_element_type=F32) for t, x in zip(ts, xs)]


def _params(sem, **extra):
    return pltpu.CompilerParams(dimension_semantics=sem, vmem_limit_bytes=VMEM_LIMIT, **extra)


def _mod_kernel(c_ref, w_ref, b_ref, o_ref):
    c = c_ref[...]
    o_ref[...] = _dot_hi(_silu(c), w_ref[...]) + b_ref[...]


def _adaln_mod(c, ada_w, ada_b):
    rows = c.shape[0]
    n = ada_w.shape[1]
    return pl.pallas_call(
        _mod_kernel,
        grid=(n // D_MODEL,),
        in_specs=[pl.BlockSpec((rows, D_MODEL), lambda j: (0, 0)),
                  pl.BlockSpec((D_MODEL, D_MODEL), lambda j: (0, j)),
                  pl.BlockSpec((1, D_MODEL), lambda j: (0, j))],
        out_specs=pl.BlockSpec((rows, D_MODEL), lambda j: (0, j)),
        out_shape=jax.ShapeDtypeStruct((rows, n), F32),
        compiler_params=_params(("arbitrary",)),
        name="adaln_mod",
    )(c, ada_w, ada_b.reshape(1, n))


def _rms_mod(x, g, sc, sh):
    y = x * lax.rsqrt(jnp.mean(x * x, axis=-1, keepdims=True) + EPS)
    return (y * g) * (1.0 + sc) + sh


def _inproj_kernel(x_ref, g_ref, sc_ref, sh_ref, w_ref, o_ref):
    h = _rms_mod(x_ref[0], g_ref[...], sc_ref[0], sh_ref[0]).astype(BF16)
    o_ref[0] = jnp.dot(h, w_ref[...], preferred_element_type=F32).astype(o_ref.dtype)


def _inproj(x, g, sc, sh, w):
    B, T, D = x.shape
    n = w.shape[1]
    tm = min(512, T)
    return pl.pallas_call(
        _inproj_kernel,
        grid=(B, T // tm),
        in_specs=[pl.BlockSpec((1, tm, D), lambda b, i: (b, i, 0)),
                  pl.BlockSpec((1, D), lambda b, i: (0, 0)),
                  pl.BlockSpec((1, 1, D), lambda b, i: (b, 0, 0)),
                  pl.BlockSpec((1, 1, D), lambda b, i: (b, 0, 0)),
                  pl.BlockSpec((D, n), lambda b, i: (0, 0))],
        out_specs=pl.BlockSpec((1, tm, n), lambda b, i: (b, i, 0)),
        out_shape=jax.ShapeDtypeStruct((B, T, n), ACT),
        compiler_params=_params(("parallel", "parallel")),
        name="in_proj",
    )(x, g, sc, sh, w)


def _halo_specs(tt, width, col_block, seq_len):
    nb = tt // HALO
    last = seq_len // HALO - 1
    return [
        pl.BlockSpec((1, tt, width), lambda b, i: (b, i, col_block)),
        pl.BlockSpec((1, HALO, width), lambda b, i: (b, jnp.maximum(i * nb - 1, 0), col_block)),
        pl.BlockSpec((1, HALO, width), lambda b, i: (b, jnp.minimum((i + 1) * nb, last), col_block)),
    ]


def _halo_rows(zp_ref, zn_ref):
    return zp_ref[0].astype(F32)[HALO - SUBLANES:HALO], zn_ref[0].astype(F32)[0:SUBLANES]


def _normed_with_halo(x_ref, xp_ref, xn_ref, g_ref, sc_ref, sh_ref):
    prev8, next8 = _halo_rows(xp_ref, xn_ref)
    x_ext = jnp.concatenate([prev8, x_ref[0], next8], axis=0)
    return _rms_mod(x_ext, g_ref[...], sc_ref[0], sh_ref[0]).astype(BF16)


def _rwkv_prep_kernel(x_ref, xp_ref, xn_ref, g1_ref, sc_ref, sh_ref, w_ref,
                      mu_ref, w0_ref, a0_ref, up_ref, gup_ref, kk_ref, ka_ref, rk_ref, ones_ref,
                      r_out, kk_out, v_out, g_out, bonus_out, lw_out, kd_out, bd_out, ext_scr):
    tt = x_ref.shape[1]
    i = pl.program_id(1)
    last = pl.num_programs(1) - 1
    h = _normed_with_halo(x_ref, xp_ref, xn_ref, g1_ref, sc_ref, sh_ref)
    project = lambda c0, c1: jnp.dot(h, w_ref[:, c0:c1], preferred_element_type=F32)
    row = lax.broadcasted_iota(jnp.int32, (tt + 2 * SUBLANES, 1), 0)
    inside = ((row >= SUBLANES) | (i > 0)) & ((row < tt + SUBLANES) | (i < last))
    W = A_WIDTH

    def shift_mix(z_ext, c0, c1):
        z_ext = z_ext.astype(ACT).astype(F32)
        ext_scr[:, c0:c1] = jnp.where(inside, z_ext, 0.0)
        z = ext_scr[pl.ds(SUBLANES, tt), c0:c1]
        z_prev = ext_scr[pl.ds(SUBLANES - 1, tt), c0:c1]
        z_next = ext_scr[pl.ds(SUBLANES + 1, tt), c0:c1]
        return z + (0.5 * (z_prev + z_next) - z) * mu_ref[:, c0:c1]

    groups = [(3 * W, A_COLS), (W, 2 * W), (0, W), (2 * W, 3 * W)]
    mixed = []
    z_nxt = project(*groups[0])
    for gi, (c0, c1) in enumerate(groups):
        z_cur = z_nxt
        if gi + 1 < len(groups):
            z_nxt = project(*groups[gi + 1])
        mixed.append(shift_mix(z_cur, c0, c1))
    zwag, zk, zr, zv = mixed
    zwa = zwag[:, 0:LANES]
    zg = zwag[:, LANES:2 * LANES]
    ones_bd = ones_ref[...]
    kk_raw = zk * kk_ref[...]
    kk = kk_raw * lax.rsqrt(_seg_sum(kk_raw * kk_raw, ones_bd) + 1e-6)
    lane = lax.broadcasted_iota(jnp.int32, zwa.shape, 1)
    lhs = jnp.where(lane < A_RANK_W, jnp.tanh(zwa), zwa)
    k_sum = jnp.zeros_like(zk)
    for d in range(2):
        up = _dot(lhs, up_ref[d])
        wl = w0_ref[d:d + 1, :] + up[:, 0:W]
        w_log = -_softplus(-wl) - 0.5
        lw_out[d, 0] = -jnp.exp(w_log)
        a = _sigmoid(a0_ref[d:d + 1, :] + up[:, W:2 * W])
        k_d = zk * (1.0 + (a - 1.0) * ka_ref[...])
        kd_out[d, 0] = k_d.astype(kd_out.dtype)
        bd_out[d, 0] = (kk * a).astype(bd_out.dtype)
        k_sum = k_sum + k_d
    r_out[0] = zr.astype(r_out.dtype)
    kk_out[0] = kk.astype(kk_out.dtype)
    v_out[0] = zv.astype(v_out.dtype)
    g_out[0] = _dot(_sigmoid(zg), gup_ref[...]).astype(g_out.dtype)
    bonus_out[0] = (_seg_sum(zr * k_sum * rk_ref[...], ones_bd) * zv).astype(bonus_out.dtype)


def _front_specs(tt, seq_len, n_cols):
    modrow = pl.BlockSpec((1, 1, D_MODEL), lambda b, i: (b, 0, 0))
    return _halo_specs(tt, D_MODEL, 0, seq_len) + [
        pl.BlockSpec((1, D_MODEL), lambda b, i: (0, 0)), modrow, modrow,
        pl.BlockSpec((D_MODEL, n_cols), lambda b, i: (0, 0))]


def _rwkv_prep(x, g1, sc, sh, w_a, mu, w0, a0, up_comb, g_up, k_k, k_a, r_k, ones_bd):
    B, T, _ = x.shape
    tt = min(512, T)
    W = A_WIDTH
    full = lambda shape: pl.BlockSpec(shape, lambda b, i: (0,) * len(shape))
    tok = pl.BlockSpec((1, tt, W), lambda b, i: (b, i, 0))
    tok2 = pl.BlockSpec((2, 1, tt, W), lambda b, i: (0, b, i, 0))
    s1 = jax.ShapeDtypeStruct((B, T, W), ACT)
    s2 = jax.ShapeDtypeStruct((2, B, T, W), ACT)
    lw = jax.ShapeDtypeStruct((2, B, T, W), F32)
    return pl.pallas_call(
        _rwkv_prep_kernel,
        grid=(B, T // tt),
        in_specs=_front_specs(tt, T, A_COLS) + [
            full((1, A_COLS)), full((2, W)), full((2, W)), full((2, LANES, 2 * W)),
            full((A_RANK_G, W)), full((1, W)), full((1, W)), full((1, W)), full((W, W))],
        out_specs=[tok, tok, tok, tok, tok, tok2, tok2, tok2],
        out_shape=[s1, s1, s1, s1, s1, lw, s2, s2],
        scratch_shapes=[pltpu.VMEM((tt + 2 * SUBLANES, A_COLS), F32)],
        compiler_params=_params(("parallel", "parallel")),
        name="rwkv_prep",
    )(x, x, x, g1, sc, sh, w_a, mu, w0, a0, up_comb, g_up, k_k, k_a, r_k, ones_bd)


def _chunk_masks(reverse):
    n = 2 * CHUNK
    row = lax.broadcasted_iota(jnp.int32, (n, n), 0)
    col = lax.broadcasted_iota(jnp.int32, (n, n), 1)
    same = (row // CHUNK) == (col // CHUNK)
    ti, tj = row % CHUNK, col % CHUNK
    if reverse:
        return same & (ti < tj), same & (ti <= tj), row == col
    return same & (ti > tj), same & (ti >= tj), row == col


def _cumsum_matrix(reverse):
    row = lax.broadcasted_iota(jnp.int32, (CHUNK, CHUNK), 0)
    col = lax.broadcasted_iota(jnp.int32, (CHUNK, CHUNK), 1)
    return ((row <= col) if reverse else (row >= col)).astype(F32)


def _rwkv_chunk_operands(r, kk, v, k, b, cs, lw, reverse):
    C = CHUNK
    cs_end = cs[0:1, :] if reverse else cs[C - 1:C, :]
    m0 = lax.broadcasted_iota(jnp.int32, (C, LANES), 1) < A_HEAD_DIM

    def stack(x):
        return jnp.concatenate([jnp.where(m0, x, 0.0), jnp.where(m0, 0.0, x)], axis=0)

    g_inv = jnp.exp(-cs)
    g_tail = jnp.exp(cs_end - cs)
    strict, incl, _ = _chunk_masks(reverse)
    return dict(rg=stack(r * jnp.exp(cs)), kkg=stack(kk * jnp.exp(cs - lw)), ki=stack(k * g_inv),
                bi=stack(b * g_inv), kt=stack(k * g_tail), bt=stack(b * g_tail), vs=stack(v),
                g_end=jnp.exp(cs_end), strict=strict, incl=incl)


def _rwkv_chunks(ops, hts):
    C = CHUNK
    n = 2 * C
    eye = _chunk_masks(False)[2].astype(F32)
    ps = [_dot_nt(jnp.concatenate([o["kkg"], o["rg"]], axis=0), jnp.concatenate([o["bi"], o["ki"]], axis=0))
          for o in ops]
    a_ab = [jnp.where(o["strict"], p[0:n, 0:n], 0.0) for o, p in zip(ops, ps)]
    a_rb = [jnp.where(o["incl"], p[n:2 * n, 0:n], 0.0).astype(BF16) for o, p in zip(ops, ps)]
    a_k = [jnp.concatenate([jnp.where(o["strict"], p[0:n, n:2 * n], 0.0),
                            jnp.where(o["incl"], p[n:2 * n, n:2 * n], 0.0)], axis=0).astype(BF16)
           for o, p in zip(ops, ps)]
    vsb = [o["vs"].astype(BF16) for o in ops]
    a_kv = [jnp.dot(a, v, preferred_element_type=F32) for a, v in zip(a_k, vsb)]
    akv = [x[0:n] for x in a_kv]
    arkv = [x[n:2 * n] for x in a_kv]
    tinv = _tri_inverse(a_ab, eye)
    wu = [_dot(t, jnp.concatenate([o["kkg"], x], axis=1)).astype(BF16)
          for t, o, x in zip(tinv, ops, akv)]
    r2 = [jnp.dot(a, w, preferred_element_type=F32) for a, w in zip(a_rb, wu)]
    btb = [o["bt"].astype(BF16) for o in ops]
    btw = [_dot_tn(b, w[:, 0:LANES]) for b, w in zip(btb, wu)]
    hloc = [_dot_tn(jnp.concatenate([v, -w[:, LANES:2 * LANES]], axis=0),
                    jnp.concatenate([o["kt"].astype(BF16), b], axis=0))
            for v, o, w, b in zip(vsb, ops, wu, btb)]
    ys, hts_new = [], []
    for o, r, yl, ht, bw, hl in zip(ops, r2, arkv, hts, btw, hloc):
        q_s = o["rg"] - r[:, 0:LANES]
        yloc_s = yl - r[:, LANES:2 * LANES]
        ys.append(yloc_s[0:C] + yloc_s[C:n] + _dot_nt(q_s[0:C] + q_s[C:n], ht))
        hts_new.append(ht * o["g_end"] - _dot_nt(ht, bw) + hl)
    return ys, hts_new


def _rwkv_scan_kernel(rf, kkf, vf, lwf, kf, bf, rb, kkb, vb, lwb, kb, bb, yf_out, yb_out, h_scr):
    @pl.when(pl.program_id(1) == 0)
    def _():
        h_scr[...] = jnp.zeros_like(h_scr)

    n_pairs = A_WIDTH // LANES
    sls = [slice(i * LANES, (i + 1) * LANES) for i in range(n_pairs)]
    f32 = lambda a: a.astype(F32)
    hts = [h_scr[d, i] for d in range(2) for i in range(n_pairs)]
    for j in range(SCAN_SUB):
        ops = []
        for d, (r_ref, kk_ref, v_ref, lw_ref, k_ref, b_ref) in enumerate(
                ((rf, kkf, vf, lwf, kf, bf), (rb, kkb, vb, lwb, kb, bb))):
            reverse = d == 1
            rows = pl.ds((SCAN_SUB - 1 - j if reverse else j) * CHUNK, CHUNK)
            lw_all = lw_ref[0, 0, rows, :]
            cs_all = _dot_hi(_cumsum_matrix(reverse), lw_all)
            ops += [_rwkv_chunk_operands(f32(r_ref[0, rows, sl]), f32(kk_ref[0, rows, sl]),
                                         f32(v_ref[0, rows, sl]), f32(k_ref[0, 0, rows, sl]),
                                         f32(b_ref[0, 0, rows, sl]), cs_all[:, sl], lw_all[:, sl], reverse)
                    for sl in sls]
        ys, hts = _rwkv_chunks(ops, hts)
        for d, y_out in enumerate((yf_out, yb_out)):
            rows = pl.ds((SCAN_SUB - 1 - j if d == 1 else j) * CHUNK, CHUNK)
            for i, sl in enumerate(sls):
                y_out[0, rows, sl] = ys[d * n_pairs + i].astype(y_out.dtype)
    for d in range(2):
        for i in range(n_pairs):
            h_scr[d, i] = hts[d * n_pairs + i]


def _rwkv_scan(r, kk, v, lw, kd, bd):
    B, T, W = r.shape
    rows = CHUNK * SCAN_SUB
    N = T // rows
    fwd = pl.BlockSpec((1, rows, W), lambda b, c: (b, c, 0))
    bwd = pl.BlockSpec((1, rows, W), lambda b, c: (b, N - 1 - c, 0))
    fwd2 = pl.BlockSpec((1, 1, rows, W), lambda b, c: (0, b, c, 0))
    bwd2 = pl.BlockSpec((1, 1, rows, W), lambda b, c: (1, b, N - 1 - c, 0))
    out = jax.ShapeDtypeStruct((B, T, W), ACT)
    return pl.pallas_call(
        _rwkv_scan_kernel,
        grid=(B, N),
        in_specs=[fwd, fwd, fwd, fwd2, fwd2, fwd2, bwd, bwd, bwd, bwd2, bwd2, bwd2],
        out_specs=[fwd, bwd],
        out_shape=[out, out],
        scratch_shapes=[pltpu.VMEM((2, W // LANES, LANES, LANES), F32)],
        compiler_params=_params(("parallel", "arbitrary")),
        name="rwkv_scan",
    )(r, kk, v, lw, kd, bd, r, kk, v, lw, kd, bd)


def _gdn_prep_kernel(x_ref, xp_ref, xn_ref, g1_ref, sc_ref, sh_ref, w_ref, cw_ref, alog_ref, dtb_ref,
                     q_out, k_out, v_out, bg_out, acc_scr):
    tt = x_ref.shape[1]
    i = pl.program_id(1)
    last = pl.num_programs(1) - 1
    half = B_CONV // 2
    Wq = 3 * B_WIDTH
    h = _normed_with_halo(x_ref, xp_ref, xn_ref, g1_ref, sc_ref, sh_ref)
    project = lambda c0, c1: jnp.dot(h, w_ref[:, c0:c1], preferred_element_type=F32)
    offsets = [j - half for j in range(B_CONV) if j != half]
    row = lax.broadcasted_iota(jnp.int32, (len(offsets) * tt, tt), 0)
    col = lax.broadcasted_iota(jnp.int32, (len(offsets) * tt, tt), 1)
    blk = row // tt
    off = jnp.where(blk < half, blk - half, blk - half + 1)
    shift_rows = (col == row - blk * tt + off).astype(BF16)
    r8 = lax.broadcasted_iota(jnp.int32, (SUBLANES, 1), 0)

    def conv_group(z_ext, c0, c1):
        zb = z_ext[SUBLANES:SUBLANES + tt].astype(BF16)
        shifted = jnp.dot(shift_rows, zb, preferred_element_type=F32)
        acc = zb.astype(F32) * cw_ref[half:half + 1, c0:c1]
        for m, o in enumerate(offsets):
            acc = acc + shifted[m * tt:(m + 1) * tt] * cw_ref[o + half:o + half + 1, c0:c1]
        acc_scr[:, c0:c1] = acc
        prev8 = jnp.where(i == 0, 0.0, z_ext[0:SUBLANES]).astype(BF16).astype(F32)
        next8 = jnp.where(i == last, 0.0, z_ext[SUBLANES + tt:]).astype(BF16).astype(F32)
        top = bot = None
        for o in offsets:
            w = cw_ref[o + half:o + half + 1, c0:c1]
            if o < 0:
                f = jnp.where(r8 < -o, pltpu.roll(prev8, -o, axis=0), 0.0) * w
                top = f if top is None else top + f
            else:
                f = jnp.where(r8 >= SUBLANES - o, pltpu.roll(next8, SUBLANES - o, axis=0), 0.0) * w
                bot = f if bot is None else bot + f
        acc_scr[pl.ds(0, SUBLANES), c0:c1] += top
        acc_scr[pl.ds(tt - SUBLANES, SUBLANES), c0:c1] += bot
        return _silu(acc_scr[:, c0:c1])

    bounds = list(range(0, Wq + 1, GDN_COL_GROUP))
    z_next = project(bounds[0], bounds[1])
    for gi in range(len(bounds) - 1):
        c0, c1 = bounds[gi], bounds[gi + 1]
        z_cur = z_next
        z_next = project(bounds[gi + 1], bounds[gi + 2]) if gi + 2 < len(bounds) else project(Wq, Wq + SMALL_COLS)
        act = conv_group(z_cur, c0, c1)
        for hh in range(GDN_COL_GROUP // B_HEAD_DIM):
            x = act[:, hh * B_HEAD_DIM:(hh + 1) * B_HEAD_DIM]
            lo = c0 % B_WIDTH + hh * B_HEAD_DIM
            if c0 < 2 * B_WIDTH:
                x = x * lax.rsqrt(jnp.sum(x * x, axis=-1, keepdims=True) + 1e-6)
                if c0 < B_WIDTH:
                    x = x * (B_HEAD_DIM ** -0.5)
            out = (q_out, k_out, v_out)[c0 // B_WIDTH]
            out[0, :, lo:lo + B_HEAD_DIM] = x.astype(out.dtype)
    s = z_next[SUBLANES:SUBLANES + tt]
    lane = lax.broadcasted_iota(jnp.int32, s.shape, 1)
    beta = _sigmoid(s)
    g = -jnp.exp(alog_ref[...]) * _softplus(s + dtb_ref[...])
    bg_out[0] = jnp.where(lane < 2 * B_HEADS, beta, g)


def _gdn_prep(x, g1, sc, sh, w_b, conv_w, alog_vec, dtb_vec):
    B, T, _ = x.shape
    Wq = 3 * B_WIDTH
    tt = min(256, T)
    full = lambda shape: pl.BlockSpec(shape, lambda b, i: (0,) * len(shape))
    tok = pl.BlockSpec((1, tt, B_WIDTH), lambda b, i: (b, i, 0))
    s1 = jax.ShapeDtypeStruct((B, T, B_WIDTH), ACT)
    return pl.pallas_call(
        _gdn_prep_kernel,
        grid=(B, T // tt),
        in_specs=_front_specs(tt, T, Wq + SMALL_COLS) + [full((B_CONV, Wq)), full((1, SMALL_COLS)),
                                                          full((1, SMALL_COLS))],
        out_specs=[tok, tok, tok, pl.BlockSpec((1, tt, SMALL_COLS), lambda b, i: (b, i, 0))],
        out_shape=[s1, s1, s1, jax.ShapeDtypeStruct((B, T, SMALL_COLS), F32)],
        scratch_shapes=[pltpu.VMEM((tt, Wq), F32)],
        compiler_params=_params(("parallel", "parallel")),
        name="gdn_prep",
    )(x, x, x, g1, sc, sh, w_b, conv_w, alog_vec, dtb_vec)


def _gdn_chunk_operands(q0, q1, k0, k1, v0, v1, gc2, gct2, bg2, j0, j1, reverse):
    C = CHUNK
    n = 2 * C
    strict, incl, _ = _chunk_masks(reverse)
    top = lax.broadcasted_iota(jnp.int32, (n, 1), 0) < C
    left = lax.broadcasted_iota(jnp.int32, (1, n), 1) < C
    g0, g1 = j0 + 2 * B_HEADS, j1 + 2 * B_HEADS
    gcol = jnp.where(top, gc2[:, g0:g0 + 1], gc2[:, g1:g1 + 1])
    grow = jnp.where(left, gct2[g0:g0 + 1, :], gct2[g1:g1 + 1, :])
    beta = jnp.where(top, bg2[:, j0:j0 + 1], bg2[:, j1:j1 + 1])
    e = 0 if reverse else C - 1
    glast = jnp.where(top, gc2[e:e + 1, g0:g0 + 1], gc2[e:e + 1, g1:g1 + 1])
    ks = jnp.concatenate([k0, k1], axis=0)
    qs = jnp.concatenate([q0, q1], axis=0)
    vs = jnp.concatenate([v0, v1], axis=0)
    egc = jnp.exp(gcol)
    kb = ks * beta
    return dict(gam=jnp.exp(jnp.where(incl, gcol - grow, NEG_INF)), strict=strict, ks=ks,
                kbq=jnp.concatenate([kb, qs], axis=0),
                rhs=jnp.concatenate([vs * beta, kb * egc], axis=1),
                qh=(qs * egc).astype(BF16), ktail=(ks * jnp.exp(glast - gcol)).astype(BF16),
                decay=jnp.exp(glast))


def _gdn_chunks(ops, states):
    C = CHUNK
    n = 2 * C
    eye = _chunk_masks(False)[2].astype(F32)
    ps = [_dot_nt(o["kbq"], o["ks"]) for o in ops]
    lower = [jnp.where(o["strict"], p[0:n] * o["gam"], 0.0) for o, p in zip(ops, ps)]
    aqk = [(p[n:2 * n] * o["gam"]).astype(BF16) for o, p in zip(ops, ps)]
    tinv = [t.astype(BF16) for t in _tri_inverse(lower, eye)]
    uw = [jnp.dot(t, o["rhs"].astype(BF16), preferred_element_type=F32) for t, o in zip(tinv, ops)]
    lx = []
    for l, x in zip(lower, uw):
        lh, ll = _split(l)
        xh, xl = _split(x)
        first = jnp.dot(lh, jnp.concatenate([xh, xl], axis=1), preferred_element_type=F32)
        w2 = x.shape[1]
        lx.append(first[:, 0:w2] + (first[:, w2:2 * w2] + jnp.dot(ll, xh, preferred_element_type=F32)))
    resid = [o["rhs"] - x - y for o, x, y in zip(ops, uw, lx)]
    uw = [x + jnp.dot(t, r.astype(BF16), preferred_element_type=F32) for x, t, r in zip(uw, tinv, resid)]
    sb = [(s0.astype(BF16), s1.astype(BF16)) for s0, s1 in states]
    wq = [[jnp.dot(jnp.concatenate([u[h * C:(h + 1) * C, LANES:].astype(BF16), o["qh"][h * C:(h + 1) * C]],
                                   axis=0), s[h], preferred_element_type=F32) for h in range(2)]
          for u, o, s in zip(uw, ops, sb)]
    ws = [jnp.concatenate([x[0][0:C], x[1][0:C]], axis=0) for x in wq]
    qss = [jnp.concatenate([x[0][C:n], x[1][C:n]], axis=0) for x in wq]
    v_new = [(u[:, 0:LANES] - w).astype(BF16) for u, w in zip(uw, ws)]
    outs = [q + jnp.dot(a, v, preferred_element_type=F32) for q, a, v in zip(qss, aqk, v_new)]
    new_states = []
    for o, v, (s0, s1) in zip(ops, v_new, states):
        new_states.append((s0 * o["decay"][0:1, :] + _dot_tn(o["ktail"][0:C], v[0:C]),
                           s1 * o["decay"][C:C + 1, :] + _dot_tn(o["ktail"][C:n], v[C:n])))
    return [(x[0:C], x[C:n]) for x in outs], new_states


def _gdn_scan_kernel(qf, kf, vf, bgf, qb, kb, vb, bgb, of_out, ob_out, s_scr):
    @pl.when(pl.program_id(1) == 0)
    def _():
        s_scr[...] = jnp.zeros_like(s_scr)

    D = B_HEAD_DIM
    n_pairs = B_HEADS // 2
    sl = lambda h: slice(h * D, (h + 1) * D)
    f32 = lambda a: a.astype(F32)
    states = [(s_scr[d, 2 * i], s_scr[d, 2 * i + 1]) for d in range(2) for i in range(n_pairs)]
    for j in range(SCAN_SUB):
        ops = []
        for d, (q_ref, k_ref, v_ref, bg_ref) in enumerate(((qf, kf, vf, bgf), (qb, kb, vb, bgb))):
            reverse = d == 1
            rows = pl.ds((SCAN_SUB - 1 - j if reverse else j) * CHUNK, CHUNK)
            bg = bg_ref[0, rows, :]
            gc = _dot_hi(_cumsum_matrix(reverse), bg)
            bg2 = jnp.concatenate([bg, bg], axis=0)
            gc2 = jnp.concatenate([gc, gc], axis=0)
            gct2 = gc2.T
            for i in range(n_pairs):
                h0, h1 = 2 * i, 2 * i + 1
                ops.append(_gdn_chunk_operands(
                    f32(q_ref[0, rows, sl(h0)]), f32(q_ref[0, rows, sl(h1)]), f32(k_ref[0, rows, sl(h0)]),
                    f32(k_ref[0, rows, sl(h1)]), f32(v_ref[0, rows, sl(h0)]), f32(v_ref[0, rows, sl(h1)]),
                    gc2, gct2, bg2, d * B_HEADS + h0, d * B_HEADS + h1, reverse))
        outs, states = _gdn_chunks(ops, states)
        for d, o_out in enumerate((of_out, ob_out)):
            rows = pl.ds((SCAN_SUB - 1 - j if d == 1 else j) * CHUNK, CHUNK)
            for i in range(n_pairs):
                o0, o1 = outs[d * n_pairs + i]
                o_out[0, rows, sl(2 * i)] = o0.astype(o_out.dtype)
                o_out[0, rows, sl(2 * i + 1)] = o1.astype(o_out.dtype)
    for d in range(2):
        for i in range(n_pairs):
            s_scr[d, 2 * i], s_scr[d, 2 * i + 1] = states[d * n_pairs + i]


def _gdn_scan(q, k, v, bg):
    B, T, W = q.shape
    rows = CHUNK * SCAN_SUB
    N = T // rows
    fwd = pl.BlockSpec((1, rows, W), lambda b, c: (b, c, 0))
    bwd = pl.BlockSpec((1, rows, W), lambda b, c: (b, N - 1 - c, 0))
    sfwd = pl.BlockSpec((1, rows, SMALL_COLS), lambda b, c: (b, c, 0))
    sbwd = pl.BlockSpec((1, rows, SMALL_COLS), lambda b, c: (b, N - 1 - c, 0))
    out = jax.ShapeDtypeStruct((B, T, W), ACT)
    return pl.pallas_call(
        _gdn_scan_kernel,
        grid=(B, N),
        in_specs=[fwd, fwd, fwd, sfwd, bwd, bwd, bwd, sbwd],
        out_specs=[fwd, bwd],
        out_shape=[out, out],
        scratch_shapes=[pltpu.VMEM((2, B_HEADS, B_HEAD_DIM, B_HEAD_DIM), F32)],
        compiler_params=_params(("parallel", "arbitrary")),
        name="gdn_scan",
    )(q, k, v, bg, q, k, v, bg)


def _pack_bf16_pairs(x):
    n = x.shape[1] // 2
    hi = lax.bitcast_convert_type(x[:, :n].astype(BF16).astype(F32), jnp.int32)
    lo = lax.bitcast_convert_type(x[:, n:].astype(BF16).astype(F32), jnp.int32)
    return hi | lax.shift_right_logical(lo, 16)


def _unpack_bf16_pairs(p):
    a = lax.bitcast_convert_type(p & jnp.int32(-65536), F32)
    b = lax.bitcast_convert_type(lax.shift_left(p, 16), F32)
    return a.astype(BF16), b.astype(BF16)


def _route(scores, biased, base):
    tt = scores.shape[-1]
    shape3 = (N_GROUPS, GROUP_SIZE, tt)
    s3 = scores.reshape(shape3)
    b3 = biased.reshape(shape3)
    jid = lax.broadcasted_iota(jnp.int32, shape3, 1).astype(F32)
    gid = lax.broadcasted_iota(jnp.int32, shape3, 0).astype(F32)
    m1 = jnp.max(b3, axis=1, keepdims=True)
    first = jnp.min(jnp.where(b3 == m1, jid, float(GROUP_SIZE)), axis=1, keepdims=True)
    m2 = jnp.max(jnp.where(jid == first, NEG_INF, b3), axis=1, keepdims=True)
    gs = m1 + m2
    grp = lax.broadcasted_iota(jnp.int32, (N_GROUPS, 1, tt), 0).astype(F32)
    keep = jnp.zeros((N_GROUPS, 1, tt), F32)
    for _ in range(TOPK_GROUPS):
        m = jnp.max(gs, axis=0, keepdims=True)
        pick = grp == jnp.min(jnp.where(gs == m, grp, float(N_GROUPS)), axis=0, keepdims=True)
        keep = jnp.where(pick, 1.0, keep)
        gs = jnp.where(pick, NEG_INF, gs)
    work = jnp.where(keep > 0.0, b3, NEG_INF)
    eid = gid * float(GROUP_SIZE) + jid
    chosen = jnp.zeros(shape3, F32)
    sum01 = lambda a: jnp.sum(jnp.sum(a, axis=1, keepdims=True), axis=0, keepdims=True)
    ids, raw = [], []
    for _ in range(TOP_K):
        m = jnp.max(jnp.max(work, axis=1, keepdims=True), axis=0, keepdims=True)
        cand = jnp.where(work == m, eid, float(N_EXPERTS))
        first = jnp.min(jnp.min(cand, axis=1, keepdims=True), axis=0, keepdims=True)
        pick = eid == first
        ids.append(first)
        raw.append(sum01(jnp.where(pick, s3, 0.0)))
        chosen = jnp.where(pick, 1.0, chosen)
        work = jnp.where(pick, NEG_INF, work)
    total = raw[0]
    for r in raw[1:]:
        total = total + r
    flat = chosen.reshape(N_EXPERTS, tt)
    earlier = (lax.broadcasted_iota(jnp.int32, (tt, tt), 0)
               < lax.broadcasted_iota(jnp.int32, (tt, tt), 1)).astype(BF16)
    prefix = (jnp.dot(flat.astype(BF16), earlier, preferred_element_type=F32) + base).reshape(shape3)
    row = lax.broadcasted_iota(jnp.int32, (SUBLANES, tt), 0)
    idx8 = jnp.zeros((SUBLANES, tt), F32)
    w8 = jnp.zeros((SUBLANES, tt), F32)
    pos8 = jnp.zeros((SUBLANES, tt), F32)
    for k in range(TOP_K):
        rank = sum01(jnp.where(eid == ids[k], prefix, 0.0))
        idx8 = jnp.where(row == k, ids[k].reshape(1, tt), idx8)
        w8 = jnp.where(row == k, (raw[k] / total * ROUTED_SCALE).reshape(1, tt), w8)
        pos8 = jnp.where(row == k, rank.reshape(1, tt), pos8)
    counts = jnp.sum(flat, axis=1, keepdims=True)
    return idx8.astype(jnp.int32), w8, pos8.astype(jnp.int32), counts


def _mixer_post_kernel(yf, yb, bonus, g, of, ob, zz, zga, zgb, x_ref, gt1, sc2, sh2,
                       gnw, gnb, ones_ref, aproj, bng, bproj, wout, n2g, rwt, rbias,
                       x1_out, h2_out, idx_out, w_out_ref, pos_out, counts_out):
    @pl.when((pl.program_id(0) == 0) & (pl.program_id(1) == 0))
    def _():
        counts_out[...] = jnp.zeros_like(counts_out)

    ones_bd = ones_ref[...]
    n_sub = MIXER_SUB
    rs = x_ref.shape[1] // n_sub
    sub = range(n_sub)
    ld = lambda ref, k: ref[0, pl.ds(k * rs, rs), :].astype(F32)
    y = [ld(yf, k) + ld(yb, k) for k in sub]
    mean = [_seg_sum(v, ones_bd) * (1.0 / A_HEAD_DIM) for v in y]
    dlt = [v - m for v, m in zip(y, mean)]
    var = [_seg_sum(v * v, ones_bd) * (1.0 / A_HEAD_DIM) for v in dlt]
    yn = [v * lax.rsqrt(s + A_GN_EPS) * gnw[...] + gnb[...] for v, s in zip(dlt, var)]
    y_a = [_dot((v + ld(bonus, k)) * ld(g, k), aproj[...]) for k, v in zip(sub, yn)]

    def head_norm(o):
        parts = []
        for h in range(B_HEADS):
            oh = o[:, h * B_HEAD_DIM:(h + 1) * B_HEAD_DIM]
            parts.append(oh * lax.rsqrt(jnp.mean(oh * oh, axis=-1, keepdims=True) + EPS))
        return jnp.concatenate(parts, axis=1)

    on = [head_norm(ld(of, k) + ld(ob, k)) * bng[...] * _silu(ld(zz, k)) for k in sub]
    y_b = [_dot(v, bproj[...]) for v in on]
    u = [_sigmoid(ld(zga, k)) * a + _sigmoid(ld(zgb, k)) * b for k, a, b in zip(sub, y_a, y_b)]
    x1 = [ld(x_ref, k) + gt1[0] * _dot(v, wout[...]) for k, v in zip(sub, u)]
    h2 = [_rms_mod(v, n2g[...], sc2[0], sh2[0]) for v in x1]
    scores = [_sigmoid(_dot_nt(rwt[...], v)) for v in h2]
    base = counts_out[:, 0:1]
    for k in sub:
        rows, lanes = pl.ds(k * rs, rs), pl.ds(k * rs, rs)
        x1_out[0, rows, :] = x1[k]
        h2_out[0, rows, :] = _pack_bf16_pairs(h2[k])
        idx8, w8, pos8, counts = _route(scores[k], scores[k] + rbias[...], base)
        idx_out[0, :, lanes] = idx8
        w_out_ref[0, :, lanes] = w8
        pos_out[0, :, lanes] = pos8
        base = base + counts
    counts_out[...] = jnp.broadcast_to(base, counts_out.shape)


def _mixer_post(yf, yb, bonus, g, of, ob, z_c, x, gt1, sc2, sh2, gn_w, gn_b, ones_bd, a_proj, bng,
                b_proj, w_out, n2g, rwt, rbias):
    B, T, D = x.shape
    tt = min(512, T)
    full = lambda shape: pl.BlockSpec(shape, lambda b, i: (0,) * len(shape))
    tokw = lambda w, col=0: pl.BlockSpec((1, tt, w), lambda b, i: (b, i, col))
    modrow = pl.BlockSpec((1, 1, D), lambda b, i: (b, 0, 0))
    pick = pl.BlockSpec((1, SUBLANES, tt), lambda b, i: (b, 0, i))
    W = A_WIDTH
    return pl.pallas_call(
        _mixer_post_kernel,
        grid=(B, T // tt),
        in_specs=[tokw(W), tokw(W), tokw(W), tokw(W), tokw(D), tokw(D),
                  tokw(D, 0), tokw(D, 1), tokw(D, 2), tokw(D), modrow, modrow, modrow,
                  full((1, W)), full((1, W)), full((W, W)), full((W, D)), full((1, D)), full((D, D)),
                  full((D, D)), full((1, D)), full((N_EXPERTS, D)), full((N_EXPERTS, 1))],
        out_specs=[tokw(D), tokw(D // 2), pick, pick, pick,
                   pl.BlockSpec((N_EXPERTS, LANES), lambda b, i: (0, 0))],
        out_shape=[jax.ShapeDtypeStruct((B, T, D), F32), jax.ShapeDtypeStruct((B, T, D // 2), jnp.int32),
                   jax.ShapeDtypeStruct((B, SUBLANES, T), jnp.int32),
                   jax.ShapeDtypeStruct((B, SUBLANES, T), F32),
                   jax.ShapeDtypeStruct((B, SUBLANES, T), jnp.int32),
                   jax.ShapeDtypeStruct((N_EXPERTS, LANES), F32)],
        compiler_params=_params(("arbitrary", "arbitrary")),
        name="mixer_post",
    )(yf, yb, bonus, g, of, ob, z_c, z_c, z_c, x, gt1, sc2, sh2, gn_w, gn_b, ones_bd, a_proj, bng,
      b_proj, w_out, n2g, rwt, rbias)


def _gather_rows(table, idx):
    n_rows, width = idx.shape[0], table.shape[1]
    per_worker = n_rows // SC_WORKERS
    n_windows = per_worker // GATHER_WINDOW
    mesh = plsc.VectorSubcoreMesh(core_axis_name="c", subcore_axis_name="s")

    @functools.partial(
        pl.kernel, mesh=mesh, out_type=jax.ShapeDtypeStruct((n_rows, width), table.dtype),
        scratch_types=[pltpu.VMEM((GATHER_WINDOW,), jnp.int32),
                       pltpu.VMEM((GATHER_WINDOW, width), table.dtype),
                       pltpu.SemaphoreType.DMA])
    def gather(table_hbm, idx_hbm, out_hbm, idx_v, rows_v, sem):
        worker = lax.axis_index("s") * SC_CORES + lax.axis_index("c")

        @pl.loop(0, n_windows)
        def _(j):
            off = pl.multiple_of(worker * per_worker + j * GATHER_WINDOW, GATHER_WINDOW)
            pltpu.sync_copy(idx_hbm.at[pl.ds(off, GATHER_WINDOW)], idx_v)
            pltpu.async_copy(table_hbm.at[idx_v], rows_v, sem).wait()
            pltpu.sync_copy(rows_v, out_hbm.at[pl.ds(off, GATHER_WINDOW)])

    return gather(table, idx)


def _scatter_rows(rows, dest_w, n_out):
    n_tok, width = rows.shape
    per_worker = n_tok // GATHER_WINDOW // SC_WORKERS
    mesh = plsc.VectorSubcoreMesh(core_axis_name="c", subcore_axis_name="s")

    @functools.partial(
        pl.kernel, mesh=mesh, out_type=jax.ShapeDtypeStruct((n_out, width), rows.dtype),
        scratch_types=[pltpu.VMEM((SUBLANES, GATHER_WINDOW), jnp.int32),
                       pltpu.VMEM((GATHER_WINDOW, width), rows.dtype)])
    def scatter(rows_hbm, dest_hbm, out_hbm, idx_v, rows_v):
        worker = lax.axis_index("s") * SC_CORES + lax.axis_index("c")

        @pl.loop(0, per_worker)
        def _(j):
            win = worker * per_worker + j
            pltpu.sync_copy(dest_hbm.at[win], idx_v)
            pltpu.sync_copy(rows_hbm.at[pl.ds(pl.multiple_of(win * GATHER_WINDOW, GATHER_WINDOW),
                                              GATHER_WINDOW)], rows_v)
            for k in range(TOP_K):
                pltpu.sync_copy(rows_v, out_hbm.at[idx_v.at[k]])

    return scatter(rows, dest_w)


def _dest_kernel(start_ref, idx_ref, pos_ref, dest_ref):
    idx = idx_ref[0]
    dest = jnp.zeros_like(idx)
    for e in range(N_EXPERTS):
        dest = jnp.where(idx == e, start_ref[e], dest)
    dest_ref[0] = dest + pos_ref[0]


def _dest_rows(pad_start, idx, pos):
    B, _, T = idx.shape
    tt = min(2048, T)
    spec = pl.BlockSpec((1, SUBLANES, tt), lambda b, i, start: (b, 0, i))
    return pl.pallas_call(
        _dest_kernel,
        grid_spec=pltpu.PrefetchScalarGridSpec(num_scalar_prefetch=1, grid=(B, T // tt),
                                               in_specs=[spec, spec], out_specs=spec),
        out_shape=jax.ShapeDtypeStruct(idx.shape, jnp.int32),
        compiler_params=_params(("parallel", "parallel")),
        name="dest_rows",
    )(pad_start, idx, pos)


def _expert_kernel(be_ref, valid_ref, xs_ref, wg, wu, wd, ys_ref):
    n_valid = valid_ref[pl.program_id(0)]

    @pl.when(n_valid > 0)
    def _():
        half = D_MODEL // 2
        row = lax.broadcasted_iota(jnp.int32, xs_ref.shape, 0)
        a, b = _unpack_bf16_pairs(jnp.where(row < n_valid, xs_ref[...], 0))
        mm = lambda w: (jnp.dot(a, w[0, 0:half, :], preferred_element_type=F32)
                        + jnp.dot(b, w[0, half:D_MODEL, :], preferred_element_type=F32))
        hid = _silu(mm(wg)) * mm(wu)
        ys_ref[...] = _pack_bf16_pairs(_dot(hid, wd[0]))


def _experts(block_expert, block_valid, xs, wg, wu, wd):
    n_rows, half = xs.shape
    rows = pl.BlockSpec((MOE_BLOCK, half), lambda i, be, valid: (i, 0))
    wspec = lambda shape: pl.BlockSpec(shape, lambda i, be, valid: (be[i], 0, 0))
    return pl.pallas_call(
        _expert_kernel,
        grid_spec=pltpu.PrefetchScalarGridSpec(
            num_scalar_prefetch=2, grid=(n_rows // MOE_BLOCK,),
            in_specs=[rows, wspec((1, D_MODEL, D_EXPERT)), wspec((1, D_MODEL, D_EXPERT)),
                      wspec((1, D_EXPERT, D_MODEL))],
            out_specs=rows),
        out_shape=jax.ShapeDtypeStruct((n_rows, half), jnp.int32),
        compiler_params=_params(("arbitrary",)),
        name="experts",
    )(block_expert, block_valid, xs, wg, wu, wd)


def _moe_final_kernel(yg_ref, w_ref, h_ref, x1_ref, gt2, fg, sg, su, sd, o_ref):
    half = D_MODEL // 2
    a, b = _unpack_bf16_pairs(h_ref[0])
    mm = lambda w: (jnp.dot(a, w[0:half, :], preferred_element_type=F32)
                    + jnp.dot(b, w[half:D_MODEL, :], preferred_element_type=F32))
    shared = _dot(_silu(mm(sg)) * mm(su), sd[...])
    w = w_ref[0]
    lo = hi = None
    for k in range(TOP_K):
        ya, yb = _unpack_bf16_pairs(yg_ref[k, 0])
        wk = w[:, k:k + 1]
        lo = ya.astype(F32) * wk if lo is None else lo + ya.astype(F32) * wk
        hi = yb.astype(F32) * wk if hi is None else hi + yb.astype(F32) * wk
    x2 = x1_ref[0] + gt2[0] * (jnp.concatenate([lo, hi], axis=1) + shared)
    o_ref[0] = x2 * lax.rsqrt(jnp.mean(x2 * x2, axis=-1, keepdims=True) + EPS) * fg[...]


def _moe_final(yg, w_t, h2p, x1, gt2, fg, sg, su, sd):
    B, T, D = x1.shape
    tm = min(512, T)
    tok = lambda w: pl.BlockSpec((1, tm, w), lambda b, i: (b, i, 0))
    full = lambda shape: pl.BlockSpec(shape, lambda b, i: (0,) * len(shape))
    return pl.pallas_call(
        _moe_final_kernel,
        grid=(B, T // tm),
        in_specs=[pl.BlockSpec((TOP_K, 1, tm, D // 2), lambda b, i: (0, b, i, 0)), tok(SUBLANES),
                  tok(D // 2), tok(D), pl.BlockSpec((1, 1, D), lambda b, i: (b, 0, 0)), full((1, D)),
                  full((D, D_EXPERT)), full((D, D_EXPERT)), full((D_EXPERT, D))],
        out_specs=tok(D),
        out_shape=jax.ShapeDtypeStruct((B, T, D), F32),
        compiler_params=_params(("parallel", "parallel")),
        name="moe_final",
    )(yg, w_t, h2p, x1, gt2, fg, sg, su, sd)


def _moe_routed(h2p, idx, w, pos, counts, x1, gt2, p):
    B, T, half = h2p.shape
    n_slots = B * T * TOP_K
    n_rows = n_slots + N_EXPERTS * MOE_BLOCK
    cnt = counts[:, 0].astype(jnp.int32)
    padded = (cnt + MOE_BLOCK - 1) // MOE_BLOCK * MOE_BLOCK
    pad_end = jnp.cumsum(padded)
    pad_start = pad_end - padded
    dest = _dest_rows(pad_start, idx, pos)
    starts = jnp.arange(n_rows // MOE_BLOCK, dtype=jnp.int32) * MOE_BLOCK
    block_expert = jnp.minimum(jnp.sum((pad_end[None, :] <= starts[:, None]).astype(jnp.int32), axis=1),
                               N_EXPERTS - 1)
    block_valid = jnp.clip((pad_start + cnt)[block_expert] - starts, 0, MOE_BLOCK)
    dest_w = jnp.swapaxes(dest.reshape(B, SUBLANES, T // GATHER_WINDOW, GATHER_WINDOW), 1, 2)
    xs = _scatter_rows(h2p.reshape(B * T, half), dest_w.reshape(-1, SUBLANES, GATHER_WINDOW), n_rows)
    ys = _experts(block_expert, block_valid, xs, p["wg"], p["wu"], p["wd"])
    yg = _gather_rows(ys, jnp.swapaxes(dest[:, :TOP_K], 0, 1).reshape(-1)).reshape(TOP_K, B, T, half)
    return _moe_final(yg, jnp.swapaxes(w, 1, 2), h2p, x1, gt2, p["fg"], p["sg"], p["su"], p["sd"])


def _prepare_weights(w_in, shift_mu, a_w0, a_w_up, a_a0, a_a_up, a_g_up, a_k_k, a_k_a, a_r_k, a_gn_w,
                     a_gn_b, a_proj, b_conv_w, b_a_log, b_dt_bias, b_norm_g, b_proj, w_out, router_w,
                     router_bias, exp_gate, exp_up, exp_down, sh_gate, sh_up, sh_down, norm1_g, norm2_g,
                     final_g):
    W = A_WIDTH
    w = w_in[0]
    c0 = A_COLS
    c1 = c0 + 3 * B_WIDTH
    c2 = c1 + B_WIDTH
    c3 = c2 + 4 * B_HEADS
    pad = jnp.zeros((D_MODEL, SMALL_COLS - 4 * B_HEADS), F32)
    w_a = w[:, :c0].astype(BF16)
    w_b = jnp.concatenate([w[:, c0:c1], w[:, c2:c3], pad], axis=1).astype(BF16)
    w_c = jnp.concatenate([w[:, c1:c2], w[:, c3:]], axis=1).astype(BF16)
    zeros = jnp.zeros((2, A_RANK_W, W), F32)
    up_comb = jnp.concatenate([jnp.concatenate([a_w_up[0], zeros], axis=2),
                               jnp.concatenate([zeros, a_a_up[0]], axis=2)], axis=1)
    head = jnp.arange(W) // A_HEAD_DIM
    ones_bd = (head[:, None] == head[None, :]).astype(BF16)
    small = lambda v: jnp.zeros((1, SMALL_COLS), F32).at[0, 2 * B_HEADS:4 * B_HEADS].set(v.reshape(-1))
    return dict(
        w_a=w_a, w_b=w_b, w_c=w_c, mu=shift_mu[0].reshape(1, A_COLS), w0=a_w0[0], a0=a_a0[0],
        up_comb=up_comb, g_up=a_g_up[0], k_k=a_k_k[0].reshape(1, W), k_a=a_k_a[0].reshape(1, W),
        r_k=a_r_k[0].reshape(1, W), ones_bd=ones_bd, gn_w=a_gn_w[0].reshape(1, W),
        gn_b=a_gn_b[0].reshape(1, W), a_proj=a_proj[0].astype(BF16),
        conv_w=b_conv_w[0].astype(BF16).astype(F32),
        alog=small(b_a_log[0]), dtb=small(b_dt_bias[0]),
        bng=jnp.tile(b_norm_g[0], B_HEADS).reshape(1, B_WIDTH), b_proj=b_proj[0].astype(BF16),
        w_out=w_out[0].astype(BF16), rwt=router_w[0].T, rbias=router_bias[0].reshape(N_EXPERTS, 1),
        wg=exp_gate[0].astype(BF16), wu=exp_up[0].astype(BF16), wd=exp_down[0].astype(BF16),
        sg=sh_gate[0].astype(BF16), su=sh_up[0].astype(BF16), sd=sh_down[0].astype(BF16),
        n1g=norm1_g[0].reshape(1, D_MODEL), n2g=norm2_g[0].reshape(1, D_MODEL),
        fg=final_g.reshape(1, D_MODEL))


def _layer(x, mod, p):
    B = x.shape[0]
    sh1, sc1, gt1, sh2, sc2, gt2 = (m.reshape(B, 1, D_MODEL) for m in jnp.split(mod, 6, axis=-1))
    z_c = _inproj(x, p["n1g"], sc1, sh1, p["w_c"])
    r, kk, v, g, bonus, lw, kd, bd = _rwkv_prep(x, p["n1g"], sc1, sh1, p["w_a"], p["mu"], p["w0"], p["a0"],
                                                p["up_comb"], p["g_up"], p["k_k"], p["k_a"], p["r_k"],
                                                p["ones_bd"])
    yf, yb = _rwkv_scan(r, kk, v, lw, kd, bd)
    q, k, vv, bg = _gdn_prep(x, p["n1g"], sc1, sh1, p["w_b"], p["conv_w"], p["alog"], p["dtb"])
    of, ob = _gdn_scan(q, k, vv, bg)
    x1, h2p, idx, w, pos, counts = _mixer_post(yf, yb, bonus, g, of, ob, z_c, x, gt1, sc2, sh2, p["gn_w"],
                                               p["gn_b"], p["ones_bd"], p["a_proj"], p["bng"], p["b_proj"],
                                               p["w_out"], p["n2g"], p["rwt"], p["rbias"])
    return _moe_routed(h2p, idx, w, pos, counts, x1, gt2, p)


def kernel(x_prompt, x_sample, c_prompt, c_sample, ada_w, ada_b, norm1_g, norm2_g, w_in, shift_mu, a_w0, a_w_up, a_a0, a_a_up, a_g_up, a_k_k, a_k_a, a_r_k, a_gn_w, a_gn_b, a_proj, b_conv_w, b_a_log, b_dt_bias, b_norm_g, b_proj, w_out, router_w, router_bias, exp_gate, exp_up, exp_down, sh_gate, sh_up, sh_down, final_g):
    p = _prepare_weights(w_in, shift_mu, a_w0, a_w_up, a_a0, a_a_up, a_g_up, a_k_k, a_k_a, a_r_k, a_gn_w,
                         a_gn_b, a_proj, b_conv_w, b_a_log, b_dt_bias, b_norm_g, b_proj, w_out, router_w,
                         router_bias, exp_gate, exp_up, exp_down, sh_gate, sh_up, sh_down, norm1_g,
                         norm2_g, final_g)
    nb_p, nb_s = c_prompt.shape[0], c_sample.shape[0]
    rows = -(-(nb_p + nb_s) // SUBLANES) * SUBLANES
    c_all = jnp.concatenate([c_prompt, c_sample, jnp.zeros((rows - nb_p - nb_s, D_MODEL), F32)], axis=0)
    mod = _adaln_mod(c_all, ada_w[0], ada_b[0])
    y_prompt = _layer(x_prompt, mod[:nb_p], p)
    y_sample = _layer(x_sample, mod[nb_p:nb_p + nb_s], p)
    return (y_prompt, y_sample)
```

```python
import functools

import jax
import jax.numpy as jnp
from jax import lax
from jax.experimental import pallas as pl
from jax.experimental.pallas import tpu as pltpu
from jax.experimental.pallas import tpu_sc as plsc

F32 = jnp.float32
BF16 = jnp.bfloat16
ACT = jnp.bfloat16
HIGHEST = lax.Precision.HIGHEST

D_MODEL = 1024
A_HEADS = 8
A_HEAD_DIM = 64
A_WIDTH = A_HEADS * A_HEAD_DIM
A_RANK_W = 64
A_RANK_A = 64
A_RANK_G = 128
A_GN_EPS = 64e-5
A_COLS = 3 * A_WIDTH + A_RANK_W + A_RANK_A + A_RANK_G
B_HEADS = 8
B_HEAD_DIM = 128
B_WIDTH = B_HEADS * B_HEAD_DIM
B_CONV = 5
CHUNK = 64
SCAN_SUB = 4
GDN_COL_GROUP = 512
MIXER_SUB = 2
N_EXPERTS = 64
TOP_K = 6
N_GROUPS = 8
TOPK_GROUPS = 4
GROUP_SIZE = N_EXPERTS // N_GROUPS
D_EXPERT = 256
ROUTED_SCALE = 2.5
EPS = 1e-6
LANES = 128
SUBLANES = 8
HALO = 16
SMALL_COLS = LANES
MOE_BLOCK = 1024
SC_CORES = 2
SC_SUBCORES = 16
SC_WORKERS = SC_CORES * SC_SUBCORES
GATHER_WINDOW = 64
VMEM_LIMIT = 56 * 1024 * 1024
NEG_INF = float("-inf")


def _dot(a, b):
    return jnp.dot(a.astype(BF16), b.astype(BF16), preferred_element_type=F32)


def _dot_nt(a, b):
    return lax.dot_general(a.astype(BF16), b.astype(BF16), (((1,), (1,)), ((), ())),
                           preferred_element_type=F32)


def _dot_tn(a, b):
    return lax.dot_general(a.astype(BF16), b.astype(BF16), (((0,), (0,)), ((), ())),
                           preferred_element_type=F32)


def _dot_hi(a, b):
    return jnp.dot(a, b, precision=HIGHEST, preferred_element_type=F32)


def _split(x):
    hi = x.astype(BF16)
    return hi, (x - hi.astype(F32)).astype(BF16)


def _dot3(a, b, dims=(((1,), (0,)), ((), ()))):
    ah, al = _split(a)
    bh, bl = _split(b)
    d = lambda u, v: lax.dot_general(u, v, dims, preferred_element_type=F32)
    return d(ah, bh) + (d(ah, bl) + d(al, bh))


def _seg_sum(x, ones_bd):
    hi = x.astype(BF16)
    lo = (x - hi.astype(F32)).astype(BF16)
    return (jnp.dot(hi, ones_bd, preferred_element_type=F32)
            + jnp.dot(lo, ones_bd, preferred_element_type=F32))


def _softplus(x):
    return jnp.maximum(x, 0.0) + jnp.log1p(jnp.exp(-jnp.abs(x)))


def _sigmoid(x):
    return 1.0 / (1.0 + jnp.exp(-x))


def _silu(x):
    return x * _sigmoid(x)


def _tri_inverse(a_list, eye):
    n = a_list[0].shape[0]
    xs = [(-a).astype(BF16) for a in a_list]
    ts = [eye - a for a in a_list]
    xs = [jnp.dot(x, x, preferred_element_type=F32).astype(BF16) for x in xs]
    power = 2
    while 2 * power < CHUNK:
        both = [jnp.dot(x, jnp.concatenate([t.astype(BF16), x], axis=1), preferred_element_type=F32)
                for t, x in zip(ts, xs)]
        ts = [t + b[:, 0:n] for t, b in zip(ts, both)]
        xs = [b[:, n:2 * n].astype(BF16) for b in both]
        power *= 2
    return [t + jnp.dot(x, t.astype(BF16), preferred_element_type=F32) for t, x in zip(ts, xs)]


def _params(sem, **extra):
    return pltpu.CompilerParams(dimension_semantics=sem, vmem_limit_bytes=VMEM_LIMIT, **extra)


def _mod_kernel(c_ref, w_ref, b_ref, o_ref):
    c = c_ref[...]
    o_ref[...] = _dot_hi(_silu(c), w_ref[...]) + b_ref[...]


def _adaln_mod(c, ada_w, ada_b):
    rows = c.shape[0]
    n = ada_w.shape[1]
    return pl.pallas_call(
        _mod_kernel,
        grid=(n // D_MODEL,),
        in_specs=[pl.BlockSpec((rows, D_MODEL), lambda j: (0, 0)),
                  pl.BlockSpec((D_MODEL, D_MODEL), lambda j: (0, j)),
                  pl.BlockSpec((1, D_MODEL), lambda j: (0, j))],
        out_specs=pl.BlockSpec((rows, D_MODEL), lambda j: (0, j)),
        out_shape=jax.ShapeDtypeStruct((rows, n), F32),
        compiler_params=_params(("arbitrary",)),
        name="adaln_mod",
    )(c, ada_w, ada_b.reshape(1, n))


def _rms_mod(x, g, sc, sh):
    y = x * lax.rsqrt(jnp.mean(x * x, axis=-1, keepdims=True) + EPS)
    return (y * g) * (1.0 + sc) + sh


def _inproj_kernel(x_ref, g_ref, sc_ref, sh_ref, w_ref, o_ref):
    h = _rms_mod(x_ref[0], g_ref[...], sc_ref[0], sh_ref[0]).astype(BF16)
    o_ref[0] = jnp.dot(h, w_ref[...], preferred_element_type=F32).astype(o_ref.dtype)


def _inproj(x, g, sc, sh, w):
    B, T, D = x.shape
    n = w.shape[1]
    tm = min(512, T)
    return pl.pallas_call(
        _inproj_kernel,
        grid=(B, T // tm),
        in_specs=[pl.BlockSpec((1, tm, D), lambda b, i: (b, i, 0)),
                  pl.BlockSpec((1, D), lambda b, i: (0, 0)),
                  pl.BlockSpec((1, 1, D), lambda b, i: (b, 0, 0)),
                  pl.BlockSpec((1, 1, D), lambda b, i: (b, 0, 0)),
                  pl.BlockSpec((D, n), lambda b, i: (0, 0))],
        out_specs=pl.BlockSpec((1, tm, n), lambda b, i: (b, i, 0)),
        out_shape=jax.ShapeDtypeStruct((B, T, n), ACT),
        compiler_params=_params(("parallel", "parallel")),
        name="in_proj",
    )(x, g, sc, sh, w)


def _halo_specs(tt, width, col_block, seq_len):
    nb = tt // HALO
    last = seq_len // HALO - 1
    return [
        pl.BlockSpec((1, tt, width), lambda b, i: (b, i, col_block)),
        pl.BlockSpec((1, HALO, width), lambda b, i: (b, jnp.maximum(i * nb - 1, 0), col_block)),
        pl.BlockSpec((1, HALO, width), lambda b, i: (b, jnp.minimum((i + 1) * nb, last), col_block)),
    ]


def _halo_rows(zp_ref, zn_ref):
    return zp_ref[0].astype(F32)[HALO - SUBLANES:HALO], zn_ref[0].astype(F32)[0:SUBLANES]


def _normed_with_halo(x_ref, xp_ref, xn_ref, g_ref, sc_ref, sh_ref):
    prev8, next8 = _halo_rows(xp_ref, xn_ref)
    x_ext = jnp.concatenate([prev8, x_ref[0], next8], axis=0)
    return _rms_mod(x_ext, g_ref[...], sc_ref[0], sh_ref[0]).astype(BF16)


def _rwkv_prep_kernel(x_ref, xp_ref, xn_ref, g1_ref, sc_ref, sh_ref, w_ref,
                      mu_ref, w0_ref, a0_ref, up_ref, gup_ref, kk_ref, ka_ref, rk_ref, ones_ref,
                      r_out, kk_out, v_out, g_out, bonus_out, lw_out, kd_out, bd_out, ext_scr):
    tt = x_ref.shape[1]
    i = pl.program_id(1)
    last = pl.num_programs(1) - 1
    h = _normed_with_halo(x_ref, xp_ref, xn_ref, g1_ref, sc_ref, sh_ref)
    project = lambda c0, c1: jnp.dot(h, w_ref[:, c0:c1], preferred_element_type=F32)
    row = lax.broadcasted_iota(jnp.int32, (tt + 2 * SUBLANES, 1), 0)
    inside = ((row >= SUBLANES) | (i > 0)) & ((row < tt + SUBLANES) | (i < last))
    W = A_WIDTH

    def shift_mix(z_ext, c0, c1):
        z_ext = z_ext.astype(ACT).astype(F32)
        ext_scr[:, c0:c1] = jnp.where(inside, z_ext, 0.0)
        z = ext_scr[pl.ds(SUBLANES, tt), c0:c1]
        z_prev = ext_scr[pl.ds(SUBLANES - 1, tt), c0:c1]
        z_next = ext_scr[pl.ds(SUBLANES + 1, tt), c0:c1]
        return z + (0.5 * (z_prev + z_next) - z) * mu_ref[:, c0:c1]

    groups = [(3 * W, A_COLS), (W, 2 * W), (0, W), (2 * W, 3 * W)]
    mixed = []
    z_nxt = project(*groups[0])
    for gi, (c0, c1) in enumerate(groups):
        z_cur = z_nxt
        if gi + 1 < len(groups):
            z_nxt = project(*groups[gi + 1])
        mixed.append(shift_mix(z_cur, c0, c1))
    zwag, zk, zr, zv = mixed
    zwa = zwag[:, 0:LANES]
    zg = zwag[:, LANES:2 * LANES]
    ones_bd = ones_ref[...]
    kk_raw = zk * kk_ref[...]
    kk = kk_raw * lax.rsqrt(_seg_sum(kk_raw * kk_raw, ones_bd) + 1e-6)
    lane = lax.broadcasted_iota(jnp.int32, zwa.shape, 1)
    lhs = jnp.where(lane < A_RANK_W, jnp.tanh(zwa), zwa)
    k_sum = jnp.zeros_like(zk)
    for d in range(2):
        up = _dot(lhs, up_ref[d])
        wl = w0_ref[d:d + 1, :] + up[:, 0:W]
        w_log = -_softplus(-wl) - 0.5
        lw_out[d, 0] = -jnp.exp(w_log)
        a = _sigmoid(a0_ref[d:d + 1, :] + up[:, W:2 * W])
        k_d = zk * (1.0 + (a - 1.0) * ka_ref[...])
        kd_out[d, 0] = k_d.astype(kd_out.dtype)
        bd_out[d, 0] = (kk * a).astype(bd_out.dtype)
        k_sum = k_sum + k_d
    r_out[0] = zr.astype(r_out.dtype)
    kk_out[0] = kk.astype(kk_out.dtype)
    v_out[0] = zv.astype(v_out.dtype)
    g_out[0] = _dot(_sigmoid(zg), gup_ref[...]).astype(g_out.dtype)
    bonus_out[0] = (_seg_sum(zr * k_sum * rk_ref[...], ones_bd) * zv).astype(bonus_out.dtype)


def _front_specs(tt, seq_len, n_cols):
    modrow = pl.BlockSpec((1, 1, D_MODEL), lambda b, i: (b, 0, 0))
    return _halo_specs(tt, D_MODEL, 0, seq_len) + [
        pl.BlockSpec((1, D_MODEL), lambda b, i: (0, 0)), modrow, modrow,
        pl.BlockSpec((D_MODEL, n_cols), lambda b, i: (0, 0))]


def _rwkv_prep(x, g1, sc, sh, w_a, mu, w0, a0, up_comb, g_up, k_k, k_a, r_k, ones_bd):
    B, T, _ = x.shape
    tt = min(512, T)
    W = A_WIDTH
    full = lambda shape: pl.BlockSpec(shape, lambda b, i: (0,) * len(shape))
    tok = pl.BlockSpec((1, tt, W), lambda b, i: (b, i, 0))
    tok2 = pl.BlockSpec((2, 1, tt, W), lambda b, i: (0, b, i, 0))
    s1 = jax.ShapeDtypeStruct((B, T, W), ACT)
    s2 = jax.ShapeDtypeStruct((2, B, T, W), ACT)
    lw = jax.ShapeDtypeStruct((2, B, T, W), F32)
    return pl.pallas_call(
        _rwkv_prep_kernel,
        grid=(B, T // tt),
        in_specs=_front_specs(tt, T, A_COLS) + [
            full((1, A_COLS)), full((2, W)), full((2, W)), full((2, LANES, 2 * W)),
            full((A_RANK_G, W)), full((1, W)), full((1, W)), full((1, W)), full((W, W))],
        out_specs=[tok, tok, tok, tok, tok, tok2, tok2, tok2],
        out_shape=[s1, s1, s1, s1, s1, lw, s2, s2],
        scratch_shapes=[pltpu.VMEM((tt + 2 * SUBLANES, A_COLS), F32)],
        compiler_params=_params(("parallel", "parallel")),
        name="rwkv_prep",
    )(x, x, x, g1, sc, sh, w_a, mu, w0, a0, up_comb, g_up, k_k, k_a, r_k, ones_bd)


def _chunk_masks(reverse):
    n = 2 * CHUNK
    row = lax.broadcasted_iota(jnp.int32, (n, n), 0)
    col = lax.broadcasted_iota(jnp.int32, (n, n), 1)
    same = (row // CHUNK) == (col // CHUNK)
    ti, tj = row % CHUNK, col % CHUNK
    if reverse:
        return same & (ti < tj), same & (ti <= tj), row == col
    return same & (ti > tj), same & (ti >= tj), row == col


def _cumsum_matrix(reverse):
    row = lax.broadcasted_iota(jnp.int32, (CHUNK, CHUNK), 0)
    col = lax.broadcasted_iota(jnp.int32, (CHUNK, CHUNK), 1)
    return ((row <= col) if reverse else (row >= col)).astype(F32)


def _rwkv_chunk_operands(r, kk, v, k, b, cs, lw, reverse):
    C = CHUNK
    cs_end = cs[0:1, :] if reverse else cs[C - 1:C, :]
    m0 = lax.broadcasted_iota(jnp.int32, (C, LANES), 1) < A_HEAD_DIM

    def stack(x):
        return jnp.concatenate([jnp.where(m0, x, 0.0), jnp.where(m0, 0.0, x)], axis=0)

    g_inv = jnp.exp(-cs)
    g_tail = jnp.exp(cs_end - cs)
    strict, incl, _ = _chunk_masks(reverse)
    return dict(rg=stack(r * jnp.exp(cs)), kkg=stack(kk * jnp.exp(cs - lw)), ki=stack(k * g_inv),
                bi=stack(b * g_inv), kt=stack(k * g_tail), bt=stack(b * g_tail), vs=stack(v),
                g_end=jnp.exp(cs_end), strict=strict, incl=incl)


def _rwkv_chunks(ops, hts):
    C = CHUNK
    n = 2 * C
    eye = _chunk_masks(False)[2].astype(F32)
    ps = [_dot_nt(jnp.concatenate([o["kkg"], o["rg"]], axis=0), jnp.concatenate([o["bi"], o["ki"]], axis=0))
          for o in ops]
    a_ab = [jnp.where(o["strict"], p[0:n, 0:n], 0.0) for o, p in zip(ops, ps)]
    a_rb = [jnp.where(o["incl"], p[n:2 * n, 0:n], 0.0).astype(BF16) for o, p in zip(ops, ps)]
    a_k = [jnp.concatenate([jnp.where(o["strict"], p[0:n, n:2 * n], 0.0),
                            jnp.where(o["incl"], p[n:2 * n, n:2 * n], 0.0)], axis=0).astype(BF16)
           for o, p in zip(ops, ps)]
    vsb = [o["vs"].astype(BF16) for o in ops]
    a_kv = [jnp.dot(a, v, preferred_element_type=F32) for a, v in zip(a_k, vsb)]
    akv = [x[0:n] for x in a_kv]
    arkv = [x[n:2 * n] for x in a_kv]
    tinv = _tri_inverse(a_ab, eye)
    wu = [_dot(t, jnp.concatenate([o["kkg"], x], axis=1)).astype(BF16)
          for t, o, x in zip(tinv, ops, akv)]
    r2 = [jnp.dot(a, w, preferred_element_type=F32) for a, w in zip(a_rb, wu)]
    btb = [o["bt"].astype(BF16) for o in ops]
    btw = [_dot_tn(b, w[:, 0:LANES]) for b, w in zip(btb, wu)]
    hloc = [_dot_tn(jnp.concatenate([v, -w[:, LANES:2 * LANES]], axis=0),
                    jnp.concatenate([o["kt"].astype(BF16), b], axis=0))
            for v, o, w, b in zip(vsb, ops, wu, btb)]
    ys, hts_new = [], []
    for o, r, yl, ht, bw, hl in zip(ops, r2, arkv, hts, btw, hloc):
        q_s = o["rg"] - r[:, 0:LANES]
        yloc_s = yl - r[:, LANES:2 * LANES]
        ys.append(yloc_s[0:C] + yloc_s[C:n] + _dot_nt(q_s[0:C] + q_s[C:n], ht))
        hts_new.append(ht * o["g_end"] - _dot_nt(ht, bw) + hl)
    return ys, hts_new


def _rwkv_scan_kernel(rf, kkf, vf, lwf, kf, bf, rb, kkb, vb, lwb, kb, bb, yf_out, yb_out, h_scr):
    @pl.when(pl.program_id(1) == 0)
    def _():
        h_scr[...] = jnp.zeros_like(h_scr)

    n_pairs = A_WIDTH // LANES
    sls = [slice(i * LANES, (i + 1) * LANES) for i in range(n_pairs)]
    f32 = lambda a: a.astype(F32)
    hts = [h_scr[d, i] for d in range(2) for i in range(n_pairs)]
    for j in range(SCAN_SUB):
        ops = []
        for d, (r_ref, kk_ref, v_ref, lw_ref, k_ref, b_ref) in enumerate(
                ((rf, kkf, vf, lwf, kf, bf), (rb, kkb, vb, lwb, kb, bb))):
            reverse = d == 1
            rows = pl.ds((SCAN_SUB - 1 - j if reverse else j) * CHUNK, CHUNK)
            lw_all = lw_ref[0, 0, rows, :]
            cs_all = _dot_hi(_cumsum_matrix(reverse), lw_all)
            ops += [_rwkv_chunk_operands(f32(r_ref[0, rows, sl]), f32(kk_ref[0, rows, sl]),
                                         f32(v_ref[0, rows, sl]), f32(k_ref[0, 0, rows, sl]),
                                         f32(b_ref[0, 0, rows, sl]), cs_all[:, sl], lw_all[:, sl], reverse)
                    for sl in sls]
        ys, hts = _rwkv_chunks(ops, hts)
        for d, y_out in enumerate((yf_out, yb_out)):
            rows = pl.ds((SCAN_SUB - 1 - j if d == 1 else j) * CHUNK, CHUNK)
            for i, sl in enumerate(sls):
                y_out[0, rows, sl] = ys[d * n_pairs + i].astype(y_out.dtype)
    for d in range(2):
        for i in range(n_pairs):
            h_scr[d, i] = hts[d * n_pairs + i]


def _rwkv_scan(r, kk, v, lw, kd, bd):
    B, T, W = r.shape
    rows = CHUNK * SCAN_SUB
    N = T // rows
    fwd = pl.BlockSpec((1, rows, W), lambda b, c: (b, c, 0))
    bwd = pl.BlockSpec((1, rows, W), lambda b, c: (b, N - 1 - c, 0))
    fwd2 = pl.BlockSpec((1, 1, rows, W), lambda b, c: (0, b, c, 0))
    bwd2 = pl.BlockSpec((1, 1, rows, W), lambda b, c: (1, b, N - 1 - c, 0))
    out = jax.ShapeDtypeStruct((B, T, W), ACT)
    return pl.pallas_call(
        _rwkv_scan_kernel,
        grid=(B, N),
        in_specs=[fwd, fwd, fwd, fwd2, fwd2, fwd2, bwd, bwd, bwd, bwd2, bwd2, bwd2],
        out_specs=[fwd, bwd],
        out_shape=[out, out],
        scratch_shapes=[pltpu.VMEM((2, W // LANES, LANES, LANES), F32)],
        compiler_params=_params(("parallel", "arbitrary")),
        name="rwkv_scan",
    )(r, kk, v, lw, kd, bd, r, kk, v, lw, kd, bd)


def _gdn_prep_kernel(x_ref, xp_ref, xn_ref, g1_ref, sc_ref, sh_ref, w_ref, cw_ref, alog_ref, dtb_ref,
                     q_out, k_out, v_out, bg_out, acc_scr):
    tt = x_ref.shape[1]
    i = pl.program_id(1)
    last = pl.num_programs(1) - 1
    half = B_CONV // 2
    Wq = 3 * B_WIDTH
    h = _normed_with_halo(x_ref, xp_ref, xn_ref, g1_ref, sc_ref, sh_ref)
    project = lambda c0, c1: jnp.dot(h, w_ref[:, c0:c1], preferred_element_type=F32)
    offsets = [j - half for j in range(B_CONV) if j != half]
    row = lax.broadcasted_iota(jnp.int32, (len(offsets) * tt, tt), 0)
    col = lax.broadcasted_iota(jnp.int32, (len(offsets) * tt, tt), 1)
    blk = row // tt
    off = jnp.where(blk < half, blk - half, blk - half + 1)
    shift_rows = (col == row - blk * tt + off).astype(BF16)
    r8 = lax.broadcasted_iota(jnp.int32, (SUBLANES, 1), 0)

    def conv_group(z_ext, c0, c1):
        zb = z_ext[SUBLANES:SUBLANES + tt].astype(BF16)
        shifted = jnp.dot(shift_rows, zb, preferred_element_type=F32)
        acc = zb.astype(F32) * cw_ref[half:half + 1, c0:c1]
        for m, o in enumerate(offsets):
            acc = acc + shifted[m * tt:(m + 1) * tt] * cw_ref[o + half:o + half + 1, c0:c1]
        acc_scr[:, c0:c1] = acc
        prev8 = jnp.where(i == 0, 0.0, z_ext[0:SUBLANES]).astype(BF16).astype(F32)
        next8 = jnp.where(i == last, 0.0, z_ext[SUBLANES + tt:]).astype(BF16).astype(F32)
        top = bot = None
        for o in offsets:
            w = cw_ref[o + half:o + half + 1, c0:c1]
            if o < 0:
                f = jnp.where(r8 < -o, pltpu.roll(prev8, -o, axis=0), 0.0) * w
                top = f if top is None else top + f
            else:
                f = jnp.where(r8 >= SUBLANES - o, pltpu.roll(next8, SUBLANES - o, axis=0), 0.0) * w
                bot = f if bot is None else bot + f
        acc_scr[pl.ds(0, SUBLANES), c0:c1] += top
        acc_scr[pl.ds(tt - SUBLANES, SUBLANES), c0:c1] += bot
        return _silu(acc_scr[:, c0:c1])

    bounds = list(range(0, Wq + 1, GDN_COL_GROUP))
    z_next = project(bounds[0], bounds[1])
    for gi in range(len(bounds) - 1):
        c0, c1 = bounds[gi], bounds[gi + 1]
        z_cur = z_next
        z_next = project(bounds[gi + 1], bounds[gi + 2]) if gi + 2 < len(bounds) else project(Wq, Wq + SMALL_COLS)
        act = conv_group(z_cur, c0, c1)
        for hh in range(GDN_COL_GROUP // B_HEAD_DIM):
            x = act[:, hh * B_HEAD_DIM:(hh + 1) * B_HEAD_DIM]
            lo = c0 % B_WIDTH + hh * B_HEAD_DIM
            if c0 < 2 * B_WIDTH:
                x = x * lax.rsqrt(jnp.sum(x * x, axis=-1, keepdims=True) + 1e-6)
                if c0 < B_WIDTH:
                    x = x * (B_HEAD_DIM ** -0.5)
            out = (q_out, k_out, v_out)[c0 // B_WIDTH]
            out[0, :, lo:lo + B_HEAD_DIM] = x.astype(out.dtype)
    s = z_next[SUBLANES:SUBLANES + tt]
    lane = lax.broadcasted_iota(jnp.int32, s.shape, 1)
    beta = _sigmoid(s)
    g = -jnp.exp(alog_ref[...]) * _softplus(s + dtb_ref[...])
    bg_out[0] = jnp.where(lane < 2 * B_HEADS, beta, g)


def _gdn_prep(x, g1, sc, sh, w_b, conv_w, alog_vec, dtb_vec):
    B, T, _ = x.shape
    Wq = 3 * B_WIDTH
    tt = min(256, T)
    full = lambda shape: pl.BlockSpec(shape, lambda b, i: (0,) * len(shape))
    tok = pl.BlockSpec((1, tt, B_WIDTH), lambda b, i: (b, i, 0))
    s1 = jax.ShapeDtypeStruct((B, T, B_WIDTH), ACT)
    return pl.pallas_call(
        _gdn_prep_kernel,
        grid=(B, T // tt),
        in_specs=_front_specs(tt, T, Wq + SMALL_COLS) + [full((B_CONV, Wq)), full((1, SMALL_COLS)),
                                                          full((1, SMALL_COLS))],
        out_specs=[tok, tok, tok, pl.BlockSpec((1, tt, SMALL_COLS), lambda b, i: (b, i, 0))],
        out_shape=[s1, s1, s1, jax.ShapeDtypeStruct((B, T, SMALL_COLS), F32)],
        scratch_shapes=[pltpu.VMEM((tt, Wq), F32)],
        compiler_params=_params(("parallel", "parallel")),
        name="gdn_prep",
    )(x, x, x, g1, sc, sh, w_b, conv_w, alog_vec, dtb_vec)


def _gdn_chunk_operands(q0, q1, k0, k1, v0, v1, gc2, gct2, bg2, j0, j1, reverse):
    C = CHUNK
    n = 2 * C
    strict, incl, _ = _chunk_masks(reverse)
    top = lax.broadcasted_iota(jnp.int32, (n, 1), 0) < C
    left = lax.broadcasted_iota(jnp.int32, (1, n), 1) < C
    g0, g1 = j0 + 2 * B_HEADS, j1 + 2 * B_HEADS
    gcol = jnp.where(top, gc2[:, g0:g0 + 1], gc2[:, g1:g1 + 1])
    grow = jnp.where(left, gct2[g0:g0 + 1, :], gct2[g1:g1 + 1, :])
    beta = jnp.where(top, bg2[:, j0:j0 + 1], bg2[:, j1:j1 + 1])
    e = 0 if reverse else C - 1
    glast = jnp.where(top, gc2[e:e + 1, g0:g0 + 1], gc2[e:e + 1, g1:g1 + 1])
    ks = jnp.concatenate([k0, k1], axis=0)
    qs = jnp.concatenate([q0, q1], axis=0)
    vs = jnp.concatenate([v0, v1], axis=0)
    egc = jnp.exp(gcol)
    kb = ks * beta
    return dict(gam=jnp.exp(jnp.where(incl, gcol - grow, NEG_INF)), strict=strict, ks=ks,
                kbq=jnp.concatenate([kb, qs], axis=0),
                rhs=jnp.concatenate([vs * beta, kb * egc], axis=1),
                qh=(qs * egc).astype(BF16), ktail=(ks * jnp.exp(glast - gcol)).astype(BF16),
                decay=jnp.exp(glast))


def _gdn_chunks(ops, states):
    C = CHUNK
    n = 2 * C
    eye = _chunk_masks(False)[2].astype(F32)
    ps = [_dot_nt(o["kbq"], o["ks"]) for o in ops]
    lower = [jnp.where(o["strict"], p[0:n] * o["gam"], 0.0) for o, p in zip(ops, ps)]
    aqk = [(p[n:2 * n] * o["gam"]).astype(BF16) for o, p in zip(ops, ps)]
    tinv = [t.astype(BF16) for t in _tri_inverse(lower, eye)]
    uw = [jnp.dot(t, o["rhs"].astype(BF16), preferred_element_type=F32) for t, o in zip(tinv, ops)]
    lx = []
    for l, x in zip(lower, uw):
        lh, ll = _split(l)
        xh, xl = _split(x)
        first = jnp.dot(lh, jnp.concatenate([xh, xl], axis=1), preferred_element_type=F32)
        w2 = x.shape[1]
        lx.append(first[:, 0:w2] + (first[:, w2:2 * w2] + jnp.dot(ll, xh, preferred_element_type=F32)))
    resid = [o["rhs"] - x - y for o, x, y in zip(ops, uw, lx)]
    uw = [x + jnp.dot(t, r.astype(BF16), preferred_element_type=F32) for x, t, r in zip(uw, tinv, resid)]
    sb = [(s0.astype(BF16), s1.astype(BF16)) for s0, s1 in states]
    wq = [[jnp.dot(jnp.concatenate([u[h * C:(h + 1) * C, LANES:].astype(BF16), o["qh"][h * C:(h + 1) * C]],
                                   axis=0), s[h], preferred_element_type=F32) for h in range(2)]
          for u, o, s in zip(uw, ops, sb)]
    ws = [jnp.concatenate([x[0][0:C], x[1][0:C]], axis=0) for x in wq]
    qss = [jnp.concatenate([x[0][C:n], x[1][C:n]], axis=0) for x in wq]
    v_new = [(u[:, 0:LANES] - w).astype(BF16) for u, w in zip(uw, ws)]
    outs = [q + jnp.dot(a, v, preferred_element_type=F32) for q, a, v in zip(qss, aqk, v_new)]
    new_states = []
    for o, v, (s0, s1) in zip(ops, v_new, states):
        new_states.append((s0 * o["decay"][0:1, :] + _dot_tn(o["ktail"][0:C], v[0:C]),
                           s1 * o["decay"][C:C + 1, :] + _dot_tn(o["ktail"][C:n], v[C:n])))
    return [(x[0:C], x[C:n]) for x in outs], new_states


def _gdn_scan_kernel(qf, kf, vf, bgf, qb, kb, vb, bgb, of_out, ob_out, s_scr):
    @pl.when(pl.program_id(1) == 0)
    def _():
        s_scr[...] = jnp.zeros_like(s_scr)

    D = B_HEAD_DIM
    n_pairs = B_HEADS // 2
    sl = lambda h: slice(h * D, (h + 1) * D)
    f32 = lambda a: a.astype(F32)
    states = [(s_scr[d, 2 * i], s_scr[d, 2 * i + 1]) for d in range(2) for i in range(n_pairs)]
    for j in range(SCAN_SUB):
        ops = []
        for d, (q_ref, k_ref, v_ref, bg_ref) in enumerate(((qf, kf, vf, bgf), (qb, kb, vb, bgb))):
            reverse = d == 1
            rows = pl.ds((SCAN_SUB - 1 - j if reverse else j) * CHUNK, CHUNK)
            bg = bg_ref[0, rows, :]
            gc = _dot_hi(_cumsum_matrix(reverse), bg)
            bg2 = jnp.concatenate([bg, bg], axis=0)
            gc2 = jnp.concatenate([gc, gc], axis=0)
            gct2 = gc2.T
            for i in range(n_pairs):
                h0, h1 = 2 * i, 2 * i + 1
                ops.append(_gdn_chunk_operands(
                    f32(q_ref[0, rows, sl(h0)]), f32(q_ref[0, rows, sl(h1)]), f32(k_ref[0, rows, sl(h0)]),
                    f32(k_ref[0, rows, sl(h1)]), f32(v_ref[0, rows, sl(h0)]), f32(v_ref[0, rows, sl(h1)]),
                    gc2, gct2, bg2, d * B_HEADS + h0, d * B_HEADS + h1, reverse))
        outs, states = _gdn_chunks(ops, states)
        for d, o_out in enumerate((of_out, ob_out)):
            rows = pl.ds((SCAN_SUB - 1 - j if d == 1 else j) * CHUNK, CHUNK)
            for i in range(n_pairs):
                o0, o1 = outs[d * n_pairs + i]
                o_out[0, rows, sl(2 * i)] = o0.astype(o_out.dtype)
                o_out[0, rows, sl(2 * i + 1)] = o1.astype(o_out.dtype)
    for d in range(2):
        for i in range(n_pairs):
            s_scr[d, 2 * i], s_scr[d, 2 * i + 1] = states[d * n_pairs + i]


def _gdn_scan(q, k, v, bg):
    B, T, W = q.shape
    rows = CHUNK * SCAN_SUB
    N = T // rows
    fwd = pl.BlockSpec((1, rows, W), lambda b, c: (b, c, 0))
    bwd = pl.BlockSpec((1, rows, W), lambda b, c: (b, N - 1 - c, 0))
    sfwd = pl.BlockSpec((1, rows, SMALL_COLS), lambda b, c: (b, c, 0))
    sbwd = pl.BlockSpec((1, rows, SMALL_COLS), lambda b, c: (b, N - 1 - c, 0))
    out = jax.ShapeDtypeStruct((B, T, W), ACT)
    return pl.pallas_call(
        _gdn_scan_kernel,
        grid=(B, N),
        in_specs=[fwd, fwd, fwd, sfwd, bwd, bwd, bwd, sbwd],
        out_specs=[fwd, bwd],
        out_shape=[out, out],
        scratch_shapes=[pltpu.VMEM((2, B_HEADS, B_HEAD_DIM, B_HEAD_DIM), F32)],
        compiler_params=_params(("parallel", "arbitrary")),
        name="gdn_scan",
    )(q, k, v, bg, q, k, v, bg)


def _pack_bf16_pairs(x):
    n = x.shape[1] // 2
    hi = lax.bitcast_convert_type(x[:, :n].astype(BF16).astype(F32), jnp.int32)
    lo = lax.bitcast_convert_type(x[:, n:].astype(BF16).astype(F32), jnp.int32)
    return hi | lax.shift_right_logical(lo, 16)


def _unpack_bf16_pairs(p):
    a = lax.bitcast_convert_type(p & jnp.int32(-65536), F32)
    b = lax.bitcast_convert_type(lax.shift_left(p, 16), F32)
    return a.astype(BF16), b.astype(BF16)


def _route(scores, biased, base):
    tt = scores.shape[-1]
    shape3 = (N_GROUPS, GROUP_SIZE, tt)
    s3 = scores.reshape(shape3)
    b3 = biased.reshape(shape3)
    jid = lax.broadcasted_iota(jnp.int32, shape3, 1).astype(F32)
    gid = lax.broadcasted_iota(jnp.int32, shape3, 0).astype(F32)
    m1 = jnp.max(b3, axis=1, keepdims=True)
    first = jnp.min(jnp.where(b3 == m1, jid, float(GROUP_SIZE)), axis=1, keepdims=True)
    m2 = jnp.max(jnp.where(jid == first, NEG_INF, b3), axis=1, keepdims=True)
    gs = m1 + m2
    grp = lax.broadcasted_iota(jnp.int32, (N_GROUPS, 1, tt), 0).astype(F32)
    keep = jnp.zeros((N_GROUPS, 1, tt), F32)
    for _ in range(TOPK_GROUPS):
        m = jnp.max(gs, axis=0, keepdims=True)
        pick = grp == jnp.min(jnp.where(gs == m, grp, float(N_GROUPS)), axis=0, keepdims=True)
        keep = jnp.where(pick, 1.0, keep)
        gs = jnp.where(pick, NEG_INF, gs)
    work = jnp.where(keep > 0.0, b3, NEG_INF)
    eid = gid * float(GROUP_SIZE) + jid
    chosen = jnp.zeros(shape3, F32)
    sum01 = lambda a: jnp.sum(jnp.sum(a, axis=1, keepdims=True), axis=0, keepdims=True)
    ids, raw = [], []
    for _ in range(TOP_K):
        m = jnp.max(jnp.max(work, axis=1, keepdims=True), axis=0, keepdims=True)
        cand = jnp.where(work == m, eid, float(N_EXPERTS))
        first = jnp.min(jnp.min(cand, axis=1, keepdims=True), axis=0, keepdims=True)
        pick = eid == first
        ids.append(first)
        raw.append(sum01(jnp.where(pick, s3, 0.0)))
        chosen = jnp.where(pick, 1.0, chosen)
        work = jnp.where(pick, NEG_INF, work)
    total = raw[0]
    for r in raw[1:]:
        total = total + r
    flat = chosen.reshape(N_EXPERTS, tt)
    earlier = (lax.broadcasted_iota(jnp.int32, (tt, tt), 0)
               < lax.broadcasted_iota(jnp.int32, (tt, tt), 1)).astype(BF16)
    prefix = (jnp.dot(flat.astype(BF16), earlier, preferred_element_type=F32) + base).reshape(shape3)
    row = lax.broadcasted_iota(jnp.int32, (SUBLANES, tt), 0)
    idx8 = jnp.zeros((SUBLANES, tt), F32)
    w8 = jnp.zeros((SUBLANES, tt), F32)
    pos8 = jnp.zeros((SUBLANES, tt), F32)
    for k in range(TOP_K):
        rank = sum01(jnp.where(eid == ids[k], prefix, 0.0))
        idx8 = jnp.where(row == k, ids[k].reshape(1, tt), idx8)
        w8 = jnp.where(row == k, (raw[k] / total * ROUTED_SCALE).reshape(1, tt), w8)
        pos8 = jnp.where(row == k, rank.reshape(1, tt), pos8)
    counts = jnp.sum(flat, axis=1, keepdims=True)
    return idx8.astype(jnp.int32), w8, pos8.astype(jnp.int32), counts


def _mixer_post_kernel(yf, yb, bonus, g, of, ob, zz, zga, zgb, x_ref, gt1, sc2, sh2,
                       gnw, gnb, ones_ref, aproj, bng, bproj, wout, n2g, rwt, rbias,
                       x1_out, h2_out, idx_out, w_out_ref, pos_out, counts_out):
    @pl.when((pl.program_id(0) == 0) & (pl.program_id(1) == 0))
    def _():
        counts_out[...] = jnp.zeros_like(counts_out)

    ones_bd = ones_ref[...]
    n_sub = MIXER_SUB
    rs = x_ref.shape[1] // n_sub
    sub = range(n_sub)
    ld = lambda ref, k: ref[0, pl.ds(k * rs, rs), :].astype(F32)
    y = [ld(yf, k) + ld(yb, k) for k in sub]
    mean = [_seg_sum(v, ones_bd) * (1.0 / A_HEAD_DIM) for v in y]
    dlt = [v - m for v, m in zip(y, mean)]
    var = [_seg_sum(v * v, ones_bd) * (1.0 / A_HEAD_DIM) for v in dlt]
    yn = [v * lax.rsqrt(s + A_GN_EPS) * gnw[...] + gnb[...] for v, s in zip(dlt, var)]
    y_a = [_dot((v + ld(bonus, k)) * ld(g, k), aproj[...]) for k, v in zip(sub, yn)]

    def head_norm(o):
        parts = []
        for h in range(B_HEADS):
            oh = o[:, h * B_HEAD_DIM:(h + 1) * B_HEAD_DIM]
            parts.append(oh * lax.rsqrt(jnp.mean(oh * oh, axis=-1, keepdims=True) + EPS))
        return jnp.concatenate(parts, axis=1)

    on = [head_norm(ld(of, k) + ld(ob, k)) * bng[...] * _silu(ld(zz, k)) for k in sub]
    y_b = [_dot(v, bproj[...]) for v in on]
    u = [_sigmoid(ld(zga, k)) * a + _sigmoid(ld(zgb, k)) * b for k, a, b in zip(sub, y_a, y_b)]
    x1 = [ld(x_ref, k) + gt1[0] * _dot(v, wout[...]) for k, v in zip(sub, u)]
    h2 = [_rms_mod(v, n2g[...], sc2[0], sh2[0]) for v in x1]
    scores = [_sigmoid(_dot_nt(rwt[...], v)) for v in h2]
    base = counts_out[:, 0:1]
    for k in sub:
        rows, lanes = pl.ds(k * rs, rs), pl.ds(k * rs, rs)
        x1_out[0, rows, :] = x1[k]
        h2_out[0, rows, :] = _pack_bf16_pairs(h2[k])
        idx8, w8, pos8, counts = _route(scores[k], scores[k] + rbias[...], base)
        idx_out[0, :, lanes] = idx8
        w_out_ref[0, :, lanes] = w8
        pos_out[0, :, lanes] = pos8
        base = base + counts
    counts_out[...] = jnp.broadcast_to(base, counts_out.shape)


def _mixer_post(yf, yb, bonus, g, of, ob, z_c, x, gt1, sc2, sh2, gn_w, gn_b, ones_bd, a_proj, bng,
                b_proj, w_out, n2g, rwt, rbias):
    B, T, D = x.shape
    tt = min(512, T)
    full = lambda shape: pl.BlockSpec(shape, lambda b, i: (0,) * len(shape))
    tokw = lambda w, col=0: pl.BlockSpec((1, tt, w), lambda b, i: (b, i, col))
    modrow = pl.BlockSpec((1, 1, D), lambda b, i: (b, 0, 0))
    pick = pl.BlockSpec((1, SUBLANES, tt), lambda b, i: (b, 0, i))
    W = A_WIDTH
    return pl.pallas_call(
        _mixer_post_kernel,
        grid=(B, T // tt),
        in_specs=[tokw(W), tokw(W), tokw(W), tokw(W), tokw(D), tokw(D),
                  tokw(D, 0), tokw(D, 1), tokw(D, 2), tokw(D), modrow, modrow, modrow,
                  full((1, W)), full((1, W)), full((W, W)), full((W, D)), full((1, D)), full((D, D)),
                  full((D, D)), full((1, D)), full((N_EXPERTS, D)), full((N_EXPERTS, 1))],
        out_specs=[tokw(D), tokw(D // 2), pick, pick, pick,
                   pl.BlockSpec((N_EXPERTS, LANES), lambda b, i: (0, 0))],
        out_shape=[jax.ShapeDtypeStruct((B, T, D), F32), jax.ShapeDtypeStruct((B, T, D // 2), jnp.int32),
                   jax.ShapeDtypeStruct((B, SUBLANES, T), jnp.int32),
                   jax.ShapeDtypeStruct((B, SUBLANES, T), F32),
                   jax.ShapeDtypeStruct((B, SUBLANES, T), jnp.int32),
                   jax.ShapeDtypeStruct((N_EXPERTS, LANES), F32)],
        compiler_params=_params(("arbitrary", "arbitrary")),
        name="mixer_post",
    )(yf, yb, bonus, g, of, ob, z_c, z_c, z_c, x, gt1, sc2, sh2, gn_w, gn_b, ones_bd, a_proj, bng,
      b_proj, w_out, n2g, rwt, rbias)


def _gather_rows(table, idx):
    n_rows, width = idx.shape[0], table.shape[1]
    per_worker = n_rows // SC_WORKERS
    n_windows = per_worker // GATHER_WINDOW
    mesh = plsc.VectorSubcoreMesh(core_axis_name="c", subcore_axis_name="s")

    @functools.partial(
        pl.kernel, mesh=mesh, out_type=jax.ShapeDtypeStruct((n_rows, width), table.dtype),
        scratch_types=[pltpu.VMEM((GATHER_WINDOW,), jnp.int32),
                       pltpu.VMEM((GATHER_WINDOW, width), table.dtype),
                       pltpu.SemaphoreType.DMA])
    def gather(table_hbm, idx_hbm, out_hbm, idx_v, rows_v, sem):
        worker = lax.axis_index("s") * SC_CORES + lax.axis_index("c")

        @pl.loop(0, n_windows)
        def _(j):
            off = pl.multiple_of(worker * per_worker + j * GATHER_WINDOW, GATHER_WINDOW)
            pltpu.sync_copy(idx_hbm.at[pl.ds(off, GATHER_WINDOW)], idx_v)
            pltpu.async_copy(table_hbm.at[idx_v], rows_v, sem).wait()
            pltpu.sync_copy(rows_v, out_hbm.at[pl.ds(off, GATHER_WINDOW)])

    return gather(table, idx)


def _scatter_rows(rows, dest_w, n_out):
    n_tok, width = rows.shape
    per_worker = n_tok // GATHER_WINDOW // SC_WORKERS
    mesh = plsc.VectorSubcoreMesh(core_axis_name="c", subcore_axis_name="s")

    @functools.partial(
        pl.kernel, mesh=mesh, out_type=jax.ShapeDtypeStruct((n_out, width), rows.dtype),
        scratch_types=[pltpu.VMEM((SUBLANES, GATHER_WINDOW), jnp.int32),
                       pltpu.VMEM((GATHER_WINDOW, width), rows.dtype)])
    def scatter(rows_hbm, dest_hbm, out_hbm, idx_v, rows_v):
        worker = lax.axis_index("s") * SC_CORES + lax.axis_index("c")

        @pl.loop(0, per_worker)
        def _(j):
            win = worker * per_worker + j
            pltpu.sync_copy(dest_hbm.at[win], idx_v)
            pltpu.sync_copy(rows_hbm.at[pl.ds(pl.multiple_of(win * GATHER_WINDOW, GATHER_WINDOW),
                                              GATHER_WINDOW)], rows_v)
            for k in range(TOP_K):
                pltpu.sync_copy(rows_v, out_hbm.at[idx_v.at[k]])

    return scatter(rows, dest_w)


def _dest_kernel(start_ref, idx_ref, pos_ref, dest_ref):
    idx = idx_ref[0]
    dest = jnp.zeros_like(idx)
    for e in range(N_EXPERTS):
        dest = jnp.where(idx == e, start_ref[e], dest)
    dest_ref[0] = dest + pos_ref[0]


def _dest_rows(pad_start, idx, pos):
    B, _, T = idx.shape
    tt = min(2048, T)
    spec = pl.BlockSpec((1, SUBLANES, tt), lambda b, i, start: (b, 0, i))
    return pl.pallas_call(
        _dest_kernel,
        grid_spec=pltpu.PrefetchScalarGridSpec(num_scalar_prefetch=1, grid=(B, T // tt),
                                               in_specs=[spec, spec], out_specs=spec),
        out_shape=jax.ShapeDtypeStruct(idx.shape, jnp.int32),
        compiler_params=_params(("parallel", "parallel")),
        name="dest_rows",
    )(pad_start, idx, pos)


def _expert_kernel(be_ref, valid_ref, xs_ref, wg, wu, wd, ys_ref):
    n_valid = valid_ref[pl.program_id(0)]

    @pl.when(n_valid > 0)
    def _():
        half = D_MODEL // 2
        row = lax.broadcasted_iota(jnp.int32, xs_ref.shape, 0)
        a, b = _unpack_bf16_pairs(jnp.where(row < n_valid, xs_ref[...], 0))
        mm = lambda w: (jnp.dot(a, w[0, 0:half, :], preferred_element_type=F32)
                        + jnp.dot(b, w[0, half:D_MODEL, :], preferred_element_type=F32))
        hid = _silu(mm(wg)) * mm(wu)
        ys_ref[...] = _pack_bf16_pairs(_dot(hid, wd[0]))


def _experts(block_expert, block_valid, xs, wg, wu, wd):
    n_rows, half = xs.shape
    rows = pl.BlockSpec((MOE_BLOCK, half), lambda i, be, valid: (i, 0))
    wspec = lambda shape: pl.BlockSpec(shape, lambda i, be, valid: (be[i], 0, 0))
    return pl.pallas_call(
        _expert_kernel,
        grid_spec=pltpu.PrefetchScalarGridSpec(
            num_scalar_prefetch=2, grid=(n_rows // MOE_BLOCK,),
            in_specs=[rows, wspec((1, D_MODEL, D_EXPERT)), wspec((1, D_MODEL, D_EXPERT)),
                      wspec((1, D_EXPERT, D_MODEL))],
            out_specs=rows),
        out_shape=jax.ShapeDtypeStruct((n_rows, half), jnp.int32),
        compiler_params=_params(("arbitrary",)),
        name="experts",
    )(block_expert, block_valid, xs, wg, wu, wd)


def _moe_final_kernel(yg_ref, w_ref, h_ref, x1_ref, gt2, fg, sg, su, sd, o_ref):
    half = D_MODEL // 2
    a, b = _unpack_bf16_pairs(h_ref[0])
    mm = lambda w: (jnp.dot(a, w[0:half, :], preferred_element_type=F32)
                    + jnp.dot(b, w[half:D_MODEL, :], preferred_element_type=F32))
    shared = _dot(_silu(mm(sg)) * mm(su), sd[...])
    w = w_ref[0]
    lo = hi = None
    for k in range(TOP_K):
        ya, yb = _unpack_bf16_pairs(yg_ref[k, 0])
        wk = w[:, k:k + 1]
        lo = ya.astype(F32) * wk if lo is None else lo + ya.astype(F32) * wk
        hi = yb.astype(F32) * wk if hi is None else hi + yb.astype(F32) * wk
    x2 = x1_ref[0] + gt2[0] * (jnp.concatenate([lo, hi], axis=1) + shared)
    o_ref[0] = x2 * lax.rsqrt(jnp.mean(x2 * x2, axis=-1, keepdims=True) + EPS) * fg[...]


def _moe_final(yg, w_t, h2p, x1, gt2, fg, sg, su, sd):
    B, T, D = x1.shape
    tm = min(512, T)
    tok = lambda w: pl.BlockSpec((1, tm, w), lambda b, i: (b, i, 0))
    full = lambda shape: pl.BlockSpec(shape, lambda b, i: (0,) * len(shape))
    return pl.pallas_call(
        _moe_final_kernel,
        grid=(B, T // tm),
        in_specs=[pl.BlockSpec((TOP_K, 1, tm, D // 2), lambda b, i: (0, b, i, 0)), tok(SUBLANES),
                  tok(D // 2), tok(D), pl.BlockSpec((1, 1, D), lambda b, i: (b, 0, 0)), full((1, D)),
                  full((D, D_EXPERT)), full((D, D_EXPERT)), full((D_EXPERT, D))],
        out_specs=tok(D),
        out_shape=jax.ShapeDtypeStruct((B, T, D), F32),
        compiler_params=_params(("parallel", "parallel")),
        name="moe_final",
    )(yg, w_t, h2p, x1, gt2, fg, sg, su, sd)


def _moe_routed(h2p, idx, w, pos, counts, x1, gt2, p):
    B, T, half = h2p.shape
    n_slots = B * T * TOP_K
    n_rows = n_slots + N_EXPERTS * MOE_BLOCK
    cnt = counts[:, 0].astype(jnp.int32)
    padded = (cnt + MOE_BLOCK - 1) // MOE_BLOCK * MOE_BLOCK
    pad_end = jnp.cumsum(padded)
    pad_start = pad_end - padded
    dest = _dest_rows(pad_start, idx, pos)
    starts = jnp.arange(n_rows // MOE_BLOCK, dtype=jnp.int32) * MOE_BLOCK
    block_expert = jnp.minimum(jnp.sum((pad_end[None, :] <= starts[:, None]).astype(jnp.int32), axis=1),
                               N_EXPERTS - 1)
    block_valid = jnp.clip((pad_start + cnt)[block_expert] - starts, 0, MOE_BLOCK)
    dest_w = jnp.swapaxes(dest.reshape(B, SUBLANES, T // GATHER_WINDOW, GATHER_WINDOW), 1, 2)
    xs = _scatter_rows(h2p.reshape(B * T, half), dest_w.reshape(-1, SUBLANES, GATHER_WINDOW), n_rows)
    ys = _experts(block_expert, block_valid, xs, p["wg"], p["wu"], p["wd"])
    yg = _gather_rows(ys, jnp.swapaxes(dest[:, :TOP_K], 0, 1).reshape(-1)).reshape(TOP_K, B, T, half)
    return _moe_final(yg, jnp.swapaxes(w, 1, 2), h2p, x1, gt2, p["fg"], p["sg"], p["su"], p["sd"])


def _prepare_weights(w_in, shift_mu, a_w0, a_w_up, a_a0, a_a_up, a_g_up, a_k_k, a_k_a, a_r_k, a_gn_w,
                     a_gn_b, a_proj, b_conv_w, b_a_log, b_dt_bias, b_norm_g, b_proj, w_out, router_w,
                     router_bias, exp_gate, exp_up, exp_down, sh_gate, sh_up, sh_down, norm1_g, norm2_g,
                     final_g):
    W = A_WIDTH
    w = w_in[0]
    c0 = A_COLS
    c1 = c0 + 3 * B_WIDTH
    c2 = c1 + B_WIDTH
    c3 = c2 + 4 * B_HEADS
    pad = jnp.zeros((D_MODEL, SMALL_COLS - 4 * B_HEADS), F32)
    w_a = w[:, :c0].astype(BF16)
    w_b = jnp.concatenate([w[:, c0:c1], w[:, c2:c3], pad], axis=1).astype(BF16)
    w_c = jnp.concatenate([w[:, c1:c2], w[:, c3:]], axis=1).astype(BF16)
    zeros = jnp.zeros((2, A_RANK_W, W), F32)
    up_comb = jnp.concatenate([jnp.concatenate([a_w_up[0], zeros], axis=2),
                               jnp.concatenate([zeros, a_a_up[0]], axis=2)], axis=1)
    head = jnp.arange(W) // A_HEAD_DIM
    ones_bd = (head[:, None] == head[None, :]).astype(BF16)
    small = lambda v: jnp.zeros((1, SMALL_COLS), F32).at[0, 2 * B_HEADS:4 * B_HEADS].set(v.reshape(-1))
    return dict(
        w_a=w_a, w_b=w_b, w_c=w_c, mu=shift_mu[0].reshape(1, A_COLS), w0=a_w0[0], a0=a_a0[0],
        up_comb=up_comb, g_up=a_g_up[0], k_k=a_k_k[0].reshape(1, W), k_a=a_k_a[0].reshape(1, W),
        r_k=a_r_k[0].reshape(1, W), ones_bd=ones_bd, gn_w=a_gn_w[0].reshape(1, W),
        gn_b=a_gn_b[0].reshape(1, W), a_proj=a_proj[0].astype(BF16),
        conv_w=b_conv_w[0].astype(BF16).astype(F32),
        alog=small(b_a_log[0]), dtb=small(b_dt_bias[0]),
        bng=jnp.tile(b_norm_g[0], B_HEADS).reshape(1, B_WIDTH), b_proj=b_proj[0].astype(BF16),
        w_out=w_out[0].astype(BF16), rwt=router_w[0].T, rbias=router_bias[0].reshape(N_EXPERTS, 1),
        wg=exp_gate[0].astype(BF16), wu=exp_up[0].astype(BF16), wd=exp_down[0].astype(BF16),
        sg=sh_gate[0].astype(BF16), su=sh_up[0].astype(BF16), sd=sh_down[0].astype(BF16),
        n1g=norm1_g[0].reshape(1, D_MODEL), n2g=norm2_g[0].reshape(1, D_MODEL),
        fg=final_g.reshape(1, D_MODEL))


def _layer(x, mod, p):
    B = x.shape[0]
    sh1, sc1, gt1, sh2, sc2, gt2 = (m.reshape(B, 1, D_MODEL) for m in jnp.split(mod, 6, axis=-1))
    z_c = _inproj(x, p["n1g"], sc1, sh1, p["w_c"])
    r, kk, v, g, bonus, lw, kd, bd = _rwkv_prep(x, p["n1g"], sc1, sh1, p["w_a"], p["mu"], p["w0"], p["a0"],
                                                p["up_comb"], p["g_up"], p["k_k"], p["k_a"], p["r_k"],
                                                p["ones_bd"])
    yf, yb = _rwkv_scan(r, kk, v, lw, kd, bd)
    q, k, vv, bg = _gdn_prep(x, p["n1g"], sc1, sh1, p["w_b"], p["conv_w"], p["alog"], p["dtb"])
    of, ob = _gdn_scan(q, k, vv, bg)
    x1, h2p, idx, w, pos, counts = _mixer_post(yf, yb, bonus, g, of, ob, z_c, x, gt1, sc2, sh2, p["gn_w"],
                                               p["gn_b"], p["ones_bd"], p["a_proj"], p["bng"], p["b_proj"],
                                               p["w_out"], p["n2g"], p["rwt"], p["rbias"])
    return _moe_routed(h2p, idx, w, pos, counts, x1, gt2, p)


def kernel(x_prompt, x_sample, c_prompt, c_sample, ada_w, ada_b, norm1_g, norm2_g, w_in, shift_mu, a_w0, a_w_up, a_a0, a_a_up, a_g_up, a_k_k, a_k_a, a_r_k, a_gn_w, a_gn_b, a_proj, b_conv_w, b_a_log, b_dt_bias, b_norm_g, b_proj, w_out, router_w, router_bias, exp_gate, exp_up, exp_down, sh_gate, sh_up, sh_down, final_g):
    p = _prepare_weights(w_in, shift_mu, a_w0, a_w_up, a_a0, a_a_up, a_g_up, a_k_k, a_k_a, a_r_k, a_gn_w,
                         a_gn_b, a_proj, b_conv_w, b_a_log, b_dt_bias, b_norm_g, b_proj, w_out, router_w,
                         router_bias, exp_gate, exp_up, exp_down, sh_gate, sh_up, sh_down, norm1_g,
                         norm2_g, final_g)
    nb_p, nb_s = c_prompt.shape[0], c_sample.shape[0]
    rows = -(-(nb_p + nb_s) // SUBLANES) * SUBLANES
    c_all = jnp.concatenate([c_prompt, c_sample, jnp.zeros((rows - nb_p - nb_s, D_MODEL), F32)], axis=0)
    mod = _adaln_mod(c_all, ada_w[0], ada_b[0])
    y_prompt = _layer(x_prompt, mod[:nb_p], p)
    y_sample = _layer(x_sample, mod[nb_p:nb_p + nb_s], p)
    return (y_prompt, y_sample)
```

```python
import functools

import jax
import jax.numpy as jnp
from jax import lax
from jax.experimental import pallas as pl
from jax.experimental.pallas import tpu as pltpu
from jax.experimental.pallas import tpu_sc as plsc

F32 = jnp.float32
BF16 = jnp.bfloat16
ACT = jnp.bfloat16
HIGHEST = lax.Precision.HIGHEST

D_MODEL = 1024
A_HEADS = 8
A_HEAD_DIM = 64
A_WIDTH = A_HEADS * A_HEAD_DIM
A_RANK_W = 64
A_RANK_A = 64
A_RANK_G = 128
A_GN_EPS = 64e-5
A_COLS = 3 * A_WIDTH + A_RANK_W + A_RANK_A + A_RANK_G
B_HEADS = 8
B_HEAD_DIM = 128
B_WIDTH = B_HEADS * B_HEAD_DIM
B_CONV = 5
CHUNK = 64
SCAN_SUB = 4
GDN_COL_GROUP = 512
MIXER_SUB = 2
N_EXPERTS = 64
TOP_K = 6
N_GROUPS = 8
TOPK_GROUPS = 4
GROUP_SIZE = N_EXPERTS // N_GROUPS
D_EXPERT = 256
ROUTED_SCALE = 2.5
EPS = 1e-6
LANES = 128
SUBLANES = 8
HALO = 16
SMALL_COLS = LANES
MOE_BLOCK = 1024
SC_CORES = 2
SC_SUBCORES = 16
SC_WORKERS = SC_CORES * SC_SUBCORES
GATHER_WINDOW = 64
VMEM_LIMIT = 56 * 1024 * 1024
NEG_INF = float("-inf")


def _dot(a, b):
    return jnp.dot(a.astype(BF16), b.astype(BF16), preferred_element_type=F32)


def _dot_nt(a, b):
    return lax.dot_general(a.astype(BF16), b.astype(BF16), (((1,), (1,)), ((), ())),
                           preferred_element_type=F32)


def _dot_tn(a, b):
    return lax.dot_general(a.astype(BF16), b.astype(BF16), (((0,), (0,)), ((), ())),
                           preferred_element_type=F32)


def _dot_hi(a, b):
    return jnp.dot(a, b, precision=HIGHEST, preferred_element_type=F32)


def _split(x):
    hi = x.astype(BF16)
    return hi, (x - hi.astype(F32)).astype(BF16)


def _dot3(a, b, dims=(((1,), (0,)), ((), ()))):
    ah, al = _split(a)
    bh, bl = _split(b)
    d = lambda u, v: lax.dot_general(u, v, dims, preferred_element_type=F32)
    return d(ah, bh) + (d(ah, bl) + d(al, bh))


def _seg_sum(x, ones_bd):
    hi = x.astype(BF16)
    lo = (x - hi.astype(F32)).astype(BF16)
    return (jnp.dot(hi, ones_bd, preferred_element_type=F32)
            + jnp.dot(lo, ones_bd, preferred_element_type=F32))


def _softplus(x):
    return jnp.maximum(x, 0.0) + jnp.log1p(jnp.exp(-jnp.abs(x)))


def _sigmoid(x):
    return 1.0 / (1.0 + jnp.exp(-x))


def _silu(x):
    return x * _sigmoid(x)


def _tri_inverse(a_list, eye):
    n = a_list[0].shape[0]
    xs = [(-a).astype(BF16) for a in a_list]
    ts = [eye - a for a in a_list]
    xs = [jnp.dot(x, x, preferred_element_type=F32).astype(BF16) for x in xs]
    power = 2
    while 2 * power < CHUNK:
        both = [jnp.dot(x, jnp.concatenate([t.astype(BF16), x], axis=1), preferred_element_type=F32)
                for t, x in zip(ts, xs)]
        ts = [t + b[:, 0:n] for t, b in zip(ts, both)]
        xs = [b[:, n:2 * n].astype(BF16) for b in both]
        power *= 2
    return [t + jnp.dot(x, t.astype(BF16), preferred_element_type=F32) for t, x in zip(ts, xs)]


def _params(sem, **extra):
    return pltpu.CompilerParams(dimension_semantics=sem, vmem_limit_bytes=VMEM_LIMIT, **extra)


def _mod_kernel(c_ref, w_ref, b_ref, o_ref):
    c = c_ref[...]
    o_ref[...] = _dot_hi(_silu(c), w_ref[...]) + b_ref[...]


def _adaln_mod(c, ada_w, ada_b):
    rows = c.shape[0]
    n = ada_w.shape[1]
    return pl.pallas_call(
        _mod_kernel,
        grid=(n // D_MODEL,),
        in_specs=[pl.BlockSpec((rows, D_MODEL), lambda j: (0, 0)),
                  pl.BlockSpec((D_MODEL, D_MODEL), lambda j: (0, j)),
                  pl.BlockSpec((1, D_MODEL), lambda j: (0, j))],
        out_specs=pl.BlockSpec((rows, D_MODEL), lambda j: (0, j)),
        out_shape=jax.ShapeDtypeStruct((rows, n), F32),
        compiler_params=_params(("arbitrary",)),
        name="adaln_mod",
    )(c, ada_w, ada_b.reshape(1, n))


def _rms_mod(x, g, sc, sh):
    y = x * lax.rsqrt(jnp.mean(x * x, axis=-1, keepdims=True) + EPS)
    return (y * g) * (1.0 + sc) + sh


def _inproj_kernel(x_ref, g_ref, sc_ref, sh_ref, w_ref, o_ref):
    h = _rms_mod(x_ref[0], g_ref[...], sc_ref[0], sh_ref[0]).astype(BF16)
    o_ref[0] = jnp.dot(h, w_ref[...], preferred_element_type=F32).astype(o_ref.dtype)


def _inproj(x, g, sc, sh, w):
    B, T, D = x.shape
    n = w.shape[1]
    tm = min(512, T)
    return pl.pallas_call(
        _inproj_kernel,
        grid=(B, T // tm),
        in_specs=[pl.BlockSpec((1, tm, D), lambda b, i: (b, i, 0)),
                  pl.BlockSpec((1, D), lambda b, i: (0, 0)),
                  pl.BlockSpec((1, 1, D), lambda b, i: (b, 0, 0)),
                  pl.BlockSpec((1, 1, D), lambda b, i: (b, 0, 0)),
                  pl.BlockSpec((D, n), lambda b, i: (0, 0))],
        out_specs=pl.BlockSpec((1, tm, n), lambda b, i: (b, i, 0)),
        out_shape=jax.ShapeDtypeStruct((B, T, n), ACT),
        compiler_params=_params(("parallel", "parallel")),
        name="in_proj",
    )(x, g, sc, sh, w)


def _halo_specs(tt, width, col_block, seq_len):
    nb = tt // HALO
    last = seq_len // HALO - 1
    return [
        pl.BlockSpec((1, tt, width), lambda b, i: (b, i, col_block)),
        pl.BlockSpec((1, HALO, width), lambda b, i: (b, jnp.maximum(i * nb - 1, 0), col_block)),
        pl.BlockSpec((1, HALO, width), lambda b, i: (b, jnp.minimum((i + 1) * nb, last), col_block)),
    ]


def _halo_rows(zp_ref, zn_ref):
    return zp_ref[0].astype(F32)[HALO - SUBLANES:HALO], zn_ref[0].astype(F32)[0:SUBLANES]


def _normed_with_halo(x_ref, xp_ref, xn_ref, g_ref, sc_ref, sh_ref):
    prev8, next8 = _halo_rows(xp_ref, xn_ref)
    x_ext = jnp.concatenate([prev8, x_ref[0], next8], axis=0)
    return _rms_mod(x_ext, g_ref[...], sc_ref[0], sh_ref[0]).astype(BF16)


def _rwkv_prep_kernel(x_ref, xp_ref, xn_ref, g1_ref, sc_ref, sh_ref, w_ref,
                      mu_ref, w0_ref, a0_ref, up_ref, gup_ref, kk_ref, ka_ref, rk_ref, ones_ref,
                      r_out, kk_out, v_out, g_out, bonus_out, lw_out, kd_out, bd_out, ext_scr):
    tt = x_ref.shape[1]
    i = pl.program_id(1)
    last = pl.num_programs(1) - 1
    h = _normed_with_halo(x_ref, xp_ref, xn_ref, g1_ref, sc_ref, sh_ref)
    project = lambda c0, c1: jnp.dot(h, w_ref[:, c0:c1], preferred_element_type=F32)
    row = lax.broadcasted_iota(jnp.int32, (tt + 2 * SUBLANES, 1), 0)
    inside = ((row >= SUBLANES) | (i > 0)) & ((row < tt + SUBLANES) | (i < last))
    W = A_WIDTH

    def shift_mix(z_ext, c0, c1):
        z_ext = z_ext.astype(ACT).astype(F32)
        ext_scr[:, c0:c1] = jnp.where(inside, z_ext, 0.0)
        z = ext_scr[pl.ds(SUBLANES, tt), c0:c1]
        z_prev = ext_scr[pl.ds(SUBLANES - 1, tt), c0:c1]
        z_next = ext_scr[pl.ds(SUBLANES + 1, tt), c0:c1]
        return z + (0.5 * (z_prev + z_next) - z) * mu_ref[:, c0:c1]

    groups = [(3 * W, A_COLS), (W, 2 * W), (0, W), (2 * W, 3 * W)]
    mixed = []
    z_nxt = project(*groups[0])
    for gi, (c0, c1) in enumerate(groups):
        z_cur = z_nxt
        if gi + 1 < len(groups):
            z_nxt = project(*groups[gi + 1])
        mixed.append(shift_mix(z_cur, c0, c1))
    zwag, zk, zr, zv = mixed
    zwa = zwag[:, 0:LANES]
    zg = zwag[:, LANES:2 * LANES]
    ones_bd = ones_ref[...]
    kk_raw = zk * kk_ref[...]
    kk = kk_raw * lax.rsqrt(_seg_sum(kk_raw * kk_raw, ones_bd) + 1e-6)
    lane = lax.broadcasted_iota(jnp.int32, zwa.shape, 1)
    lhs = jnp.where(lane < A_RANK_W, jnp.tanh(zwa), zwa)
    k_sum = jnp.zeros_like(zk)
    for d in range(2):
        up = _dot(lhs, up_ref[d])
        wl = w0_ref[d:d + 1, :] + up[:, 0:W]
        w_log = -_softplus(-wl) - 0.5
        lw_out[d, 0] = -jnp.exp(w_log)
        a = _sigmoid(a0_ref[d:d + 1, :] + up[:, W:2 * W])
        k_d = zk * (1.0 + (a - 1.0) * ka_ref[...])
        kd_out[d, 0] = k_d.astype(kd_out.dtype)
        bd_out[d, 0] = (kk * a).astype(bd_out.dtype)
        k_sum = k_sum + k_d
    r_out[0] = zr.astype(r_out.dtype)
    kk_out[0] = kk.astype(kk_out.dtype)
    v_out[0] = zv.astype(v_out.dtype)
    g_out[0] = _dot(_sigmoid(zg), gup_ref[...]).astype(g_out.dtype)
    bonus_out[0] = (_seg_sum(zr * k_sum * rk_ref[...], ones_bd) * zv).astype(bonus_out.dtype)


def _front_specs(tt, seq_len, n_cols):
    modrow = pl.BlockSpec((1, 1, D_MODEL), lambda b, i: (b, 0, 0))
    return _halo_specs(tt, D_MODEL, 0, seq_len) + [
        pl.BlockSpec((1, D_MODEL), lambda b, i: (0, 0)), modrow, modrow,
        pl.BlockSpec((D_MODEL, n_cols), lambda b, i: (0, 0))]


def _rwkv_prep(x, g1, sc, sh, w_a, mu, w0, a0, up_comb, g_up, k_k, k_a, r_k, ones_bd):
    B, T, _ = x.shape
    tt = min(512, T)
    W = A_WIDTH
    full = lambda shape: pl.BlockSpec(shape, lambda b, i: (0,) * len(shape))
    tok = pl.BlockSpec((1, tt, W), lambda b, i: (b, i, 0))
    tok2 = pl.BlockSpec((2, 1, tt, W), lambda b, i: (0, b, i, 0))
    s1 = jax.ShapeDtypeStruct((B, T, W), ACT)
    s2 = jax.ShapeDtypeStruct((2, B, T, W), ACT)
    lw = jax.ShapeDtypeStruct((2, B, T, W), F32)
    return pl.pallas_call(
        _rwkv_prep_kernel,
        grid=(B, T // tt),
        in_specs=_front_specs(tt, T, A_COLS) + [
            full((1, A_COLS)), full((2, W)), full((2, W)), full((2, LANES, 2 * W)),
            full((A_RANK_G, W)), full((1, W)), full((1, W)), full((1, W)), full((W, W))],
        out_specs=[tok, tok, tok, tok, tok, tok2, tok2, tok2],
        out_shape=[s1, s1, s1, s1, s1, lw, s2, s2],
        scratch_shapes=[pltpu.VMEM((tt + 2 * SUBLANES, A_COLS), F32)],
        compiler_params=_params(("parallel", "parallel")),
        name="rwkv_prep",
    )(x, x, x, g1, sc, sh, w_a, mu, w0, a0, up_comb, g_up, k_k, k_a, r_k, ones_bd)


def _chunk_masks(reverse):
    n = 2 * CHUNK
    row = lax.broadcasted_iota(jnp.int32, (n, n), 0)
    col = lax.broadcasted_iota(jnp.int32, (n, n), 1)
    same = (row // CHUNK) == (col // CHUNK)
    ti, tj = row % CHUNK, col % CHUNK
    if reverse:
        return same & (ti < tj), same & (ti <= tj), row == col
    return same & (ti > tj), same & (ti >= tj), row == col


def _cumsum_matrix(reverse):
    row = lax.broadcasted_iota(jnp.int32, (CHUNK, CHUNK), 0)
    col = lax.broadcasted_iota(jnp.int32, (CHUNK, CHUNK), 1)
    return ((row <= col) if reverse else (row >= col)).astype(F32)


def _rwkv_chunk_operands(r, kk, v, k, b, cs, lw, reverse):
    C = CHUNK
    cs_end = cs[0:1, :] if reverse else cs[C - 1:C, :]
    m0 = lax.broadcasted_iota(jnp.int32, (C, LANES), 1) < A_HEAD_DIM

    def stack(x):
        return jnp.concatenate([jnp.where(m0, x, 0.0), jnp.where(m0, 0.0, x)], axis=0)

    g_inv = jnp.exp(-cs)
    g_tail = jnp.exp(cs_end - cs)
    strict, incl, _ = _chunk_masks(reverse)
    return dict(rg=stack(r * jnp.exp(cs)), kkg=stack(kk * jnp.exp(cs - lw)), ki=stack(k * g_inv),
                bi=stack(b * g_inv), kt=stack(k * g_tail), bt=stack(b * g_tail), vs=stack(v),
                g_end=jnp.exp(cs_end), strict=strict, incl=incl)


def _rwkv_chunks(ops, hts):
    C = CHUNK
    n = 2 * C
    eye = _chunk_masks(False)[2].astype(F32)
    ps = [_dot_nt(jnp.concatenate([o["kkg"], o["rg"]], axis=0), jnp.concatenate([o["bi"], o["ki"]], axis=0))
          for o in ops]
    a_ab = [jnp.where(o["strict"], p[0:n, 0:n], 0.0) for o, p in zip(ops, ps)]
    a_rb = [jnp.where(o["incl"], p[n:2 * n, 0:n], 0.0).astype(BF16) for o, p in zip(ops, ps)]
    a_k = [jnp.concatenate([jnp.where(o["strict"], p[0:n, n:2 * n], 0.0),
                            jnp.where(o["incl"], p[n:2 * n, n:2 * n], 0.0)], axis=0).astype(BF16)
           for o, p in zip(ops, ps)]
    vsb = [o["vs"].astype(BF16) for o in ops]
    a_kv = [jnp.dot(a, v, preferred_element_type=F32) for a, v in zip(a_k, vsb)]
    akv = [x[0:n] for x in a_kv]
    arkv = [x[n:2 * n] for x in a_kv]
    tinv = _tri_inverse(a_ab, eye)
    wu = [_dot(t, jnp.concatenate([o["kkg"], x], axis=1)).astype(BF16)
          for t, o, x in zip(tinv, ops, akv)]
    r2 = [jnp.dot(a, w, preferred_element_type=F32) for a, w in zip(a_rb, wu)]
    btb = [o["bt"].astype(BF16) for o in ops]
    btw = [_dot_tn(b, w[:, 0:LANES]) for b, w in zip(btb, wu)]
    hloc = [_dot_tn(jnp.concatenate([v, -w[:, LANES:2 * LANES]], axis=0),
                    jnp.concatenate([o["kt"].astype(BF16), b], axis=0))
            for v, o, w, b in zip(vsb, ops, wu, btb)]
    ys, hts_new = [], []
    for o, r, yl, ht, bw, hl in zip(ops, r2, arkv, hts, btw, hloc):
        q_s = o["rg"] - r[:, 0:LANES]
        yloc_s = yl - r[:, LANES:2 * LANES]
        ys.append(yloc_s[0:C] + yloc_s[C:n] + _dot_nt(q_s[0:C] + q_s[C:n], ht))
        hts_new.append(ht * o["g_end"] - _dot_nt(ht, bw) + hl)
    return ys, hts_new


def _rwkv_scan_kernel(rf, kkf, vf, lwf, kf, bf, rb, kkb, vb, lwb, kb, bb, yf_out, yb_out, h_scr):
    @pl.when(pl.program_id(1) == 0)
    def _():
        h_scr[...] = jnp.zeros_like(h_scr)

    n_pairs = A_WIDTH // LANES
    sls = [slice(i * LANES, (i + 1) * LANES) for i in range(n_pairs)]
    f32 = lambda a: a.astype(F32)
    hts = [h_scr[d, i] for d in range(2) for i in range(n_pairs)]
    for j in range(SCAN_SUB):
        ops = []
        for d, (r_ref, kk_ref, v_ref, lw_ref, k_ref, b_ref) in enumerate(
                ((rf, kkf, vf, lwf, kf, bf), (rb, kkb, vb, lwb, kb, bb))):
            reverse = d == 1
            rows = pl.ds((SCAN_SUB - 1 - j if reverse else j) * CHUNK, CHUNK)
            lw_all = lw_ref[0, 0, rows, :]
            cs_all = _dot_hi(_cumsum_matrix(reverse), lw_all)
            ops += [_rwkv_chunk_operands(f32(r_ref[0, rows, sl]), f32(kk_ref[0, rows, sl]),
                                         f32(v_ref[0, rows, sl]), f32(k_ref[0, 0, rows, sl]),
                                         f32(b_ref[0, 0, rows, sl]), cs_all[:, sl], lw_all[:, sl], reverse)
                    for sl in sls]
        ys, hts = _rwkv_chunks(ops, hts)
        for d, y_out in enumerate((yf_out, yb_out)):
            rows = pl.ds((SCAN_SUB - 1 - j if d == 1 else j) * CHUNK, CHUNK)
            for i, sl in enumerate(sls):
                y_out[0, rows, sl] = ys[d * n_pairs + i].astype(y_out.dtype)
    for d in range(2):
        for i in range(n_pairs):
            h_scr[d, i] = hts[d * n_pairs + i]


def _rwkv_scan(r, kk, v, lw, kd, bd):
    B, T, W = r.shape
    rows = CHUNK * SCAN_SUB
    N = T // rows
    fwd = pl.BlockSpec((1, rows, W), lambda b, c: (b, c, 0))
    bwd = pl.BlockSpec((1, rows, W), lambda b, c: (b, N - 1 - c, 0))
    fwd2 = pl.BlockSpec((1, 1, rows, W), lambda b, c: (0, b, c, 0))
    bwd2 = pl.BlockSpec((1, 1, rows, W), lambda b, c: (1, b, N - 1 - c, 0))
    out = jax.ShapeDtypeStruct((B, T, W), ACT)
    return pl.pallas_call(
        _rwkv_scan_kernel,
        grid=(B, N),
        in_specs=[fwd, fwd, fwd, fwd2, fwd2, fwd2, bwd, bwd, bwd, bwd2, bwd2, bwd2],
        out_specs=[fwd, bwd],
        out_shape=[out, out],
        scratch_shapes=[pltpu.VMEM((2, W // LANES, LANES, LANES), F32)],
        compiler_params=_params(("parallel", "arbitrary")),
        name="rwkv_scan",
    )(r, kk, v, lw, kd, bd, r, kk, v, lw, kd, bd)


def _gdn_prep_kernel(x_ref, xp_ref, xn_ref, g1_ref, sc_ref, sh_ref, w_ref, cw_ref, alog_ref, dtb_ref,
                     q_out, k_out, v_out, bg_out, acc_scr):
    tt = x_ref.shape[1]
    i = pl.program_id(1)
    last = pl.num_programs(1) - 1
    half = B_CONV // 2
    Wq = 3 * B_WIDTH
    h = _normed_with_halo(x_ref, xp_ref, xn_ref, g1_ref, sc_ref, sh_ref)
    project = lambda c0, c1: jnp.dot(h, w_ref[:, c0:c1], preferred_element_type=F32)
    offsets = [j - half for j in range(B_CONV) if j != half]
    row = lax.broadcasted_iota(jnp.int32, (len(offsets) * tt, tt), 0)
    col = lax.broadcasted_iota(jnp.int32, (len(offsets) * tt, tt), 1)
    blk = row // tt
    off = jnp.where(blk < half, blk - half, blk - half + 1)
    shift_rows = (col == row - blk * tt + off).astype(BF16)
    r8 = lax.broadcasted_iota(jnp.int32, (SUBLANES, 1), 0)

    def conv_group(z_ext, c0, c1):
        zb = z_ext[SUBLANES:SUBLANES + tt].astype(BF16)
        shifted = jnp.dot(shift_rows, zb, preferred_element_type=F32)
        acc = zb.astype(F32) * cw_ref[half:half + 1, c0:c1]
        for m, o in enumerate(offsets):
            acc = acc + shifted[m * tt:(m + 1) * tt] * cw_ref[o + half:o + half + 1, c0:c1]
        acc_scr[:, c0:c1] = acc
        prev8 = jnp.where(i == 0, 0.0, z_ext[0:SUBLANES]).astype(BF16).astype(F32)
        next8 = jnp.where(i == last, 0.0, z_ext[SUBLANES + tt:]).astype(BF16).astype(F32)
        top = bot = None
        for o in offsets:
            w = cw_ref[o + half:o + half + 1, c0:c1]
            if o < 0:
                f = jnp.where(r8 < -o, pltpu.roll(prev8, -o, axis=0), 0.0) * w
                top = f if top is None else top + f
            else:
                f = jnp.where(r8 >= SUBLANES - o, pltpu.roll(next8, SUBLANES - o, axis=0), 0.0) * w
                bot = f if bot is None else bot + f
        acc_scr[pl.ds(0, SUBLANES), c0:c1] += top
        acc_scr[pl.ds(tt - SUBLANES, SUBLANES), c0:c1] += bot
        return _silu(acc_scr[:, c0:c1])

    bounds = list(range(0, Wq + 1, GDN_COL_GROUP))
    z_next = project(bounds[0], bounds[1])
    for gi in range(len(bounds) - 1):
        c0, c1 = bounds[gi], bounds[gi + 1]
        z_cur = z_next
        z_next = project(bounds[gi + 1], bounds[gi + 2]) if gi + 2 < len(bounds) else project(Wq, Wq + SMALL_COLS)
        act = conv_group(z_cur, c0, c1)
        for hh in range(GDN_COL_GROUP // B_HEAD_DIM):
            x = act[:, hh * B_HEAD_DIM:(hh + 1) * B_HEAD_DIM]
            lo = c0 % B_WIDTH + hh * B_HEAD_DIM
            if c0 < 2 * B_WIDTH:
                x = x * lax.rsqrt(jnp.sum(x * x, axis=-1, keepdims=True) + 1e-6)
                if c0 < B_WIDTH:
                    x = x * (B_HEAD_DIM ** -0.5)
            out = (q_out, k_out, v_out)[c0 // B_WIDTH]
            out[0, :, lo:lo + B_HEAD_DIM] = x.astype(out.dtype)
    s = z_next[SUBLANES:SUBLANES + tt]
    lane = lax.broadcasted_iota(jnp.int32, s.shape, 1)
    beta = _sigmoid(s)
    g = -jnp.exp(alog_ref[...]) * _softplus(s + dtb_ref[...])
    bg_out[0] = jnp.where(lane < 2 * B_HEADS, beta, g)


def _gdn_prep(x, g1, sc, sh, w_b, conv_w, alog_vec, dtb_vec):
    B, T, _ = x.shape
    Wq = 3 * B_WIDTH
    tt = min(256, T)
    full = lambda shape: pl.BlockSpec(shape, lambda b, i: (0,) * len(shape))
    tok = pl.BlockSpec((1, tt, B_WIDTH), lambda b, i: (b, i, 0))
    s1 = jax.ShapeDtypeStruct((B, T, B_WIDTH), ACT)
    return pl.pallas_call(
        _gdn_prep_kernel,
        grid=(B, T // tt),
        in_specs=_front_specs(tt, T, Wq + SMALL_COLS) + [full((B_CONV, Wq)), full((1, SMALL_COLS)),
                                                          full((1, SMALL_COLS))],
        out_specs=[tok, tok, tok, pl.BlockSpec((1, tt, SMALL_COLS), lambda b, i: (b, i, 0))],
        out_shape=[s1, s1, s1, jax.ShapeDtypeStruct((B, T, SMALL_COLS), F32)],
        scratch_shapes=[pltpu.VMEM((tt, Wq), F32)],
        compiler_params=_params(("parallel", "parallel")),
        name="gdn_prep",
    )(x, x, x, g1, sc, sh, w_b, conv_w, alog_vec, dtb_vec)


def _gdn_chunk_operands(q0, q1, k0, k1, v0, v1, gc2, gct2, bg2, j0, j1, reverse):
    C = CHUNK
    n = 2 * C
    strict, incl, _ = _chunk_masks(reverse)
    top = lax.broadcasted_iota(jnp.int32, (n, 1), 0) < C
    left = lax.broadcasted_iota(jnp.int32, (1, n), 1) < C
    g0, g1 = j0 + 2 * B_HEADS, j1 + 2 * B_HEADS
    gcol = jnp.where(top, gc2[:, g0:g0 + 1], gc2[:, g1:g1 + 1])
    grow = jnp.where(left, gct2[g0:g0 + 1, :], gct2[g1:g1 + 1, :])
    beta = jnp.where(top, bg2[:, j0:j0 + 1], bg2[:, j1:j1 + 1])
    e = 0 if reverse else C - 1
    glast = jnp.where(top, gc2[e:e + 1, g0:g0 + 1], gc2[e:e + 1, g1:g1 + 1])
    ks = jnp.concatenate([k0, k1], axis=0)
    qs = jnp.concatenate([q0, q1], axis=0)
    vs = jnp.concatenate([v0, v1], axis=0)
    egc = jnp.exp(gcol)
    kb = ks * beta
    return dict(gam=jnp.exp(jnp.where(incl, gcol - grow, NEG_INF)), strict=strict, ks=ks,
                kbq=jnp.concatenate([kb, qs], axis=0),
                rhs=jnp.concatenate([vs * beta, kb * egc], axis=1),
                qh=(qs * egc).astype(BF16), ktail=(ks * jnp.exp(glast - gcol)).astype(BF16),
                decay=jnp.exp(glast))


def _gdn_chunks(ops, states):
    C = CHUNK
    n = 2 * C
    eye = _chunk_masks(False)[2].astype(F32)
    ps = [_dot_nt(o["kbq"], o["ks"]) for o in ops]
    lower = [jnp.where(o["strict"], p[0:n] * o["gam"], 0.0) for o, p in zip(ops, ps)]
    aqk = [(p[n:2 * n] * o["gam"]).astype(BF16) for o, p in zip(ops, ps)]
    tinv = [t.astype(BF16) for t in _tri_inverse(lower, eye)]
    uw = [jnp.dot(t, o["rhs"].astype(BF16), preferred_element_type=F32) for t, o in zip(tinv, ops)]
    lx = []
    for l, x in zip(lower, uw):
        lh, ll = _split(l)
        xh, xl = _split(x)
        first = jnp.dot(lh, jnp.concatenate([xh, xl], axis=1), preferred_element_type=F32)
        w2 = x.shape[1]
        lx.append(first[:, 0:w2] + (first[:, w2:2 * w2] + jnp.dot(ll, xh, preferred_element_type=F32)))
    resid = [o["rhs"] - x - y for o, x, y in zip(ops, uw, lx)]
    uw = [x + jnp.dot(t, r.astype(BF16), preferred_element_type=F32) for x, t, r in zip(uw, tinv, resid)]
    sb = [(s0.astype(BF16), s1.astype(BF16)) for s0, s1 in states]
    wq = [[jnp.dot(jnp.concatenate([u[h * C:(h + 1) * C, LANES:].astype(BF16), o["qh"][h * C:(h + 1) * C]],
                                   axis=0), s[h], preferred_element_type=F32) for h in range(2)]
          for u, o, s in zip(uw, ops, sb)]
    ws = [jnp.concatenate([x[0][0:C], x[1][0:C]], axis=0) for x in wq]
    qss = [jnp.concatenate([x[0][C:n], x[1][C:n]], axis=0) for x in wq]
    v_new = [(u[:, 0:LANES] - w).astype(BF16) for u, w in zip(uw, ws)]
    outs = [q + jnp.dot(a, v, preferred_element_type=F32) for q, a, v in zip(qss, aqk, v_new)]
    new_states = []
    for o, v, (s0, s1) in zip(ops, v_new, states):
        new_states.append((s0 * o["decay"][0:1, :] + _dot_tn(o["ktail"][0:C], v[0:C]),
                           s1 * o["decay"][C:C + 1, :] + _dot_tn(o["ktail"][C:n], v[C:n])))
    return [(x[0:C], x[C:n]) for x in outs], new_states


def _gdn_scan_kernel(qf, kf, vf, bgf, qb, kb, vb, bgb, of_out, ob_out, s_scr):
    @pl.when(pl.program_id(1) == 0)
    def _():
        s_scr[...] = jnp.zeros_like(s_scr)

    D = B_HEAD_DIM
    n_pairs = B_HEADS // 2
    sl = lambda h: slice(h * D, (h + 1) * D)
    f32 = lambda a: a.astype(F32)
    states = [(s_scr[d, 2 * i], s_scr[d, 2 * i + 1]) for d in range(2) for i in range(n_pairs)]
    for j in range(SCAN_SUB):
        ops = []
        for d, (q_ref, k_ref, v_ref, bg_ref) in enumerate(((qf, kf, vf, bgf), (qb, kb, vb, bgb))):
            reverse = d == 1
            rows = pl.ds((SCAN_SUB - 1 - j if reverse else j) * CHUNK, CHUNK)
            bg = bg_ref[0, rows, :]
            gc = _dot_hi(_cumsum_matrix(reverse), bg)
            bg2 = jnp.concatenate([bg, bg], axis=0)
            gc2 = jnp.concatenate([gc, gc], axis=0)
            gct2 = gc2.T
            for i in range(n_pairs):
                h0, h1 = 2 * i, 2 * i + 1
                ops.append(_gdn_chunk_operands(
                    f32(q_ref[0, rows, sl(h0)]), f32(q_ref[0, rows, sl(h1)]), f32(k_ref[0, rows, sl(h0)]),
                    f32(k_ref[0, rows, sl(h1)]), f32(v_ref[0, rows, sl(h0)]), f32(v_ref[0, rows, sl(h1)]),
                    gc2, gct2, bg2, d * B_HEADS + h0, d * B_HEADS + h1, reverse))
        outs, states = _gdn_chunks(ops, states)
        for d, o_out in enumerate((of_out, ob_out)):
            rows = pl.ds((SCAN_SUB - 1 - j if d == 1 else j) * CHUNK, CHUNK)
            for i in range(n_pairs):
                o0, o1 = outs[d * n_pairs + i]
                o_out[0, rows, sl(2 * i)] = o0.astype(o_out.dtype)
                o_out[0, rows, sl(2 * i + 1)] = o1.astype(o_out.dtype)
    for d in range(2):
        for i in range(n_pairs):
            s_scr[d, 2 * i], s_scr[d, 2 * i + 1] = states[d * n_pairs + i]


def _gdn_scan(q, k, v, bg):
    B, T, W = q.shape
    rows = CHUNK * SCAN_SUB
    N = T // rows
    fwd = pl.BlockSpec((1, rows, W), lambda b, c: (b, c, 0))
    bwd = pl.BlockSpec((1, rows, W), lambda b, c: (b, N - 1 - c, 0))
    sfwd = pl.BlockSpec((1, rows, SMALL_COLS), lambda b, c: (b, c, 0))
    sbwd = pl.BlockSpec((1, rows, SMALL_COLS), lambda b, c: (b, N - 1 - c, 0))
    out = jax.ShapeDtypeStruct((B, T, W), ACT)
    return pl.pallas_call(
        _gdn_scan_kernel,
        grid=(B, N),
        in_specs=[fwd, fwd, fwd, sfwd, bwd, bwd, bwd, sbwd],
        out_specs=[fwd, bwd],
        out_shape=[out, out],
        scratch_shapes=[pltpu.VMEM((2, B_HEADS, B_HEAD_DIM, B_HEAD_DIM), F32)],
        compiler_params=_params(("parallel", "arbitrary")),
        name="gdn_scan",
    )(q, k, v, bg, q, k, v, bg)


def _pack_bf16_pairs(x):
    n = x.shape[1] // 2
    hi = lax.bitcast_convert_type(x[:, :n].astype(BF16).astype(F32), jnp.int32)
    lo = lax.bitcast_convert_type(x[:, n:].astype(BF16).astype(F32), jnp.int32)
    return hi | lax.shift_right_logical(lo, 16)


def _unpack_bf16_pairs(p):
    a = lax.bitcast_convert_type(p & jnp.int32(-65536), F32)
    b = lax.bitcast_convert_type(lax.shift_left(p, 16), F32)
    return a.astype(BF16), b.astype(BF16)


def _route(scores, biased, base):
    tt = scores.shape[-1]
    shape3 = (N_GROUPS, GROUP_SIZE, tt)
    s3 = scores.reshape(shape3)
    b3 = biased.reshape(shape3)
    jid = lax.broadcasted_iota(jnp.int32, shape3, 1).astype(F32)
    gid = lax.broadcasted_iota(jnp.int32, shape3, 0).astype(F32)
    m1 = jnp.max(b3, axis=1, keepdims=True)
    first = jnp.min(jnp.where(b3 == m1, jid, float(GROUP_SIZE)), axis=1, keepdims=True)
    m2 = jnp.max(jnp.where(jid == first, NEG_INF, b3), axis=1, keepdims=True)
    gs = m1 + m2
    grp = lax.broadcasted_iota(jnp.int32, (N_GROUPS, 1, tt), 0).astype(F32)
    keep = jnp.zeros((N_GROUPS, 1, tt), F32)
    for _ in range(TOPK_GROUPS):
        m = jnp.max(gs, axis=0, keepdims=True)
        pick = grp == jnp.min(jnp.where(gs == m, grp, float(N_GROUPS)), axis=0, keepdims=True)
        keep = jnp.where(pick, 1.0, keep)
        gs = jnp.where(pick, NEG_INF, gs)
    work = jnp.where(keep > 0.0, b3, NEG_INF)
    eid = gid * float(GROUP_SIZE) + jid
    chosen = jnp.zeros(shape3, F32)
    sum01 = lambda a: jnp.sum(jnp.sum(a, axis=1, keepdims=True), axis=0, keepdims=True)
    ids, raw = [], []
    for _ in range(TOP_K):
        m = jnp.max(jnp.max(work, axis=1, keepdims=True), axis=0, keepdims=True)
        cand = jnp.where(work == m, eid, float(N_EXPERTS))
        first = jnp.min(jnp.min(cand, axis=1, keepdims=True), axis=0, keepdims=True)
        pick = eid == first
        ids.append(first)
        raw.append(sum01(jnp.where(pick, s3, 0.0)))
        chosen = jnp.where(pick, 1.0, chosen)
        work = jnp.where(pick, NEG_INF, work)
    total = raw[0]
    for r in raw[1:]:
        total = total + r
    flat = chosen.reshape(N_EXPERTS, tt)
    earlier = (lax.broadcasted_iota(jnp.int32, (tt, tt), 0)
               < lax.broadcasted_iota(jnp.int32, (tt, tt), 1)).astype(BF16)
    prefix = (jnp.dot(flat.astype(BF16), earlier, preferred_element_type=F32) + base).reshape(shape3)
    row = lax.broadcasted_iota(jnp.int32, (SUBLANES, tt), 0)
    idx8 = jnp.zeros((SUBLANES, tt), F32)
    w8 = jnp.zeros((SUBLANES, tt), F32)
    pos8 = jnp.zeros((SUBLANES, tt), F32)
    for k in range(TOP_K):
        rank = sum01(jnp.where(eid == ids[k], prefix, 0.0))
        idx8 = jnp.where(row == k, ids[k].reshape(1, tt), idx8)
        w8 = jnp.where(row == k, (raw[k] / total * ROUTED_SCALE).reshape(1, tt), w8)
        pos8 = jnp.where(row == k, rank.reshape(1, tt), pos8)
    counts = jnp.sum(flat, axis=1, keepdims=True)
    return idx8.astype(jnp.int32), w8, pos8.astype(jnp.int32), counts


def _mixer_post_kernel(yf, yb, bonus, g, of, ob, n1g, sc1, sh1, wc, x_ref, gt1, sc2, sh2,
                       gnw, gnb, ones_ref, aproj, bng, bproj, wout, n2g, rwt, rbias,
                       x1_out, h2_out, idx_out, w_out_ref, pos_out, counts_out):
    @pl.when((pl.program_id(0) == 0) & (pl.program_id(1) == 0))
    def _():
        counts_out[...] = jnp.zeros_like(counts_out)

    ones_bd = ones_ref[...]
    n_sub = MIXER_SUB
    rs = x_ref.shape[1] // n_sub
    sub = range(n_sub)
    ld = lambda ref, k: ref[0, pl.ds(k * rs, rs), :].astype(F32)
    y = [ld(yf, k) + ld(yb, k) for k in sub]
    mean = [_seg_sum(v, ones_bd) * (1.0 / A_HEAD_DIM) for v in y]
    dlt = [v - m for v, m in zip(y, mean)]
    var = [_seg_sum(v * v, ones_bd) * (1.0 / A_HEAD_DIM) for v in dlt]
    yn = [v * lax.rsqrt(s + A_GN_EPS) * gnw[...] + gnb[...] for v, s in zip(dlt, var)]
    y_a = [_dot((v + ld(bonus, k)) * ld(g, k), aproj[...]) for k, v in zip(sub, yn)]

    def head_norm(o):
        parts = []
        for h in range(B_HEADS):
            oh = o[:, h * B_HEAD_DIM:(h + 1) * B_HEAD_DIM]
            parts.append(oh * lax.rsqrt(jnp.mean(oh * oh, axis=-1, keepdims=True) + EPS))
        return jnp.concatenate(parts, axis=1)

    zc = [jnp.dot(_rms_mod(ld(x_ref, k), n1g[...], sc1[0], sh1[0]).astype(BF16), wc[...],
                  preferred_element_type=F32).astype(ACT).astype(F32) for k in sub]
    D = D_MODEL
    on = [head_norm(ld(of, k) + ld(ob, k)) * bng[...] * _silu(zc[k][:, 0:D]) for k in sub]
    y_b = [_dot(v, bproj[...]) for v in on]
    u = [_sigmoid(zc[k][:, D:2 * D]) * a + _sigmoid(zc[k][:, 2 * D:3 * D]) * b
         for k, a, b in zip(sub, y_a, y_b)]
    x1 = [ld(x_ref, k) + gt1[0] * _dot(v, wout[...]) for k, v in zip(sub, u)]
    h2 = [_rms_mod(v, n2g[...], sc2[0], sh2[0]) for v in x1]
    scores = [_sigmoid(_dot_nt(rwt[...], v)) for v in h2]
    base = counts_out[:, 0:1]
    for k in sub:
        rows, lanes = pl.ds(k * rs, rs), pl.ds(k * rs, rs)
        x1_out[0, rows, :] = x1[k]
        h2_out[0, rows, :] = _pack_bf16_pairs(h2[k])
        idx8, w8, pos8, counts = _route(scores[k], scores[k] + rbias[...], base)
        idx_out[0, :, lanes] = idx8
        w_out_ref[0, :, lanes] = w8
        pos_out[0, :, lanes] = pos8
        base = base + counts
    counts_out[...] = jnp.broadcast_to(base, counts_out.shape)


def _mixer_post(yf, yb, bonus, g, of, ob, n1g, sc1, sh1, w_c, x, gt1, sc2, sh2, gn_w, gn_b, ones_bd, a_proj,
                bng, b_proj, w_out, n2g, rwt, rbias):
    B, T, D = x.shape
    tt = min(512, T)
    full = lambda shape: pl.BlockSpec(shape, lambda b, i: (0,) * len(shape))
    tokw = lambda w, col=0: pl.BlockSpec((1, tt, w), lambda b, i: (b, i, col))
    modrow = pl.BlockSpec((1, 1, D), lambda b, i: (b, 0, 0))
    pick = pl.BlockSpec((1, SUBLANES, tt), lambda b, i: (b, 0, i))
    W = A_WIDTH
    return pl.pallas_call(
        _mixer_post_kernel,
        grid=(B, T // tt),
        in_specs=[tokw(W), tokw(W), tokw(W), tokw(W), tokw(D), tokw(D),
                  full((1, D)), modrow, modrow, full((D, 3 * D)), tokw(D), modrow, modrow, modrow,
                  full((1, W)), full((1, W)), full((W, W)), full((W, D)), full((1, D)), full((D, D)),
                  full((D, D)), full((1, D)), full((N_EXPERTS, D)), full((N_EXPERTS, 1))],
        out_specs=[tokw(D), tokw(D // 2), pick, pick, pick,
                   pl.BlockSpec((N_EXPERTS, LANES), lambda b, i: (0, 0))],
        out_shape=[jax.ShapeDtypeStruct((B, T, D), F32), jax.ShapeDtypeStruct((B, T, D // 2), jnp.int32),
                   jax.ShapeDtypeStruct((B, SUBLANES, T), jnp.int32),
                   jax.ShapeDtypeStruct((B, SUBLANES, T), F32),
                   jax.ShapeDtypeStruct((B, SUBLANES, T), jnp.int32),
                   jax.ShapeDtypeStruct((N_EXPERTS, LANES), F32)],
        compiler_params=_params(("arbitrary", "arbitrary")),
        name="mixer_post",
    )(yf, yb, bonus, g, of, ob, n1g, sc1, sh1, w_c, x, gt1, sc2, sh2, gn_w, gn_b, ones_bd, a_proj, bng,
      b_proj, w_out, n2g, rwt, rbias)


def _gather_rows(table, idx):
    n_rows, width = idx.shape[0], table.shape[1]
    per_worker = n_rows // SC_WORKERS
    n_windows = per_worker // GATHER_WINDOW
    mesh = plsc.VectorSubcoreMesh(core_axis_name="c", subcore_axis_name="s")

    @functools.partial(
        pl.kernel, mesh=mesh, out_type=jax.ShapeDtypeStruct((n_rows, width), table.dtype),
        scratch_types=[pltpu.VMEM((GATHER_WINDOW,), jnp.int32),
                       pltpu.VMEM((GATHER_WINDOW, width), table.dtype),
                       pltpu.SemaphoreType.DMA])
    def gather(table_hbm, idx_hbm, out_hbm, idx_v, rows_v, sem):
        worker = lax.axis_index("s") * SC_CORES + lax.axis_index("c")

        @pl.loop(0, n_windows)
        def _(j):
            off = pl.multiple_of(worker * per_worker + j * GATHER_WINDOW, GATHER_WINDOW)
            pltpu.sync_copy(idx_hbm.at[pl.ds(off, GATHER_WINDOW)], idx_v)
            pltpu.async_copy(table_hbm.at[idx_v], rows_v, sem).wait()
            pltpu.sync_copy(rows_v, out_hbm.at[pl.ds(off, GATHER_WINDOW)])

    return gather(table, idx)


def _scatter_rows(rows, dest_w, n_out):
    n_tok, width = rows.shape
    per_worker = n_tok // GATHER_WINDOW // SC_WORKERS
    mesh = plsc.VectorSubcoreMesh(core_axis_name="c", subcore_axis_name="s")

    @functools.partial(
        pl.kernel, mesh=mesh, out_type=jax.ShapeDtypeStruct((n_out, width), rows.dtype),
        scratch_types=[pltpu.VMEM((SUBLANES, GATHER_WINDOW), jnp.int32),
                       pltpu.VMEM((GATHER_WINDOW, width), rows.dtype)])
    def scatter(rows_hbm, dest_hbm, out_hbm, idx_v, rows_v):
        worker = lax.axis_index("s") * SC_CORES + lax.axis_index("c")

        @pl.loop(0, per_worker)
        def _(j):
            win = worker * per_worker + j
            pltpu.sync_copy(dest_hbm.at[win], idx_v)
            pltpu.sync_copy(rows_hbm.at[pl.ds(pl.multiple_of(win * GATHER_WINDOW, GATHER_WINDOW),
                                              GATHER_WINDOW)], rows_v)
            for k in range(TOP_K):
                pltpu.sync_copy(rows_v, out_hbm.at[idx_v.at[k]])

    return scatter(rows, dest_w)


def _dest_kernel(start_ref, idx_ref, pos_ref, dest_ref):
    idx = idx_ref[0]
    dest = jnp.zeros_like(idx)
    for e in range(N_EXPERTS):
        dest = jnp.where(idx == e, start_ref[e], dest)
    dest_ref[0] = dest + pos_ref[0]


def _dest_rows(pad_start, idx, pos):
    B, _, T = idx.shape
    tt = min(2048, T)
    spec = pl.BlockSpec((1, SUBLANES, tt), lambda b, i, start: (b, 0, i))
    return pl.pallas_call(
        _dest_kernel,
        grid_spec=pltpu.PrefetchScalarGridSpec(num_scalar_prefetch=1, grid=(B, T // tt),
                                               in_specs=[spec, spec], out_specs=spec),
        out_shape=jax.ShapeDtypeStruct(idx.shape, jnp.int32),
        compiler_params=_params(("parallel", "parallel")),
        name="dest_rows",
    )(pad_start, idx, pos)


def _expert_kernel(be_ref, valid_ref, xs_ref, wg, wu, wd, ys_ref):
    n_valid = valid_ref[pl.program_id(0)]

    @pl.when(n_valid > 0)
    def _():
        half = D_MODEL // 2
        row = lax.broadcasted_iota(jnp.int32, xs_ref.shape, 0)
        a, b = _unpack_bf16_pairs(jnp.where(row < n_valid, xs_ref[...], 0))
        mm = lambda w: (jnp.dot(a, w[0, 0:half, :], preferred_element_type=F32)
                        + jnp.dot(b, w[0, half:D_MODEL, :], preferred_element_type=F32))
        hid = _silu(mm(wg)) * mm(wu)
        ys_ref[...] = _pack_bf16_pairs(_dot(hid, wd[0]))


def _experts(block_expert, block_valid, xs, wg, wu, wd):
    n_rows, half = xs.shape
    rows = pl.BlockSpec((MOE_BLOCK, half), lambda i, be, valid: (i, 0))
    wspec = lambda shape: pl.BlockSpec(shape, lambda i, be, valid: (be[i], 0, 0))
    return pl.pallas_call(
        _expert_kernel,
        grid_spec=pltpu.PrefetchScalarGridSpec(
            num_scalar_prefetch=2, grid=(n_rows // MOE_BLOCK,),
            in_specs=[rows, wspec((1, D_MODEL, D_EXPERT)), wspec((1, D_MODEL, D_EXPERT)),
                      wspec((1, D_EXPERT, D_MODEL))],
            out_specs=rows),
        out_shape=jax.ShapeDtypeStruct((n_rows, half), jnp.int32),
        compiler_params=_params(("arbitrary",)),
        name="experts",
    )(block_expert, block_valid, xs, wg, wu, wd)


def _moe_final_kernel(yg_ref, w_ref, h_ref, x1_ref, gt2, fg, sg, su, sd, o_ref):
    half = D_MODEL // 2
    a, b = _unpack_bf16_pairs(h_ref[0])
    mm = lambda w: (jnp.dot(a, w[0:half, :], preferred_element_type=F32)
                    + jnp.dot(b, w[half:D_MODEL, :], preferred_element_type=F32))
    shared = _dot(_silu(mm(sg)) * mm(su), sd[...])
    w = w_ref[0]
    lo = hi = None
    for k in range(TOP_K):
        ya, yb = _unpack_bf16_pairs(yg_ref[k, 0])
        wk = w[:, k:k + 1]
        lo = ya.astype(F32) * wk if lo is None else lo + ya.astype(F32) * wk
        hi = yb.astype(F32) * wk if hi is None else hi + yb.astype(F32) * wk
    x2 = x1_ref[0] + gt2[0] * (jnp.concatenate([lo, hi], axis=1) + shared)
    o_ref[0] = x2 * lax.rsqrt(jnp.mean(x2 * x2, axis=-1, keepdims=True) + EPS) * fg[...]


def _moe_final(yg, w_t, h2p, x1, gt2, fg, sg, su, sd):
    B, T, D = x1.shape
    tm = min(512, T)
    tok = lambda w: pl.BlockSpec((1, tm, w), lambda b, i: (b, i, 0))
    full = lambda shape: pl.BlockSpec(shape, lambda b, i: (0,) * len(shape))
    return pl.pallas_call(
        _moe_final_kernel,
        grid=(B, T // tm),
        in_specs=[pl.BlockSpec((TOP_K, 1, tm, D // 2), lambda b, i: (0, b, i, 0)), tok(SUBLANES),
                  tok(D // 2), tok(D), pl.BlockSpec((1, 1, D), lambda b, i: (b, 0, 0)), full((1, D)),
                  full((D, D_EXPERT)), full((D, D_EXPERT)), full((D_EXPERT, D))],
        out_specs=tok(D),
        out_shape=jax.ShapeDtypeStruct((B, T, D), F32),
        compiler_params=_params(("parallel", "parallel")),
        name="moe_final",
    )(yg, w_t, h2p, x1, gt2, fg, sg, su, sd)


def _moe_routed(h2p, idx, w, pos, counts, x1, gt2, p):
    B, T, half = h2p.shape
    n_slots = B * T * TOP_K
    n_rows = n_slots + N_EXPERTS * MOE_BLOCK
    cnt = counts[:, 0].astype(jnp.int32)
    padded = (cnt + MOE_BLOCK - 1) // MOE_BLOCK * MOE_BLOCK
    pad_end = jnp.cumsum(padded)
    pad_start = pad_end - padded
    dest = _dest_rows(pad_start, idx, pos)
    starts = jnp.arange(n_rows // MOE_BLOCK, dtype=jnp.int32) * MOE_BLOCK
    block_expert = jnp.minimum(jnp.sum((pad_end[None, :] <= starts[:, None]).astype(jnp.int32), axis=1),
                               N_EXPERTS - 1)
    block_valid = jnp.clip((pad_start + cnt)[block_expert] - starts, 0, MOE_BLOCK)
    dest_w = jnp.swapaxes(dest.reshape(B, SUBLANES, T // GATHER_WINDOW, GATHER_WINDOW), 1, 2)
    xs = _scatter_rows(h2p.reshape(B * T, half), dest_w.reshape(-1, SUBLANES, GATHER_WINDOW), n_rows)
    ys = _experts(block_expert, block_valid, xs, p["wg"], p["wu"], p["wd"])
    yg = _gather_rows(ys, jnp.swapaxes(dest[:, :TOP_K], 0, 1).reshape(-1)).reshape(TOP_K, B, T, half)
    return _moe_final(yg, jnp.swapaxes(w, 1, 2), h2p, x1, gt2, p["fg"], p["sg"], p["su"], p["sd"])


def _prepare_weights(w_in, shift_mu, a_w0, a_w_up, a_a0, a_a_up, a_g_up, a_k_k, a_k_a, a_r_k, a_gn_w,
                     a_gn_b, a_proj, b_conv_w, b_a_log, b_dt_bias, b_norm_g, b_proj, w_out, router_w,
                     router_bias, exp_gate, exp_up, exp_down, sh_gate, sh_up, sh_down, norm1_g, norm2_g,
                     final_g):
    W = A_WIDTH
    w = w_in[0]
    c0 = A_COLS
    c1 = c0 + 3 * B_WIDTH
    c2 = c1 + B_WIDTH
    c3 = c2 + 4 * B_HEADS
    pad = jnp.zeros((D_MODEL, SMALL_COLS - 4 * B_HEADS), F32)
    w_a = w[:, :c0].astype(BF16)
    w_b = jnp.concatenate([w[:, c0:c1], w[:, c2:c3], pad], axis=1).astype(BF16)
    w_c = jnp.concatenate([w[:, c1:c2], w[:, c3:]], axis=1).astype(BF16)
    zeros = jnp.zeros((2, A_RANK_W, W), F32)
    up_comb = jnp.concatenate([jnp.concatenate([a_w_up[0], zeros], axis=2),
                               jnp.concatenate([zeros, a_a_up[0]], axis=2)], axis=1)
    head = jnp.arange(W) // A_HEAD_DIM
    ones_bd = (head[:, None] == head[None, :]).astype(BF16)
    small = lambda v: jnp.zeros((1, SMALL_COLS), F32).at[0, 2 * B_HEADS:4 * B_HEADS].set(v.reshape(-1))
    return dict(
        w_a=w_a, w_b=w_b, w_c=w_c, mu=shift_mu[0].reshape(1, A_COLS), w0=a_w0[0], a0=a_a0[0],
        up_comb=up_comb, g_up=a_g_up[0], k_k=a_k_k[0].reshape(1, W), k_a=a_k_a[0].reshape(1, W),
        r_k=a_r_k[0].reshape(1, W), ones_bd=ones_bd, gn_w=a_gn_w[0].reshape(1, W),
        gn_b=a_gn_b[0].reshape(1, W), a_proj=a_proj[0].astype(BF16),
        conv_w=b_conv_w[0].astype(BF16).astype(F32),
        alog=small(b_a_log[0]), dtb=small(b_dt_bias[0]),
        bng=jnp.tile(b_norm_g[0], B_HEADS).reshape(1, B_WIDTH), b_proj=b_proj[0].astype(BF16),
        w_out=w_out[0].astype(BF16), rwt=router_w[0].T, rbias=router_bias[0].reshape(N_EXPERTS, 1),
        wg=exp_gate[0].astype(BF16), wu=exp_up[0].astype(BF16), wd=exp_down[0].astype(BF16),
        sg=sh_gate[0].astype(BF16), su=sh_up[0].astype(BF16), sd=sh_down[0].astype(BF16),
        n1g=norm1_g[0].reshape(1, D_MODEL), n2g=norm2_g[0].reshape(1, D_MODEL),
        fg=final_g.reshape(1, D_MODEL))


def _layer(x, mod, p):
    B = x.shape[0]
    sh1, sc1, gt1, sh2, sc2, gt2 = (m.reshape(B, 1, D_MODEL) for m in jnp.split(mod, 6, axis=-1))
    r, kk, v, g, bonus, lw, kd, bd = _rwkv_prep(x, p["n1g"], sc1, sh1, p["w_a"], p["mu"], p["w0"], p["a0"],
                                                p["up_comb"], p["g_up"], p["k_k"], p["k_a"], p["r_k"],
                                                p["ones_bd"])
    yf, yb = _rwkv_scan(r, kk, v, lw, kd, bd)
    q, k, vv, bg = _gdn_prep(x, p["n1g"], sc1, sh1, p["w_b"], p["conv_w"], p["alog"], p["dtb"])
    of, ob = _gdn_scan(q, k, vv, bg)
    x1, h2p, idx, w, pos, counts = _mixer_post(yf, yb, bonus, g, of, ob, p["n1g"], sc1, sh1, p["w_c"], x,
                                               gt1, sc2, sh2, p["gn_w"],
                                               p["gn_b"], p["ones_bd"], p["a_proj"], p["bng"], p["b_proj"],
                                               p["w_out"], p["n2g"], p["rwt"], p["rbias"])
    return _moe_routed(h2p, idx, w, pos, counts, x1, gt2, p)


def kernel(x_prompt, x_sample, c_prompt, c_sample, ada_w, ada_b, norm1_g, norm2_g, w_in, shift_mu, a_w0, a_w_up, a_a0, a_a_up, a_g_up, a_k_k, a_k_a, a_r_k, a_gn_w, a_gn_b, a_proj, b_conv_w, b_a_log, b_dt_bias, b_norm_g, b_proj, w_out, router_w, router_bias, exp_gate, exp_up, exp_down, sh_gate, sh_up, sh_down, final_g):
    p = _prepare_weights(w_in, shift_mu, a_w0, a_w_up, a_a0, a_a_up, a_g_up, a_k_k, a_k_a, a_r_k, a_gn_w,
                         a_gn_b, a_proj, b_conv_w, b_a_log, b_dt_bias, b_norm_g, b_proj, w_out, router_w,
                         router_bias, exp_gate, exp_up, exp_down, sh_gate, sh_up, sh_down, norm1_g,
                         norm2_g, final_g)
    nb_p, nb_s = c_prompt.shape[0], c_sample.shape[0]
    rows = -(-(nb_p + nb_s) // SUBLANES) * SUBLANES
    c_all = jnp.concatenate([c_prompt, c_sample, jnp.zeros((rows - nb_p - nb_s, D_MODEL), F32)], axis=0)
    mod = _adaln_mod(c_all, ada_w[0], ada_b[0])
    y_prompt = _layer(x_prompt, mod[:nb_p], p)
    y_sample = _layer(x_sample, mod[nb_p:nb_p + nb_s], p)
    return (y_prompt, y_sample)
```
